```python
import math
import jax, jax.numpy as jnp
from jax import lax
import numpy as np

D_MODEL = 2048
BATCH = 1
SEQ = 8192
DEPTH = 4

QBLK = 128
DA_HEADS = 4
DA_DIM = 64
DSA_HEADS = 8
DSA_DIM = 64
DSA_KV_RANK = 128
IDX_HEADS = 8
IDX_DIM = 64
DSA_TOPK_MAX = 256
SWA_HEADS = 8
SWA_KV_HEADS = 2
SWA_DIM = 64
SWA_WINDOW = 128
NSA_HEADS = 8
NSA_KV_HEADS = 2
NSA_DIM = 64
NSA_CMP_LEN = 32
NSA_CMP_STRIDE = 16
NSA_CMP_HID = 256
NSA_SLC_LEN = 64
NSA_TOPN = 16
NSA_WINDOW = 512
NSA_FORCE = 1e9
N_BRANCH = 4
BRANCH_W = 512
SPLIT_SIZES = (
    DA_HEADS * 2 * DA_DIM, DA_HEADS * 2 * DA_DIM, DA_HEADS * 2 * DA_DIM,
    DSA_HEADS * DSA_DIM, DSA_KV_RANK, IDX_HEADS * IDX_DIM, IDX_DIM, IDX_HEADS,
    SWA_HEADS * SWA_DIM, SWA_KV_HEADS * SWA_DIM, SWA_KV_HEADS * SWA_DIM,
    NSA_HEADS * NSA_DIM, 6 * NSA_KV_HEADS * NSA_DIM, 3 * NSA_HEADS,
)
IN_W = sum(SPLIT_SIZES)
D_FF = 5632
N_EXPERTS = 8
TOP_K = 2
D_EXPERT = 4096
ALPHA = (2.0 * DEPTH) ** 0.25
BETA = (8.0 * DEPTH) ** -0.25
NEG = -1e30

kernel_name = "hybrid_gated_diff_dsa_swa_nsa_moe_trunk"


def alibi_slopes(n_heads):
    return np.array([2.0 ** (-8.0 * (h + 1) / n_heads) for h in range(n_heads)], dtype=np.float32)


def layer_norm(x, g, b, eps=1e-5):
    xf = x.astype(jnp.float32)
    mu = jnp.mean(xf, -1, keepdims=True)
    var = jnp.mean(jnp.square(xf - mu), -1, keepdims=True)
    return ((xf - mu) * lax.rsqrt(var + eps) * g.astype(jnp.float32) + b.astype(jnp.float32)).astype(x.dtype)


def rms_norm(x, g, eps=1e-6):
    xf = x.astype(jnp.float32)
    return (xf * lax.rsqrt(jnp.mean(xf * xf, -1, keepdims=True) + eps) * g.astype(jnp.float32)).astype(x.dtype)


def diff_attention(q, k, v, lam, subln_g, lambda_init):
    B, S, H, _, d = q.shape
    nb = S // QBLK
    slopes = jnp.asarray(alibi_slopes(H))
    kpos = jnp.arange(S)
    scale = d ** -0.5
    qb = jnp.moveaxis(q.reshape(B, nb, QBLK, H, 2, d), 1, 0)

    def block(args):
        qi, i = args
        qpos = i * QBLK + jnp.arange(QBLK)
        dist = qpos[:, None] - kpos[None, :]
        s = jnp.einsum('bqhmd,bkhmd->bhmqk', qi, k).astype(jnp.float32) * scale
        s = s - slopes[:, None, None, None] * dist.astype(jnp.float32)
        s = jnp.where(dist >= 0, s, NEG)
        p = jax.nn.softmax(s, axis=-1)
        p = p[:, :, 0] - lam * p[:, :, 1]
        return jnp.einsum('bhqk,bkhe->bqhe', p.astype(v.dtype), v)

    o = lax.map(block, (qb, jnp.arange(nb)))
    o = jnp.moveaxis(o, 0, 1).reshape(B, S, H, 2 * d)
    o = rms_norm(o, subln_g) * (1.0 - lambda_init)
    return o.reshape(B, S, H * 2 * d)


def dsa_attention(q, k, v, q_idx, k_idx, w_idx):
    B, S, H, d = q.shape
    nb = S // QBLK
    topk = min(DSA_TOPK_MAX, S // 4)
    slopes = jnp.asarray(alibi_slopes(H))
    kpos = jnp.arange(S)
    bidx = jnp.arange(B)[:, None, None]
    scale = d ** -0.5
    qb = jnp.moveaxis(q.reshape(B, nb, QBLK, H, d), 1, 0)
    qib = jnp.moveaxis(q_idx.reshape(B, nb, QBLK, IDX_HEADS, IDX_DIM), 1, 0)
    wib = jnp.moveaxis(w_idx.reshape(B, nb, QBLK, IDX_HEADS), 1, 0)

    def block(args):
        qq, qi, wi, i = args
        qpos = i * QBLK + jnp.arange(QBLK)
        logits = jnp.einsum('bqhd,bkd->bqhk', qi, k_idx).astype(jnp.float32)
        score = jnp.einsum('bqh,bqhk->bqk', wi.astype(jnp.float32), jax.nn.relu(logits))
        score = jnp.where(kpos[None, None, :] <= qpos[None, :, None], score, NEG)
        _, sel = lax.top_k(score, topk)
        ks = k[bidx, sel]
        vs = v[bidx, sel]
        dist = qpos[None, :, None] - sel
        s = jnp.einsum('bqhd,bqkhd->bhqk', qq, ks).astype(jnp.float32) * scale
        s = s - slopes[None, :, None, None] * dist[:, None].astype(jnp.float32)
        s = jnp.where(dist[:, None] >= 0, s, NEG)
        p = jax.nn.softmax(s, axis=-1)
        return jnp.einsum('bhqk,bqkhd->bqhd', p.astype(v.dtype), vs)

    o = lax.map(block, (qb, qib, wib, jnp.arange(nb)))
    return jnp.moveaxis(o, 0, 1).reshape(B, S, H * d)


def band_keys(x, n_prev):
    B, S = x.shape[:2]
    nb = S // QBLK
    xb = x.reshape((B, nb, QBLK) + x.shape[2:])
    pad = jnp.zeros((B, n_prev, QBLK) + x.shape[2:], x.dtype)
    xp = jnp.concatenate([pad, xb], axis=1)
    return jnp.concatenate([xp[:, j:j + nb] for j in range(n_prev + 1)], axis=2)


def banded_attention(q, k, v, window, slopes, sinks=None):
    B, S, G, R, d = q.shape
    nb = S // QBLK
    n_prev = -(-(window - 1) // QBLK)
    kw = (n_prev + 1) * QBLK
    kb = band_keys(k, n_prev)
    vb = band_keys(v, n_prev)
    qb = q.reshape(B, nb, QBLK, G, R, d)
    dist = n_prev * QBLK + jnp.arange(QBLK)[:, None] - jnp.arange(kw)[None, :]
    kpos = (jnp.arange(nb)[:, None] - n_prev) * QBLK + jnp.arange(kw)[None, :]
    valid = ((dist >= 0) & (dist < window))[None] & (kpos >= 0)[:, None, :]
    s = jnp.einsum('bnqgrd,bnkgd->bngrqk', qb, kb).astype(jnp.float32) * (d ** -0.5)
    s = s - slopes[:, :, None, None] * dist.astype(jnp.float32)
    s = jnp.where(valid[None, :, None, None], s, NEG)
    if sinks is None:
        p = jax.nn.softmax(s, axis=-1)
    else:
        sk = sinks.astype(jnp.float32)[:, :, None, None]
        m = jnp.maximum(jnp.max(s, -1, keepdims=True), sk)
        e = jnp.exp(s - m)
        p = e / (jnp.sum(e, -1, keepdims=True) + jnp.exp(sk - m))
    o = jnp.einsum('bngrqk,bnkgd->bnqgrd', p.astype(v.dtype), vb)
    return o.reshape(B, S, G, R, d)


def nsa_attention(q, kv, gates, cmp_pos, cmp_w1, cmp_w2):
    B, S, G, R, d = q.shape
    k_c, v_c, k_s, v_s, k_w, v_w = [kv[:, :, j] for j in range(6)]
    slopes = jnp.asarray(alibi_slopes(G * R)).reshape(G, R)
    scale = d ** -0.5
    n_cmp = (S - NSA_CMP_LEN) // NSA_CMP_STRIDE + 1
    cidx = np.arange(n_cmp)[:, None] * NSA_CMP_STRIDE + np.arange(NSA_CMP_LEN)[None, :]

    def compress(t, j):
        blocks = t[:, cidx] + cmp_pos[j][:, None, :]
        flat = jnp.swapaxes(blocks, 2, 3).reshape(B, n_cmp, G, NSA_CMP_LEN * d)
        return jax.nn.silu(flat @ cmp_w1[j]) @ cmp_w2[j]

    k_cmp = compress(k_c, 0)
    v_cmp = compress(v_c, 1)
    cmp_end = jnp.asarray(cidx[:, -1])
    n_slc = S // NSA_SLC_LEN
    topn = min(NSA_TOPN, n_slc)
    starts = cidx[:, 0]
    slc_start = np.arange(n_slc) * NSA_SLC_LEN
    overlap = jnp.asarray(((starts[:, None] < slc_start[None, :] + NSA_SLC_LEN)
                           & (starts[:, None] + NSA_CMP_LEN > slc_start[None, :])).astype(np.float32))
    ks_g = jnp.swapaxes(k_s, 1, 2)
    vs_g = jnp.swapaxes(v_s, 1, 2)
    bidx = jnp.arange(B)[:, None, None, None]
    gidx = jnp.arange(G)[None, :, None, None]
    blk = jnp.arange(n_slc)
    nb = S // QBLK
    qb = jnp.moveaxis(q.reshape(B, nb, QBLK, G, R, d), 1, 0)

    def block(args):
        qq, i = args
        qpos = i * QBLK + jnp.arange(QBLK)
        dist_c = qpos[:, None] - cmp_end[None, :]
        valid_c = dist_c >= 0
        s = jnp.einsum('bqgrd,bcgd->bgrqc', qq, k_cmp).astype(jnp.float32) * scale
        s = s - slopes[:, :, None, None] * dist_c.astype(jnp.float32)
        s = jnp.where(valid_c, s, NEG)
        e = jnp.where(valid_c, jnp.exp(s - jnp.max(s, -1, keepdims=True)), 0.0)
        p_cmp = e / jnp.maximum(jnp.sum(e, -1, keepdims=True), 1e-30)
        o_cmp = jnp.einsum('bgrqc,bcgd->bqgrd', p_cmp.astype(v_cmp.dtype), v_cmp)
        imp = jnp.einsum('bgrqc,cj->bgqj', p_cmp, overlap)
        cur = (qpos // NSA_SLC_LEN)[:, None]
        forced = (blk[None, :] == 0) | (blk[None, :] == cur) | (blk[None, :] == cur - 1)
        imp = jnp.where(forced, NSA_FORCE, imp)
        imp = jnp.where(blk[None, :] * NSA_SLC_LEN <= qpos[:, None], imp, NEG)
        _, sel = lax.top_k(imp, topn)
        tok = (sel[..., None] * NSA_SLC_LEN + jnp.arange(NSA_SLC_LEN)).reshape(B, G, QBLK, topn * NSA_SLC_LEN)
        ksel = ks_g[bidx, gidx, tok]
        vsel = vs_g[bidx, gidx, tok]
        dist_s = qpos[None, None, :, None] - tok
        s2 = jnp.einsum('bqgrd,bgqtd->bgrqt', qq, ksel).astype(jnp.float32) * scale
        s2 = s2 - slopes[None, :, :, None, None] * dist_s[:, :, None].astype(jnp.float32)
        s2 = jnp.where(dist_s[:, :, None] >= 0, s2, NEG)
        p2 = jax.nn.softmax(s2, axis=-1)
        o_slc = jnp.einsum('bgrqt,bgqtd->bqgrd', p2.astype(vsel.dtype), vsel)
        return o_cmp, o_slc

    o_cmp, o_slc = lax.map(block, (qb, jnp.arange(nb)))
    o_cmp = jnp.moveaxis(o_cmp, 0, 1).reshape(B, S, G, R, d)
    o_slc = jnp.moveaxis(o_slc, 0, 1).reshape(B, S, G, R, d)
    o_win = banded_attention(q, k_w, v_w, NSA_WINDOW, slopes)
    o = gates[..., 0:1] * o_cmp + gates[..., 1:2] * o_slc + gates[..., 2:3] * o_win
    return o.reshape(B, S, G * R * d)


def token_mixers(u, layer, w_in, diff_lambda, diff_subln_g, dsa_kv_norm_g, dsa_w_uk, dsa_w_uv,
                 swa_sinks, nsa_cmp_pos, nsa_cmp_w1, nsa_cmp_w2, w_branch, w_gate, w_o):
    B, S, _ = u.shape
    points = [int(p) for p in np.cumsum(SPLIT_SIZES)[:-1]]
    (a_q, a_k, a_v, b_q, b_kv, b_iq, b_ik, b_iw, c_q, c_k, c_v, d_q, d_kv, d_g) = jnp.split(u @ w_in, points, axis=-1)
    lambda_init = 0.8 - 0.6 * math.exp(-0.3 * layer)
    lf = diff_lambda.astype(jnp.float32)
    lam = jnp.exp(jnp.sum(lf[0] * lf[1])) - jnp.exp(jnp.sum(lf[2] * lf[3])) + lambda_init
    o_a = diff_attention(a_q.reshape(B, S, DA_HEADS, 2, DA_DIM), a_k.reshape(B, S, DA_HEADS, 2, DA_DIM),
                         a_v.reshape(B, S, DA_HEADS, 2 * DA_DIM), lam, diff_subln_g, lambda_init)
    c_kv = rms_norm(b_kv, dsa_kv_norm_g)
    o_b = dsa_attention(b_q.reshape(B, S, DSA_HEADS, DSA_DIM),
                        (c_kv @ dsa_w_uk).reshape(B, S, DSA_HEADS, DSA_DIM),
                        (c_kv @ dsa_w_uv).reshape(B, S, DSA_HEADS, DSA_DIM),
                        b_iq.reshape(B, S, IDX_HEADS, IDX_DIM), b_ik, b_iw)
    rc = SWA_HEADS // SWA_KV_HEADS
    o_c = banded_attention(c_q.reshape(B, S, SWA_KV_HEADS, rc, SWA_DIM),
                           c_k.reshape(B, S, SWA_KV_HEADS, SWA_DIM), c_v.reshape(B, S, SWA_KV_HEADS, SWA_DIM),
                           SWA_WINDOW, jnp.asarray(alibi_slopes(SWA_HEADS)).reshape(SWA_KV_HEADS, rc),
                           swa_sinks.reshape(SWA_KV_HEADS, rc)).reshape(B, S, SWA_HEADS * SWA_DIM)
    rd = NSA_HEADS // NSA_KV_HEADS
    o_d = nsa_attention(d_q.reshape(B, S, NSA_KV_HEADS, rd, NSA_DIM),
                        d_kv.reshape(B, S, 6, NSA_KV_HEADS, NSA_DIM),
                        jax.nn.sigmoid(d_g.reshape(B, S, NSA_KV_HEADS, rd, 3)),
                        nsa_cmp_pos, nsa_cmp_w1, nsa_cmp_w2)
    branches = jnp.stack([o_a, o_b, o_c, o_d], axis=2)
    z = jnp.einsum('bsmc,mcd->bsmd', branches, w_branch)
    g = jax.nn.sigmoid(u @ w_gate).reshape(B, S, N_BRANCH, D_MODEL)
    return jnp.sum(g * z, axis=2) @ w_o


def swiglu(u, wg, wu, wd):
    return (jax.nn.silu(u @ wg) * (u @ wu)) @ wd


def moe_swiglu(u, router, wg, wu, wd):
    logits = (u @ router).astype(jnp.float32)
    top_v, top_i = lax.top_k(logits, TOP_K)
    top_w = jax.nn.softmax(top_v, axis=-1)
    y = jnp.zeros_like(u)
    for e in range(N_EXPERTS):
        we = jnp.sum(jnp.where(top_i == e, top_w, 0.0), axis=-1)
        y = y + we[..., None].astype(u.dtype) * swiglu(u, wg[e], wu[e], wd[e])
    return y


def setup_inputs(seed: int = 0) -> dict:
    key = jax.random.key(seed)
    keys = iter(jax.random.split(key, 40))

    def nrm(shape, scale):
        return scale * jax.random.normal(next(keys), shape, jnp.float32)

    L, D = DEPTH, D_MODEL
    nd, nm = (DEPTH + 1) // 2, DEPTH // 2
    return {
        "x": nrm((BATCH, SEQ, D), 1.0),
        "c": nrm((BATCH, D), 1.0),
        "cond_w": nrm((L, D, 6 * D), 0.5 * D ** -0.5),
        "cond_b": nrm((L, 6 * D), 0.02),
        "w_in": nrm((L, D, IN_W), D ** -0.5),
        "diff_lambda": nrm((L, 4, DA_DIM), 0.1),
        "diff_subln_g": 1.0 + nrm((L, 2 * DA_DIM), 0.02),
        "dsa_kv_norm_g": 1.0 + nrm((L, DSA_KV_RANK), 0.02),
        "dsa_w_uk": nrm((L, DSA_KV_RANK, DSA_HEADS * DSA_DIM), DSA_KV_RANK ** -0.5),
        "dsa_w_uv": nrm((L, DSA_KV_RANK, DSA_HEADS * DSA_DIM), DSA_KV_RANK ** -0.5),
        "swa_sinks": nrm((L, SWA_HEADS), 0.5),
        "nsa_cmp_pos": nrm((L, 2, NSA_CMP_LEN, NSA_DIM), 0.2),
        "nsa_cmp_w1": nrm((L, 2, NSA_CMP_LEN * NSA_DIM, NSA_CMP_HID), (NSA_CMP_LEN * NSA_DIM) ** -0.5),
        "nsa_cmp_w2": nrm((L, 2, NSA_CMP_HID, NSA_DIM), NSA_CMP_HID ** -0.5),
        "w_branch": nrm((L, N_BRANCH, BRANCH_W, D), BRANCH_W ** -0.5),
        "w_gate": nrm((L, D, N_BRANCH * D), D ** -0.5),
        "w_o": nrm((L, D, D), BETA * D ** -0.5),
        "ln1_g": 1.0 + nrm((L, D), 0.02),
        "ln1_b": nrm((L, D), 0.02),
        "ln2_g": 1.0 + nrm((L, D), 0.02),
        "ln2_b": nrm((L, D), 0.02),
        "ffn_w_gate": nrm((nd, D, D_FF), D ** -0.5),
        "ffn_w_up": nrm((nd, D, D_FF), D ** -0.5),
        "ffn_w_down": nrm((nd, D_FF, D), BETA * D_FF ** -0.5),
        "moe_router": nrm((nm, D, N_EXPERTS), D ** -0.5),
        "moe_w_gate": nrm((nm, N_EXPERTS, D, D_EXPERT), D ** -0.5),
        "moe_w_up": nrm((nm, N_EXPERTS, D, D_EXPERT), D ** -0.5),
        "moe_w_down": nrm((nm, N_EXPERTS, D_EXPERT, D), BETA * D_EXPERT ** -0.5),
    }


def reference(x, c, cond_w, cond_b, w_in, diff_lambda, diff_subln_g, dsa_kv_norm_g, dsa_w_uk, dsa_w_uv,
              swa_sinks, nsa_cmp_pos, nsa_cmp_w1, nsa_cmp_w2, w_branch, w_gate, w_o,
              ln1_g, ln1_b, ln2_g, ln2_b, ffn_w_gate, ffn_w_up, ffn_w_down,
              moe_router, moe_w_gate, moe_w_up, moe_w_down):
    c_act = jax.nn.silu(c)
    for layer in range(DEPTH):
        mod = (c_act @ cond_w[layer] + cond_b[layer])[:, None, :]
        shift1, scale1, gate1, shift2, scale2, gate2 = jnp.split(mod, 6, axis=-1)
        u = x * (1.0 + scale1) + shift1
        y = token_mixers(u, layer, w_in[layer], diff_lambda[layer], diff_subln_g[layer], dsa_kv_norm_g[layer],
                         dsa_w_uk[layer], dsa_w_uv[layer], swa_sinks[layer], nsa_cmp_pos[layer],
                         nsa_cmp_w1[layer], nsa_cmp_w2[layer], w_branch[layer], w_gate[layer], w_o[layer])
        x = layer_norm(ALPHA * x + gate1 * y, ln1_g[layer], ln1_b[layer])
        u = x * (1.0 + scale2) + shift2
        if layer % 2 == 0:
            j = layer // 2
            y = swiglu(u, ffn_w_gate[j], ffn_w_up[j], ffn_w_down[j])
        else:
            j = layer // 2
            y = moe_swiglu(u, moe_router[j], moe_w_gate[j], moe_w_up[j], moe_w_down[j])
        x = layer_norm(ALPHA * x + gate2 * y, ln2_g[layer], ln2_b[layer])
    return x
```

```python
import functools
import math

import numpy as np
import jax
import jax.numpy as jnp
from jax import lax
from jax.experimental import pallas as pl
from jax.experimental.pallas import tpu as pltpu

F32 = jnp.float32
BF16 = jnp.bfloat16
NEG = -1e30

D_MODEL = 2048
DEPTH = 4
HEAD_DIM = 64
DA_HEADS = 4
DSA_HEADS = 8
DSA_KV_RANK = 128
IDX_HEADS = 8
DSA_TOPK_MAX = 256
SWA_HEADS = 8
SWA_WINDOW = 128
NSA_HEADS = 8
NSA_GROUPS = 2
NSA_CMP_LEN = 32
NSA_CMP_STRIDE = 16
NSA_CMP_HID = 256
NSA_SLC_LEN = 64
NSA_TOPN = 16
NSA_WINDOW = 512
NSA_FORCE = 1e9
N_BRANCH = 4
BRANCH_W = 512
N_EXPERTS = 8
ALPHA = (2.0 * DEPTH) ** 0.25

VMEM_LIMIT_BYTES = 56 * 1024 * 1024
LANES = 128

_ORIG = dict(a_q=(0, 512), a_k=(512, 512), a_v=(1024, 512), b_q=(1536, 512), b_kv=(2048, 128),
             b_iq=(2176, 512), b_ik=(2688, 64), b_iw=(2752, 8), c_q=(2760, 512), c_k=(3272, 128),
             c_v=(3400, 128), d_q=(3528, 512), d_kv=(4040, 768), d_g=(4808, 24))
_NEW_ORDER = ("a_q", "a_k", "a_v", "b_q", "b_iq", "c_q", "d_q", "b_kv", "c_k", "c_v", "d_kv",
              "b_ik", "pad64", "b_iw", "d_g", "pad96", "pad128")
PROJ_W = 5120
BLK512 = dict(a_q=0, a_k=1, a_v=2, b_q=3, b_iq=4, c_q=5, d_q=6)
BLK128 = dict(b_kv=28, c_k=29, c_v=30, d_kc=31, d_vc=32, d_ks=33, d_vs=34, d_kw=35, d_vw=36,
              b_ik=37, small=38)
COL_DKV = 3968


def _cparams(sem):
    return pltpu.CompilerParams(dimension_semantics=sem, vmem_limit_bytes=VMEM_LIMIT_BYTES)


def _sigmoid(x):
    return 1.0 / (1.0 + jnp.exp(-x))


def _silu(x):
    return x * _sigmoid(x)


def _alibi(n_heads):
    return [2.0 ** (-8.0 * (h + 1) / n_heads) for h in range(n_heads)]


def _dot(a, b):
    return jnp.dot(a, b, preferred_element_type=F32)


def _dot_nt(a, b):
    return lax.dot_general(a, b, (((1,), (1,)), ((), ())), preferred_element_type=F32)


def _mm_kernel(*refs, prologue, has_bias, eps):
    it = iter(refs)
    a_ref = next(it)
    g_ref = next(it) if prologue == "rms" else None
    w_ref = next(it)
    b_ref = next(it) if has_bias else None
    o_ref = next(it)
    wb_ref = next(it)

    @pl.when(pl.program_id(1) == 0)
    def _():
        wb_ref[...] = w_ref[...].astype(BF16)

    a = a_ref[...]
    if prologue == "silu":
        a = _silu(a.astype(F32))
    elif prologue == "rms":
        a = a.astype(F32)
        a = a * lax.rsqrt(jnp.mean(a * a, axis=-1, keepdims=True) + eps) * g_ref[...]
    acc = _dot(a.astype(BF16), wb_ref[...])
    if has_bias:
        acc = acc + b_ref[...]
    o_ref[...] = acc.astype(o_ref.dtype)


def _mm(a, w, *, tm, tn, out_dtype=F32, a_blk=0, k=None, prologue=None, gain=None, bias=None,
        eps=1e-6, name="mm"):
    m = a.shape[0]
    k = a.shape[1] if k is None else k
    n = w.shape[1]
    assert w.shape[0] == k and m % tm == 0 and n % tn == 0
    in_specs = [pl.BlockSpec((tm, k), lambda j, i: (i, a_blk))]
    args = [a]
    if prologue == "rms":
        in_specs.append(pl.BlockSpec((1, k), lambda j, i: (0, 0)))
        args.append(gain.reshape(1, k))
    in_specs.append(pl.BlockSpec((k, tn), lambda j, i: (0, j)))
    args.append(w)
    if bias is not None:
        in_specs.append(pl.BlockSpec((1, tn), lambda j, i: (0, j)))
        args.append(bias.reshape(1, n))
    return pl.pallas_call(
        functools.partial(_mm_kernel, prologue=prologue, has_bias=bias is not None, eps=eps),
        grid=(n // tn, m // tm),
        in_specs=in_specs,
        out_specs=pl.BlockSpec((tm, tn), lambda j, i: (i, j)),
        out_shape=jax.ShapeDtypeStruct((m, n), out_dtype),
        scratch_shapes=[pltpu.VMEM((k, tn), BF16)],
        compiler_params=_cparams(("arbitrary", "arbitrary")),
        name=name,
    )(*args)


def _mmk_kernel(a_ref, w_ref, o_ref, acc_ref, *, nk):
    kk = pl.program_id(2)

    @pl.when(kk == 0)
    def _():
        acc_ref[...] = jnp.zeros_like(acc_ref)

    acc_ref[...] += _dot(a_ref[...], w_ref[...].astype(BF16))

    @pl.when(kk == nk - 1)
    def _():
        o_ref[...] = acc_ref[...]


def _mmk(a, w, *, tm, tn, tk, name="mmk"):
    m, k = a.shape
    n = w.shape[1]
    assert w.shape[0] == k and m % tm == 0 and n % tn == 0 and k % tk == 0
    nk = k // tk
    return pl.pallas_call(
        functools.partial(_mmk_kernel, nk=nk),
        grid=(m // tm, n // tn, nk),
        in_specs=[pl.BlockSpec((tm, tk), lambda i, j, kk: (i, kk)),
                  pl.BlockSpec((tk, tn), lambda i, j, kk: (kk, j))],
        out_specs=pl.BlockSpec((tm, tn), lambda i, j, kk: (i, j)),
        out_shape=jax.ShapeDtypeStruct((m, n), F32),
        scratch_shapes=[pltpu.VMEM((tm, tn), F32)],
        compiler_params=_cparams(("arbitrary", "arbitrary", "arbitrary")),
        name=name,
    )(a, w)


def _swiglu_kernel(*refs, has_scale):
    it = iter(refs)
    a_ref = next(it)
    wg_ref = next(it)
    wu_ref = next(it)
    s_ref = next(it) if has_scale else None
    o_ref = next(it)
    wgb_ref = next(it)
    wub_ref = next(it)

    @pl.when(pl.program_id(2) == 0)
    def _():
        wgb_ref[...] = wg_ref[0].astype(BF16)
        wub_ref[...] = wu_ref[0].astype(BF16)

    a = a_ref[...]
    h = _silu(_dot(a, wgb_ref[...])) * _dot(a, wub_ref[...])
    if has_scale:
        h = h * s_ref[0]
    o_ref[...] = h.astype(o_ref.dtype)


def _swiglu_up(u, wg, wu, scale=None, *, tm, tn, name="swiglu_up"):
    m, k = u.shape
    e, _, f = wg.shape
    assert f % tn == 0 and m % tm == 0
    nf = f // tn
    in_specs = [pl.BlockSpec((tm, k), lambda ee, j, i: (i, 0)),
                pl.BlockSpec((1, k, tn), lambda ee, j, i: (ee, 0, j)),
                pl.BlockSpec((1, k, tn), lambda ee, j, i: (ee, 0, j))]
    args = [u, wg, wu]
    if scale is not None:
        in_specs.append(pl.BlockSpec((1, tm, 1), lambda ee, j, i: (ee, i, 0)))
        args.append(scale)
    return pl.pallas_call(
        functools.partial(_swiglu_kernel, has_scale=scale is not None),
        grid=(e, nf, m // tm),
        in_specs=in_specs,
        out_specs=pl.BlockSpec((tm, tn), lambda ee, j, i: (i, ee * nf + j)),
        out_shape=jax.ShapeDtypeStruct((m, e * f), BF16),
        scratch_shapes=[pltpu.VMEM((k, tn), BF16), pltpu.VMEM((k, tn), BF16)],
        compiler_params=_cparams(("arbitrary", "arbitrary", "arbitrary")),
        name=name,
    )(*args)


def _modulate_kernel(x_ref, sc_ref, sh_ref, u_ref):
    u_ref[...] = (x_ref[...] * (1.0 + sc_ref[...]) + sh_ref[...]).astype(u_ref.dtype)


def _modulate(x, mod, sc_blk, sh_blk, *, tm):
    m, d = x.shape
    return pl.pallas_call(
        _modulate_kernel,
        grid=(m // tm,),
        in_specs=[pl.BlockSpec((tm, d), lambda i: (i, 0)),
                  pl.BlockSpec((1, d), lambda i: (0, sc_blk)),
                  pl.BlockSpec((1, d), lambda i: (0, sh_blk))],
        out_specs=pl.BlockSpec((tm, d), lambda i: (i, 0)),
        out_shape=jax.ShapeDtypeStruct((m, d), BF16),
        compiler_params=_cparams(("arbitrary",)),
        name="modulate",
    )(x, mod, mod)


def _resid_ln_kernel(x_ref, y_ref, gate_ref, g_ref, b_ref, sc_ref, sh_ref, xo_ref, u_ref):
    z = ALPHA * x_ref[...] + gate_ref[...] * y_ref[...]
    mu = jnp.mean(z, axis=-1, keepdims=True)
    zc = z - mu
    var = jnp.mean(zc * zc, axis=-1, keepdims=True)
    xn = zc * lax.rsqrt(var + 1e-5) * g_ref[...] + b_ref[...]
    xo_ref[...] = xn
    u_ref[...] = (xn * (1.0 + sc_ref[...]) + sh_ref[...]).astype(u_ref.dtype)


def _resid_ln(x, y, mod, gate_blk, g, b, mod_next, sc_blk, sh_blk, *, tm):
    m, d = x.shape
    row = lambda blk: pl.BlockSpec((1, d), lambda i: (0, blk))
    return pl.pallas_call(
        _resid_ln_kernel,
        grid=(m // tm,),
        in_specs=[pl.BlockSpec((tm, d), lambda i: (i, 0)),
                  pl.BlockSpec((tm, d), lambda i: (i, 0)),
                  row(gate_blk), row(0), row(0), row(sc_blk), row(sh_blk)],
        out_specs=[pl.BlockSpec((tm, d), lambda i: (i, 0)),
                   pl.BlockSpec((tm, d), lambda i: (i, 0))],
        out_shape=[jax.ShapeDtypeStruct((m, d), F32), jax.ShapeDtypeStruct((m, d), BF16)],
        compiler_params=_cparams(("arbitrary",)),
        name="resid_ln",
    )(x, y, mod, g.reshape(1, d), b.reshape(1, d), mod_next, mod_next)


def _flash_kernel(*refs, heads, tq, tk, window, dense, n_prev, nsteps, n_mask, has_sink):
    it = iter(refs)
    q_ref, k_ref, v_ref = next(it), next(it), next(it)
    mask_ref = next(it) if n_mask else None
    sink_ref = next(it) if has_sink else None
    o_ref = next(it)
    m_scr, l_scr, acc_scr = next(it), next(it), next(it)

    qi = pl.program_id(0)
    j = pl.program_id(1)
    last_kb = (qi * tq + tq - 1) // tk
    kb = j if dense else qi - n_prev + j
    scale = HEAD_DIM ** -0.5

    @pl.when(j == 0)
    def _init():
        for hi, h in enumerate(heads):
            if has_sink:
                m_scr[hi] = jnp.broadcast_to(sink_ref[:, hi:hi + 1], (tq, 1))
                l_scr[hi] = jnp.ones((tq, 1), F32)
            else:
                m_scr[hi] = jnp.full((tq, 1), NEG, F32)
                l_scr[hi] = jnp.zeros((tq, 1), F32)
            acc_scr[hi] = jnp.zeros(acc_scr.shape[1:], F32)

    @pl.when((kb >= 0) & (kb <= last_kb))
    def _step():
        qpos = qi * tq + lax.broadcasted_iota(jnp.int32, (tq, tk), 0)
        kpos = kb * tk + lax.broadcasted_iota(jnp.int32, (tq, tk), 1)
        dist = qpos - kpos
        valid = dist >= 0
        if not dense:
            valid = valid & (dist < window)
        distf = dist.astype(F32)
        if n_mask:
            valids = [valid & (mask_ref[g].astype(F32) > 0.5) for g in range(n_mask)]
        else:
            valids = [valid]
        for hi, h in enumerate(heads):
            qo, ko, vo, dv, slope, mg, _ = h
            qh = q_ref[:, qo:qo + HEAD_DIM].astype(BF16)
            kh = k_ref[:, ko:ko + HEAD_DIM].astype(BF16)
            s = _dot_nt(qh, kh) * scale - slope * distf
            s = jnp.where(valids[mg], s, NEG)
            m_old = m_scr[hi]
            m_new = jnp.maximum(m_old, jnp.max(s, axis=1, keepdims=True))
            p = jnp.exp(s - m_new)
            alpha = jnp.exp(m_old - m_new)
            l_scr[hi] = alpha * l_scr[hi] + jnp.sum(p, axis=1, keepdims=True)
            vh = v_ref[:, vo:vo + dv].astype(BF16)
            acc_scr[hi] = alpha * acc_scr[hi] + _dot(p.astype(BF16), vh)
            m_scr[hi] = m_new

    @pl.when(j == nsteps - 1)
    def _fin():
        for hi, h in enumerate(heads):
            dv, oo = h[3], h[6]
            o_ref[:, oo:oo + dv] = acc_scr[hi] / l_scr[hi]


def _flash(q_arr, k_arr, v_arr, *, heads, q_spec, k_spec, v_spec, out_w, tq, tk, window=None,
           mask=None, sinks=None, name="flash"):
    s_len = q_arr.shape[0]
    dense = window is None
    if dense:
        n_prev, nsteps = 0, s_len // tk
    else:
        assert tq == tk
        n_prev = -(-(window - 1) // tk)
        nsteps = n_prev + 1
    n_mask = 0 if mask is None else mask.shape[0]
    dv = heads[0][3]

    def kv_index(qi, j):
        last_kb = (qi * tq + tq - 1) // tk
        kb = j if dense else qi - n_prev + j
        return jnp.clip(kb, 0, last_kb)

    in_specs = [pl.BlockSpec((tq, q_spec[0]), lambda qi, j: (qi, q_spec[1])),
                pl.BlockSpec((tk, k_spec[0]), lambda qi, j: (kv_index(qi, j), k_spec[1])),
                pl.BlockSpec((tk, v_spec[0]), lambda qi, j: (kv_index(qi, j), v_spec[1]))]
    args = [q_arr, k_arr, v_arr]
    if n_mask:
        in_specs.append(pl.BlockSpec((n_mask, tq, tk), lambda qi, j: (0, qi, kv_index(qi, j))))
        args.append(mask)
    if sinks is not None:
        in_specs.append(pl.BlockSpec((1, LANES), lambda qi, j: (0, 0)))
        args.append(sinks)
    nh = len(heads)
    return pl.pallas_call(
        functools.partial(_flash_kernel, heads=tuple(heads), tq=tq, tk=tk, window=window, dense=dense,
                          n_prev=n_prev, nsteps=nsteps, n_mask=n_mask, has_sink=sinks is not None),
        grid=(s_len // tq, nsteps),
        in_specs=in_specs,
        out_specs=pl.BlockSpec((tq, out_w), lambda qi, j: (qi, 0)),
        out_shape=jax.ShapeDtypeStruct((s_len, out_w), F32),
        scratch_shapes=[pltpu.VMEM((nh, tq, 1), F32), pltpu.VMEM((nh, tq, 1), F32),
                        pltpu.VMEM((nh, tq, dv), F32)],
        compiler_params=_cparams(("arbitrary", "arbitrary")),
        name=name,
    )(*args)


def _diff_final_kernel(o_ref, lam_ref, g_ref, out_ref, *, lambda_init):
    lf = lam_ref[...]
    lam = (jnp.exp(jnp.sum(lf[0:1] * lf[1:2])) - jnp.exp(jnp.sum(lf[2:3] * lf[3:4])) + lambda_init)
    w = 2 * HEAD_DIM
    for h in range(DA_HEADS):
        o = o_ref[:, (2 * h) * w:(2 * h + 1) * w] - lam * o_ref[:, (2 * h + 1) * w:(2 * h + 2) * w]
        o = o * lax.rsqrt(jnp.mean(o * o, axis=-1, keepdims=True) + 1e-6) * g_ref[...]
        out_ref[:, h * w:(h + 1) * w] = o * (1.0 - lambda_init)


def _diff_final(o, diff_lambda, subln_g, lambda_init, *, tm):
    m = o.shape[0]
    w = 2 * HEAD_DIM
    return pl.pallas_call(
        functools.partial(_diff_final_kernel, lambda_init=lambda_init),
        grid=(m // tm,),
        in_specs=[pl.BlockSpec((tm, 2 * DA_HEADS * w), lambda i: (i, 0)),
                  pl.BlockSpec((4, HEAD_DIM), lambda i: (0, 0)),
                  pl.BlockSpec((1, w), lambda i: (0, 0))],
        out_specs=pl.BlockSpec((tm, DA_HEADS * w), lambda i: (i, 0)),
        out_shape=jax.ShapeDtypeStruct((m, DA_HEADS * w), F32),
        compiler_params=_cparams(("arbitrary",)),
        name="diff_final",
    )(o, diff_lambda, subln_g.reshape(1, w))


INT_MIN = -(2 ** 31)


def _f32_key_const(x):
    b = int(np.array(x, np.float32).view(np.int32))
    return b ^ ((b >> 31) & 0x7FFFFFFF)


def _dsa_select_kernel(qi_ref, w_ref, kidx_ref, mask_ref, key_scr, j_scr, *, tq, ch, nch, topk, s_len):
    i = pl.program_id(0)
    q0 = i * tq
    n_need = (q0 + tq + ch - 1) // ch
    qpos = q0 + lax.broadcasted_iota(jnp.int32, (tq, 1), 0)
    lane = lax.broadcasted_iota(jnp.int32, (1, ch), 1)
    w = w_ref[:, 0:IDX_HEADS]

    def score_chunk(c, carry):
        kc = kidx_ref[pl.ds(pl.multiple_of(c * ch, ch), ch), 0:HEAD_DIM].astype(BF16)
        acc = jnp.zeros((tq, ch), F32)
        for h in range(IDX_HEADS):
            qh = qi_ref[:, h * HEAD_DIM:(h + 1) * HEAD_DIM].astype(BF16)
            acc = acc + w[:, h:h + 1] * jnp.maximum(_dot_nt(qh, kc), 0.0)
        acc = jnp.where(c * ch + lane <= qpos, acc, NEG) + 0.0
        bits = pltpu.bitcast(acc, jnp.int32)
        key_scr[c] = bits ^ ((bits >> 31) & 0x7FFFFFFF)
        return carry

    lax.fori_loop(0, n_need, score_chunk, 0)

    def count(pred):
        def body(c, acc):
            m = jnp.where(pred(key_scr[c], c), 1, 0)
            part = m[:, 0:LANES]
            for t in range(1, ch // LANES):
                part = part + m[:, t * LANES:(t + 1) * LANES]
            return acc + part
        acc = lax.fori_loop(0, n_need, body, jnp.zeros((tq, LANES), jnp.int32))
        return jnp.sum(acc, axis=1, keepdims=True)

    def bit_step(b, t):
        cand = t + jnp.left_shift(jnp.int32(1), 31 - b)
        cnt = count(lambda blk, c: blk >= cand)
        return jnp.where(cnt >= topk, cand, t)

    thr = lax.fori_loop(0, 32, bit_step, jnp.full((tq, 1), INT_MIN, jnp.int32))
    cnt_gt = count(lambda blk, c: blk > thr)
    cnt_ge = count(lambda blk, c: blk >= thr)
    need = topk - cnt_gt
    tie_rows = (cnt_ge > topk) & (thr > _f32_key_const(NEG))
    j_scr[...] = jnp.full((tq, 1), s_len, jnp.int32)

    @pl.when(jnp.max(jnp.where(tie_rows, 1, 0)) > 0)
    def _ties():
        nbits = int(math.log2(s_len))

        def idx_step(b, jv):
            cand = jv + jnp.left_shift(jnp.int32(1), nbits - 1 - b)
            cnt = count(lambda blk, c: (blk == thr) & (c * ch + lane < cand))
            return jnp.where(cnt < need, cand, jv)

        jv = lax.fori_loop(0, nbits, idx_step, jnp.zeros((tq, 1), jnp.int32))
        j_scr[...] = jnp.where(tie_rows, jv, s_len)

    jv = j_scr[...]
    for c in range(nch):
        @pl.when(c < n_need)
        def _w():
            key = key_scr[c]
            idx = c * ch + lane
            sel = ((key > thr) | ((key == thr) & (idx <= jv))) & (idx <= qpos)
            mask_ref[:, c * ch:(c + 1) * ch] = jnp.where(sel, 1, 0).astype(jnp.int8)

        @pl.when(c >= n_need)
        def _z():
            mask_ref[:, c * ch:(c + 1) * ch] = jnp.zeros((tq, ch), jnp.int8)


def _dsa_select(proj, *, topk, tq=128):
    s_len = proj.shape[0]
    ch = min(1024, s_len)
    nch = s_len // ch
    return pl.pallas_call(
        functools.partial(_dsa_select_kernel, tq=tq, ch=ch, nch=nch, topk=topk, s_len=s_len),
        grid=(s_len // tq,),
        in_specs=[pl.BlockSpec((tq, 512), lambda i: (i, BLK512["b_iq"])),
                  pl.BlockSpec((tq, LANES), lambda i: (i, BLK128["small"])),
                  pl.BlockSpec((s_len, LANES), lambda i: (0, BLK128["b_ik"]))],
        out_specs=pl.BlockSpec((tq, s_len), lambda i: (i, 0)),
        out_shape=jax.ShapeDtypeStruct((s_len, s_len), jnp.int8),
        scratch_shapes=[pltpu.VMEM((nch, tq, ch), jnp.int32), pltpu.VMEM((tq, 1), jnp.int32)],
        compiler_params=_cparams(("arbitrary",)),
        name="dsa_select",
    )(proj, proj, proj)


def _nsa_compress_kernel(x_ref, pos_ref, w1_ref, w2_ref, o_ref):
    x = (x_ref[0] + pos_ref[0]).astype(BF16)
    hdn = _silu(_dot(x, w1_ref[0].astype(BF16)))
    o_ref[0] = _dot(hdn.astype(BF16), w2_ref[0].astype(BF16))


def _nsa_compress(xc, pos, w1, w2):
    _, ncp, kdim = xc.shape
    return pl.pallas_call(
        _nsa_compress_kernel,
        grid=(4,),
        in_specs=[pl.BlockSpec((1, ncp, kdim), lambda i: (i, 0, 0)),
                  pl.BlockSpec((1, 1, kdim), lambda i: (i // 2, 0, 0)),
                  pl.BlockSpec((1, kdim, NSA_CMP_HID), lambda i: (i // 2, 0, 0)),
                  pl.BlockSpec((1, NSA_CMP_HID, HEAD_DIM), lambda i: (i // 2, 0, 0))],
        out_specs=pl.BlockSpec((1, ncp, HEAD_DIM), lambda i: (i, 0, 0)),
        out_shape=jax.ShapeDtypeStruct((4, ncp, HEAD_DIM), F32),
        compiler_params=_cparams(("arbitrary",)),
        name="nsa_compress",
    )(xc, pos, w1, w2)


def _nsa_cmp_kernel(q_ref, kv_ref, ov_ref, ex_ref, o_ref, mask_ref, *, tq, ncp, n_slc, topn):
    i = pl.program_id(0)
    q0 = i * tq
    rpg = NSA_HEADS // NSA_GROUPS
    slopes = _alibi(NSA_HEADS)
    scale = HEAD_DIM ** -0.5
    qpos_c = q0 + lax.broadcasted_iota(jnp.int32, (tq, ncp), 0)
    cend = lax.broadcasted_iota(jnp.int32, (tq, ncp), 1) * NSA_CMP_STRIDE + (NSA_CMP_LEN - 1)
    dist_c = qpos_c - cend
    valid_c = dist_c >= 0
    distf = dist_c.astype(F32)
    qpos = q0 + lax.broadcasted_iota(jnp.int32, (tq, n_slc), 0)
    blk = lax.broadcasted_iota(jnp.int32, (tq, n_slc), 1)
    cur = qpos // NSA_SLC_LEN
    forced = (blk == 0) | (blk == cur) | (blk == cur - 1)
    blk_ok = blk * NSA_SLC_LEN <= qpos
    ov = ov_ref[...]
    for g in range(NSA_GROUPS):
        kc = kv_ref[g].astype(BF16)
        vc = kv_ref[NSA_GROUPS + g].astype(BF16)
        psum = jnp.zeros((tq, ncp), F32)
        for r in range(rpg):
            h = g * rpg + r
            qh = q_ref[:, h * HEAD_DIM:(h + 1) * HEAD_DIM].astype(BF16)
            s = _dot_nt(qh, kc) * scale - slopes[h] * distf
            s = jnp.where(valid_c, s, NEG)
            e = jnp.where(valid_c, jnp.exp(s - jnp.max(s, axis=1, keepdims=True)), 0.0)
            p = e / jnp.maximum(jnp.sum(e, axis=1, keepdims=True), 1e-30)
            o_ref[:, h * HEAD_DIM:(h + 1) * HEAD_DIM] = _dot(p.astype(BF16), vc)
            psum = psum + p
        p_hi = psum.astype(BF16)
        p_lo = (psum - p_hi.astype(F32)).astype(BF16)
        imp = _dot(p_hi, ov) + _dot(p_lo, ov)
        imp = jnp.where(forced, NSA_FORCE, imp)
        imp = jnp.where(blk_ok, imp, NEG)
        sel = jnp.zeros((tq, n_slc), F32)
        for _ in range(topn):
            mx = jnp.max(imp, axis=1, keepdims=True)
            first = jnp.min(jnp.where(imp == mx, blk, n_slc), axis=1, keepdims=True)
            hit = blk == first
            sel = jnp.where(hit, 1.0, sel)
            imp = jnp.where(hit, -jnp.inf, imp)
        tok = _dot(sel.astype(BF16), ex_ref[...])
        mask_ref[g] = tok.astype(jnp.int32).astype(jnp.int8)


def _nsa_cmp(proj, kv_cmp, *, tq=128):
    s_len = proj.shape[0]
    ncp = kv_cmp.shape[1]
    n_slc = s_len // NSA_SLC_LEN
    topn = min(NSA_TOPN, n_slc)
    starts = np.arange(ncp) * NSA_CMP_STRIDE
    slc_start = np.arange(n_slc) * NSA_SLC_LEN
    overlap = ((starts[:, None] < slc_start[None, :] + NSA_SLC_LEN)
               & (starts[:, None] + NSA_CMP_LEN > slc_start[None, :])).astype(np.float32)
    expand = (np.arange(s_len)[None, :] // NSA_SLC_LEN == np.arange(n_slc)[:, None]).astype(np.float32)
    return pl.pallas_call(
        functools.partial(_nsa_cmp_kernel, tq=tq, ncp=ncp, n_slc=n_slc, topn=topn),
        grid=(s_len // tq,),
        in_specs=[pl.BlockSpec((tq, 512), lambda i: (i, BLK512["d_q"])),
                  pl.BlockSpec((4, ncp, HEAD_DIM), lambda i: (0, 0, 0)),
                  pl.BlockSpec((ncp, n_slc), lambda i: (0, 0)),
                  pl.BlockSpec((n_slc, s_len), lambda i: (0, 0))],
        out_specs=[pl.BlockSpec((tq, 512), lambda i: (i, 0)),
                   pl.BlockSpec((NSA_GROUPS, tq, s_len), lambda i: (0, i, 0))],
        out_shape=[jax.ShapeDtypeStruct((s_len, 512), F32),
                   jax.ShapeDtypeStruct((NSA_GROUPS, s_len, s_len), jnp.int8)],
        compiler_params=_cparams(("arbitrary",)),
        name="nsa_cmp",
    )(proj, kv_cmp, jnp.asarray(overlap, BF16), jnp.asarray(expand, BF16))


def _nsa_combine_kernel(g_ref, oc_ref, os_ref, ow_ref, o_ref):
    gt = _sigmoid(g_ref[...])
    for h in range(NSA_HEADS):
        sl = slice(h * HEAD_DIM, (h + 1) * HEAD_DIM)
        c0 = IDX_HEADS + 3 * h
        o_ref[:, sl] = (gt[:, c0:c0 + 1] * oc_ref[:, sl] + gt[:, c0 + 1:c0 + 2] * os_ref[:, sl]
                        + gt[:, c0 + 2:c0 + 3] * ow_ref[:, sl])


def _nsa_combine(proj, o_cmp, o_slc, o_win, *, tm):
    m = proj.shape[0]
    spec = pl.BlockSpec((tm, 512), lambda i: (i, 0))
    return pl.pallas_call(
        _nsa_combine_kernel,
        grid=(m // tm,),
        in_specs=[pl.BlockSpec((tm, LANES), lambda i: (i, BLK128["small"])), spec, spec, spec],
        out_specs=spec,
        out_shape=jax.ShapeDtypeStruct((m, 512), F32),
        compiler_params=_cparams(("arbitrary",)),
        name="nsa_combine",
    )(proj, o_cmp, o_slc, o_win)


def _merge_kernel(u_ref, oa_ref, ob_ref, oc_ref, od_ref, wg0, wg1, wg2, wg3, wb_ref, o_ref,
                  wgb_ref, wbb_ref):
    wgs = (wg0, wg1, wg2, wg3)

    @pl.when(pl.program_id(1) == 0)
    def _():
        for mch in range(N_BRANCH):
            wgb_ref[mch] = wgs[mch][...].astype(BF16)
            wbb_ref[mch] = wb_ref[mch].astype(BF16)

    u = u_ref[...]
    acc = None
    for mch, o_ref_m in enumerate((oa_ref, ob_ref, oc_ref, od_ref)):
        gte = _sigmoid(_dot(u, wgb_ref[mch]))
        z = _dot(o_ref_m[...].astype(BF16), wbb_ref[mch])
        acc = gte * z if acc is None else acc + gte * z
    o_ref[...] = acc.astype(o_ref.dtype)


def _merge(u, branches, w_gate, w_branch, *, tm, tn):
    m, d = u.shape
    nj = d // tn
    bspec = pl.BlockSpec((tm, BRANCH_W), lambda j, i: (i, 0))
    wg_specs = [pl.BlockSpec((d, tn), functools.partial(lambda j, i, mch: (0, mch * nj + j), mch=mch))
                for mch in range(N_BRANCH)]
    return pl.pallas_call(
        _merge_kernel,
        grid=(nj, m // tm),
        in_specs=[pl.BlockSpec((tm, d), lambda j, i: (i, 0)), bspec, bspec, bspec, bspec,
                  *wg_specs, pl.BlockSpec((N_BRANCH, BRANCH_W, tn), lambda j, i: (0, 0, j))],
        out_specs=pl.BlockSpec((tm, tn), lambda j, i: (i, j)),
        out_shape=jax.ShapeDtypeStruct((m, d), BF16),
        scratch_shapes=[pltpu.VMEM((N_BRANCH, d, tn), BF16), pltpu.VMEM((N_BRANCH, BRANCH_W, tn), BF16)],
        compiler_params=_cparams(("arbitrary", "arbitrary")),
        name="merge",
    )(u, *branches, w_gate, w_gate, w_gate, w_gate, w_branch)


def _router_kernel(u_ref, r_ref, o_ref):
    logits = _dot(u_ref[...], r_ref[...].astype(BF16))
    lane = lax.broadcasted_iota(jnp.int32, logits.shape, 1)
    lg = jnp.where(lane < N_EXPERTS, logits, -jnp.inf)
    m1 = jnp.max(lg, axis=1, keepdims=True)
    i1 = jnp.min(jnp.where(lg == m1, lane, LANES), axis=1, keepdims=True)
    lg2 = jnp.where(lane == i1, -jnp.inf, lg)
    m2 = jnp.max(lg2, axis=1, keepdims=True)
    i2 = jnp.min(jnp.where(lg2 == m2, lane, LANES), axis=1, keepdims=True)
    e2 = jnp.exp(m2 - m1)
    w1 = 1.0 / (1.0 + e2)
    w2 = e2 / (1.0 + e2)
    o_ref[...] = jnp.where(lane == i1, w1, 0.0) + jnp.where(lane == i2, w2, 0.0)


def _router(u, router, *, tm):
    m, d = u.shape
    rp = jnp.pad(router, ((0, 0), (0, LANES - N_EXPERTS)))
    return pl.pallas_call(
        _router_kernel,
        grid=(m // tm,),
        in_specs=[pl.BlockSpec((tm, d), lambda i: (i, 0)), pl.BlockSpec((d, LANES), lambda i: (0, 0))],
        out_specs=pl.BlockSpec((tm, LANES), lambda i: (i, 0)),
        out_shape=jax.ShapeDtypeStruct((m, LANES), F32),
        compiler_params=_cparams(("arbitrary",)),
        name="router",
    )(u, rp)


def _permute_w_in(w):
    d = w.shape[0]
    cols = []
    for nm in _NEW_ORDER:
        if nm.startswith("pad"):
            cols.append(jnp.zeros((d, int(nm[3:])), w.dtype))
        else:
            o, n = _ORIG[nm]
            cols.append(w[:, o:o + n])
    out = jnp.concatenate(cols, axis=1)
    assert out.shape[1] == PROJ_W
    return out


def _nsa_cmp_inputs(proj):
    s_len = proj.shape[0]
    n_cmp = (s_len - NSA_CMP_LEN) // NSA_CMP_STRIDE + 1
    ncp = s_len // NSA_CMP_STRIDE
    xs = []
    for jj in range(2):
        for g in range(NSA_GROUPS):
            c0 = COL_DKV + jj * 128 + g * HEAD_DIM
            r = proj[:, c0:c0 + HEAD_DIM].reshape(ncp, NSA_CMP_STRIDE * HEAD_DIM)
            x = jnp.concatenate([r[:-1], r[1:]], axis=1)
            xs.append(jnp.pad(x, ((0, ncp - n_cmp), (0, 0))))
    return jnp.stack(xs)


def _token_mixers(u, layer, w_in, diff_lambda, diff_subln_g, dsa_kv_norm_g, dsa_w_uk, dsa_w_uv,
                  swa_sinks, nsa_cmp_pos, nsa_cmp_w1, nsa_cmp_w2, w_branch, w_gate, w_o, cfg):
    s_len = u.shape[0]
    tm = cfg["tm"]
    proj = _mm(u, _permute_w_in(w_in), tm=tm, tn=512, name="in_proj")

    lambda_init = 0.8 - 0.6 * math.exp(-0.3 * layer)
    sl_a = _alibi(DA_HEADS)
    heads_a = [(h * 128 + mp * 64, h * 128 + mp * 64, h * 128, 128, sl_a[h], 0, (2 * h + mp) * 128)
               for h in range(DA_HEADS) for mp in range(2)]
    o_a2 = _flash(proj, proj, proj, heads=heads_a, q_spec=(512, BLK512["a_q"]),
                  k_spec=(512, BLK512["a_k"]), v_spec=(512, BLK512["a_v"]), out_w=1024,
                  tq=cfg["tq"], tk=cfg["tk"], name="diff_attn")
    o_a = _diff_final(o_a2, diff_lambda, diff_subln_g, lambda_init, tm=tm)

    kv_b = _mm(proj, jnp.concatenate([dsa_w_uk, dsa_w_uv], axis=1), tm=tm, tn=512,
               a_blk=BLK128["b_kv"], k=DSA_KV_RANK, prologue="rms", gain=dsa_kv_norm_g, name="dsa_kv")
    topk = min(DSA_TOPK_MAX, s_len // 4)
    mask_b = _dsa_select(proj, topk=topk)
    sl8 = _alibi(8)
    heads_b = [(h * 64, h * 64, h * 64, 64, sl8[h], 0, h * 64) for h in range(DSA_HEADS)]
    o_b = _flash(proj, kv_b, kv_b, heads=heads_b, q_spec=(512, BLK512["b_q"]), k_spec=(512, 0),
                 v_spec=(512, 1), out_w=512, tq=cfg["tq"], tk=cfg["tk"],
                 mask=mask_b.reshape(1, s_len, s_len), name="dsa_attn")

    heads_g = [(h * 64, (h // 4) * 64, (h // 4) * 64, 64, sl8[h], h // 4, h * 64) for h in range(8)]
    heads_c = [(a, b, c, d, e, 0, f) for (a, b, c, d, e, _, f) in heads_g]
    sinks = jnp.pad(swa_sinks.reshape(1, SWA_HEADS), ((0, 0), (0, LANES - SWA_HEADS)))
    o_c = _flash(proj, proj, proj, heads=heads_c, q_spec=(512, BLK512["c_q"]),
                 k_spec=(128, BLK128["c_k"]), v_spec=(128, BLK128["c_v"]), out_w=512,
                 tq=cfg["tb"], tk=cfg["tb"], window=SWA_WINDOW, sinks=sinks, name="swa_attn")

    kv_cmp = _nsa_compress(_nsa_cmp_inputs(proj),
                           nsa_cmp_pos.reshape(2, 1, NSA_CMP_LEN * HEAD_DIM), nsa_cmp_w1, nsa_cmp_w2)
    o_cmp, mask_d = _nsa_cmp(proj, kv_cmp)
    o_slc = _flash(proj, proj, proj, heads=heads_g, q_spec=(512, BLK512["d_q"]),
                   k_spec=(128, BLK128["d_ks"]), v_spec=(128, BLK128["d_vs"]), out_w=512,
                   tq=cfg["tq"], tk=cfg["tk"], mask=mask_d, name="nsa_slc_attn")
    o_win = _flash(proj, proj, proj, heads=heads_c, q_spec=(512, BLK512["d_q"]),
                   k_spec=(128, BLK128["d_kw"]), v_spec=(128, BLK128["d_vw"]), out_w=512,
                   tq=cfg["tb"], tk=cfg["tb"], window=NSA_WINDOW, name="nsa_win_attn")
    o_d = _nsa_combine(proj, o_cmp, o_slc, o_win, tm=tm)

    merged = _merge(u, (o_a, o_b, o_c, o_d), w_gate, w_branch, tm=cfg["tm_merge"], tn=256)
    return _mm(merged, w_o, tm=tm, tn=512, name="out_proj")


def _config(s_len):
    return dict(tm=min(1024, s_len), tm_merge=min(512, s_len), tm_ln=min(512, s_len),
                tq=min(256, s_len), tk=min(512, s_len), tb=min(256, s_len))


def kernel(x, c, cond_w, cond_b, w_in, diff_lambda, diff_subln_g, dsa_kv_norm_g, dsa_w_uk, dsa_w_uv,
           swa_sinks, nsa_cmp_pos, nsa_cmp_w1, nsa_cmp_w2, w_branch, w_gate, w_o,
           ln1_g, ln1_b, ln2_g, ln2_b, ffn_w_gate, ffn_w_up, ffn_w_down,
           moe_router, moe_w_gate, moe_w_up, moe_w_down):
    bsz, s_len, d = x.shape
    assert bsz == 1 and d == D_MODEL
    depth = cond_w.shape[0]
    cfg = _config(s_len)
    xs = x.reshape(s_len, d)
    c8 = jnp.broadcast_to(c.reshape(1, d), (8, d))
    mods = [_mm(c8, cond_w[l], tm=8, tn=512, prologue="silu", bias=cond_b[l], name="cond")[0:1]
            for l in range(depth)]
    u = _modulate(xs, mods[0], 1, 0, tm=cfg["tm_ln"])
    for l in range(depth):
        y = _token_mixers(u, l, w_in[l], diff_lambda[l], diff_subln_g[l], dsa_kv_norm_g[l],
                          dsa_w_uk[l], dsa_w_uv[l], swa_sinks[l], nsa_cmp_pos[l], nsa_cmp_w1[l],
                          nsa_cmp_w2[l], w_branch[l], w_gate[l], w_o[l], cfg)
        xs, u = _resid_ln(xs, y, mods[l], 2, ln1_g[l], ln1_b[l], mods[l], 4, 3, tm=cfg["tm_ln"])
        jx = l // 2
        if l % 2 == 0:
            hdn = _swiglu_up(u, ffn_w_gate[jx][None], ffn_w_up[jx][None], tm=cfg["tm"], tn=512,
                             name="ffn_up")
            y = _mmk(hdn, ffn_w_down[jx], tm=cfg["tm"], tn=d, tk=512, name="ffn_down")
        else:
            we = _router(u, moe_router[jx], tm=cfg["tm"])
            scale = jnp.transpose(we[:, :N_EXPERTS])[:, :, None]
            hdn = _swiglu_up(u, moe_w_gate[jx], moe_w_up[jx], scale, tm=cfg["tm"], tn=512,
                             name="moe_up")
            y = _mmk(hdn, moe_w_down[jx].reshape(-1, d), tm=cfg["tm"], tn=d, tk=512, name="moe_down")
        nxt = min(l + 1, depth - 1)
        xs, u = _resid_ln(xs, y, mods[l], 5, ln2_g[l], ln2_b[l], mods[nxt], 1, 0, tm=cfg["tm_ln"])
    return xs.reshape(bsz, s_len, d)
```

```python
import functools
import math

import numpy as np
import jax
import jax.numpy as jnp
from jax import lax
from jax.experimental import pallas as pl
from jax.experimental.pallas import tpu as pltpu

F32 = jnp.float32
BF16 = jnp.bfloat16
NEG = -1e30

D_MODEL = 2048
DEPTH = 4
HEAD_DIM = 64
DA_HEADS = 4
DSA_HEADS = 8
DSA_KV_RANK = 128
IDX_HEADS = 8
DSA_TOPK_MAX = 256
SWA_HEADS = 8
SWA_WINDOW = 128
NSA_HEADS = 8
NSA_GROUPS = 2
NSA_CMP_LEN = 32
NSA_CMP_STRIDE = 16
NSA_CMP_HID = 256
NSA_SLC_LEN = 64
NSA_TOPN = 16
NSA_WINDOW = 512
NSA_FORCE = 1e9
N_BRANCH = 4
BRANCH_W = 512
N_EXPERTS = 8
ALPHA = (2.0 * DEPTH) ** 0.25

VMEM_LIMIT_BYTES = 56 * 1024 * 1024
LANES = 128

_ORIG = dict(a_q=(0, 512), a_k=(512, 512), a_v=(1024, 512), b_q=(1536, 512), b_kv=(2048, 128),
             b_iq=(2176, 512), b_ik=(2688, 64), b_iw=(2752, 8), c_q=(2760, 512), c_k=(3272, 128),
             c_v=(3400, 128), d_q=(3528, 512), d_kv=(4040, 768), d_g=(4808, 24))
_NEW_ORDER = ("a_q", "a_k", "a_v", "b_q", "b_iq", "c_q", "d_q", "b_kv", "c_k", "c_v", "d_kv",
              "b_ik", "pad64", "b_iw", "d_g", "pad96", "pad128")
PROJ_W = 5120
BLK512 = dict(a_q=0, a_k=1, a_v=2, b_q=3, b_iq=4, c_q=5, d_q=6)
BLK128 = dict(b_kv=28, c_k=29, c_v=30, d_kc=31, d_vc=32, d_ks=33, d_vs=34, d_kw=35, d_vw=36,
              b_ik=37, small=38)
COL_DKV = 3968


def _cparams(sem):
    return pltpu.CompilerParams(dimension_semantics=sem, vmem_limit_bytes=VMEM_LIMIT_BYTES)


def _sigmoid(x):
    return 1.0 / (1.0 + jnp.exp(-x))


def _silu(x):
    return x * _sigmoid(x)


def _alibi(n_heads):
    return [2.0 ** (-8.0 * (h + 1) / n_heads) for h in range(n_heads)]


def _dot(a, b):
    return jnp.dot(a, b, preferred_element_type=F32)


def _dot_nt(a, b):
    return lax.dot_general(a, b, (((1,), (1,)), ((), ())), preferred_element_type=F32)


def _mm_kernel(*refs, prologue, has_bias, eps):
    it = iter(refs)
    a_ref = next(it)
    g_ref = next(it) if prologue == "rms" else None
    w_ref = next(it)
    b_ref = next(it) if has_bias else None
    o_ref = next(it)
    wb_ref = next(it)

    @pl.when(pl.program_id(1) == 0)
    def _():
        wb_ref[...] = w_ref[0].astype(BF16)

    a = a_ref[...]
    if prologue == "silu":
        a = _silu(a.astype(F32))
    elif prologue == "rms":
        a = a.astype(F32)
        a = a * lax.rsqrt(jnp.mean(a * a, axis=-1, keepdims=True) + eps) * g_ref[...]
    acc = _dot(a.astype(BF16), wb_ref[...])
    if has_bias:
        acc = acc + b_ref[...]
    o_ref[...] = acc.astype(o_ref.dtype)


def _mm(a, w, layer, *, tm, tn, out_dtype=F32, a_blk=0, k=None, prologue=None, gain=None, bias=None,
        eps=1e-6, name="mm"):
    m = a.shape[0]
    k = a.shape[1] if k is None else k
    n = w.shape[2]
    assert w.shape[1] == k and m % tm == 0 and n % tn == 0
    in_specs = [pl.BlockSpec((tm, k), lambda j, i: (i, a_blk))]
    args = [a]
    if prologue == "rms":
        in_specs.append(pl.BlockSpec((1, k), lambda j, i: (0, 0)))
        args.append(gain.reshape(1, k))
    in_specs.append(pl.BlockSpec((1, k, tn), lambda j, i: (layer, 0, j)))
    args.append(w)
    if bias is not None:
        in_specs.append(pl.BlockSpec((1, tn), lambda j, i: (0, j)))
        args.append(bias.reshape(1, n))
    return pl.pallas_call(
        functools.partial(_mm_kernel, prologue=prologue, has_bias=bias is not None, eps=eps),
        grid=(n // tn, m // tm),
        in_specs=in_specs,
        out_specs=pl.BlockSpec((tm, tn), lambda j, i: (i, j)),
        out_shape=jax.ShapeDtypeStruct((m, n), out_dtype),
        scratch_shapes=[pltpu.VMEM((k, tn), BF16)],
        compiler_params=_cparams(("arbitrary", "arbitrary")),
        name=name,
    )(*args)


def _mmk_kernel(a_ref, w_ref, o_ref, acc_ref, *, nk):
    kk = pl.program_id(2)

    @pl.when(kk == 0)
    def _():
        acc_ref[...] = jnp.zeros_like(acc_ref)

    acc_ref[...] += _dot(a_ref[...], w_ref[0].astype(BF16))

    @pl.when(kk == nk - 1)
    def _():
        o_ref[...] = acc_ref[...]


def _mmk(a, w, layer, *, tm, tn, tk, name="mmk"):
    m, k = a.shape
    n = w.shape[2]
    assert w.shape[1] == k and m % tm == 0 and n % tn == 0 and k % tk == 0
    nk = k // tk
    return pl.pallas_call(
        functools.partial(_mmk_kernel, nk=nk),
        grid=(m // tm, n // tn, nk),
        in_specs=[pl.BlockSpec((tm, tk), lambda i, j, kk: (i, kk)),
                  pl.BlockSpec((1, tk, tn), lambda i, j, kk: (layer, kk, j))],
        out_specs=pl.BlockSpec((tm, tn), lambda i, j, kk: (i, j)),
        out_shape=jax.ShapeDtypeStruct((m, n), F32),
        scratch_shapes=[pltpu.VMEM((tm, tn), F32)],
        compiler_params=_cparams(("arbitrary", "arbitrary", "arbitrary")),
        name=name,
    )(a, w)


def _swiglu_kernel(*refs, has_scale):
    it = iter(refs)
    a_ref = next(it)
    wg_ref = next(it)
    wu_ref = next(it)
    s_ref = next(it) if has_scale else None
    o_ref = next(it)
    wgb_ref = next(it)
    wub_ref = next(it)

    @pl.when(pl.program_id(2) == 0)
    def _():
        wgb_ref[...] = wg_ref[0, 0].astype(BF16)
        wub_ref[...] = wu_ref[0, 0].astype(BF16)

    a = a_ref[...]
    h = _silu(_dot(a, wgb_ref[...])) * _dot(a, wub_ref[...])
    if has_scale:
        we = s_ref[...]
        lane = lax.broadcasted_iota(jnp.int32, we.shape, 1)
        h = h * jnp.sum(jnp.where(lane == pl.program_id(0), we, 0.0), axis=1, keepdims=True)
    o_ref[...] = h.astype(o_ref.dtype)


def _swiglu_up(u, wg, wu, layer, scale=None, *, tm, tn, name="swiglu_up"):
    m, k = u.shape
    _, e, _, f = wg.shape
    assert f % tn == 0 and m % tm == 0
    nf = f // tn
    wspec = pl.BlockSpec((1, 1, k, tn), lambda ee, j, i: (layer, ee, 0, j))
    in_specs = [pl.BlockSpec((tm, k), lambda ee, j, i: (i, 0)), wspec, wspec]
    args = [u, wg, wu]
    if scale is not None:
        in_specs.append(pl.BlockSpec((tm, LANES), lambda ee, j, i: (i, 0)))
        args.append(scale)
    return pl.pallas_call(
        functools.partial(_swiglu_kernel, has_scale=scale is not None),
        grid=(e, nf, m // tm),
        in_specs=in_specs,
        out_specs=pl.BlockSpec((tm, tn), lambda ee, j, i: (i, ee * nf + j)),
        out_shape=jax.ShapeDtypeStruct((m, e * f), BF16),
        scratch_shapes=[pltpu.VMEM((k, tn), BF16), pltpu.VMEM((k, tn), BF16)],
        compiler_params=_cparams(("arbitrary", "arbitrary", "arbitrary")),
        name=name,
    )(*args)


def _modulate_kernel(x_ref, sc_ref, sh_ref, u_ref):
    u_ref[...] = (x_ref[...] * (1.0 + sc_ref[...]) + sh_ref[...]).astype(u_ref.dtype)


def _modulate(x, mod, sc_blk, sh_blk, *, tm):
    m, d = x.shape
    return pl.pallas_call(
        _modulate_kernel,
        grid=(m // tm,),
        in_specs=[pl.BlockSpec((tm, d), lambda i: (i, 0)),
                  pl.BlockSpec((1, d), lambda i: (0, sc_blk)),
                  pl.BlockSpec((1, d), lambda i: (0, sh_blk))],
        out_specs=pl.BlockSpec((tm, d), lambda i: (i, 0)),
        out_shape=jax.ShapeDtypeStruct((m, d), BF16),
        compiler_params=_cparams(("arbitrary",)),
        name="modulate",
    )(x, mod, mod)


def _resid_ln_kernel(x_ref, y_ref, gate_ref, g_ref, b_ref, sc_ref, sh_ref, xo_ref, u_ref):
    z = ALPHA * x_ref[...] + gate_ref[...] * y_ref[...]
    mu = jnp.mean(z, axis=-1, keepdims=True)
    zc = z - mu
    var = jnp.mean(zc * zc, axis=-1, keepdims=True)
    xn = zc * lax.rsqrt(var + 1e-5) * g_ref[...] + b_ref[...]
    xo_ref[...] = xn
    u_ref[...] = (xn * (1.0 + sc_ref[...]) + sh_ref[...]).astype(u_ref.dtype)


def _resid_ln(x, y, mod, gate_blk, g, b, mod_next, sc_blk, sh_blk, *, tm):
    m, d = x.shape
    row = lambda blk: pl.BlockSpec((1, d), lambda i: (0, blk))
    return pl.pallas_call(
        _resid_ln_kernel,
        grid=(m // tm,),
        in_specs=[pl.BlockSpec((tm, d), lambda i: (i, 0)),
                  pl.BlockSpec((tm, d), lambda i: (i, 0)),
                  row(gate_blk), row(0), row(0), row(sc_blk), row(sh_blk)],
        out_specs=[pl.BlockSpec((tm, d), lambda i: (i, 0)),
                   pl.BlockSpec((tm, d), lambda i: (i, 0))],
        out_shape=[jax.ShapeDtypeStruct((m, d), F32), jax.ShapeDtypeStruct((m, d), BF16)],
        compiler_params=_cparams(("arbitrary",)),
        name="resid_ln",
    )(x, y, mod, g.reshape(1, d), b.reshape(1, d), mod_next, mod_next)


FLASH_ROW_CHUNK = 32
POS_SPLIT = 128


def _head_pad(x, n_heads, kind):
    s_len = x.shape[0]
    w = x.shape[1] // n_heads
    xh = x.reshape(s_len, n_heads, w)
    pos = jnp.arange(s_len, dtype=jnp.int32)
    one = jnp.ones_like(pos)
    cols = [pos // POS_SPLIT, pos % POS_SPLIT, one, one] if kind == "k" else [one]
    tail = jnp.pad(jnp.stack(cols, axis=1).astype(F32), ((0, 0), (0, w - len(cols))))
    tail = jnp.broadcast_to(tail[:, None, :], xh.shape)
    return jnp.concatenate([xh, tail], axis=2).reshape(s_len, n_heads * 2 * w).astype(BF16)


def _flash_kernel(*refs, units, tq, tk, window, dense, n_prev, nsteps, n_mask, has_sink):
    it = iter(refs)
    q_ref, k_ref, v_ref = next(it), next(it), next(it)
    mask_ref = next(it) if n_mask else None
    sink_ref = next(it) if has_sink else None
    o_ref = next(it)
    q_scr, m_scr, acc_scr, bias_scr = (next(it) for _ in range(4))
    dv = acc_scr.shape[2]
    lcol = dv // 2

    qi = pl.program_id(0)
    j = pl.program_id(1)
    last_kb = (qi * tq + tq - 1) // tk
    kb = j if dense else qi - n_prev + j
    rows = q_scr.shape[1]
    rb = FLASH_ROW_CHUNK
    kw = 2 * HEAD_DIM

    @pl.when(j == 0)
    def _init():
        lane = lax.broadcasted_iota(jnp.int32, (tq, HEAD_DIM), 1)
        qpos = qi * tq + lax.broadcasted_iota(jnp.int32, (tq, HEAD_DIM), 0)
        qhi = (qpos // POS_SPLIT).astype(F32)
        qlo = (qpos % POS_SPLIT).astype(F32)
        for ui, (_, _, _, _, hds) in enumerate(units):
            for r, (qo, slope, _, sink_idx) in enumerate(hds):
                rsl = slice(r * tq, (r + 1) * tq)
                tail = jnp.where(lane == 0, POS_SPLIT * slope,
                                 jnp.where(lane == 1, slope,
                                           jnp.where(lane == 2, -POS_SPLIT * slope * qhi,
                                                     jnp.where(lane == 3, -slope * qlo, 0.0))))
                qs = q_ref[:, qo:qo + HEAD_DIM] * HEAD_DIM ** -0.5
                q_scr[ui, rsl] = jnp.concatenate([qs, tail], axis=1).astype(BF16)
                if has_sink:
                    m_scr[ui, rsl] = jnp.broadcast_to(sink_ref[:, sink_idx:sink_idx + 1], (tq, 1))
                else:
                    m_scr[ui, rsl] = jnp.full((tq, 1), NEG, F32)
            alane = lax.broadcasted_iota(jnp.int32, acc_scr.shape[1:], 1)
            acc_scr[ui] = jnp.where(alane == lcol, 1.0 if has_sink else 0.0, 0.0)

    def scores(ui):
        ku = units[ui][0]
        return _dot_nt(q_scr[ui], k_ref[:, ku * kw:(ku + 1) * kw])

    def step(masked):
        if masked:
            qpos = qi * tq + lax.broadcasted_iota(jnp.int32, (tq, tk), 0)
            kpos = kb * tk + lax.broadcasted_iota(jnp.int32, (tq, tk), 1)
            dist = qpos - kpos
            valid = dist >= 0
            if not dense:
                valid = valid & (dist < window)
            if n_mask:
                for g in range(n_mask):
                    bias_scr[g] = jnp.where(valid, mask_ref[g].astype(F32), NEG)
            else:
                bias_scr[0] = jnp.where(valid, 0.0, NEG)

        def chunk(s, mg, c):
            r0 = c * rb
            sc = s[r0:r0 + rb]
            if masked:
                rw = r0 % tq
                sc = sc + bias_scr[mg, rw:rw + rb]
            return sc

        s_next = scores(0)
        for ui, (_, vo, _, mg, _) in enumerate(units):
            s = s_next
            if ui + 1 < len(units):
                s_next = scores(ui + 1)
            nchunk = rows // rb
            m_old = m_scr[ui]
            m_cur = jnp.concatenate([jnp.max(chunk(s, mg, c), axis=1, keepdims=True) for c in range(nchunk)],
                                    axis=0)
            m_new = jnp.maximum(m_old, m_cur)
            alpha = jnp.exp(m_old - m_new)
            m_scr[ui] = m_new
            p_all = jnp.concatenate(
                [jnp.exp(chunk(s, mg, c) - m_new[c * rb:(c + 1) * rb]).astype(BF16) for c in range(nchunk)],
                axis=0)
            acc_scr[ui] = alpha * acc_scr[ui] + _dot(p_all, v_ref[:, vo:vo + dv])

    needed = (kb >= 0) & (kb <= last_kb)
    if dense and not n_mask:
        interior = kb * tk + tk - 1 <= qi * tq
        pl.when(needed & interior)(lambda: step(False))
        pl.when(needed & jnp.logical_not(interior))(lambda: step(True))
    else:
        pl.when(needed)(lambda: step(True))

    @pl.when(j == nsteps - 1)
    def _fin():
        for ui, (_, _, _, _, hds) in enumerate(units):
            for r, (_, _, (oo, ow), _) in enumerate(hds):
                rsl = slice(r * tq, (r + 1) * tq)
                acc = acc_scr[ui, rsl]
                o_ref[:, oo:oo + ow] = acc[:, 0:ow] / acc[:, lcol:lcol + 1]


def _flash(q_arr, k_arr, v_arr, *, units, q_spec, out_w, tq, tk, window=None,
           mask=None, sinks=None, name="flash"):
    s_len = q_arr.shape[0]
    dense = window is None
    if dense:
        n_prev, nsteps = 0, s_len // tk
    else:
        assert tq == tk
        n_prev = -(-(window - 1) // tk)
        nsteps = n_prev + 1
    n_mask = 0 if mask is None else mask.shape[0]
    dv = units[0][2]
    nu = len(units)
    rows = len(units[0][4]) * tq
    assert all(len(un[4]) * tq == rows and un[2] == dv for un in units) and rows % FLASH_ROW_CHUNK == 0

    def kv_index(qi, j):
        last_kb = (qi * tq + tq - 1) // tk
        kb = j if dense else qi - n_prev + j
        return jnp.clip(kb, 0, last_kb)

    in_specs = [pl.BlockSpec((tq, q_spec[0]), lambda qi, j: (qi, q_spec[1])),
                pl.BlockSpec((tk, k_arr.shape[1]), lambda qi, j: (kv_index(qi, j), 0)),
                pl.BlockSpec((tk, v_arr.shape[1]), lambda qi, j: (kv_index(qi, j), 0))]
    args = [q_arr, k_arr, v_arr]
    if n_mask:
        in_specs.append(pl.BlockSpec((n_mask, tq, tk), lambda qi, j: (0, qi, kv_index(qi, j))))
        args.append(mask)
    if sinks is not None:
        in_specs.append(pl.BlockSpec((1, LANES), lambda qi, j: (0, 0)))
        args.append(sinks)
    return pl.pallas_call(
        functools.partial(_flash_kernel, units=tuple(units), tq=tq, tk=tk, window=window, dense=dense,
                          n_prev=n_prev, nsteps=nsteps, n_mask=n_mask, has_sink=sinks is not None),
        grid=(s_len // tq, nsteps),
        in_specs=in_specs,
        out_specs=pl.BlockSpec((tq, out_w), lambda qi, j: (qi, 0)),
        out_shape=jax.ShapeDtypeStruct((s_len, out_w), F32),
        scratch_shapes=[pltpu.VMEM((nu, rows, 2 * HEAD_DIM), BF16), pltpu.VMEM((nu, rows, 1), F32),
                        pltpu.VMEM((nu, rows, dv), F32), pltpu.VMEM((max(n_mask, 1), tq, tk), F32)],
        compiler_params=_cparams(("arbitrary", "arbitrary")),
        name=name,
    )(*args)


def _diff_final_kernel(o_ref, lam_ref, g_ref, out_ref, *, lambda_init):
    lf = lam_ref[0]
    lam = (jnp.exp(jnp.sum(lf[0:1] * lf[1:2])) - jnp.exp(jnp.sum(lf[2:3] * lf[3:4])) + lambda_init)
    w = 2 * HEAD_DIM
    for h in range(DA_HEADS):
        o = o_ref[:, (2 * h) * w:(2 * h + 1) * w] - lam * o_ref[:, (2 * h + 1) * w:(2 * h + 2) * w]
        o = o * lax.rsqrt(jnp.mean(o * o, axis=-1, keepdims=True) + 1e-6) * g_ref[...]
        out_ref[:, h * w:(h + 1) * w] = o * (1.0 - lambda_init)


def _diff_final(o, diff_lambda, layer, subln_g, lambda_init, *, tm):
    m = o.shape[0]
    w = 2 * HEAD_DIM
    return pl.pallas_call(
        functools.partial(_diff_final_kernel, lambda_init=lambda_init),
        grid=(m // tm,),
        in_specs=[pl.BlockSpec((tm, 2 * DA_HEADS * w), lambda i: (i, 0)),
                  pl.BlockSpec((1, 4, HEAD_DIM), lambda i: (layer, 0, 0)),
                  pl.BlockSpec((1, w), lambda i: (0, 0))],
        out_specs=pl.BlockSpec((tm, DA_HEADS * w), lambda i: (i, 0)),
        out_shape=jax.ShapeDtypeStruct((m, DA_HEADS * w), F32),
        compiler_params=_cparams(("arbitrary",)),
        name="diff_final",
    )(o, diff_lambda, subln_g.reshape(1, w))


INT_MIN = -(2 ** 31)


def _f32_key_const(x):
    b = int(np.array(x, np.float32).view(np.int32))
    return b ^ ((b >> 31) & 0x7FFFFFFF)


def _dsa_select_kernel(qi_ref, w_ref, kidx_ref, mask_ref, key_scr, j_scr, *, tq, ch, nch, topk, s_len):
    i = pl.program_id(0)
    q0 = i * tq
    n_need = (q0 + tq + ch - 1) // ch
    qpos = q0 + lax.broadcasted_iota(jnp.int32, (tq, 1), 0)
    lane = lax.broadcasted_iota(jnp.int32, (1, ch), 1)
    w = w_ref[:, 0:IDX_HEADS]

    def score_chunk(c, carry):
        kc = kidx_ref[pl.ds(pl.multiple_of(c * ch, ch), ch), 0:HEAD_DIM].astype(BF16)
        acc = jnp.zeros((tq, ch), F32)
        for h in range(IDX_HEADS):
            qh = qi_ref[:, h * HEAD_DIM:(h + 1) * HEAD_DIM].astype(BF16)
            acc = acc + w[:, h:h + 1] * jnp.maximum(_dot_nt(qh, kc), 0.0)
        acc = jnp.where(c * ch + lane <= qpos, acc, NEG) + 0.0
        bits = pltpu.bitcast(acc, jnp.int32)
        key_scr[c] = bits ^ ((bits >> 31) & 0x7FFFFFFF)
        return carry

    lax.fori_loop(0, n_need, score_chunk, 0)

    def count(pred):
        def body(c, acc):
            m = jnp.where(pred(key_scr[c], c), 1, 0)
            part = m[:, 0:LANES]
            for t in range(1, ch // LANES):
                part = part + m[:, t * LANES:(t + 1) * LANES]
            return acc + part
        acc = lax.fori_loop(0, n_need, body, jnp.zeros((tq, LANES), jnp.int32))
        return jnp.sum(acc, axis=1, keepdims=True)

    def bit_step(b, t):
        cand = t + jnp.left_shift(jnp.int32(1), 31 - b)
        cnt = count(lambda blk, c: blk >= cand)
        return jnp.where(cnt >= topk, cand, t)

    thr = lax.fori_loop(0, 32, bit_step, jnp.full((tq, 1), INT_MIN, jnp.int32))
    cnt_gt = count(lambda blk, c: blk > thr)
    cnt_ge = count(lambda blk, c: blk >= thr)
    need = topk - cnt_gt
    tie_rows = (cnt_ge > topk) & (thr > _f32_key_const(NEG))
    j_scr[...] = jnp.full((tq, 1), s_len, jnp.int32)

    @pl.when(jnp.max(jnp.where(tie_rows, 1, 0)) > 0)
    def _ties():
        nbits = int(math.log2(s_len))

        def idx_step(b, jv):
            cand = jv + jnp.left_shift(jnp.int32(1), nbits - 1 - b)
            cnt = count(lambda blk, c: (blk == thr) & (c * ch + lane < cand))
            return jnp.where(cnt < need, cand, jv)

        jv = lax.fori_loop(0, nbits, idx_step, jnp.zeros((tq, 1), jnp.int32))
        j_scr[...] = jnp.where(tie_rows, jv, s_len)

    jv = j_scr[...]
    for c in range(nch):
        @pl.when(c < n_need)
        def _w():
            key = key_scr[c]
            idx = c * ch + lane
            sel = ((key > thr) | ((key == thr) & (idx <= jv))) & (idx <= qpos)
            mask_ref[:, c * ch:(c + 1) * ch] = jnp.where(sel, 0.0, NEG).astype(mask_ref.dtype)

        @pl.when(c >= n_need)
        def _z():
            mask_ref[:, c * ch:(c + 1) * ch] = jnp.full((tq, ch), NEG, mask_ref.dtype)


def _dsa_select(proj, *, topk, tq=128):
    s_len = proj.shape[0]
    ch = min(1024, s_len)
    nch = s_len // ch
    return pl.pallas_call(
        functools.partial(_dsa_select_kernel, tq=tq, ch=ch, nch=nch, topk=topk, s_len=s_len),
        grid=(s_len // tq,),
        in_specs=[pl.BlockSpec((tq, 512), lambda i: (i, BLK512["b_iq"])),
                  pl.BlockSpec((tq, LANES), lambda i: (i, BLK128["small"])),
                  pl.BlockSpec((s_len, LANES), lambda i: (0, BLK128["b_ik"]))],
        out_specs=pl.BlockSpec((tq, s_len), lambda i: (i, 0)),
        out_shape=jax.ShapeDtypeStruct((s_len, s_len), BF16),
        scratch_shapes=[pltpu.VMEM((nch, tq, ch), jnp.int32), pltpu.VMEM((tq, 1), jnp.int32)],
        compiler_params=_cparams(("arbitrary",)),
        name="dsa_select",
    )(proj, proj, proj)


def _nsa_compress_kernel(x_ref, pos_ref, w1_ref, w2_ref, o_ref):
    x = (x_ref[0] + pos_ref[0, 0]).astype(BF16)
    hdn = _silu(_dot(x, w1_ref[0, 0].astype(BF16)))
    o_ref[0] = _dot(hdn.astype(BF16), w2_ref[0, 0].astype(BF16))


def _nsa_compress(xc, pos, w1, w2, layer):
    _, ncp, kdim = xc.shape
    return pl.pallas_call(
        _nsa_compress_kernel,
        grid=(4,),
        in_specs=[pl.BlockSpec((1, ncp, kdim), lambda i: (i, 0, 0)),
                  pl.BlockSpec((1, 1, 1, kdim), lambda i: (layer, i // 2, 0, 0)),
                  pl.BlockSpec((1, 1, kdim, NSA_CMP_HID), lambda i: (layer, i // 2, 0, 0)),
                  pl.BlockSpec((1, 1, NSA_CMP_HID, HEAD_DIM), lambda i: (layer, i // 2, 0, 0))],
        out_specs=pl.BlockSpec((1, ncp, HEAD_DIM), lambda i: (i, 0, 0)),
        out_shape=jax.ShapeDtypeStruct((4, ncp, HEAD_DIM), F32),
        compiler_params=_cparams(("arbitrary",)),
        name="nsa_compress",
    )(xc, pos, w1, w2)


def _nsa_cmp_kernel(q_ref, kv_ref, ov_ref, ex_ref, o_ref, mask_ref, *, tq, ncp, n_slc, topn, ch, nch):
    i = pl.program_id(0)
    q0 = i * tq
    n_need = (q0 + tq + ch - 1) // ch
    rpg = NSA_HEADS // NSA_GROUPS
    slopes = _alibi(NSA_HEADS)
    scale = HEAD_DIM ** -0.5
    qpos_c = q0 + lax.broadcasted_iota(jnp.int32, (tq, ncp), 0)
    cend = lax.broadcasted_iota(jnp.int32, (tq, ncp), 1) * NSA_CMP_STRIDE + (NSA_CMP_LEN - 1)
    dist_c = qpos_c - cend
    valid_c = dist_c >= 0
    distf = dist_c.astype(F32)
    qpos = q0 + lax.broadcasted_iota(jnp.int32, (tq, n_slc), 0)
    blk = lax.broadcasted_iota(jnp.int32, (tq, n_slc), 1)
    cur = qpos // NSA_SLC_LEN
    forced = (blk == 0) | (blk == cur) | (blk == cur - 1)
    blk_ok = blk * NSA_SLC_LEN <= qpos
    ov = ov_ref[...]
    imps = []
    for g in range(NSA_GROUPS):
        kc = kv_ref[g].astype(BF16)
        vc = kv_ref[NSA_GROUPS + g].astype(BF16)
        psum = jnp.zeros((tq, ncp), F32)
        for r in range(rpg):
            h = g * rpg + r
            qh = q_ref[:, h * HEAD_DIM:(h + 1) * HEAD_DIM].astype(BF16)
            s = _dot_nt(qh, kc) * scale - slopes[h] * distf
            s = jnp.where(valid_c, s, NEG)
            e = jnp.where(valid_c, jnp.exp(s - jnp.max(s, axis=1, keepdims=True)), 0.0)
            p = e / jnp.maximum(jnp.sum(e, axis=1, keepdims=True), 1e-30)
            o_ref[:, h * HEAD_DIM:(h + 1) * HEAD_DIM] = _dot(p.astype(BF16), vc)
            psum = psum + p
        p_hi = psum.astype(BF16)
        p_lo = (psum - p_hi.astype(F32)).astype(BF16)
        imp = _dot(p_hi, ov) + _dot(p_lo, ov)
        imp = jnp.where(forced, NSA_FORCE, imp)
        imps.append(jnp.where(blk_ok, imp, NEG))
    sels = [jnp.full((tq, n_slc), NEG, F32) for _ in range(NSA_GROUPS)]
    for _ in range(topn):
        for g in range(NSA_GROUPS):
            mx = jnp.max(imps[g], axis=1, keepdims=True)
            first = jnp.min(jnp.where(imps[g] == mx, blk, n_slc), axis=1, keepdims=True)
            hit = blk == first
            sels[g] = jnp.where(hit, 0.0, sels[g])
            imps[g] = jnp.where(hit, -jnp.inf, imps[g])
    for g in range(NSA_GROUPS):
        selb = sels[g].astype(BF16)
        for c in range(nch):
            @pl.when(c < n_need)
            def _w():
                tok = _dot(selb, ex_ref[:, c * ch:(c + 1) * ch])
                mask_ref[g, :, c * ch:(c + 1) * ch] = tok.astype(mask_ref.dtype)

            @pl.when(c >= n_need)
            def _z():
                mask_ref[g, :, c * ch:(c + 1) * ch] = jnp.full((tq, ch), NEG, mask_ref.dtype)


def _nsa_cmp(proj, kv_cmp, *, tq=128):
    s_len = proj.shape[0]
    ncp = kv_cmp.shape[1]
    n_slc = s_len // NSA_SLC_LEN
    topn = min(NSA_TOPN, n_slc)
    ch = min(1024, s_len)
    nch = s_len // ch
    starts = np.arange(ncp) * NSA_CMP_STRIDE
    slc_start = np.arange(n_slc) * NSA_SLC_LEN
    overlap = ((starts[:, None] < slc_start[None, :] + NSA_SLC_LEN)
               & (starts[:, None] + NSA_CMP_LEN > slc_start[None, :])).astype(np.float32)
    expand = (np.arange(s_len)[None, :] // NSA_SLC_LEN == np.arange(n_slc)[:, None]).astype(np.float32)
    return pl.pallas_call(
        functools.partial(_nsa_cmp_kernel, tq=tq, ncp=ncp, n_slc=n_slc, topn=topn, ch=ch, nch=nch),
        grid=(s_len // tq,),
        in_specs=[pl.BlockSpec((tq, 512), lambda i: (i, BLK512["d_q"])),
                  pl.BlockSpec((4, ncp, HEAD_DIM), lambda i: (0, 0, 0)),
                  pl.BlockSpec((ncp, n_slc), lambda i: (0, 0)),
                  pl.BlockSpec((n_slc, s_len), lambda i: (0, 0))],
        out_specs=[pl.BlockSpec((tq, 512), lambda i: (i, 0)),
                   pl.BlockSpec((NSA_GROUPS, tq, s_len), lambda i: (0, i, 0))],
        out_shape=[jax.ShapeDtypeStruct((s_len, 512), F32),
                   jax.ShapeDtypeStruct((NSA_GROUPS, s_len, s_len), BF16)],
        compiler_params=_cparams(("arbitrary",)),
        name="nsa_cmp",
    )(proj, kv_cmp, jnp.asarray(overlap, BF16), jnp.asarray(expand, BF16))


def _nsa_combine_kernel(g_ref, oc_ref, os_ref, ow_ref, o_ref):
    gt = _sigmoid(g_ref[...])
    for h in range(NSA_HEADS):
        sl = slice(h * HEAD_DIM, (h + 1) * HEAD_DIM)
        c0 = IDX_HEADS + 3 * h
        o_ref[:, sl] = (gt[:, c0:c0 + 1] * oc_ref[:, sl] + gt[:, c0 + 1:c0 + 2] * os_ref[:, sl]
                        + gt[:, c0 + 2:c0 + 3] * ow_ref[:, sl])


def _nsa_combine(proj, o_cmp, o_slc, o_win, *, tm):
    m = proj.shape[0]
    spec = pl.BlockSpec((tm, 512), lambda i: (i, 0))
    return pl.pallas_call(
        _nsa_combine_kernel,
        grid=(m // tm,),
        in_specs=[pl.BlockSpec((tm, LANES), lambda i: (i, BLK128["small"])), spec, spec, spec],
        out_specs=spec,
        out_shape=jax.ShapeDtypeStruct((m, 512), F32),
        compiler_params=_cparams(("arbitrary",)),
        name="nsa_combine",
    )(proj, o_cmp, o_slc, o_win)


def _merge_kernel(u_ref, oa_ref, ob_ref, oc_ref, od_ref, wg0, wg1, wg2, wg3, wb_ref, o_ref,
                  wgb_ref, wbb_ref):
    wgs = (wg0, wg1, wg2, wg3)

    @pl.when(pl.program_id(1) == 0)
    def _():
        for mch in range(N_BRANCH):
            wgb_ref[mch] = wgs[mch][0].astype(BF16)
            wbb_ref[mch] = wb_ref[0, mch].astype(BF16)

    u = u_ref[...]
    acc = None
    for mch, o_ref_m in enumerate((oa_ref, ob_ref, oc_ref, od_ref)):
        gte = _sigmoid(_dot(u, wgb_ref[mch]))
        z = _dot(o_ref_m[...].astype(BF16), wbb_ref[mch])
        acc = gte * z if acc is None else acc + gte * z
    o_ref[...] = acc.astype(o_ref.dtype)


def _merge(u, branches, w_gate, w_branch, layer, *, tm, tn):
    m, d = u.shape
    nj = d // tn
    bspec = pl.BlockSpec((tm, BRANCH_W), lambda j, i: (i, 0))
    wg_specs = [pl.BlockSpec((1, d, tn),
                             functools.partial(lambda j, i, mch: (layer, 0, mch * nj + j), mch=mch))
                for mch in range(N_BRANCH)]
    return pl.pallas_call(
        _merge_kernel,
        grid=(nj, m // tm),
        in_specs=[pl.BlockSpec((tm, d), lambda j, i: (i, 0)), bspec, bspec, bspec, bspec,
                  *wg_specs,
                  pl.BlockSpec((1, N_BRANCH, BRANCH_W, tn), lambda j, i: (layer, 0, 0, j))],
        out_specs=pl.BlockSpec((tm, tn), lambda j, i: (i, j)),
        out_shape=jax.ShapeDtypeStruct((m, d), BF16),
        scratch_shapes=[pltpu.VMEM((N_BRANCH, d, tn), BF16), pltpu.VMEM((N_BRANCH, BRANCH_W, tn), BF16)],
        compiler_params=_cparams(("arbitrary", "arbitrary")),
        name="merge",
    )(u, *branches, w_gate, w_gate, w_gate, w_gate, w_branch)


def _router_kernel(u_ref, r_ref, o_ref):
    logits = _dot(u_ref[...], r_ref[0].astype(BF16))
    lane = lax.broadcasted_iota(jnp.int32, logits.shape, 1)
    lg = jnp.where(lane < N_EXPERTS, logits, -jnp.inf)
    m1 = jnp.max(lg, axis=1, keepdims=True)
    i1 = jnp.min(jnp.where(lg == m1, lane, LANES), axis=1, keepdims=True)
    lg2 = jnp.where(lane == i1, -jnp.inf, lg)
    m2 = jnp.max(lg2, axis=1, keepdims=True)
    i2 = jnp.min(jnp.where(lg2 == m2, lane, LANES), axis=1, keepdims=True)
    e2 = jnp.exp(m2 - m1)
    w1 = 1.0 / (1.0 + e2)
    w2 = e2 / (1.0 + e2)
    o_ref[...] = jnp.where(lane == i1, w1, 0.0) + jnp.where(lane == i2, w2, 0.0)


def _router(u, router_padded, layer, *, tm):
    m, d = u.shape
    return pl.pallas_call(
        _router_kernel,
        grid=(m // tm,),
        in_specs=[pl.BlockSpec((tm, d), lambda i: (i, 0)),
                  pl.BlockSpec((1, d, LANES), lambda i: (layer, 0, 0))],
        out_specs=pl.BlockSpec((tm, LANES), lambda i: (i, 0)),
        out_shape=jax.ShapeDtypeStruct((m, LANES), F32),
        compiler_params=_cparams(("arbitrary",)),
        name="router",
    )(u, router_padded)


def _permute_w_in(w):
    cols = []
    for nm in _NEW_ORDER:
        if nm.startswith("pad"):
            cols.append(jnp.zeros(w.shape[:2] + (int(nm[3:]),), w.dtype))
        else:
            o, n = _ORIG[nm]
            cols.append(w[:, :, o:o + n])
    out = jnp.concatenate(cols, axis=2)
    assert out.shape[2] == PROJ_W
    return out


def _nsa_cmp_inputs(proj):
    s_len = proj.shape[0]
    n_cmp = (s_len - NSA_CMP_LEN) // NSA_CMP_STRIDE + 1
    ncp = s_len // NSA_CMP_STRIDE
    xs = []
    for jj in range(2):
        for g in range(NSA_GROUPS):
            c0 = COL_DKV + jj * 128 + g * HEAD_DIM
            r = proj[:, c0:c0 + HEAD_DIM].reshape(ncp, NSA_CMP_STRIDE * HEAD_DIM)
            x = jnp.concatenate([r[:-1], r[1:]], axis=1)
            xs.append(jnp.pad(x, ((0, ncp - n_cmp), (0, 0))))
    return jnp.stack(xs)


def _token_mixers(u, layer, p, cfg):
    s_len = u.shape[0]
    tm = cfg["tm"]
    proj = _mm(u, p["w_in"], layer, tm=tm, tn=512, name="in_proj")

    lambda_init = 0.8 - 0.6 * math.exp(-0.3 * layer)
    sl_a = _alibi(DA_HEADS)
    units_a = [(2 * h + mp, h * 256, 256, 0,
                ((h * 128 + mp * 64, sl_a[h], ((2 * h + mp) * 128, 128), 0),))
               for h in range(DA_HEADS) for mp in range(2)]
    col = lambda blk, w: proj[:, blk * w:(blk + 1) * w]
    o_a2 = _flash(proj, _head_pad(col(BLK512["a_k"], 512), 2 * DA_HEADS, "k"),
                  _head_pad(col(BLK512["a_v"], 512), DA_HEADS, "v"), units=units_a,
                  q_spec=(512, BLK512["a_q"]), out_w=1024, tq=cfg["tq"], tk=cfg["tk"], name="diff_attn")
    o_a = _diff_final(o_a2, p["diff_lambda"], layer, p["diff_subln_g"][layer], lambda_init, tm=tm)

    kv_b = _mm(proj, p["dsa_w_ukv"], layer, tm=tm, tn=512, a_blk=BLK128["b_kv"], k=DSA_KV_RANK,
               prologue="rms", gain=p["dsa_kv_norm_g"][layer], name="dsa_kv")
    topk = min(DSA_TOPK_MAX, s_len // 4)
    mask_b = _dsa_select(proj, topk=topk)
    sl8 = _alibi(8)
    units_b = [(h, h * 128, 128, 0, ((h * 64, sl8[h], (h * 64, 64), 0),)) for h in range(DSA_HEADS)]
    o_b = _flash(proj, _head_pad(kv_b[:, :512], DSA_HEADS, "k"), _head_pad(kv_b[:, 512:], DSA_HEADS, "v"),
                 units=units_b, q_spec=(512, BLK512["b_q"]), out_w=512, tq=cfg["tq"], tk=cfg["tk"],
                 mask=mask_b.reshape(1, s_len, s_len), name="dsa_attn")

    def gqa_units(masked):
        return [(g, g * 128, 128, g if masked else 0,
                 tuple(((g * 4 + r) * 64, sl8[g * 4 + r], ((g * 4 + r) * 64, 64), g * 4 + r)
                       for r in range(4)))
                for g in range(2)]

    def gqa_kv(k_name, v_name):
        return (_head_pad(col(BLK128[k_name], 128), 2, "k"), _head_pad(col(BLK128[v_name], 128), 2, "v"))

    sinks = jnp.pad(p["swa_sinks"][layer].reshape(1, SWA_HEADS), ((0, 0), (0, LANES - SWA_HEADS)))
    o_c = _flash(proj, *gqa_kv("c_k", "c_v"), units=gqa_units(False), q_spec=(512, BLK512["c_q"]),
                 out_w=512, tq=cfg["tb"], tk=cfg["tb"], window=SWA_WINDOW, sinks=sinks, name="swa_attn")

    kv_cmp = _nsa_compress(_nsa_cmp_inputs(proj), p["nsa_cmp_pos"], p["nsa_cmp_w1"], p["nsa_cmp_w2"],
                           layer)
    o_cmp, mask_d = _nsa_cmp(proj, kv_cmp)
    o_slc = _flash(proj, *gqa_kv("d_ks", "d_vs"), units=gqa_units(True), q_spec=(512, BLK512["d_q"]),
                   out_w=512, tq=cfg["tq"], tk=cfg["tk"], mask=mask_d, name="nsa_slc_attn")
    o_win = _flash(proj, *gqa_kv("d_kw", "d_vw"), units=gqa_units(False), q_spec=(512, BLK512["d_q"]),
                   out_w=512, tq=cfg["tb"], tk=cfg["tb"], window=NSA_WINDOW, name="nsa_win_attn")
    o_d = _nsa_combine(proj, o_cmp, o_slc, o_win, tm=tm)

    merged = _merge(u, (o_a, o_b, o_c, o_d), p["w_gate"], p["w_branch"], layer,
                    tm=cfg["tm_merge"], tn=256)
    return _mm(merged, p["w_o"], layer, tm=tm, tn=512, name="out_proj")


def _config(s_len):
    return dict(tm=min(1024, s_len), tm_merge=min(512, s_len), tm_ln=min(512, s_len),
                tq=min(256, s_len), tk=min(1024, s_len), tb=min(256, s_len))


def kernel(x, c, cond_w, cond_b, w_in, diff_lambda, diff_subln_g, dsa_kv_norm_g, dsa_w_uk, dsa_w_uv,
           swa_sinks, nsa_cmp_pos, nsa_cmp_w1, nsa_cmp_w2, w_branch, w_gate, w_o,
           ln1_g, ln1_b, ln2_g, ln2_b, ffn_w_gate, ffn_w_up, ffn_w_down,
           moe_router, moe_w_gate, moe_w_up, moe_w_down):
    bsz, s_len, d = x.shape
    assert bsz == 1 and d == D_MODEL
    depth = cond_w.shape[0]
    cfg = _config(s_len)
    xs = x.reshape(s_len, d)
    c8 = jnp.broadcast_to(c.reshape(1, d), (8, d))
    p = dict(w_in=_permute_w_in(w_in), diff_lambda=diff_lambda, diff_subln_g=diff_subln_g,
             dsa_kv_norm_g=dsa_kv_norm_g, dsa_w_ukv=jnp.concatenate([dsa_w_uk, dsa_w_uv], axis=2),
             swa_sinks=swa_sinks,
             nsa_cmp_pos=nsa_cmp_pos.reshape(depth, 2, 1, NSA_CMP_LEN * HEAD_DIM),
             nsa_cmp_w1=nsa_cmp_w1, nsa_cmp_w2=nsa_cmp_w2, w_branch=w_branch, w_gate=w_gate, w_o=w_o)
    ffn_wg, ffn_wu = ffn_w_gate[:, None], ffn_w_up[:, None]
    moe_wd = moe_w_down.reshape(moe_w_down.shape[0], -1, d)
    router_p = jnp.pad(moe_router, ((0, 0), (0, 0), (0, LANES - N_EXPERTS)))
    mods = [_mm(c8, cond_w, l, tm=8, tn=512, prologue="silu", bias=cond_b[l], name="cond")[0:1]
            for l in range(depth)]
    u = _modulate(xs, mods[0], 1, 0, tm=cfg["tm_ln"])
    for l in range(depth):
        y = _token_mixers(u, l, p, cfg)
        xs, u = _resid_ln(xs, y, mods[l], 2, ln1_g[l], ln1_b[l], mods[l], 4, 3, tm=cfg["tm_ln"])
        jx = l // 2
        if l % 2 == 0:
            hdn = _swiglu_up(u, ffn_wg, ffn_wu, jx, tm=cfg["tm"], tn=512, name="ffn_up")
            y = _mmk(hdn, ffn_w_down, jx, tm=cfg["tm"], tn=d, tk=512, name="ffn_down")
        else:
            we = _router(u, router_p, jx, tm=cfg["tm"])
            hdn = _swiglu_up(u, moe_w_gate, moe_w_up, jx, we, tm=cfg["tm"], tn=512, name="moe_up")
            y = _mmk(hdn, moe_wd, jx, tm=cfg["tm"], tn=d, tk=512, name="moe_down")
        nxt = min(l + 1, depth - 1)
        xs, u = _resid_ln(xs, y, mods[l], 5, ln2_g[l], ln2_b[l], mods[nxt], 1, 0, tm=cfg["tm_ln"])
    return xs.reshape(bsz, s_len, d)
```

```python
import functools
import math

import numpy as np
import jax
import jax.numpy as jnp
from jax import lax
from jax.experimental import pallas as pl
from jax.experimental.pallas import tpu as pltpu

F32 = jnp.float32
BF16 = jnp.bfloat16
NEG = -1e30

D_MODEL = 2048
DEPTH = 4
HEAD_DIM = 64
DA_HEADS = 4
DSA_HEADS = 8
DSA_KV_RANK = 128
IDX_HEADS = 8
DSA_TOPK_MAX = 256
SWA_HEADS = 8
SWA_WINDOW = 128
NSA_HEADS = 8
NSA_GROUPS = 2
NSA_CMP_LEN = 32
NSA_CMP_STRIDE = 16
NSA_CMP_HID = 256
NSA_SLC_LEN = 64
NSA_TOPN = 16
NSA_WINDOW = 512
NSA_FORCE = 1e9
N_BRANCH = 4
BRANCH_W = 512
N_EXPERTS = 8
ALPHA = (2.0 * DEPTH) ** 0.25

VMEM_LIMIT_BYTES = 56 * 1024 * 1024
LANES = 128

_ORIG = dict(a_q=(0, 512), a_k=(512, 512), a_v=(1024, 512), b_q=(1536, 512), b_kv=(2048, 128),
             b_iq=(2176, 512), b_ik=(2688, 64), b_iw=(2752, 8), c_q=(2760, 512), c_k=(3272, 128),
             c_v=(3400, 128), d_q=(3528, 512), d_kv=(4040, 768), d_g=(4808, 24))
_NEW_ORDER = ("a_q", "a_k", "a_v", "b_q", "b_iq", "c_q", "d_q", "b_kv", "c_k", "c_v", "d_kv",
              "b_ik", "pad64", "b_iw", "d_g", "pad96", "pad128")
PROJ_W = 5120
BLK512 = dict(a_q=0, a_k=1, a_v=2, b_q=3, b_iq=4, c_q=5, d_q=6)
BLK128 = dict(b_kv=28, c_k=29, c_v=30, d_kc=31, d_vc=32, d_ks=33, d_vs=34, d_kw=35, d_vw=36,
              b_ik=37, small=38)
COL_DKV = 3968


def _cparams(sem):
    return pltpu.CompilerParams(dimension_semantics=sem, vmem_limit_bytes=VMEM_LIMIT_BYTES)


def _sigmoid(x):
    return 1.0 / (1.0 + jnp.exp(-x))


def _silu(x):
    return x * _sigmoid(x)


def _alibi(n_heads):
    return [2.0 ** (-8.0 * (h + 1) / n_heads) for h in range(n_heads)]


def _dot(a, b):
    return jnp.dot(a, b, preferred_element_type=F32)


def _dot_nt(a, b):
    return lax.dot_general(a, b, (((1,), (1,)), ((), ())), preferred_element_type=F32)


def _mm_kernel(*refs, prologue, has_bias, eps):
    it = iter(refs)
    a_ref = next(it)
    g_ref = next(it) if prologue == "rms" else None
    w_ref = next(it)
    b_ref = next(it) if has_bias else None
    o_ref = next(it)
    wb_ref = next(it)

    @pl.when(pl.program_id(1) == 0)
    def _():
        wb_ref[...] = w_ref[0].astype(BF16)

    a = a_ref[...]
    if prologue == "silu":
        a = _silu(a.astype(F32))
    elif prologue == "rms":
        a = a.astype(F32)
        a = a * lax.rsqrt(jnp.mean(a * a, axis=-1, keepdims=True) + eps) * g_ref[...]
    acc = _dot(a.astype(BF16), wb_ref[...])
    if has_bias:
        acc = acc + b_ref[...]
    o_ref[...] = acc.astype(o_ref.dtype)


def _mm(a, w, layer, *, tm, tn, out_dtype=F32, a_blk=0, k=None, prologue=None, gain=None, bias=None,
        eps=1e-6, name="mm"):
    m = a.shape[0]
    k = a.shape[1] if k is None else k
    n = w.shape[2]
    assert w.shape[1] == k and m % tm == 0 and n % tn == 0
    in_specs = [pl.BlockSpec((tm, k), lambda j, i: (i, a_blk))]
    args = [a]
    if prologue == "rms":
        in_specs.append(pl.BlockSpec((1, k), lambda j, i: (0, 0)))
        args.append(gain.reshape(1, k))
    in_specs.append(pl.BlockSpec((1, k, tn), lambda j, i: (layer, 0, j)))
    args.append(w)
    if bias is not None:
        in_specs.append(pl.BlockSpec((1, tn), lambda j, i: (0, j)))
        args.append(bias.reshape(1, n))
    return pl.pallas_call(
        functools.partial(_mm_kernel, prologue=prologue, has_bias=bias is not None, eps=eps),
        grid=(n // tn, m // tm),
        in_specs=in_specs,
        out_specs=pl.BlockSpec((tm, tn), lambda j, i: (i, j)),
        out_shape=jax.ShapeDtypeStruct((m, n), out_dtype),
        scratch_shapes=[pltpu.VMEM((k, tn), BF16)],
        compiler_params=_cparams(("arbitrary", "arbitrary")),
        name=name,
    )(*args)


def _mmk_kernel(a_ref, w_ref, o_ref, acc_ref, *, nk):
    kk = pl.program_id(2)

    @pl.when(kk == 0)
    def _():
        acc_ref[...] = jnp.zeros_like(acc_ref)

    acc_ref[...] += _dot(a_ref[...], w_ref[0].astype(BF16))

    @pl.when(kk == nk - 1)
    def _():
        o_ref[...] = acc_ref[...]


def _mmk(a, w, layer, *, tm, tn, tk, name="mmk"):
    m, k = a.shape
    n = w.shape[2]
    assert w.shape[1] == k and m % tm == 0 and n % tn == 0 and k % tk == 0
    nk = k // tk
    return pl.pallas_call(
        functools.partial(_mmk_kernel, nk=nk),
        grid=(m // tm, n // tn, nk),
        in_specs=[pl.BlockSpec((tm, tk), lambda i, j, kk: (i, kk)),
                  pl.BlockSpec((1, tk, tn), lambda i, j, kk: (layer, kk, j))],
        out_specs=pl.BlockSpec((tm, tn), lambda i, j, kk: (i, j)),
        out_shape=jax.ShapeDtypeStruct((m, n), F32),
        scratch_shapes=[pltpu.VMEM((tm, tn), F32)],
        compiler_params=_cparams(("arbitrary", "arbitrary", "arbitrary")),
        name=name,
    )(a, w)


def _swiglu_kernel(a_ref, wg_ref, wu_ref, o_ref, wgb_ref, wub_ref):
    @pl.when(pl.program_id(1) == 0)
    def _():
        wgb_ref[...] = wg_ref[0].astype(BF16)
        wub_ref[...] = wu_ref[0].astype(BF16)

    a = a_ref[...]
    o_ref[...] = (_silu(_dot(a, wgb_ref[...])) * _dot(a, wub_ref[...])).astype(o_ref.dtype)


def _swiglu_up(u, wg, wu, layer, *, tm, tn, name="swiglu_up"):
    m, k = u.shape
    f = wg.shape[2]
    assert f % tn == 0 and m % tm == 0
    wspec = pl.BlockSpec((1, k, tn), lambda j, i: (layer, 0, j))
    return pl.pallas_call(
        _swiglu_kernel,
        grid=(f // tn, m // tm),
        in_specs=[pl.BlockSpec((tm, k), lambda j, i: (i, 0)), wspec, wspec],
        out_specs=pl.BlockSpec((tm, tn), lambda j, i: (i, j)),
        out_shape=jax.ShapeDtypeStruct((m, f), BF16),
        scratch_shapes=[pltpu.VMEM((k, tn), BF16), pltpu.VMEM((k, tn), BF16)],
        compiler_params=_cparams(("arbitrary", "arbitrary")),
        name=name,
    )(u, wg, wu)


def _modulate_kernel(x_ref, sc_ref, sh_ref, u_ref):
    u_ref[...] = (x_ref[...] * (1.0 + sc_ref[...]) + sh_ref[...]).astype(u_ref.dtype)


def _modulate(x, mod, sc_blk, sh_blk, *, tm):
    m, d = x.shape
    return pl.pallas_call(
        _modulate_kernel,
        grid=(m // tm,),
        in_specs=[pl.BlockSpec((tm, d), lambda i: (i, 0)),
                  pl.BlockSpec((1, d), lambda i: (0, sc_blk)),
                  pl.BlockSpec((1, d), lambda i: (0, sh_blk))],
        out_specs=pl.BlockSpec((tm, d), lambda i: (i, 0)),
        out_shape=jax.ShapeDtypeStruct((m, d), BF16),
        compiler_params=_cparams(("arbitrary",)),
        name="modulate",
    )(x, mod, mod)


def _resid_ln_kernel(x_ref, y_ref, gate_ref, g_ref, b_ref, sc_ref, sh_ref, xo_ref, u_ref):
    z = ALPHA * x_ref[...] + gate_ref[...] * y_ref[...]
    mu = jnp.mean(z, axis=-1, keepdims=True)
    zc = z - mu
    var = jnp.mean(zc * zc, axis=-1, keepdims=True)
    xn = zc * lax.rsqrt(var + 1e-5) * g_ref[...] + b_ref[...]
    xo_ref[...] = xn
    u_ref[...] = (xn * (1.0 + sc_ref[...]) + sh_ref[...]).astype(u_ref.dtype)


def _resid_ln(x, y, mod, gate_blk, g, b, mod_next, sc_blk, sh_blk, *, tm):
    m, d = x.shape
    row = lambda blk: pl.BlockSpec((1, d), lambda i: (0, blk))
    return pl.pallas_call(
        _resid_ln_kernel,
        grid=(m // tm,),
        in_specs=[pl.BlockSpec((tm, d), lambda i: (i, 0)),
                  pl.BlockSpec((tm, d), lambda i: (i, 0)),
                  row(gate_blk), row(0), row(0), row(sc_blk), row(sh_blk)],
        out_specs=[pl.BlockSpec((tm, d), lambda i: (i, 0)),
                   pl.BlockSpec((tm, d), lambda i: (i, 0))],
        out_shape=[jax.ShapeDtypeStruct((m, d), F32), jax.ShapeDtypeStruct((m, d), BF16)],
        compiler_params=_cparams(("arbitrary",)),
        name="resid_ln",
    )(x, y, mod, g.reshape(1, d), b.reshape(1, d), mod_next, mod_next)


FLASH_ROW_CHUNK = 32
POS_SPLIT = 128


def _head_pad(x, n_heads, kind):
    s_len = x.shape[0]
    w = x.shape[1] // n_heads
    xh = x.reshape(s_len, n_heads, w)
    pos = jnp.arange(s_len, dtype=jnp.int32)
    one = jnp.ones_like(pos)
    cols = [pos // POS_SPLIT, pos % POS_SPLIT, one, one] if kind == "k" else [one]
    tail = jnp.pad(jnp.stack(cols, axis=1).astype(F32), ((0, 0), (0, w - len(cols))))
    tail = jnp.broadcast_to(tail[:, None, :], xh.shape)
    return jnp.concatenate([xh, tail], axis=2).reshape(s_len, n_heads * 2 * w).astype(BF16)


def _flash_kernel(*refs, units, tq, tk, window, dense, n_prev, nsteps, n_mask, has_sink):
    it = iter(refs)
    q_ref, k_ref, v_ref = next(it), next(it), next(it)
    mask_ref = next(it) if n_mask else None
    sink_ref = next(it) if has_sink else None
    o_ref = next(it)
    q_scr, m_scr, acc_scr, bias_scr = (next(it) for _ in range(4))
    dv = acc_scr.shape[2]
    lcol = dv // 2

    qi = pl.program_id(0)
    j = pl.program_id(1)
    last_kb = (qi * tq + tq - 1) // tk
    kb = j if dense else qi - n_prev + j
    rows = q_scr.shape[1]
    rb = FLASH_ROW_CHUNK
    kw = 2 * HEAD_DIM

    @pl.when(j == 0)
    def _init():
        lane = lax.broadcasted_iota(jnp.int32, (tq, HEAD_DIM), 1)
        qpos = qi * tq + lax.broadcasted_iota(jnp.int32, (tq, HEAD_DIM), 0)
        qhi = (qpos // POS_SPLIT).astype(F32)
        qlo = (qpos % POS_SPLIT).astype(F32)
        for ui, (_, _, _, _, hds) in enumerate(units):
            for r, (qo, slope, _, sink_idx) in enumerate(hds):
                rsl = slice(r * tq, (r + 1) * tq)
                tail = jnp.where(lane == 0, POS_SPLIT * slope,
                                 jnp.where(lane == 1, slope,
                                           jnp.where(lane == 2, -POS_SPLIT * slope * qhi,
                                                     jnp.where(lane == 3, -slope * qlo, 0.0))))
                qs = q_ref[:, qo:qo + HEAD_DIM] * HEAD_DIM ** -0.5
                q_scr[ui, rsl] = jnp.concatenate([qs, tail], axis=1).astype(BF16)
                if has_sink:
                    m_scr[ui, rsl] = jnp.broadcast_to(sink_ref[:, sink_idx:sink_idx + 1], (tq, 1))
                else:
                    m_scr[ui, rsl] = jnp.full((tq, 1), NEG, F32)
            alane = lax.broadcasted_iota(jnp.int32, acc_scr.shape[1:], 1)
            acc_scr[ui] = jnp.where(alane == lcol, 1.0 if has_sink else 0.0, 0.0)

    def scores(ui):
        ku = units[ui][0]
        return _dot_nt(q_scr[ui], k_ref[:, ku * kw:(ku + 1) * kw])

    def step(masked):
        if masked:
            qpos = qi * tq + lax.broadcasted_iota(jnp.int32, (tq, tk), 0)
            kpos = kb * tk + lax.broadcasted_iota(jnp.int32, (tq, tk), 1)
            dist = qpos - kpos
            valid = dist >= 0
            if not dense:
                valid = valid & (dist < window)
            if n_mask:
                for g in range(n_mask):
                    bias_scr[g] = jnp.where(valid, mask_ref[g].astype(F32), NEG)
            else:
                bias_scr[0] = jnp.where(valid, 0.0, NEG)

        def chunk(s, mg, c):
            r0 = c * rb
            sc = s[r0:r0 + rb]
            if masked:
                rw = r0 % tq
                sc = sc + bias_scr[mg, rw:rw + rb]
            return sc

        s_next = scores(0)
        for ui, (_, vo, _, mg, _) in enumerate(units):
            s = s_next
            if ui + 1 < len(units):
                s_next = scores(ui + 1)
            nchunk = rows // rb
            m_old = m_scr[ui]
            m_cur = jnp.concatenate([jnp.max(chunk(s, mg, c), axis=1, keepdims=True) for c in range(nchunk)],
                                    axis=0)
            m_new = jnp.maximum(m_old, m_cur)
            alpha = jnp.exp(m_old - m_new)
            m_scr[ui] = m_new
            p_all = jnp.concatenate(
                [jnp.exp(chunk(s, mg, c) - m_new[c * rb:(c + 1) * rb]).astype(BF16) for c in range(nchunk)],
                axis=0)
            acc_scr[ui] = alpha * acc_scr[ui] + _dot(p_all, v_ref[:, vo:vo + dv])

    needed = (kb >= 0) & (kb <= last_kb)
    if dense and not n_mask:
        interior = kb * tk + tk - 1 <= qi * tq
        pl.when(needed & interior)(lambda: step(False))
        pl.when(needed & jnp.logical_not(interior))(lambda: step(True))
    else:
        pl.when(needed)(lambda: step(True))

    @pl.when(j == nsteps - 1)
    def _fin():
        for ui, (_, _, _, _, hds) in enumerate(units):
            for r, (_, _, (oo, ow), _) in enumerate(hds):
                rsl = slice(r * tq, (r + 1) * tq)
                acc = acc_scr[ui, rsl]
                o_ref[:, oo:oo + ow] = acc[:, 0:ow] / acc[:, lcol:lcol + 1]


def _flash(q_arr, k_arr, v_arr, *, units, q_spec, out_w, tq, tk, window=None,
           mask=None, sinks=None, name="flash"):
    s_len = q_arr.shape[0]
    dense = window is None
    if dense:
        n_prev, nsteps = 0, s_len // tk
    else:
        assert tq == tk
        n_prev = -(-(window - 1) // tk)
        nsteps = n_prev + 1
    n_mask = 0 if mask is None else mask.shape[0]
    dv = units[0][2]
    nu = len(units)
    rows = len(units[0][4]) * tq
    assert all(len(un[4]) * tq == rows and un[2] == dv for un in units) and rows % FLASH_ROW_CHUNK == 0

    def kv_index(qi, j):
        last_kb = (qi * tq + tq - 1) // tk
        kb = j if dense else qi - n_prev + j
        return jnp.clip(kb, 0, last_kb)

    in_specs = [pl.BlockSpec((tq, q_spec[0]), lambda qi, j: (qi, q_spec[1])),
                pl.BlockSpec((tk, k_arr.shape[1]), lambda qi, j: (kv_index(qi, j), 0)),
                pl.BlockSpec((tk, v_arr.shape[1]), lambda qi, j: (kv_index(qi, j), 0))]
    args = [q_arr, k_arr, v_arr]
    if n_mask:
        in_specs.append(pl.BlockSpec((n_mask, tq, tk), lambda qi, j: (0, qi, kv_index(qi, j))))
        args.append(mask)
    if sinks is not None:
        in_specs.append(pl.BlockSpec((1, LANES), lambda qi, j: (0, 0)))
        args.append(sinks)
    return pl.pallas_call(
        functools.partial(_flash_kernel, units=tuple(units), tq=tq, tk=tk, window=window, dense=dense,
                          n_prev=n_prev, nsteps=nsteps, n_mask=n_mask, has_sink=sinks is not None),
        grid=(s_len // tq, nsteps),
        in_specs=in_specs,
        out_specs=pl.BlockSpec((tq, out_w), lambda qi, j: (qi, 0)),
        out_shape=jax.ShapeDtypeStruct((s_len, out_w), F32),
        scratch_shapes=[pltpu.VMEM((nu, rows, 2 * HEAD_DIM), BF16), pltpu.VMEM((nu, rows, 1), F32),
                        pltpu.VMEM((nu, rows, dv), F32), pltpu.VMEM((max(n_mask, 1), tq, tk), F32)],
        compiler_params=_cparams(("arbitrary", "arbitrary")),
        name=name,
    )(*args)


def _diff_final_kernel(o_ref, lam_ref, g_ref, out_ref, *, lambda_init):
    lf = lam_ref[0]
    lam = (jnp.exp(jnp.sum(lf[0:1] * lf[1:2])) - jnp.exp(jnp.sum(lf[2:3] * lf[3:4])) + lambda_init)
    w = 2 * HEAD_DIM
    for h in range(DA_HEADS):
        o = o_ref[:, (2 * h) * w:(2 * h + 1) * w] - lam * o_ref[:, (2 * h + 1) * w:(2 * h + 2) * w]
        o = o * lax.rsqrt(jnp.mean(o * o, axis=-1, keepdims=True) + 1e-6) * g_ref[...]
        out_ref[:, h * w:(h + 1) * w] = o * (1.0 - lambda_init)


def _diff_final(o, diff_lambda, layer, subln_g, lambda_init, *, tm):
    m = o.shape[0]
    w = 2 * HEAD_DIM
    return pl.pallas_call(
        functools.partial(_diff_final_kernel, lambda_init=lambda_init),
        grid=(m // tm,),
        in_specs=[pl.BlockSpec((tm, 2 * DA_HEADS * w), lambda i: (i, 0)),
                  pl.BlockSpec((1, 4, HEAD_DIM), lambda i: (layer, 0, 0)),
                  pl.BlockSpec((1, w), lambda i: (0, 0))],
        out_specs=pl.BlockSpec((tm, DA_HEADS * w), lambda i: (i, 0)),
        out_shape=jax.ShapeDtypeStruct((m, DA_HEADS * w), F32),
        compiler_params=_cparams(("arbitrary",)),
        name="diff_final",
    )(o, diff_lambda, subln_g.reshape(1, w))


INT_MIN = -(2 ** 31)


def _f32_key_const(x):
    b = int(np.array(x, np.float32).view(np.int32))
    return b ^ ((b >> 31) & 0x7FFFFFFF)


def _dsa_select_kernel(qi_ref, w_ref, kidx_ref, mask_ref, key_scr, j_scr, *, tq, ch, nch, topk, s_len):
    i = pl.program_id(0)
    q0 = i * tq
    n_need = (q0 + tq + ch - 1) // ch
    qpos = q0 + lax.broadcasted_iota(jnp.int32, (tq, 1), 0)
    lane = lax.broadcasted_iota(jnp.int32, (1, ch), 1)
    w = w_ref[:, 0:IDX_HEADS]

    def score_chunk(c, carry):
        kc = kidx_ref[pl.ds(pl.multiple_of(c * ch, ch), ch), 0:HEAD_DIM].astype(BF16)
        acc = jnp.zeros((tq, ch), F32)
        for h in range(IDX_HEADS):
            qh = qi_ref[:, h * HEAD_DIM:(h + 1) * HEAD_DIM].astype(BF16)
            acc = acc + w[:, h:h + 1] * jnp.maximum(_dot_nt(qh, kc), 0.0)
        acc = jnp.where(c * ch + lane <= qpos, acc, NEG) + 0.0
        bits = pltpu.bitcast(acc, jnp.int32)
        key_scr[c] = bits ^ ((bits >> 31) & 0x7FFFFFFF)
        return carry

    lax.fori_loop(0, n_need, score_chunk, 0)

    def count(pred):
        def body(c, acc):
            m = jnp.where(pred(key_scr[c], c), 1, 0)
            part = m[:, 0:LANES]
            for t in range(1, ch // LANES):
                part = part + m[:, t * LANES:(t + 1) * LANES]
            return acc + part
        acc = lax.fori_loop(0, n_need, body, jnp.zeros((tq, LANES), jnp.int32))
        return jnp.sum(acc, axis=1, keepdims=True)

    def bit_cond(st):
        return (st[0] < 32) & (st[3] > 0)

    def bit_step(st):
        b, t, c_t, _ = st
        cand = t + jnp.left_shift(jnp.int32(1), 31 - b)
        cnt = count(lambda blk, c: blk >= cand)
        take = cnt >= topk
        c_t = jnp.where(take, cnt, c_t)
        return b + 1, jnp.where(take, cand, t), c_t, jnp.max(jnp.where(c_t > topk, 1, 0))

    _, thr, _, _ = lax.while_loop(
        bit_cond, bit_step,
        (jnp.int32(0), jnp.full((tq, 1), INT_MIN, jnp.int32), jnp.full((tq, 1), s_len, jnp.int32),
         jnp.int32(1)))
    cnt_gt = count(lambda blk, c: blk > thr)
    cnt_ge = count(lambda blk, c: blk >= thr)
    need = topk - cnt_gt
    tie_rows = (cnt_ge > topk) & (thr > _f32_key_const(NEG))
    j_scr[...] = jnp.full((tq, 1), s_len, jnp.int32)

    @pl.when(jnp.max(jnp.where(tie_rows, 1, 0)) > 0)
    def _ties():
        nbits = int(math.log2(s_len))

        def idx_step(b, jv):
            cand = jv + jnp.left_shift(jnp.int32(1), nbits - 1 - b)
            cnt = count(lambda blk, c: (blk == thr) & (c * ch + lane < cand))
            return jnp.where(cnt < need, cand, jv)

        jv = lax.fori_loop(0, nbits, idx_step, jnp.zeros((tq, 1), jnp.int32))
        j_scr[...] = jnp.where(tie_rows, jv, s_len)

    jv = j_scr[...]
    for c in range(nch):
        @pl.when(c < n_need)
        def _w():
            key = key_scr[c]
            idx = c * ch + lane
            sel = ((key > thr) | ((key == thr) & (idx <= jv))) & (idx <= qpos)
            mask_ref[:, c * ch:(c + 1) * ch] = jnp.where(sel, 0.0, NEG).astype(mask_ref.dtype)

        @pl.when(c >= n_need)
        def _z():
            mask_ref[:, c * ch:(c + 1) * ch] = jnp.full((tq, ch), NEG, mask_ref.dtype)


def _dsa_select(proj, *, topk, tq=128):
    s_len = proj.shape[0]
    ch = min(1024, s_len)
    nch = s_len // ch
    return pl.pallas_call(
        functools.partial(_dsa_select_kernel, tq=tq, ch=ch, nch=nch, topk=topk, s_len=s_len),
        grid=(s_len // tq,),
        in_specs=[pl.BlockSpec((tq, 512), lambda i: (i, BLK512["b_iq"])),
                  pl.BlockSpec((tq, LANES), lambda i: (i, BLK128["small"])),
                  pl.BlockSpec((s_len, LANES), lambda i: (0, BLK128["b_ik"]))],
        out_specs=pl.BlockSpec((tq, s_len), lambda i: (i, 0)),
        out_shape=jax.ShapeDtypeStruct((s_len, s_len), BF16),
        scratch_shapes=[pltpu.VMEM((nch, tq, ch), jnp.int32), pltpu.VMEM((tq, 1), jnp.int32)],
        compiler_params=_cparams(("arbitrary",)),
        name="dsa_select",
    )(proj, proj, proj)


def _nsa_compress_kernel(x_ref, pos_ref, w1_ref, w2_ref, o_ref):
    x = (x_ref[0] + pos_ref[0, 0]).astype(BF16)
    hdn = _silu(_dot(x, w1_ref[0, 0].astype(BF16)))
    o_ref[0] = _dot(hdn.astype(BF16), w2_ref[0, 0].astype(BF16))


def _nsa_compress(xc, pos, w1, w2, layer):
    _, ncp, kdim = xc.shape
    return pl.pallas_call(
        _nsa_compress_kernel,
        grid=(4,),
        in_specs=[pl.BlockSpec((1, ncp, kdim), lambda i: (i, 0, 0)),
                  pl.BlockSpec((1, 1, 1, kdim), lambda i: (layer, i // 2, 0, 0)),
                  pl.BlockSpec((1, 1, kdim, NSA_CMP_HID), lambda i: (layer, i // 2, 0, 0)),
                  pl.BlockSpec((1, 1, NSA_CMP_HID, HEAD_DIM), lambda i: (layer, i // 2, 0, 0))],
        out_specs=pl.BlockSpec((1, ncp, HEAD_DIM), lambda i: (i, 0, 0)),
        out_shape=jax.ShapeDtypeStruct((4, ncp, HEAD_DIM), F32),
        compiler_params=_cparams(("arbitrary",)),
        name="nsa_compress",
    )(xc, pos, w1, w2)


def _nsa_cmp_kernel(q_ref, kv_ref, ov_ref, ex_ref, o_ref, mask_ref, *, tq, ncp, n_slc, topn, ch, nch):
    i = pl.program_id(0)
    q0 = i * tq
    n_need = (q0 + tq + ch - 1) // ch
    rpg = NSA_HEADS // NSA_GROUPS
    slopes = _alibi(NSA_HEADS)
    scale = HEAD_DIM ** -0.5
    qpos_c = q0 + lax.broadcasted_iota(jnp.int32, (tq, ncp), 0)
    cend = lax.broadcasted_iota(jnp.int32, (tq, ncp), 1) * NSA_CMP_STRIDE + (NSA_CMP_LEN - 1)
    dist_c = qpos_c - cend
    valid_c = dist_c >= 0
    distf = dist_c.astype(F32)
    qpos = q0 + lax.broadcasted_iota(jnp.int32, (tq, n_slc), 0)
    blk = lax.broadcasted_iota(jnp.int32, (tq, n_slc), 1)
    cur = qpos // NSA_SLC_LEN
    forced = (blk == 0) | (blk == cur) | (blk == cur - 1)
    blk_ok = blk * NSA_SLC_LEN <= qpos
    ov = ov_ref[...]
    imps = []
    for g in range(NSA_GROUPS):
        kc = kv_ref[g].astype(BF16)
        vc = kv_ref[NSA_GROUPS + g].astype(BF16)
        psum = jnp.zeros((tq, ncp), F32)
        for r in range(rpg):
            h = g * rpg + r
            qh = q_ref[:, h * HEAD_DIM:(h + 1) * HEAD_DIM].astype(BF16)
            s = _dot_nt(qh, kc) * scale - slopes[h] * distf
            s = jnp.where(valid_c, s, NEG)
            e = jnp.where(valid_c, jnp.exp(s - jnp.max(s, axis=1, keepdims=True)), 0.0)
            p = e / jnp.maximum(jnp.sum(e, axis=1, keepdims=True), 1e-30)
            o_ref[:, h * HEAD_DIM:(h + 1) * HEAD_DIM] = _dot(p.astype(BF16), vc)
            psum = psum + p
        p_hi = psum.astype(BF16)
        p_lo = (psum - p_hi.astype(F32)).astype(BF16)
        imp = _dot(p_hi, ov) + _dot(p_lo, ov)
        imp = jnp.where(forced, NSA_FORCE, imp)
        imps.append(jnp.where(blk_ok, imp, NEG))
    sels = [jnp.full((tq, n_slc), NEG, F32) for _ in range(NSA_GROUPS)]
    for _ in range(topn):
        for g in range(NSA_GROUPS):
            mx = jnp.max(imps[g], axis=1, keepdims=True)
            first = jnp.min(jnp.where(imps[g] == mx, blk, n_slc), axis=1, keepdims=True)
            hit = blk == first
            sels[g] = jnp.where(hit, 0.0, sels[g])
            imps[g] = jnp.where(hit, -jnp.inf, imps[g])
    for g in range(NSA_GROUPS):
        selb = sels[g].astype(BF16)
        for c in range(nch):
            @pl.when(c < n_need)
            def _w():
                tok = _dot(selb, ex_ref[:, c * ch:(c + 1) * ch])
                mask_ref[g, :, c * ch:(c + 1) * ch] = tok.astype(mask_ref.dtype)

            @pl.when(c >= n_need)
            def _z():
                mask_ref[g, :, c * ch:(c + 1) * ch] = jnp.full((tq, ch), NEG, mask_ref.dtype)


def _nsa_cmp(proj, kv_cmp, *, tq=128):
    s_len = proj.shape[0]
    ncp = kv_cmp.shape[1]
    n_slc = s_len // NSA_SLC_LEN
    topn = min(NSA_TOPN, n_slc)
    ch = min(1024, s_len)
    nch = s_len // ch
    starts = np.arange(ncp) * NSA_CMP_STRIDE
    slc_start = np.arange(n_slc) * NSA_SLC_LEN
    overlap = ((starts[:, None] < slc_start[None, :] + NSA_SLC_LEN)
               & (starts[:, None] + NSA_CMP_LEN > slc_start[None, :])).astype(np.float32)
    expand = (np.arange(s_len)[None, :] // NSA_SLC_LEN == np.arange(n_slc)[:, None]).astype(np.float32)
    return pl.pallas_call(
        functools.partial(_nsa_cmp_kernel, tq=tq, ncp=ncp, n_slc=n_slc, topn=topn, ch=ch, nch=nch),
        grid=(s_len // tq,),
        in_specs=[pl.BlockSpec((tq, 512), lambda i: (i, BLK512["d_q"])),
                  pl.BlockSpec((4, ncp, HEAD_DIM), lambda i: (0, 0, 0)),
                  pl.BlockSpec((ncp, n_slc), lambda i: (0, 0)),
                  pl.BlockSpec((n_slc, s_len), lambda i: (0, 0))],
        out_specs=[pl.BlockSpec((tq, 512), lambda i: (i, 0)),
                   pl.BlockSpec((NSA_GROUPS, tq, s_len), lambda i: (0, i, 0))],
        out_shape=[jax.ShapeDtypeStruct((s_len, 512), F32),
                   jax.ShapeDtypeStruct((NSA_GROUPS, s_len, s_len), BF16)],
        compiler_params=_cparams(("arbitrary",)),
        name="nsa_cmp",
    )(proj, kv_cmp, jnp.asarray(overlap, BF16), jnp.asarray(expand, BF16))


def _nsa_combine_kernel(g_ref, oc_ref, os_ref, ow_ref, o_ref):
    gt = _sigmoid(g_ref[...])
    for h in range(NSA_HEADS):
        sl = slice(h * HEAD_DIM, (h + 1) * HEAD_DIM)
        c0 = IDX_HEADS + 3 * h
        o_ref[:, sl] = (gt[:, c0:c0 + 1] * oc_ref[:, sl] + gt[:, c0 + 1:c0 + 2] * os_ref[:, sl]
                        + gt[:, c0 + 2:c0 + 3] * ow_ref[:, sl])


def _nsa_combine(proj, o_cmp, o_slc, o_win, *, tm):
    m = proj.shape[0]
    spec = pl.BlockSpec((tm, 512), lambda i: (i, 0))
    return pl.pallas_call(
        _nsa_combine_kernel,
        grid=(m // tm,),
        in_specs=[pl.BlockSpec((tm, LANES), lambda i: (i, BLK128["small"])), spec, spec, spec],
        out_specs=spec,
        out_shape=jax.ShapeDtypeStruct((m, 512), F32),
        compiler_params=_cparams(("arbitrary",)),
        name="nsa_combine",
    )(proj, o_cmp, o_slc, o_win)


def _merge_kernel(u_ref, oa_ref, ob_ref, oc_ref, od_ref, wg0, wg1, wg2, wg3, wb_ref, o_ref,
                  wgb_ref, wbb_ref):
    wgs = (wg0, wg1, wg2, wg3)

    @pl.when(pl.program_id(1) == 0)
    def _():
        for mch in range(N_BRANCH):
            wgb_ref[mch] = wgs[mch][0].astype(BF16)
            wbb_ref[mch] = wb_ref[0, mch].astype(BF16)

    u = u_ref[...]
    acc = None
    for mch, o_ref_m in enumerate((oa_ref, ob_ref, oc_ref, od_ref)):
        gte = _sigmoid(_dot(u, wgb_ref[mch]))
        z = _dot(o_ref_m[...].astype(BF16), wbb_ref[mch])
        acc = gte * z if acc is None else acc + gte * z
    o_ref[...] = acc.astype(o_ref.dtype)


def _merge(u, branches, w_gate, w_branch, layer, *, tm, tn):
    m, d = u.shape
    nj = d // tn
    bspec = pl.BlockSpec((tm, BRANCH_W), lambda j, i: (i, 0))
    wg_specs = [pl.BlockSpec((1, d, tn),
                             functools.partial(lambda j, i, mch: (layer, 0, mch * nj + j), mch=mch))
                for mch in range(N_BRANCH)]
    return pl.pallas_call(
        _merge_kernel,
        grid=(nj, m // tm),
        in_specs=[pl.BlockSpec((tm, d), lambda j, i: (i, 0)), bspec, bspec, bspec, bspec,
                  *wg_specs,
                  pl.BlockSpec((1, N_BRANCH, BRANCH_W, tn), lambda j, i: (layer, 0, 0, j))],
        out_specs=pl.BlockSpec((tm, tn), lambda j, i: (i, j)),
        out_shape=jax.ShapeDtypeStruct((m, d), BF16),
        scratch_shapes=[pltpu.VMEM((N_BRANCH, d, tn), BF16), pltpu.VMEM((N_BRANCH, BRANCH_W, tn), BF16)],
        compiler_params=_cparams(("arbitrary", "arbitrary")),
        name="merge",
    )(u, *branches, w_gate, w_gate, w_gate, w_gate, w_branch)


def _router_kernel(u_ref, r_ref, o_ref):
    logits = _dot(u_ref[...], r_ref[0].astype(BF16))
    lane = lax.broadcasted_iota(jnp.int32, logits.shape, 1)
    lg = jnp.where(lane < N_EXPERTS, logits, -jnp.inf)
    m1 = jnp.max(lg, axis=1, keepdims=True)
    i1 = jnp.min(jnp.where(lg == m1, lane, LANES), axis=1, keepdims=True)
    lg2 = jnp.where(lane == i1, -jnp.inf, lg)
    m2 = jnp.max(lg2, axis=1, keepdims=True)
    i2 = jnp.min(jnp.where(lg2 == m2, lane, LANES), axis=1, keepdims=True)
    e2 = jnp.exp(m2 - m1)
    w1 = 1.0 / (1.0 + e2)
    w2 = e2 / (1.0 + e2)
    o_ref[...] = jnp.where(lane == 0, i1.astype(F32),
                           jnp.where(lane == 1, i2.astype(F32),
                                     jnp.where(lane == 2, w1, jnp.where(lane == 3, w2, 0.0))))


def _router(u, router_padded, layer, *, tm):
    m, d = u.shape
    return pl.pallas_call(
        _router_kernel,
        grid=(m // tm,),
        in_specs=[pl.BlockSpec((tm, d), lambda i: (i, 0)),
                  pl.BlockSpec((1, d, LANES), lambda i: (layer, 0, 0))],
        out_specs=pl.BlockSpec((tm, LANES), lambda i: (i, 0)),
        out_shape=jax.ShapeDtypeStruct((m, LANES), F32),
        compiler_params=_cparams(("arbitrary",)),
        name="router",
    )(u, router_padded)


MOE_ROW_TILE = 256
MOE_TOK_CHUNK = 256


def _moe_plan(ridx, rw, s_len):
    tm, ct, n_e = MOE_ROW_TILE, MOE_TOK_CHUNK, N_EXPERTS
    i32 = jnp.int32
    e_a = ridx.reshape(-1).astype(i32)
    oh = (e_a[:, None] == jnp.arange(n_e, dtype=i32)[None, :]).astype(i32)
    csum = jnp.cumsum(oh, axis=0)
    rank_a = jnp.sum((csum - oh) * oh, axis=1)
    ntile_e = (csum[-1] + tm - 1) // tm
    tile_end = jnp.cumsum(ntile_e)
    pos_a = jnp.take(tile_end - ntile_e, e_a) * tm + rank_a
    n_rows = 2 * s_len + n_e * tm
    n_tiles = n_rows // tm
    n_chunks = s_len // ct
    row_tok = jnp.full((n_rows,), -1, i32).at[pos_a].set(jnp.arange(2 * s_len, dtype=i32) // 2)
    row_w = jnp.zeros((n_rows,), F32).at[pos_a].set(rw.reshape(-1))
    tile_e = jnp.minimum(jnp.searchsorted(tile_end, jnp.arange(n_tiles, dtype=i32), side="right"),
                         n_e - 1).astype(i32)
    rt = row_tok.reshape(n_tiles, tm)
    lo = jnp.min(jnp.where(rt >= 0, rt, s_len - 1), axis=1) // ct
    hi = jnp.maximum(jnp.max(jnp.where(rt >= 0, rt, 0), axis=1) // ct, lo)
    n_i = hi - lo + 1
    end = jnp.cumsum(n_i)
    n_work = n_tiles + n_e * n_chunks
    w = jnp.arange(n_work, dtype=i32)
    wt = jnp.minimum(jnp.searchsorted(end, w, side="right"), n_tiles - 1).astype(i32)
    wc = jnp.clip(jnp.take(lo, wt) + w - jnp.take(end - n_i, wt), 0, n_chunks - 1).astype(i32)
    wa = (w < end[-1]).astype(i32)
    order = jnp.argsort(jnp.where(wa > 0, wc * n_tiles + wt, n_chunks * n_tiles + w))
    vc = jnp.where(wa > 0, wc, n_chunks - 1)[order]
    return dict(row_tok=row_tok, row_w=row_w, tile_e=tile_e, n_tiles=n_tiles, n_work=n_work,
                gather=(wt, wc, wa), combine=(vc, wt[order], wa[order]))


def _moe_gather_kernel(wt_ref, wc_ref, wa_ref, tok_ref, u_ref, o_ref, acc_ref, *, n_work):
    w = pl.program_id(0)
    tile = wt_ref[w]

    @pl.when((w == 0) | (wt_ref[jnp.maximum(w - 1, 0)] != tile))
    def _():
        acc_ref[...] = jnp.zeros_like(acc_ref)

    @pl.when(wa_ref[w] > 0)
    def _():
        ct = u_ref.shape[0]
        cols = wc_ref[w] * ct + lax.broadcasted_iota(jnp.int32, (1, ct), 1)
        onehot = jnp.where(tok_ref[...] == cols, 1.0, 0.0).astype(BF16)
        acc_ref[...] += _dot(onehot, u_ref[...])

    @pl.when((w == n_work - 1) | (wt_ref[jnp.minimum(w + 1, n_work - 1)] != tile))
    def _():
        o_ref[...] = acc_ref[...].astype(o_ref.dtype)


def _moe_gather(u, plan):
    s_len, d = u.shape
    tm, ct = MOE_ROW_TILE, MOE_TOK_CHUNK
    n_rows = plan["row_tok"].shape[0]
    return pl.pallas_call(
        functools.partial(_moe_gather_kernel, n_work=plan["n_work"]),
        grid_spec=pltpu.PrefetchScalarGridSpec(
            num_scalar_prefetch=3, grid=(plan["n_work"],),
            in_specs=[pl.BlockSpec((tm, 1), lambda w, wt, wc, wa: (wt[w], 0)),
                      pl.BlockSpec((ct, d), lambda w, wt, wc, wa: (wc[w], 0))],
            out_specs=pl.BlockSpec((tm, d), lambda w, wt, wc, wa: (wt[w], 0)),
            scratch_shapes=[pltpu.VMEM((tm, d), F32)]),
        out_shape=jax.ShapeDtypeStruct((n_rows, d), BF16),
        compiler_params=_cparams(("arbitrary",)),
        name="moe_gather",
    )(*plan["gather"], plan["row_tok"].reshape(n_rows, 1), u)


def _moe_combine_kernel(vc_ref, vt_ref, va_ref, tok_ref, ys_ref, o_ref, *, n_work):
    w = pl.program_id(0)
    chunk = vc_ref[w]

    @pl.when((w == 0) | (vc_ref[jnp.maximum(w - 1, 0)] != chunk))
    def _():
        o_ref[...] = jnp.zeros_like(o_ref)

    @pl.when(va_ref[w] > 0)
    def _():
        ct = o_ref.shape[0]
        rows = chunk * ct + lax.broadcasted_iota(jnp.int32, (ct, 1), 0)
        onehot_t = jnp.where(rows == tok_ref[0], 1.0, 0.0).astype(BF16)
        ys = ys_ref[...]
        hi = ys.astype(BF16)
        r1 = ys - hi.astype(F32)
        mid = r1.astype(BF16)
        low = (r1 - mid.astype(F32)).astype(BF16)
        o_ref[...] += _dot(onehot_t, hi) + _dot(onehot_t, mid) + _dot(onehot_t, low)


def _moe_combine(ys, plan, s_len):
    n_rows, d = ys.shape
    tm, ct = MOE_ROW_TILE, MOE_TOK_CHUNK
    return pl.pallas_call(
        functools.partial(_moe_combine_kernel, n_work=plan["n_work"]),
        grid_spec=pltpu.PrefetchScalarGridSpec(
            num_scalar_prefetch=3, grid=(plan["n_work"],),
            in_specs=[pl.BlockSpec((1, 1, tm), lambda w, vc, vt, va: (vt[w], 0, 0)),
                      pl.BlockSpec((tm, d), lambda w, vc, vt, va: (vt[w], 0))],
            out_specs=pl.BlockSpec((ct, d), lambda w, vc, vt, va: (vc[w], 0))),
        out_shape=jax.ShapeDtypeStruct((s_len, d), F32),
        compiler_params=_cparams(("arbitrary",)),
        name="moe_combine",
    )(*plan["combine"], plan["row_tok"].reshape(plan["n_tiles"], 1, tm), ys)


def _gmm_kernel(te_ref, *refs, swiglu):
    it = iter(refs)
    a_ref = next(it)
    w_refs = [next(it), next(it)] if swiglu else [next(it)]
    rw_ref = next(it) if swiglu else None
    o_ref = next(it)
    wb_refs = [next(it) for _ in w_refs]
    i = pl.program_id(1)

    @pl.when((i == 0) | (te_ref[i] != te_ref[jnp.maximum(i - 1, 0)]))
    def _():
        for w_ref, wb_ref in zip(w_refs, wb_refs):
            wb_ref[...] = w_ref[0, 0].astype(BF16)

    a = a_ref[...]
    if swiglu:
        h = _silu(_dot(a, wb_refs[0][...])) * _dot(a, wb_refs[1][...]) * rw_ref[...]
        o_ref[...] = h.astype(o_ref.dtype)
    else:
        o_ref[...] = _dot(a, wb_refs[0][...])


def _gmm(a, ws, layer, plan, *, tn, row_w=None, name="gmm"):
    n_rows, k = a.shape
    n = ws[0].shape[3]
    tm = MOE_ROW_TILE
    swiglu = len(ws) == 2
    wspec = pl.BlockSpec((1, 1, k, tn), lambda j, i, te: (layer, te[i], 0, j))
    in_specs = [pl.BlockSpec((tm, k), lambda j, i, te: (i, 0))] + [wspec] * len(ws)
    args = [a, *ws]
    if swiglu:
        in_specs.append(pl.BlockSpec((tm, 1), lambda j, i, te: (i, 0)))
        args.append(row_w.reshape(n_rows, 1))
    return pl.pallas_call(
        functools.partial(_gmm_kernel, swiglu=swiglu),
        grid_spec=pltpu.PrefetchScalarGridSpec(
            num_scalar_prefetch=1, grid=(n // tn, n_rows // tm),
            in_specs=in_specs,
            out_specs=pl.BlockSpec((tm, tn), lambda j, i, te: (i, j)),
            scratch_shapes=[pltpu.VMEM((k, tn), BF16) for _ in ws]),
        out_shape=jax.ShapeDtypeStruct((n_rows, n), BF16 if swiglu else F32),
        compiler_params=_cparams(("arbitrary", "arbitrary")),
        name=name,
    )(plan["tile_e"], *args)


def _permute_w_in(w):
    cols = []
    for nm in _NEW_ORDER:
        if nm.startswith("pad"):
            cols.append(jnp.zeros(w.shape[:2] + (int(nm[3:]),), w.dtype))
        else:
            o, n = _ORIG[nm]
            cols.append(w[:, :, o:o + n])
    out = jnp.concatenate(cols, axis=2)
    assert out.shape[2] == PROJ_W
    return out


def _nsa_cmp_inputs(proj):
    s_len = proj.shape[0]
    n_cmp = (s_len - NSA_CMP_LEN) // NSA_CMP_STRIDE + 1
    ncp = s_len // NSA_CMP_STRIDE
    xs = []
    for jj in range(2):
        for g in range(NSA_GROUPS):
            c0 = COL_DKV + jj * 128 + g * HEAD_DIM
            r = proj[:, c0:c0 + HEAD_DIM].reshape(ncp, NSA_CMP_STRIDE * HEAD_DIM)
            x = jnp.concatenate([r[:-1], r[1:]], axis=1)
            xs.append(jnp.pad(x, ((0, ncp - n_cmp), (0, 0))))
    return jnp.stack(xs)


def _token_mixers(u, layer, p, cfg):
    s_len = u.shape[0]
    tm = cfg["tm"]
    proj = _mm(u, p["w_in"], layer, tm=tm, tn=512, name="in_proj")

    lambda_init = 0.8 - 0.6 * math.exp(-0.3 * layer)
    sl_a = _alibi(DA_HEADS)
    units_a = [(2 * h + mp, h * 256, 256, 0,
                ((h * 128 + mp * 64, sl_a[h], ((2 * h + mp) * 128, 128), 0),))
               for h in range(DA_HEADS) for mp in range(2)]
    col = lambda blk, w: proj[:, blk * w:(blk + 1) * w]
    o_a2 = _flash(proj, _head_pad(col(BLK512["a_k"], 512), 2 * DA_HEADS, "k"),
                  _head_pad(col(BLK512["a_v"], 512), DA_HEADS, "v"), units=units_a,
                  q_spec=(512, BLK512["a_q"]), out_w=1024, tq=cfg["tq"], tk=cfg["tk"], name="diff_attn")
    o_a = _diff_final(o_a2, p["diff_lambda"], layer, p["diff_subln_g"][layer], lambda_init, tm=tm)

    kv_b = _mm(proj, p["dsa_w_ukv"], layer, tm=tm, tn=512, a_blk=BLK128["b_kv"], k=DSA_KV_RANK,
               prologue="rms", gain=p["dsa_kv_norm_g"][layer], name="dsa_kv")
    topk = min(DSA_TOPK_MAX, s_len // 4)
    mask_b = _dsa_select(proj, topk=topk)
    sl8 = _alibi(8)
    units_b = [(h, h * 128, 128, 0, ((h * 64, sl8[h], (h * 64, 64), 0),)) for h in range(DSA_HEADS)]
    o_b = _flash(proj, _head_pad(kv_b[:, :512], DSA_HEADS, "k"), _head_pad(kv_b[:, 512:], DSA_HEADS, "v"),
                 units=units_b, q_spec=(512, BLK512["b_q"]), out_w=512, tq=cfg["tq"], tk=cfg["tk"],
                 mask=mask_b.reshape(1, s_len, s_len), name="dsa_attn")

    def gqa_units(masked):
        return [(g, g * 128, 128, g if masked else 0,
                 tuple(((g * 4 + r) * 64, sl8[g * 4 + r], ((g * 4 + r) * 64, 64), g * 4 + r)
                       for r in range(4)))
                for g in range(2)]

    def gqa_kv(k_name, v_name):
        return (_head_pad(col(BLK128[k_name], 128), 2, "k"), _head_pad(col(BLK128[v_name], 128), 2, "v"))

    sinks = jnp.pad(p["swa_sinks"][layer].reshape(1, SWA_HEADS), ((0, 0), (0, LANES - SWA_HEADS)))
    o_c = _flash(proj, *gqa_kv("c_k", "c_v"), units=gqa_units(False), q_spec=(512, BLK512["c_q"]),
                 out_w=512, tq=cfg["tb"], tk=cfg["tb"], window=SWA_WINDOW, sinks=sinks, name="swa_attn")

    kv_cmp = _nsa_compress(_nsa_cmp_inputs(proj), p["nsa_cmp_pos"], p["nsa_cmp_w1"], p["nsa_cmp_w2"],
                           layer)
    o_cmp, mask_d = _nsa_cmp(proj, kv_cmp)
    o_slc = _flash(proj, *gqa_kv("d_ks", "d_vs"), units=gqa_units(True), q_spec=(512, BLK512["d_q"]),
                   out_w=512, tq=cfg["tq"], tk=cfg["tk"], mask=mask_d, name="nsa_slc_attn")
    o_win = _flash(proj, *gqa_kv("d_kw", "d_vw"), units=gqa_units(False), q_spec=(512, BLK512["d_q"]),
                   out_w=512, tq=cfg["tb"], tk=cfg["tb"], window=NSA_WINDOW, name="nsa_win_attn")
    o_d = _nsa_combine(proj, o_cmp, o_slc, o_win, tm=tm)

    merged = _merge(u, (o_a, o_b, o_c, o_d), p["w_gate"], p["w_branch"], layer,
                    tm=cfg["tm_merge"], tn=256)
    return _mm(merged, p["w_o"], layer, tm=tm, tn=512, name="out_proj")


def _config(s_len):
    return dict(tm=min(1024, s_len), tm_merge=min(512, s_len), tm_ln=min(512, s_len),
                tq=min(256, s_len), tk=min(1024, s_len), tb=min(256, s_len))


def kernel(x, c, cond_w, cond_b, w_in, diff_lambda, diff_subln_g, dsa_kv_norm_g, dsa_w_uk, dsa_w_uv,
           swa_sinks, nsa_cmp_pos, nsa_cmp_w1, nsa_cmp_w2, w_branch, w_gate, w_o,
           ln1_g, ln1_b, ln2_g, ln2_b, ffn_w_gate, ffn_w_up, ffn_w_down,
           moe_router, moe_w_gate, moe_w_up, moe_w_down):
    bsz, s_len, d = x.shape
    assert bsz == 1 and d == D_MODEL
    depth = cond_w.shape[0]
    cfg = _config(s_len)
    xs = x.reshape(s_len, d)
    c8 = jnp.broadcast_to(c.reshape(1, d), (8, d))
    p = dict(w_in=_permute_w_in(w_in), diff_lambda=diff_lambda, diff_subln_g=diff_subln_g,
             dsa_kv_norm_g=dsa_kv_norm_g, dsa_w_ukv=jnp.concatenate([dsa_w_uk, dsa_w_uv], axis=2),
             swa_sinks=swa_sinks,
             nsa_cmp_pos=nsa_cmp_pos.reshape(depth, 2, 1, NSA_CMP_LEN * HEAD_DIM),
             nsa_cmp_w1=nsa_cmp_w1, nsa_cmp_w2=nsa_cmp_w2, w_branch=w_branch, w_gate=w_gate, w_o=w_o)
    router_p = jnp.pad(moe_router, ((0, 0), (0, 0), (0, LANES - N_EXPERTS)))
    mods = [_mm(c8, cond_w, l, tm=8, tn=512, prologue="silu", bias=cond_b[l], name="cond")[0:1]
            for l in range(depth)]
    u = _modulate(xs, mods[0], 1, 0, tm=cfg["tm_ln"])
    for l in range(depth):
        y = _token_mixers(u, l, p, cfg)
        xs, u = _resid_ln(xs, y, mods[l], 2, ln1_g[l], ln1_b[l], mods[l], 4, 3, tm=cfg["tm_ln"])
        jx = l // 2
        if l % 2 == 0:
            hdn = _swiglu_up(u, ffn_w_gate, ffn_w_up, jx, tm=cfg["tm"], tn=512, name="ffn_up")
            y = _mmk(hdn, ffn_w_down, jx, tm=cfg["tm"], tn=d, tk=512, name="ffn_down")
        else:
            rt = _router(u, router_p, jx, tm=cfg["tm"])
            plan = _moe_plan(rt[:, 0:2], rt[:, 2:4], s_len)
            hdn = _gmm(_moe_gather(u, plan), (moe_w_gate, moe_w_up), jx, plan, tn=512,
                       row_w=plan["row_w"], name="moe_up")
            y = _moe_combine(_gmm(hdn, (moe_w_down,), jx, plan, tn=512, name="moe_down"), plan, s_len)
        nxt = min(l + 1, depth - 1)
        xs, u = _resid_ln(xs, y, mods[l], 5, ln2_g[l], ln2_b[l], mods[nxt], 1, 0, tm=cfg["tm_ln"])
    return xs.reshape(bsz, s_len, d)
```

```python
import functools
import math

import numpy as np
import jax
import jax.numpy as jnp
from jax import lax
from jax.experimental import pallas as pl
from jax.experimental.pallas import tpu as pltpu

F32 = jnp.float32
BF16 = jnp.bfloat16
NEG = -1e30

D_MODEL = 2048
DEPTH = 4
HEAD_DIM = 64
DA_HEADS = 4
DSA_HEADS = 8
DSA_KV_RANK = 128
IDX_HEADS = 8
DSA_TOPK_MAX = 256
SWA_HEADS = 8
SWA_WINDOW = 128
NSA_HEADS = 8
NSA_GROUPS = 2
NSA_CMP_LEN = 32
NSA_CMP_STRIDE = 16
NSA_CMP_HID = 256
NSA_SLC_LEN = 64
NSA_TOPN = 16
NSA_WINDOW = 512
NSA_FORCE = 1e9
N_BRANCH = 4
BRANCH_W = 512
N_EXPERTS = 8
ALPHA = (2.0 * DEPTH) ** 0.25

VMEM_LIMIT_BYTES = 56 * 1024 * 1024
LANES = 128

_ORIG = dict(a_q=(0, 512), a_k=(512, 512), a_v=(1024, 512), b_q=(1536, 512), b_kv=(2048, 128),
             b_iq=(2176, 512), b_ik=(2688, 64), b_iw=(2752, 8), c_q=(2760, 512), c_k=(3272, 128),
             c_v=(3400, 128), d_q=(3528, 512), d_kv=(4040, 768), d_g=(4808, 24))
_NEW_ORDER = ("a_q", "a_k", "a_v", "b_q", "b_iq", "c_q", "d_q", "b_kv", "c_k", "c_v", "d_kv",
              "b_ik", "pad64", "b_iw", "d_g", "pad96", "pad128")
PROJ_W = 5120
BLK512 = dict(a_q=0, a_k=1, a_v=2, b_q=3, b_iq=4, c_q=5, d_q=6)
BLK128 = dict(b_kv=28, c_k=29, c_v=30, d_kc=31, d_vc=32, d_ks=33, d_vs=34, d_kw=35, d_vw=36,
              b_ik=37, small=38)
COL_DKV = 3968


def _cparams(sem):
    return pltpu.CompilerParams(dimension_semantics=sem, vmem_limit_bytes=VMEM_LIMIT_BYTES)


def _sigmoid(x):
    return 1.0 / (1.0 + jnp.exp(-x))


def _silu(x):
    return x * _sigmoid(x)


def _alibi(n_heads):
    return [2.0 ** (-8.0 * (h + 1) / n_heads) for h in range(n_heads)]


def _dot(a, b):
    return jnp.dot(a, b, preferred_element_type=F32)


def _dot_nt(a, b):
    return lax.dot_general(a, b, (((1,), (1,)), ((), ())), preferred_element_type=F32)


def _mm_kernel(*refs, prologue, has_bias, eps):
    it = iter(refs)
    a_ref = next(it)
    g_ref = next(it) if prologue == "rms" else None
    w_ref = next(it)
    b_ref = next(it) if has_bias else None
    o_ref = next(it)
    wb_ref = next(it)

    @pl.when(pl.program_id(1) == 0)
    def _():
        wb_ref[...] = w_ref[0].astype(BF16)

    a = a_ref[...]
    if prologue == "silu":
        a = _silu(a.astype(F32))
    elif prologue == "rms":
        a = a.astype(F32)
        a = a * lax.rsqrt(jnp.mean(a * a, axis=-1, keepdims=True) + eps) * g_ref[...]
    acc = _dot(a.astype(BF16), wb_ref[...])
    if has_bias:
        acc = acc + b_ref[...]
    o_ref[...] = acc.astype(o_ref.dtype)


def _mm(a, w, layer, *, tm, tn, out_dtype=F32, a_blk=0, k=None, prologue=None, gain=None, bias=None,
        eps=1e-6, name="mm"):
    m = a.shape[0]
    k = a.shape[1] if k is None else k
    n = w.shape[2]
    assert w.shape[1] == k and m % tm == 0 and n % tn == 0
    in_specs = [pl.BlockSpec((tm, k), lambda j, i: (i, a_blk))]
    args = [a]
    if prologue == "rms":
        in_specs.append(pl.BlockSpec((1, k), lambda j, i: (0, 0)))
        args.append(gain.reshape(1, k))
    in_specs.append(pl.BlockSpec((1, k, tn), lambda j, i: (layer, 0, j)))
    args.append(w)
    if bias is not None:
        in_specs.append(pl.BlockSpec((1, tn), lambda j, i: (0, j)))
        args.append(bias.reshape(1, n))
    return pl.pallas_call(
        functools.partial(_mm_kernel, prologue=prologue, has_bias=bias is not None, eps=eps),
        grid=(n // tn, m // tm),
        in_specs=in_specs,
        out_specs=pl.BlockSpec((tm, tn), lambda j, i: (i, j)),
        out_shape=jax.ShapeDtypeStruct((m, n), out_dtype),
        scratch_shapes=[pltpu.VMEM((k, tn), BF16)],
        compiler_params=_cparams(("arbitrary", "arbitrary")),
        name=name,
    )(*args)


def _mmk_kernel(a_ref, w_ref, o_ref, acc_ref, *, nk):
    kk = pl.program_id(2)

    @pl.when(kk == 0)
    def _():
        acc_ref[...] = jnp.zeros_like(acc_ref)

    acc_ref[...] += _dot(a_ref[...], w_ref[0].astype(BF16))

    @pl.when(kk == nk - 1)
    def _():
        o_ref[...] = acc_ref[...]


def _mmk(a, w, layer, *, tm, tn, tk, name="mmk"):
    m, k = a.shape
    n = w.shape[2]
    assert w.shape[1] == k and m % tm == 0 and n % tn == 0 and k % tk == 0
    nk = k // tk
    return pl.pallas_call(
        functools.partial(_mmk_kernel, nk=nk),
        grid=(m // tm, n // tn, nk),
        in_specs=[pl.BlockSpec((tm, tk), lambda i, j, kk: (i, kk)),
                  pl.BlockSpec((1, tk, tn), lambda i, j, kk: (layer, kk, j))],
        out_specs=pl.BlockSpec((tm, tn), lambda i, j, kk: (i, j)),
        out_shape=jax.ShapeDtypeStruct((m, n), F32),
        scratch_shapes=[pltpu.VMEM((tm, tn), F32)],
        compiler_params=_cparams(("arbitrary", "arbitrary", "arbitrary")),
        name=name,
    )(a, w)


def _swiglu_kernel(a_ref, wg_ref, wu_ref, o_ref, wgb_ref, wub_ref):
    @pl.when(pl.program_id(1) == 0)
    def _():
        wgb_ref[...] = wg_ref[0].astype(BF16)
        wub_ref[...] = wu_ref[0].astype(BF16)

    a = a_ref[...]
    o_ref[...] = (_silu(_dot(a, wgb_ref[...])) * _dot(a, wub_ref[...])).astype(o_ref.dtype)


def _swiglu_up(u, wg, wu, layer, *, tm, tn, name="swiglu_up"):
    m, k = u.shape
    f = wg.shape[2]
    assert f % tn == 0 and m % tm == 0
    wspec = pl.BlockSpec((1, k, tn), lambda j, i: (layer, 0, j))
    return pl.pallas_call(
        _swiglu_kernel,
        grid=(f // tn, m // tm),
        in_specs=[pl.BlockSpec((tm, k), lambda j, i: (i, 0)), wspec, wspec],
        out_specs=pl.BlockSpec((tm, tn), lambda j, i: (i, j)),
        out_shape=jax.ShapeDtypeStruct((m, f), BF16),
        scratch_shapes=[pltpu.VMEM((k, tn), BF16), pltpu.VMEM((k, tn), BF16)],
        compiler_params=_cparams(("arbitrary", "arbitrary")),
        name=name,
    )(u, wg, wu)


def _modulate_kernel(x_ref, sc_ref, sh_ref, u_ref):
    u_ref[...] = (x_ref[...] * (1.0 + sc_ref[...]) + sh_ref[...]).astype(u_ref.dtype)


def _modulate(x, mod, sc_blk, sh_blk, *, tm):
    m, d = x.shape
    return pl.pallas_call(
        _modulate_kernel,
        grid=(m // tm,),
        in_specs=[pl.BlockSpec((tm, d), lambda i: (i, 0)),
                  pl.BlockSpec((1, d), lambda i: (0, sc_blk)),
                  pl.BlockSpec((1, d), lambda i: (0, sh_blk))],
        out_specs=pl.BlockSpec((tm, d), lambda i: (i, 0)),
        out_shape=jax.ShapeDtypeStruct((m, d), BF16),
        compiler_params=_cparams(("arbitrary",)),
        name="modulate",
    )(x, mod, mod)


def _resid_ln_kernel(x_ref, y_ref, gate_ref, g_ref, b_ref, sc_ref, sh_ref, xo_ref, u_ref):
    z = ALPHA * x_ref[...] + gate_ref[...] * y_ref[...]
    mu = jnp.mean(z, axis=-1, keepdims=True)
    zc = z - mu
    var = jnp.mean(zc * zc, axis=-1, keepdims=True)
    xn = zc * lax.rsqrt(var + 1e-5) * g_ref[...] + b_ref[...]
    xo_ref[...] = xn
    u_ref[...] = (xn * (1.0 + sc_ref[...]) + sh_ref[...]).astype(u_ref.dtype)


def _resid_ln(x, y, mod, gate_blk, g, b, mod_next, sc_blk, sh_blk, *, tm):
    m, d = x.shape
    row = lambda blk: pl.BlockSpec((1, d), lambda i: (0, blk))
    return pl.pallas_call(
        _resid_ln_kernel,
        grid=(m // tm,),
        in_specs=[pl.BlockSpec((tm, d), lambda i: (i, 0)),
                  pl.BlockSpec((tm, d), lambda i: (i, 0)),
                  row(gate_blk), row(0), row(0), row(sc_blk), row(sh_blk)],
        out_specs=[pl.BlockSpec((tm, d), lambda i: (i, 0)),
                   pl.BlockSpec((tm, d), lambda i: (i, 0))],
        out_shape=[jax.ShapeDtypeStruct((m, d), F32), jax.ShapeDtypeStruct((m, d), BF16)],
        compiler_params=_cparams(("arbitrary",)),
        name="resid_ln",
    )(x, y, mod, g.reshape(1, d), b.reshape(1, d), mod_next, mod_next)


FLASH_ROW_CHUNK = 32
POS_SPLIT = 128


def _head_pad(x, n_heads, kind):
    s_len = x.shape[0]
    w = x.shape[1] // n_heads
    xh = x.reshape(s_len, n_heads, w)
    pos = jnp.arange(s_len, dtype=jnp.int32)
    one = jnp.ones_like(pos)
    cols = [pos // POS_SPLIT, pos % POS_SPLIT, one, one] if kind == "k" else [one]
    tail = jnp.pad(jnp.stack(cols, axis=1).astype(F32), ((0, 0), (0, w - len(cols))))
    tail = jnp.broadcast_to(tail[:, None, :], xh.shape)
    return jnp.concatenate([xh, tail], axis=2).reshape(s_len, n_heads * 2 * w).astype(BF16)


def _flash_kernel(*refs, units, tq, tk, window, dense, n_prev, nsteps, n_mask, has_sink):
    it = iter(refs)
    q_ref, k_ref, v_ref = next(it), next(it), next(it)
    mask_ref = next(it) if n_mask else None
    sink_ref = next(it) if has_sink else None
    o_ref = next(it)
    q_scr, m_scr, acc_scr, bias_scr = (next(it) for _ in range(4))
    dv = acc_scr.shape[2]
    lcol = dv // 2

    qi = pl.program_id(0)
    j = pl.program_id(1)
    last_kb = (qi * tq + tq - 1) // tk
    kb = j if dense else qi - n_prev + j
    rows = q_scr.shape[1]
    rb = FLASH_ROW_CHUNK
    kw = 2 * HEAD_DIM

    @pl.when(j == 0)
    def _init():
        lane = lax.broadcasted_iota(jnp.int32, (tq, HEAD_DIM), 1)
        qpos = qi * tq + lax.broadcasted_iota(jnp.int32, (tq, HEAD_DIM), 0)
        qhi = (qpos // POS_SPLIT).astype(F32)
        qlo = (qpos % POS_SPLIT).astype(F32)
        for ui, (_, _, _, _, hds) in enumerate(units):
            for r, (qo, slope, _, sink_idx) in enumerate(hds):
                rsl = slice(r * tq, (r + 1) * tq)
                tail = jnp.where(lane == 0, POS_SPLIT * slope,
                                 jnp.where(lane == 1, slope,
                                           jnp.where(lane == 2, -POS_SPLIT * slope * qhi,
                                                     jnp.where(lane == 3, -slope * qlo, 0.0))))
                qs = q_ref[:, qo:qo + HEAD_DIM] * HEAD_DIM ** -0.5
                q_scr[ui, rsl] = jnp.concatenate([qs, tail], axis=1).astype(BF16)
                if has_sink:
                    m_scr[ui, rsl] = jnp.broadcast_to(sink_ref[:, sink_idx:sink_idx + 1], (tq, 1))
                else:
                    m_scr[ui, rsl] = jnp.full((tq, 1), NEG, F32)
            alane = lax.broadcasted_iota(jnp.int32, acc_scr.shape[1:], 1)
            acc_scr[ui] = jnp.where(alane == lcol, 1.0 if has_sink else 0.0, 0.0)

    def scores(ui):
        ku = units[ui][0]
        return _dot_nt(q_scr[ui], k_ref[:, ku * kw:(ku + 1) * kw])

    def step(masked):
        if masked:
            qpos = qi * tq + lax.broadcasted_iota(jnp.int32, (tq, tk), 0)
            kpos = kb * tk + lax.broadcasted_iota(jnp.int32, (tq, tk), 1)
            dist = qpos - kpos
            valid = dist >= 0
            if not dense:
                valid = valid & (dist < window)
            if n_mask:
                for g in range(n_mask):
                    bias_scr[g] = jnp.where(valid, mask_ref[g].astype(F32), NEG)
            else:
                bias_scr[0] = jnp.where(valid, 0.0, NEG)

        def chunk(s, mg, c):
            r0 = c * rb
            sc = s[r0:r0 + rb]
            if masked:
                rw = r0 % tq
                sc = sc + bias_scr[mg, rw:rw + rb]
            return sc

        s_next = scores(0)
        for ui, (_, vo, _, mg, _) in enumerate(units):
            s = s_next
            if ui + 1 < len(units):
                s_next = scores(ui + 1)
            nchunk = rows // rb
            m_old = m_scr[ui]
            m_cur = jnp.concatenate([jnp.max(chunk(s, mg, c), axis=1, keepdims=True) for c in range(nchunk)],
                                    axis=0)
            m_new = jnp.maximum(m_old, m_cur)
            alpha = jnp.exp(m_old - m_new)
            m_scr[ui] = m_new
            p_all = jnp.concatenate(
                [jnp.exp(chunk(s, mg, c) - m_new[c * rb:(c + 1) * rb]).astype(BF16) for c in range(nchunk)],
                axis=0)
            acc_scr[ui] = alpha * acc_scr[ui] + _dot(p_all, v_ref[:, vo:vo + dv])

    needed = (kb >= 0) & (kb <= last_kb)
    if dense and not n_mask:
        interior = kb * tk + tk - 1 <= qi * tq
        pl.when(needed & interior)(lambda: step(False))
        pl.when(needed & jnp.logical_not(interior))(lambda: step(True))
    else:
        pl.when(needed)(lambda: step(True))

    @pl.when(j == nsteps - 1)
    def _fin():
        for ui, (_, _, _, _, hds) in enumerate(units):
            for r, (_, _, (oo, ow), _) in enumerate(hds):
                rsl = slice(r * tq, (r + 1) * tq)
                acc = acc_scr[ui, rsl]
                o_ref[:, oo:oo + ow] = acc[:, 0:ow] / acc[:, lcol:lcol + 1]


def _flash(q_arr, k_arr, v_arr, *, units, q_spec, out_w, tq, tk, window=None,
           mask=None, sinks=None, name="flash"):
    s_len = q_arr.shape[0]
    dense = window is None
    if dense:
        n_prev, nsteps = 0, s_len // tk
    else:
        assert tq == tk
        n_prev = -(-(window - 1) // tk)
        nsteps = n_prev + 1
    n_mask = 0 if mask is None else mask.shape[0]
    dv = units[0][2]
    nu = len(units)
    rows = len(units[0][4]) * tq
    assert all(len(un[4]) * tq == rows and un[2] == dv for un in units) and rows % FLASH_ROW_CHUNK == 0

    def kv_index(qi, j):
        last_kb = (qi * tq + tq - 1) // tk
        kb = j if dense else qi - n_prev + j
        return jnp.clip(kb, 0, last_kb)

    in_specs = [pl.BlockSpec((tq, q_spec[0]), lambda qi, j: (qi, q_spec[1])),
                pl.BlockSpec((tk, k_arr.shape[1]), lambda qi, j: (kv_index(qi, j), 0)),
                pl.BlockSpec((tk, v_arr.shape[1]), lambda qi, j: (kv_index(qi, j), 0))]
    args = [q_arr, k_arr, v_arr]
    if n_mask:
        in_specs.append(pl.BlockSpec((n_mask, tq, tk), lambda qi, j: (0, qi, kv_index(qi, j))))
        args.append(mask)
    if sinks is not None:
        in_specs.append(pl.BlockSpec((1, LANES), lambda qi, j: (0, 0)))
        args.append(sinks)
    return pl.pallas_call(
        functools.partial(_flash_kernel, units=tuple(units), tq=tq, tk=tk, window=window, dense=dense,
                          n_prev=n_prev, nsteps=nsteps, n_mask=n_mask, has_sink=sinks is not None),
        grid=(s_len // tq, nsteps),
        in_specs=in_specs,
        out_specs=pl.BlockSpec((tq, out_w), lambda qi, j: (qi, 0)),
        out_shape=jax.ShapeDtypeStruct((s_len, out_w), F32),
        scratch_shapes=[pltpu.VMEM((nu, rows, 2 * HEAD_DIM), BF16), pltpu.VMEM((nu, rows, 1), F32),
                        pltpu.VMEM((nu, rows, dv), F32), pltpu.VMEM((max(n_mask, 1), tq, tk), F32)],
        compiler_params=_cparams(("arbitrary", "arbitrary")),
        name=name,
    )(*args)


def _diff_final_kernel(o_ref, lam_ref, g_ref, out_ref, *, lambda_init):
    lf = lam_ref[0]
    lam = (jnp.exp(jnp.sum(lf[0:1] * lf[1:2])) - jnp.exp(jnp.sum(lf[2:3] * lf[3:4])) + lambda_init)
    w = 2 * HEAD_DIM
    for h in range(DA_HEADS):
        o = o_ref[:, (2 * h) * w:(2 * h + 1) * w] - lam * o_ref[:, (2 * h + 1) * w:(2 * h + 2) * w]
        o = o * lax.rsqrt(jnp.mean(o * o, axis=-1, keepdims=True) + 1e-6) * g_ref[...]
        out_ref[:, h * w:(h + 1) * w] = o * (1.0 - lambda_init)


def _diff_final(o, diff_lambda, layer, subln_g, lambda_init, *, tm):
    m = o.shape[0]
    w = 2 * HEAD_DIM
    return pl.pallas_call(
        functools.partial(_diff_final_kernel, lambda_init=lambda_init),
        grid=(m // tm,),
        in_specs=[pl.BlockSpec((tm, 2 * DA_HEADS * w), lambda i: (i, 0)),
                  pl.BlockSpec((1, 4, HEAD_DIM), lambda i: (layer, 0, 0)),
                  pl.BlockSpec((1, w), lambda i: (0, 0))],
        out_specs=pl.BlockSpec((tm, DA_HEADS * w), lambda i: (i, 0)),
        out_shape=jax.ShapeDtypeStruct((m, DA_HEADS * w), F32),
        compiler_params=_cparams(("arbitrary",)),
        name="diff_final",
    )(o, diff_lambda, subln_g.reshape(1, w))


def _f32_key_const(x):
    b = int(np.array(x, np.float32).view(np.int32))
    return b ^ ((b >> 31) & 0x7FFFFFFF)


I16_MIN = -(2 ** 15)


def _dsa_select_kernel(qi_ref, w_ref, kidx_ref, mask_ref, key_scr, half_scr, j_scr, *, tq, ch, nch, topk,
                       s_len):
    i = pl.program_id(0)
    q0 = i * tq
    n_need = (q0 + tq + ch - 1) // ch
    qpos = q0 + lax.broadcasted_iota(jnp.int32, (tq, 1), 0)
    lane = lax.broadcasted_iota(jnp.int32, (1, ch), 1)
    w = w_ref[:, 0:IDX_HEADS]
    q_all = jnp.concatenate([qi_ref[:, h * HEAD_DIM:(h + 1) * HEAD_DIM] for h in range(IDX_HEADS)],
                            axis=0).astype(BF16)

    def score_chunk(c, carry):
        kc = kidx_ref[pl.ds(pl.multiple_of(c * ch, ch), ch), 0:HEAD_DIM].astype(BF16)
        lg = _dot_nt(q_all, kc)
        acc = jnp.zeros((tq, ch), F32)
        for h in range(IDX_HEADS):
            acc = acc + w[:, h:h + 1] * jnp.maximum(lg[h * tq:(h + 1) * tq], 0.0)
        acc = jnp.where(c * ch + lane <= qpos, acc, NEG) + 0.0
        bits = pltpu.bitcast(acc, jnp.int32)
        key = bits ^ ((bits >> 31) & 0x7FFFFFFF)
        key_scr[c] = key
        half_scr[c] = (key >> 16).astype(jnp.int16)
        return carry

    lax.fori_loop(0, n_need, score_chunk, 0)

    def count16(cand, strict):
        cand16 = jnp.broadcast_to(cand, (tq, LANES)).astype(jnp.int16)
        one, zero = jnp.int16(1), jnp.int16(0)

        def body(c, acc):
            blk = half_scr[c]
            for t in range(ch // LANES):
                tile = blk[:, t * LANES:(t + 1) * LANES]
                acc = acc + jnp.where(tile > cand16 if strict else tile >= cand16, one, zero)
            return acc
        acc = lax.fori_loop(0, n_need, body, jnp.zeros((tq, LANES), jnp.int16))
        return jnp.sum(acc.astype(jnp.int32), axis=1, keepdims=True)

    def search16(need_cnt):
        def bit_step(b, t):
            cand = t + jnp.left_shift(jnp.int32(1), 15 - b)
            return jnp.where(count16(cand, False) >= need_cnt, cand, t)
        return lax.fori_loop(0, 16, bit_step, jnp.full((tq, 1), I16_MIN, jnp.int32))

    t_hi = search16(topk)
    need_lo = topk - count16(t_hi, True)

    def low_chunk(c, carry):
        key = key_scr[c]
        low = (key & 0xFFFF) + I16_MIN
        half_scr[c] = jnp.where((key >> 16) == t_hi, low, I16_MIN).astype(jnp.int16)
        return carry

    lax.fori_loop(0, n_need, low_chunk, 0)
    t_lo = search16(need_lo)
    thr = jnp.left_shift(t_hi, 16) + (t_lo - I16_MIN)

    def count(pred):
        def body(c, acc):
            m = jnp.where(pred(key_scr[c], c), 1, 0)
            part = m[:, 0:LANES]
            for t in range(1, ch // LANES):
                part = part + m[:, t * LANES:(t + 1) * LANES]
            return acc + part
        acc = lax.fori_loop(0, n_need, body, jnp.zeros((tq, LANES), jnp.int32))
        return jnp.sum(acc, axis=1, keepdims=True)

    cnt_gt = count(lambda blk, c: blk > thr)
    cnt_ge = count(lambda blk, c: blk >= thr)
    need = topk - cnt_gt
    tie_rows = (cnt_ge > topk) & (thr > _f32_key_const(NEG))
    j_scr[...] = jnp.full((tq, 1), s_len, jnp.int32)
    any_tie = jnp.max(jnp.where(tie_rows, 1, 0)) > 0

    @pl.when(any_tie)
    def _ties():
        nbits = int(math.log2(s_len))

        def idx_step(b, jv):
            cand = jv + jnp.left_shift(jnp.int32(1), nbits - 1 - b)
            cnt = count(lambda blk, c: (blk == thr) & (c * ch + lane < cand))
            return jnp.where(cnt < need, cand, jv)

        jv = lax.fori_loop(0, nbits, idx_step, jnp.zeros((tq, 1), jnp.int32))
        j_scr[...] = jnp.where(tie_rows, jv, s_len)

    jv = j_scr[...]
    for c in range(nch):
        @pl.when((c < n_need) & any_tie)
        def _w():
            key = key_scr[c]
            sel = (key > thr) | ((key == thr) & (c * ch + lane <= jv))
            mask_ref[:, c * ch:(c + 1) * ch] = jnp.where(sel, 0.0, NEG).astype(mask_ref.dtype)

        @pl.when((c < n_need) & jnp.logical_not(any_tie))
        def _wf():
            mask_ref[:, c * ch:(c + 1) * ch] = jnp.where(key_scr[c] >= thr, 0.0, NEG).astype(mask_ref.dtype)

        @pl.when(c >= n_need)
        def _z():
            mask_ref[:, c * ch:(c + 1) * ch] = jnp.full((tq, ch), NEG, mask_ref.dtype)


def _dsa_select(proj, *, topk, tq=128):
    s_len = proj.shape[0]
    ch = min(1024, s_len)
    nch = s_len // ch
    return pl.pallas_call(
        functools.partial(_dsa_select_kernel, tq=tq, ch=ch, nch=nch, topk=topk, s_len=s_len),
        grid=(s_len // tq,),
        in_specs=[pl.BlockSpec((tq, 512), lambda i: (i, BLK512["b_iq"])),
                  pl.BlockSpec((tq, LANES), lambda i: (i, BLK128["small"])),
                  pl.BlockSpec((s_len, LANES), lambda i: (0, BLK128["b_ik"]))],
        out_specs=pl.BlockSpec((tq, s_len), lambda i: (i, 0)),
        out_shape=jax.ShapeDtypeStruct((s_len, s_len), BF16),
        scratch_shapes=[pltpu.VMEM((nch, tq, ch), jnp.int32), pltpu.VMEM((nch, tq, ch), jnp.int16),
                        pltpu.VMEM((tq, 1), jnp.int32)],
        compiler_params=_cparams(("arbitrary",)),
        name="dsa_select",
    )(proj, proj, proj)


def _nsa_compress_kernel(x_ref, pos_ref, w1_ref, w2_ref, o_ref):
    x = (x_ref[0] + pos_ref[0, 0]).astype(BF16)
    hdn = _silu(_dot(x, w1_ref[0, 0].astype(BF16)))
    o_ref[0] = _dot(hdn.astype(BF16), w2_ref[0, 0].astype(BF16))


def _nsa_compress(xc, pos, w1, w2, layer):
    _, ncp, kdim = xc.shape
    return pl.pallas_call(
        _nsa_compress_kernel,
        grid=(4,),
        in_specs=[pl.BlockSpec((1, ncp, kdim), lambda i: (i, 0, 0)),
                  pl.BlockSpec((1, 1, 1, kdim), lambda i: (layer, i // 2, 0, 0)),
                  pl.BlockSpec((1, 1, kdim, NSA_CMP_HID), lambda i: (layer, i // 2, 0, 0)),
                  pl.BlockSpec((1, 1, NSA_CMP_HID, HEAD_DIM), lambda i: (layer, i // 2, 0, 0))],
        out_specs=pl.BlockSpec((1, ncp, HEAD_DIM), lambda i: (i, 0, 0)),
        out_shape=jax.ShapeDtypeStruct((4, ncp, HEAD_DIM), F32),
        compiler_params=_cparams(("arbitrary",)),
        name="nsa_compress",
    )(xc, pos, w1, w2)


def _nsa_cmp_kernel(q_ref, kv_ref, ov_ref, ex_ref, o_ref, mask_ref, *, tq, ncp, n_slc, topn, ch, nch):
    i = pl.program_id(0)
    q0 = i * tq
    n_need = (q0 + tq + ch - 1) // ch
    rpg = NSA_HEADS // NSA_GROUPS
    slopes = _alibi(NSA_HEADS)
    scale = HEAD_DIM ** -0.5
    qpos_c = q0 + lax.broadcasted_iota(jnp.int32, (tq, ncp), 0)
    cend = lax.broadcasted_iota(jnp.int32, (tq, ncp), 1) * NSA_CMP_STRIDE + (NSA_CMP_LEN - 1)
    dist_c = qpos_c - cend
    valid_c = dist_c >= 0
    distf = dist_c.astype(F32)
    qpos = q0 + lax.broadcasted_iota(jnp.int32, (tq, n_slc), 0)
    blk = lax.broadcasted_iota(jnp.int32, (tq, n_slc), 1)
    cur = qpos // NSA_SLC_LEN
    forced = (blk == 0) | (blk == cur) | (blk == cur - 1)
    blk_ok = blk * NSA_SLC_LEN <= qpos
    ov = ov_ref[...]
    imps = []
    for g in range(NSA_GROUPS):
        kc = kv_ref[g].astype(BF16)
        vc = kv_ref[NSA_GROUPS + g].astype(BF16)
        psum = jnp.zeros((tq, ncp), F32)
        for r in range(rpg):
            h = g * rpg + r
            qh = q_ref[:, h * HEAD_DIM:(h + 1) * HEAD_DIM].astype(BF16)
            s = _dot_nt(qh, kc) * scale - slopes[h] * distf
            s = jnp.where(valid_c, s, NEG)
            e = jnp.where(valid_c, jnp.exp(s - jnp.max(s, axis=1, keepdims=True)), 0.0)
            p = e / jnp.maximum(jnp.sum(e, axis=1, keepdims=True), 1e-30)
            o_ref[:, h * HEAD_DIM:(h + 1) * HEAD_DIM] = _dot(p.astype(BF16), vc)
            psum = psum + p
        p_hi = psum.astype(BF16)
        p_lo = (psum - p_hi.astype(F32)).astype(BF16)
        imp = _dot(p_hi, ov) + _dot(p_lo, ov)
        imp = jnp.where(forced, NSA_FORCE, imp)
        imps.append(jnp.where(blk_ok, imp, NEG))
    imps = [imp.T for imp in imps]
    blk_t = lax.broadcasted_iota(jnp.int32, (n_slc, tq), 0)
    sels = [jnp.full((n_slc, tq), NEG, F32) for _ in range(NSA_GROUPS)]
    for _ in range(topn):
        for g in range(NSA_GROUPS):
            mx = jnp.max(imps[g], axis=0, keepdims=True)
            first = jnp.min(jnp.where(imps[g] == mx, blk_t, n_slc), axis=0, keepdims=True)
            hit = blk_t == first
            sels[g] = jnp.where(hit, 0.0, sels[g])
            imps[g] = jnp.where(hit, -jnp.inf, imps[g])
    sels = [sel.T for sel in sels]
    for g in range(NSA_GROUPS):
        selb = sels[g].astype(BF16)
        for c in range(nch):
            @pl.when(c < n_need)
            def _w():
                tok = _dot(selb, ex_ref[:, c * ch:(c + 1) * ch])
                mask_ref[g, :, c * ch:(c + 1) * ch] = tok.astype(mask_ref.dtype)

            @pl.when(c >= n_need)
            def _z():
                mask_ref[g, :, c * ch:(c + 1) * ch] = jnp.full((tq, ch), NEG, mask_ref.dtype)


def _nsa_cmp(proj, kv_cmp, *, tq=128):
    s_len = proj.shape[0]
    ncp = kv_cmp.shape[1]
    n_slc = s_len // NSA_SLC_LEN
    topn = min(NSA_TOPN, n_slc)
    ch = min(1024, s_len)
    nch = s_len // ch
    starts = np.arange(ncp) * NSA_CMP_STRIDE
    slc_start = np.arange(n_slc) * NSA_SLC_LEN
    overlap = ((starts[:, None] < slc_start[None, :] + NSA_SLC_LEN)
               & (starts[:, None] + NSA_CMP_LEN > slc_start[None, :])).astype(np.float32)
    expand = (np.arange(s_len)[None, :] // NSA_SLC_LEN == np.arange(n_slc)[:, None]).astype(np.float32)
    return pl.pallas_call(
        functools.partial(_nsa_cmp_kernel, tq=tq, ncp=ncp, n_slc=n_slc, topn=topn, ch=ch, nch=nch),
        grid=(s_len // tq,),
        in_specs=[pl.BlockSpec((tq, 512), lambda i: (i, BLK512["d_q"])),
                  pl.BlockSpec((4, ncp, HEAD_DIM), lambda i: (0, 0, 0)),
                  pl.BlockSpec((ncp, n_slc), lambda i: (0, 0)),
                  pl.BlockSpec((n_slc, s_len), lambda i: (0, 0))],
        out_specs=[pl.BlockSpec((tq, 512), lambda i: (i, 0)),
                   pl.BlockSpec((NSA_GROUPS, tq, s_len), lambda i: (0, i, 0))],
        out_shape=[jax.ShapeDtypeStruct((s_len, 512), F32),
                   jax.ShapeDtypeStruct((NSA_GROUPS, s_len, s_len), BF16)],
        compiler_params=_cparams(("arbitrary",)),
        name="nsa_cmp",
    )(proj, kv_cmp, jnp.asarray(overlap, BF16), jnp.asarray(expand, BF16))


def _nsa_combine_kernel(g_ref, oc_ref, os_ref, ow_ref, o_ref):
    gt = _sigmoid(g_ref[...])
    for h in range(NSA_HEADS):
        sl = slice(h * HEAD_DIM, (h + 1) * HEAD_DIM)
        c0 = IDX_HEADS + 3 * h
        o_ref[:, sl] = (gt[:, c0:c0 + 1] * oc_ref[:, sl] + gt[:, c0 + 1:c0 + 2] * os_ref[:, sl]
                        + gt[:, c0 + 2:c0 + 3] * ow_ref[:, sl])


def _nsa_combine(proj, o_cmp, o_slc, o_win, *, tm):
    m = proj.shape[0]
    spec = pl.BlockSpec((tm, 512), lambda i: (i, 0))
    return pl.pallas_call(
        _nsa_combine_kernel,
        grid=(m // tm,),
        in_specs=[pl.BlockSpec((tm, LANES), lambda i: (i, BLK128["small"])), spec, spec, spec],
        out_specs=spec,
        out_shape=jax.ShapeDtypeStruct((m, 512), F32),
        compiler_params=_cparams(("arbitrary",)),
        name="nsa_combine",
    )(proj, o_cmp, o_slc, o_win)


def _merge_kernel(u_ref, oa_ref, ob_ref, oc_ref, od_ref, wg0, wg1, wg2, wg3, wb_ref, o_ref,
                  wgb_ref, wbb_ref):
    wgs = (wg0, wg1, wg2, wg3)

    @pl.when(pl.program_id(1) == 0)
    def _():
        for mch in range(N_BRANCH):
            wgb_ref[mch] = wgs[mch][0].astype(BF16)
            wbb_ref[mch] = wb_ref[0, mch].astype(BF16)

    u = u_ref[...]
    acc = None
    for mch, o_ref_m in enumerate((oa_ref, ob_ref, oc_ref, od_ref)):
        gte = _sigmoid(_dot(u, wgb_ref[mch]))
        z = _dot(o_ref_m[...].astype(BF16), wbb_ref[mch])
        acc = gte * z if acc is None else acc + gte * z
    o_ref[...] = acc.astype(o_ref.dtype)


def _merge(u, branches, w_gate, w_branch, layer, *, tm, tn):
    m, d = u.shape
    nj = d // tn
    bspec = pl.BlockSpec((tm, BRANCH_W), lambda j, i: (i, 0))
    wg_specs = [pl.BlockSpec((1, d, tn),
                             functools.partial(lambda j, i, mch: (layer, 0, mch * nj + j), mch=mch))
                for mch in range(N_BRANCH)]
    return pl.pallas_call(
        _merge_kernel,
        grid=(nj, m // tm),
        in_specs=[pl.BlockSpec((tm, d), lambda j, i: (i, 0)), bspec, bspec, bspec, bspec,
                  *wg_specs,
                  pl.BlockSpec((1, N_BRANCH, BRANCH_W, tn), lambda j, i: (layer, 0, 0, j))],
        out_specs=pl.BlockSpec((tm, tn), lambda j, i: (i, j)),
        out_shape=jax.ShapeDtypeStruct((m, d), BF16),
        scratch_shapes=[pltpu.VMEM((N_BRANCH, d, tn), BF16), pltpu.VMEM((N_BRANCH, BRANCH_W, tn), BF16)],
        compiler_params=_cparams(("arbitrary", "arbitrary")),
        name="merge",
    )(u, *branches, w_gate, w_gate, w_gate, w_gate, w_branch)


def _router_kernel(u_ref, r_ref, o_ref):
    logits = _dot(u_ref[...], r_ref[0].astype(BF16))
    lane = lax.broadcasted_iota(jnp.int32, logits.shape, 1)
    lg = jnp.where(lane < N_EXPERTS, logits, -jnp.inf)
    m1 = jnp.max(lg, axis=1, keepdims=True)
    i1 = jnp.min(jnp.where(lg == m1, lane, LANES), axis=1, keepdims=True)
    lg2 = jnp.where(lane == i1, -jnp.inf, lg)
    m2 = jnp.max(lg2, axis=1, keepdims=True)
    i2 = jnp.min(jnp.where(lg2 == m2, lane, LANES), axis=1, keepdims=True)
    e2 = jnp.exp(m2 - m1)
    w1 = 1.0 / (1.0 + e2)
    w2 = e2 / (1.0 + e2)
    o_ref[...] = jnp.where(lane == 0, i1.astype(F32),
                           jnp.where(lane == 1, i2.astype(F32),
                                     jnp.where(lane == 2, w1, jnp.where(lane == 3, w2, 0.0))))


def _router(u, router_padded, layer, *, tm):
    m, d = u.shape
    return pl.pallas_call(
        _router_kernel,
        grid=(m // tm,),
        in_specs=[pl.BlockSpec((tm, d), lambda i: (i, 0)),
                  pl.BlockSpec((1, d, LANES), lambda i: (layer, 0, 0))],
        out_specs=pl.BlockSpec((tm, LANES), lambda i: (i, 0)),
        out_shape=jax.ShapeDtypeStruct((m, LANES), F32),
        compiler_params=_cparams(("arbitrary",)),
        name="router",
    )(u, router_padded)


MOE_ROW_TILE = 256
MOE_TOK_CHUNK = 256


def _moe_plan(ridx, rw, s_len):
    tm, ct, n_e = MOE_ROW_TILE, MOE_TOK_CHUNK, N_EXPERTS
    i32 = jnp.int32
    e_a = ridx.reshape(-1).astype(i32)
    oh = (e_a[:, None] == jnp.arange(n_e, dtype=i32)[None, :]).astype(i32)
    csum = jnp.cumsum(oh, axis=0)
    rank_a = jnp.sum((csum - oh) * oh, axis=1)
    ntile_e = (csum[-1] + tm - 1) // tm
    tile_end = jnp.cumsum(ntile_e)
    pos_a = jnp.take(tile_end - ntile_e, e_a) * tm + rank_a
    n_rows = 2 * s_len + n_e * tm
    n_tiles = n_rows // tm
    n_chunks = s_len // ct
    row_tok = jnp.full((n_rows,), -1, i32).at[pos_a].set(jnp.arange(2 * s_len, dtype=i32) // 2)
    row_w = jnp.zeros((n_rows,), F32).at[pos_a].set(rw.reshape(-1))
    tile_e = jnp.minimum(jnp.searchsorted(tile_end, jnp.arange(n_tiles, dtype=i32), side="right"),
                         n_e - 1).astype(i32)
    rt = row_tok.reshape(n_tiles, tm)
    lo = jnp.min(jnp.where(rt >= 0, rt, s_len - 1), axis=1) // ct
    hi = jnp.maximum(jnp.max(jnp.where(rt >= 0, rt, 0), axis=1) // ct, lo)
    n_i = hi - lo + 1
    end = jnp.cumsum(n_i)
    n_work = n_tiles + n_e * n_chunks
    w = jnp.arange(n_work, dtype=i32)
    wt = jnp.minimum(jnp.searchsorted(end, w, side="right"), n_tiles - 1).astype(i32)
    wc = jnp.clip(jnp.take(lo, wt) + w - jnp.take(end - n_i, wt), 0, n_chunks - 1).astype(i32)
    wa = (w < end[-1]).astype(i32)
    order = jnp.argsort(jnp.where(wa > 0, wc * n_tiles + wt, n_chunks * n_tiles + w))
    vc = jnp.where(wa > 0, wc, n_chunks - 1)[order]
    return dict(row_tok=row_tok, row_w=row_w, tile_e=tile_e, n_tiles=n_tiles, n_work=n_work,
                gather=(wt, wc, wa), combine=(vc, wt[order], wa[order]))


def _moe_gather_kernel(wt_ref, wc_ref, wa_ref, tok_ref, u_ref, o_ref):
    w = pl.program_id(0)

    @pl.when((w == 0) | (wt_ref[jnp.maximum(w - 1, 0)] != wt_ref[w]))
    def _():
        o_ref[...] = jnp.zeros_like(o_ref)

    @pl.when(wa_ref[w] > 0)
    def _():
        ct = u_ref.shape[0]
        cols = wc_ref[w] * ct + lax.broadcasted_iota(jnp.int32, (1, ct), 1)
        onehot = jnp.where(tok_ref[...] == cols, 1.0, 0.0).astype(BF16)
        o_ref[...] += _dot(onehot, u_ref[...]).astype(o_ref.dtype)


def _moe_gather(u, plan):
    s_len, d = u.shape
    tm, ct = MOE_ROW_TILE, MOE_TOK_CHUNK
    n_rows = plan["row_tok"].shape[0]
    return pl.pallas_call(
        _moe_gather_kernel,
        grid_spec=pltpu.PrefetchScalarGridSpec(
            num_scalar_prefetch=3, grid=(plan["n_work"],),
            in_specs=[pl.BlockSpec((tm, 1), lambda w, wt, wc, wa: (wt[w], 0)),
                      pl.BlockSpec((ct, d), lambda w, wt, wc, wa: (wc[w], 0))],
            out_specs=pl.BlockSpec((tm, d), lambda w, wt, wc, wa: (wt[w], 0))),
        out_shape=jax.ShapeDtypeStruct((n_rows, d), BF16),
        compiler_params=_cparams(("arbitrary",)),
        name="moe_gather",
    )(*plan["gather"], plan["row_tok"].reshape(n_rows, 1), u)


def _moe_combine_kernel(vc_ref, vt_ref, va_ref, tok_ref, y0_ref, y1_ref, y2_ref, o_ref):
    w = pl.program_id(0)
    chunk = vc_ref[w]

    @pl.when((w == 0) | (vc_ref[jnp.maximum(w - 1, 0)] != chunk))
    def _():
        o_ref[...] = jnp.zeros_like(o_ref)

    @pl.when(va_ref[w] > 0)
    def _():
        ct = o_ref.shape[0]
        rows = chunk * ct + lax.broadcasted_iota(jnp.int32, (ct, 1), 0)
        onehot_t = jnp.where(rows == tok_ref[0], 1.0, 0.0).astype(BF16)
        o_ref[...] += (_dot(onehot_t, y0_ref[...]) + _dot(onehot_t, y1_ref[...])
                       + _dot(onehot_t, y2_ref[...]))


def _moe_combine(ys3, plan, s_len):
    n_rows, d = ys3[0].shape
    tm, ct = MOE_ROW_TILE, MOE_TOK_CHUNK
    yspec = pl.BlockSpec((tm, d), lambda w, vc, vt, va: (vt[w], 0))
    return pl.pallas_call(
        _moe_combine_kernel,
        grid_spec=pltpu.PrefetchScalarGridSpec(
            num_scalar_prefetch=3, grid=(plan["n_work"],),
            in_specs=[pl.BlockSpec((1, 1, tm), lambda w, vc, vt, va: (vt[w], 0, 0)), yspec, yspec, yspec],
            out_specs=pl.BlockSpec((ct, d), lambda w, vc, vt, va: (vc[w], 0))),
        out_shape=jax.ShapeDtypeStruct((s_len, d), F32),
        compiler_params=_cparams(("arbitrary",)),
        name="moe_combine",
    )(*plan["combine"], plan["row_tok"].reshape(plan["n_tiles"], 1, tm), *ys3)


def _gmm_kernel(te_ref, *refs, swiglu):
    it = iter(refs)
    a_ref = next(it)
    w_refs = [next(it), next(it)] if swiglu else [next(it)]
    rw_ref = next(it) if swiglu else None
    o_refs = [next(it)] if swiglu else [next(it), next(it), next(it)]
    wb_refs = [next(it) for _ in w_refs]
    i = pl.program_id(1)

    @pl.when((i == 0) | (te_ref[i] != te_ref[jnp.maximum(i - 1, 0)]))
    def _():
        for w_ref, wb_ref in zip(w_refs, wb_refs):
            wb_ref[...] = w_ref[0, 0].astype(BF16)

    a = a_ref[...]
    if swiglu:
        h = _silu(_dot(a, wb_refs[0][...])) * _dot(a, wb_refs[1][...]) * rw_ref[...]
        o_refs[0][...] = h.astype(BF16)
    else:
        y = _dot(a, wb_refs[0][...])
        hi = y.astype(BF16)
        r1 = y - hi.astype(F32)
        mid = r1.astype(BF16)
        o_refs[0][...] = hi
        o_refs[1][...] = mid
        o_refs[2][...] = (r1 - mid.astype(F32)).astype(BF16)


def _gmm(a, ws, layer, plan, *, tn, row_w=None, name="gmm"):
    n_rows, k = a.shape
    n = ws[0].shape[3]
    tm = MOE_ROW_TILE
    swiglu = len(ws) == 2
    wspec = pl.BlockSpec((1, 1, k, tn), lambda j, i, te: (layer, te[i], 0, j))
    in_specs = [pl.BlockSpec((tm, k), lambda j, i, te: (i, 0))] + [wspec] * len(ws)
    args = [a, *ws]
    if swiglu:
        in_specs.append(pl.BlockSpec((tm, 1), lambda j, i, te: (i, 0)))
        args.append(row_w.reshape(n_rows, 1))
    ospec = pl.BlockSpec((tm, tn), lambda j, i, te: (i, j))
    oshape = jax.ShapeDtypeStruct((n_rows, n), BF16)
    return pl.pallas_call(
        functools.partial(_gmm_kernel, swiglu=swiglu),
        grid_spec=pltpu.PrefetchScalarGridSpec(
            num_scalar_prefetch=1, grid=(n // tn, n_rows // tm),
            in_specs=in_specs,
            out_specs=ospec if swiglu else [ospec] * 3,
            scratch_shapes=[pltpu.VMEM((k, tn), BF16) for _ in ws]),
        out_shape=oshape if swiglu else [oshape] * 3,
        compiler_params=_cparams(("arbitrary", "arbitrary")),
        name=name,
    )(plan["tile_e"], *args)


def _permute_w_in(w):
    cols = []
    for nm in _NEW_ORDER:
        if nm.startswith("pad"):
            cols.append(jnp.zeros(w.shape[:2] + (int(nm[3:]),), w.dtype))
        else:
            o, n = _ORIG[nm]
            cols.append(w[:, :, o:o + n])
    out = jnp.concatenate(cols, axis=2)
    assert out.shape[2] == PROJ_W
    return out


def _nsa_cmp_inputs(proj):
    s_len = proj.shape[0]
    n_cmp = (s_len - NSA_CMP_LEN) // NSA_CMP_STRIDE + 1
    ncp = s_len // NSA_CMP_STRIDE
    xs = []
    for jj in range(2):
        for g in range(NSA_GROUPS):
            c0 = COL_DKV + jj * 128 + g * HEAD_DIM
            r = proj[:, c0:c0 + HEAD_DIM].reshape(ncp, NSA_CMP_STRIDE * HEAD_DIM)
            x = jnp.concatenate([r[:-1], r[1:]], axis=1)
            xs.append(jnp.pad(x, ((0, ncp - n_cmp), (0, 0))))
    return jnp.stack(xs)


def _token_mixers(u, layer, p, cfg):
    s_len = u.shape[0]
    tm = cfg["tm"]
    proj = _mm(u, p["w_in"], layer, tm=tm, tn=512, name="in_proj")

    lambda_init = 0.8 - 0.6 * math.exp(-0.3 * layer)
    sl_a = _alibi(DA_HEADS)
    units_a = [(2 * h + mp, h * 256, 256, 0,
                ((h * 128 + mp * 64, sl_a[h], ((2 * h + mp) * 128, 128), 0),))
               for h in range(DA_HEADS) for mp in range(2)]
    col = lambda blk, w: proj[:, blk * w:(blk + 1) * w]
    o_a2 = _flash(proj, _head_pad(col(BLK512["a_k"], 512), 2 * DA_HEADS, "k"),
                  _head_pad(col(BLK512["a_v"], 512), DA_HEADS, "v"), units=units_a,
                  q_spec=(512, BLK512["a_q"]), out_w=1024, tq=cfg["tq"], tk=cfg["tk"], name="diff_attn")
    o_a = _diff_final(o_a2, p["diff_lambda"], layer, p["diff_subln_g"][layer], lambda_init, tm=tm)

    kv_b = _mm(proj, p["dsa_w_ukv"], layer, tm=tm, tn=512, a_blk=BLK128["b_kv"], k=DSA_KV_RANK,
               prologue="rms", gain=p["dsa_kv_norm_g"][layer], name="dsa_kv")
    topk = min(DSA_TOPK_MAX, s_len // 4)
    mask_b = _dsa_select(proj, topk=topk)
    sl8 = _alibi(8)
    units_b = [(h, h * 128, 128, 0, ((h * 64, sl8[h], (h * 64, 64), 0),)) for h in range(DSA_HEADS)]
    o_b = _flash(proj, _head_pad(kv_b[:, :512], DSA_HEADS, "k"), _head_pad(kv_b[:, 512:], DSA_HEADS, "v"),
                 units=units_b, q_spec=(512, BLK512["b_q"]), out_w=512, tq=cfg["tq"], tk=cfg["tk"],
                 mask=mask_b.reshape(1, s_len, s_len), name="dsa_attn")

    def gqa_units(masked):
        return [(g, g * 128, 128, g if masked else 0,
                 tuple(((g * 4 + r) * 64, sl8[g * 4 + r], ((g * 4 + r) * 64, 64), g * 4 + r)
                       for r in range(4)))
                for g in range(2)]

    def gqa_kv(k_name, v_name):
        return (_head_pad(col(BLK128[k_name], 128), 2, "k"), _head_pad(col(BLK128[v_name], 128), 2, "v"))

    sinks = jnp.pad(p["swa_sinks"][layer].reshape(1, SWA_HEADS), ((0, 0), (0, LANES - SWA_HEADS)))
    o_c = _flash(proj, *gqa_kv("c_k", "c_v"), units=gqa_units(False), q_spec=(512, BLK512["c_q"]),
                 out_w=512, tq=cfg["tb"], tk=cfg["tb"], window=SWA_WINDOW, sinks=sinks, name="swa_attn")

    kv_cmp = _nsa_compress(_nsa_cmp_inputs(proj), p["nsa_cmp_pos"], p["nsa_cmp_w1"], p["nsa_cmp_w2"],
                           layer)
    o_cmp, mask_d = _nsa_cmp(proj, kv_cmp)
    o_slc = _flash(proj, *gqa_kv("d_ks", "d_vs"), units=gqa_units(True), q_spec=(512, BLK512["d_q"]),
                   out_w=512, tq=cfg["tq"], tk=cfg["tk"], mask=mask_d, name="nsa_slc_attn")
    o_win = _flash(proj, *gqa_kv("d_kw", "d_vw"), units=gqa_units(False), q_spec=(512, BLK512["d_q"]),
                   out_w=512, tq=cfg["tb"], tk=cfg["tb"], window=NSA_WINDOW, name="nsa_win_attn")
    o_d = _nsa_combine(proj, o_cmp, o_slc, o_win, tm=tm)

    merged = _merge(u, (o_a, o_b, o_c, o_d), p["w_gate"], p["w_branch"], layer,
                    tm=cfg["tm_merge"], tn=256)
    return _mm(merged, p["w_o"], layer, tm=tm, tn=512, name="out_proj")


def _config(s_len):
    return dict(tm=min(1024, s_len), tm_merge=min(512, s_len), tm_ln=min(512, s_len),
                tq=min(256, s_len), tk=min(1024, s_len), tb=min(256, s_len))


def kernel(x, c, cond_w, cond_b, w_in, diff_lambda, diff_subln_g, dsa_kv_norm_g, dsa_w_uk, dsa_w_uv,
           swa_sinks, nsa_cmp_pos, nsa_cmp_w1, nsa_cmp_w2, w_branch, w_gate, w_o,
           ln1_g, ln1_b, ln2_g, ln2_b, ffn_w_gate, ffn_w_up, ffn_w_down,
           moe_router, moe_w_gate, moe_w_up, moe_w_down):
    bsz, s_len, d = x.shape
    assert bsz == 1 and d == D_MODEL
    depth = cond_w.shape[0]
    cfg = _config(s_len)
    xs = x.reshape(s_len, d)
    c8 = jnp.broadcast_to(c.reshape(1, d), (8, d))
    p = dict(w_in=_permute_w_in(w_in), diff_lambda=diff_lambda, diff_subln_g=diff_subln_g,
             dsa_kv_norm_g=dsa_kv_norm_g, dsa_w_ukv=jnp.concatenate([dsa_w_uk, dsa_w_uv], axis=2),
             swa_sinks=swa_sinks,
             nsa_cmp_pos=nsa_cmp_pos.reshape(depth, 2, 1, NSA_CMP_LEN * HEAD_DIM),
             nsa_cmp_w1=nsa_cmp_w1, nsa_cmp_w2=nsa_cmp_w2, w_branch=w_branch, w_gate=w_gate, w_o=w_o)
    router_p = jnp.pad(moe_router, ((0, 0), (0, 0), (0, LANES - N_EXPERTS)))
    mods = [_mm(c8, cond_w, l, tm=8, tn=512, prologue="silu", bias=cond_b[l], name="cond")[0:1]
            for l in range(depth)]
    u = _modulate(xs, mods[0], 1, 0, tm=cfg["tm_ln"])
    for l in range(depth):
        y = _token_mixers(u, l, p, cfg)
        xs, u = _resid_ln(xs, y, mods[l], 2, ln1_g[l], ln1_b[l], mods[l], 4, 3, tm=cfg["tm_ln"])
        jx = l // 2
        if l % 2 == 0:
            hdn = _swiglu_up(u, ffn_w_gate, ffn_w_up, jx, tm=cfg["tm"], tn=512, name="ffn_up")
            y = _mmk(hdn, ffn_w_down, jx, tm=cfg["tm"], tn=d, tk=512, name="ffn_down")
        else:
            rt = _router(u, router_p, jx, tm=cfg["tm"])
            plan = _moe_plan(rt[:, 0:2], rt[:, 2:4], s_len)
            hdn = _gmm(_moe_gather(u, plan), (moe_w_gate, moe_w_up), jx, plan, tn=512,
                       row_w=plan["row_w"], name="moe_up")
            y = _moe_combine(_gmm(hdn, (moe_w_down,), jx, plan, tn=512, name="moe_down"), plan, s_len)
        nxt = min(l + 1, depth - 1)
        xs, u = _resid_ln(xs, y, mods[l], 5, ln2_g[l], ln2_b[l], mods[nxt], 1, 0, tm=cfg["tm_ln"])
    return xs.reshape(bsz, s_len, d)
```

```python
import functools
import math

import numpy as np
import jax
import jax.numpy as jnp
from jax import lax
from jax.experimental import pallas as pl
from jax.experimental.pallas import tpu as pltpu

F32 = jnp.float32
BF16 = jnp.bfloat16
NEG = -1e30

D_MODEL = 2048
DEPTH = 4
HEAD_DIM = 64
DA_HEADS = 4
DSA_HEADS = 8
DSA_KV_RANK = 128
IDX_HEADS = 8
DSA_TOPK_MAX = 256
SWA_HEADS = 8
SWA_WINDOW = 128
NSA_HEADS = 8
NSA_GROUPS = 2
NSA_CMP_LEN = 32
NSA_CMP_STRIDE = 16
NSA_CMP_HID = 256
NSA_SLC_LEN = 64
NSA_TOPN = 16
NSA_WINDOW = 512
NSA_FORCE = 1e9
N_BRANCH = 4
BRANCH_W = 512
N_EXPERTS = 8
ALPHA = (2.0 * DEPTH) ** 0.25

VMEM_LIMIT_BYTES = 56 * 1024 * 1024
LANES = 128

_ORIG = dict(a_q=(0, 512), a_k=(512, 512), a_v=(1024, 512), b_q=(1536, 512), b_kv=(2048, 128),
             b_iq=(2176, 512), b_ik=(2688, 64), b_iw=(2752, 8), c_q=(2760, 512), c_k=(3272, 128),
             c_v=(3400, 128), d_q=(3528, 512), d_kv=(4040, 768), d_g=(4808, 24))
_NEW_ORDER = ("a_q", "a_k", "a_v", "b_q", "b_iq", "c_q", "d_q", "b_kv", "c_k", "c_v", "d_kv",
              "b_ik", "pad64", "b_iw", "d_g", "pad96", "pad128")
PROJ_W = 5120
BLK512 = dict(a_q=0, a_k=1, a_v=2, b_q=3, b_iq=4, c_q=5, d_q=6)
BLK128 = dict(b_kv=28, c_k=29, c_v=30, d_kc=31, d_vc=32, d_ks=33, d_vs=34, d_kw=35, d_vw=36,
              b_ik=37, small=38)
COL_DKV = 3968


def _cparams(sem):
    return pltpu.CompilerParams(dimension_semantics=sem, vmem_limit_bytes=VMEM_LIMIT_BYTES)


def _sigmoid(x):
    return 1.0 / (1.0 + jnp.exp(-x))


def _silu(x):
    return x * _sigmoid(x)


def _alibi(n_heads):
    return [2.0 ** (-8.0 * (h + 1) / n_heads) for h in range(n_heads)]


def _dot(a, b):
    return jnp.dot(a, b, preferred_element_type=F32)


def _dot_nt(a, b):
    return lax.dot_general(a, b, (((1,), (1,)), ((), ())), preferred_element_type=F32)


def _mm_kernel(*refs, prologue, has_bias, eps):
    it = iter(refs)
    a_ref = next(it)
    g_ref = next(it) if prologue == "rms" else None
    w_ref = next(it)
    b_ref = next(it) if has_bias else None
    o_ref = next(it)
    wb_ref = next(it)

    @pl.when(pl.program_id(1) == 0)
    def _():
        wb_ref[...] = w_ref[0].astype(BF16)

    a = a_ref[...]
    if prologue == "silu":
        a = _silu(a.astype(F32))
    elif prologue == "rms":
        a = a.astype(F32)
        a = a * lax.rsqrt(jnp.mean(a * a, axis=-1, keepdims=True) + eps) * g_ref[...]
    acc = _dot(a.astype(BF16), wb_ref[...])
    if has_bias:
        acc = acc + b_ref[...]
    o_ref[...] = acc.astype(o_ref.dtype)


def _mm(a, w, layer, *, tm, tn, out_dtype=F32, a_blk=0, k=None, prologue=None, gain=None, bias=None,
        eps=1e-6, name="mm"):
    m = a.shape[0]
    k = a.shape[1] if k is None else k
    n = w.shape[2]
    assert w.shape[1] == k and m % tm == 0 and n % tn == 0
    in_specs = [pl.BlockSpec((tm, k), lambda j, i: (i, a_blk))]
    args = [a]
    if prologue == "rms":
        in_specs.append(pl.BlockSpec((1, k), lambda j, i: (0, 0)))
        args.append(gain.reshape(1, k))
    in_specs.append(pl.BlockSpec((1, k, tn), lambda j, i: (layer, 0, j)))
    args.append(w)
    if bias is not None:
        in_specs.append(pl.BlockSpec((1, tn), lambda j, i: (0, j)))
        args.append(bias.reshape(1, n))
    return pl.pallas_call(
        functools.partial(_mm_kernel, prologue=prologue, has_bias=bias is not None, eps=eps),
        grid=(n // tn, m // tm),
        in_specs=in_specs,
        out_specs=pl.BlockSpec((tm, tn), lambda j, i: (i, j)),
        out_shape=jax.ShapeDtypeStruct((m, n), out_dtype),
        scratch_shapes=[pltpu.VMEM((k, tn), BF16)],
        compiler_params=_cparams(("arbitrary", "arbitrary")),
        name=name,
    )(*args)


def _mmk_kernel(a_ref, w_ref, o_ref, acc_ref, *, nk):
    kk = pl.program_id(2)

    @pl.when(kk == 0)
    def _():
        acc_ref[...] = jnp.zeros_like(acc_ref)

    acc_ref[...] += _dot(a_ref[...], w_ref[0].astype(BF16))

    @pl.when(kk == nk - 1)
    def _():
        o_ref[...] = acc_ref[...]


def _mmk(a, w, layer, *, tm, tn, tk, name="mmk"):
    m, k = a.shape
    n = w.shape[2]
    assert w.shape[1] == k and m % tm == 0 and n % tn == 0 and k % tk == 0
    nk = k // tk
    return pl.pallas_call(
        functools.partial(_mmk_kernel, nk=nk),
        grid=(m // tm, n // tn, nk),
        in_specs=[pl.BlockSpec((tm, tk), lambda i, j, kk: (i, kk)),
                  pl.BlockSpec((1, tk, tn), lambda i, j, kk: (layer, kk, j))],
        out_specs=pl.BlockSpec((tm, tn), lambda i, j, kk: (i, j)),
        out_shape=jax.ShapeDtypeStruct((m, n), F32),
        scratch_shapes=[pltpu.VMEM((tm, tn), F32)],
        compiler_params=_cparams(("arbitrary", "arbitrary", "arbitrary")),
        name=name,
    )(a, w)


def _swiglu_kernel(a_ref, wg_ref, wu_ref, o_ref, wgb_ref, wub_ref):
    @pl.when(pl.program_id(1) == 0)
    def _():
        wgb_ref[...] = wg_ref[0].astype(BF16)
        wub_ref[...] = wu_ref[0].astype(BF16)

    a = a_ref[...]
    o_ref[...] = (_silu(_dot(a, wgb_ref[...])) * _dot(a, wub_ref[...])).astype(o_ref.dtype)


def _swiglu_up(u, wg, wu, layer, *, tm, tn, name="swiglu_up"):
    m, k = u.shape
    f = wg.shape[2]
    assert f % tn == 0 and m % tm == 0
    wspec = pl.BlockSpec((1, k, tn), lambda j, i: (layer, 0, j))
    return pl.pallas_call(
        _swiglu_kernel,
        grid=(f // tn, m // tm),
        in_specs=[pl.BlockSpec((tm, k), lambda j, i: (i, 0)), wspec, wspec],
        out_specs=pl.BlockSpec((tm, tn), lambda j, i: (i, j)),
        out_shape=jax.ShapeDtypeStruct((m, f), BF16),
        scratch_shapes=[pltpu.VMEM((k, tn), BF16), pltpu.VMEM((k, tn), BF16)],
        compiler_params=_cparams(("arbitrary", "arbitrary")),
        name=name,
    )(u, wg, wu)


def _modulate_kernel(x_ref, sc_ref, sh_ref, u_ref):
    u_ref[...] = (x_ref[...] * (1.0 + sc_ref[...]) + sh_ref[...]).astype(u_ref.dtype)


def _modulate(x, mod, sc_blk, sh_blk, *, tm):
    m, d = x.shape
    return pl.pallas_call(
        _modulate_kernel,
        grid=(m // tm,),
        in_specs=[pl.BlockSpec((tm, d), lambda i: (i, 0)),
                  pl.BlockSpec((1, d), lambda i: (0, sc_blk)),
                  pl.BlockSpec((1, d), lambda i: (0, sh_blk))],
        out_specs=pl.BlockSpec((tm, d), lambda i: (i, 0)),
        out_shape=jax.ShapeDtypeStruct((m, d), BF16),
        compiler_params=_cparams(("arbitrary",)),
        name="modulate",
    )(x, mod, mod)


def _resid_ln_kernel(x_ref, y_ref, gate_ref, g_ref, b_ref, sc_ref, sh_ref, xo_ref, u_ref):
    z = ALPHA * x_ref[...] + gate_ref[...] * y_ref[...]
    mu = jnp.mean(z, axis=-1, keepdims=True)
    zc = z - mu
    var = jnp.mean(zc * zc, axis=-1, keepdims=True)
    xn = zc * lax.rsqrt(var + 1e-5) * g_ref[...] + b_ref[...]
    xo_ref[...] = xn
    u_ref[...] = (xn * (1.0 + sc_ref[...]) + sh_ref[...]).astype(u_ref.dtype)


def _resid_ln(x, y, mod, gate_blk, g, b, mod_next, sc_blk, sh_blk, *, tm):
    m, d = x.shape
    row = lambda blk: pl.BlockSpec((1, d), lambda i: (0, blk))
    return pl.pallas_call(
        _resid_ln_kernel,
        grid=(m // tm,),
        in_specs=[pl.BlockSpec((tm, d), lambda i: (i, 0)),
                  pl.BlockSpec((tm, d), lambda i: (i, 0)),
                  row(gate_blk), row(0), row(0), row(sc_blk), row(sh_blk)],
        out_specs=[pl.BlockSpec((tm, d), lambda i: (i, 0)),
                   pl.BlockSpec((tm, d), lambda i: (i, 0))],
        out_shape=[jax.ShapeDtypeStruct((m, d), F32), jax.ShapeDtypeStruct((m, d), BF16)],
        compiler_params=_cparams(("arbitrary",)),
        name="resid_ln",
    )(x, y, mod, g.reshape(1, d), b.reshape(1, d), mod_next, mod_next)


FLASH_ROW_CHUNK = 32
POS_SPLIT = 128


_KV_PACK = (("a_k", "a_k", 0, 2 * DA_HEADS, 64, "k"), ("a_v", "a_v", 0, DA_HEADS, 128, "v"),
            ("b_k", "kv_b", 0, DSA_HEADS, 64, "k"), ("b_v", "kv_b", 512, DSA_HEADS, 64, "v"),
            ("c_k", "c_k", 0, 2, 64, "k"), ("c_v", "c_v", 0, 2, 64, "v"),
            ("d_ks", "d_ks", 0, 2, 64, "k"), ("d_vs", "d_vs", 0, 2, 64, "v"),
            ("d_kw", "d_kw", 0, 2, 64, "k"), ("d_vw", "d_vw", 0, 2, 64, "v"))
_KV_SOURCES = ("a_k", "a_v", "c_k", "c_v", "d_ks", "d_vs", "d_kw", "d_vw", "kv_b")


def _kv_pack_kernel(*refs, tm):
    src = dict(zip(_KV_SOURCES, refs[:len(_KV_SOURCES)]))
    outs = refs[len(_KV_SOURCES):]
    i = pl.program_id(0)
    tails = {}
    for w in (64, 128):
        lane = lax.broadcasted_iota(jnp.int32, (tm, w), 1)
        pos = i * tm + lax.broadcasted_iota(jnp.int32, (tm, w), 0)
        tails[("v", w)] = jnp.where(lane == 0, 1.0, 0.0)
        tails[("k", w)] = jnp.where(lane == 0, (pos // POS_SPLIT).astype(F32),
                                    jnp.where(lane == 1, (pos % POS_SPLIT).astype(F32),
                                              jnp.where(lane < 4, 1.0, 0.0)))
    for (_, sname, c0, n_heads, w, kind), o_ref in zip(_KV_PACK, outs):
        for h in range(n_heads):
            x = src[sname][:, c0 + h * w:c0 + (h + 1) * w]
            o_ref[:, 2 * h * w:2 * (h + 1) * w] = jnp.concatenate([x, tails[(kind, w)]], axis=1).astype(BF16)


def _kv_pack(proj, kv_b, *, tm):
    s_len = proj.shape[0]
    in_specs, args = [], []
    for sname in _KV_SOURCES:
        if sname == "kv_b":
            in_specs.append(pl.BlockSpec((tm, kv_b.shape[1]), lambda i: (i, 0)))
            args.append(kv_b)
        else:
            wblk, blk = (512, BLK512[sname]) if sname in BLK512 else (128, BLK128[sname])
            in_specs.append(pl.BlockSpec((tm, wblk), functools.partial(lambda i, blk: (i, blk), blk=blk)))
            args.append(proj)
    widths = [2 * n_heads * w for (_, _, _, n_heads, w, _) in _KV_PACK]
    outs = pl.pallas_call(
        functools.partial(_kv_pack_kernel, tm=tm),
        grid=(s_len // tm,),
        in_specs=in_specs,
        out_specs=[pl.BlockSpec((tm, wd), lambda i: (i, 0)) for wd in widths],
        out_shape=[jax.ShapeDtypeStruct((s_len, wd), BF16) for wd in widths],
        compiler_params=_cparams(("arbitrary",)),
        name="kv_pack",
    )(*args)
    return {name: o for (name, *_), o in zip(_KV_PACK, outs)}


def _flash_kernel(*refs, units, tq, tk, window, dense, n_prev, nsteps, n_mask, has_sink):
    it = iter(refs)
    q_ref, k_ref, v_ref = next(it), next(it), next(it)
    mask_ref = next(it) if n_mask else None
    sink_ref = next(it) if has_sink else None
    o_ref = next(it)
    q_scr, m_scr, acc_scr, bias_scr = (next(it) for _ in range(4))
    dv = acc_scr.shape[2]
    lcol = dv // 2

    qi = pl.program_id(0)
    j = pl.program_id(1)
    last_kb = (qi * tq + tq - 1) // tk
    kb = j if dense else qi - n_prev + j
    rows = q_scr.shape[1]
    rb = FLASH_ROW_CHUNK
    kw = 2 * HEAD_DIM

    @pl.when(j == 0)
    def _init():
        lane = lax.broadcasted_iota(jnp.int32, (tq, HEAD_DIM), 1)
        qpos = qi * tq + lax.broadcasted_iota(jnp.int32, (tq, HEAD_DIM), 0)
        qhi = (qpos // POS_SPLIT).astype(F32)
        qlo = (qpos % POS_SPLIT).astype(F32)
        for ui, (_, _, _, _, hds) in enumerate(units):
            for r, (qo, slope, _, sink_idx) in enumerate(hds):
                rsl = slice(r * tq, (r + 1) * tq)
                tail = jnp.where(lane == 0, POS_SPLIT * slope,
                                 jnp.where(lane == 1, slope,
                                           jnp.where(lane == 2, -POS_SPLIT * slope * qhi,
                                                     jnp.where(lane == 3, -slope * qlo, 0.0))))
                qs = q_ref[:, qo:qo + HEAD_DIM] * HEAD_DIM ** -0.5
                q_scr[ui, rsl] = jnp.concatenate([qs, tail], axis=1).astype(BF16)
                if has_sink:
                    m_scr[ui, rsl] = jnp.broadcast_to(sink_ref[:, sink_idx:sink_idx + 1], (tq, 1))
                else:
                    m_scr[ui, rsl] = jnp.full((tq, 1), NEG, F32)
            alane = lax.broadcasted_iota(jnp.int32, acc_scr.shape[1:], 1)
            acc_scr[ui] = jnp.where(alane == lcol, 1.0 if has_sink else 0.0, 0.0)

    def scores(ui):
        ku = units[ui][0]
        return _dot_nt(q_scr[ui], k_ref[:, ku * kw:(ku + 1) * kw])

    def step(masked):
        if masked:
            qpos = qi * tq + lax.broadcasted_iota(jnp.int32, (tq, tk), 0)
            kpos = kb * tk + lax.broadcasted_iota(jnp.int32, (tq, tk), 1)
            dist = qpos - kpos
            valid = dist >= 0
            if not dense:
                valid = valid & (dist < window)
            if n_mask:
                for g in range(n_mask):
                    bias_scr[g] = jnp.where(valid, mask_ref[g].astype(F32), NEG)
            else:
                bias_scr[0] = jnp.where(valid, 0.0, NEG)

        def chunk(s, mg, c):
            r0 = c * rb
            sc = s[r0:r0 + rb]
            if masked:
                rw = r0 % tq
                sc = sc + bias_scr[mg, rw:rw + rb]
            return sc

        s_next = scores(0)
        for ui, (_, vo, _, mg, _) in enumerate(units):
            s = s_next
            if ui + 1 < len(units):
                s_next = scores(ui + 1)
            nchunk = rows // rb
            m_old = m_scr[ui]
            m_cur = jnp.concatenate([jnp.max(chunk(s, mg, c), axis=1, keepdims=True) for c in range(nchunk)],
                                    axis=0)
            m_new = jnp.maximum(m_old, m_cur)
            alpha = jnp.exp(m_old - m_new)
            m_scr[ui] = m_new
            p_all = jnp.concatenate(
                [jnp.exp(chunk(s, mg, c) - m_new[c * rb:(c + 1) * rb]).astype(BF16) for c in range(nchunk)],
                axis=0)
            acc_scr[ui] = alpha * acc_scr[ui] + _dot(p_all, v_ref[:, vo:vo + dv])

    needed = (kb >= 0) & (kb <= last_kb)
    if dense and not n_mask:
        interior = kb * tk + tk - 1 <= qi * tq
        pl.when(needed & interior)(lambda: step(False))
        pl.when(needed & jnp.logical_not(interior))(lambda: step(True))
    else:
        pl.when(needed)(lambda: step(True))

    @pl.when(j == nsteps - 1)
    def _fin():
        for ui, (_, _, _, _, hds) in enumerate(units):
            for r, (_, _, (oo, ow), _) in enumerate(hds):
                rsl = slice(r * tq, (r + 1) * tq)
                acc = acc_scr[ui, rsl]
                o_ref[:, oo:oo + ow] = acc[:, 0:ow] / acc[:, lcol:lcol + 1]


def _flash(q_arr, k_arr, v_arr, *, units, q_spec, out_w, tq, tk, window=None,
           mask=None, sinks=None, name="flash"):
    s_len = q_arr.shape[0]
    dense = window is None
    if dense:
        n_prev, nsteps = 0, s_len // tk
    else:
        assert tq == tk
        n_prev = -(-(window - 1) // tk)
        nsteps = n_prev + 1
    n_mask = 0 if mask is None else mask.shape[0]
    dv = units[0][2]
    nu = len(units)
    rows = len(units[0][4]) * tq
    assert all(len(un[4]) * tq == rows and un[2] == dv for un in units) and rows % FLASH_ROW_CHUNK == 0

    def kv_index(qi, j):
        last_kb = (qi * tq + tq - 1) // tk
        kb = j if dense else qi - n_prev + j
        return jnp.clip(kb, 0, last_kb)

    in_specs = [pl.BlockSpec((tq, q_spec[0]), lambda qi, j: (qi, q_spec[1])),
                pl.BlockSpec((tk, k_arr.shape[1]), lambda qi, j: (kv_index(qi, j), 0)),
                pl.BlockSpec((tk, v_arr.shape[1]), lambda qi, j: (kv_index(qi, j), 0))]
    args = [q_arr, k_arr, v_arr]
    if n_mask:
        in_specs.append(pl.BlockSpec((n_mask, tq, tk), lambda qi, j: (0, qi, kv_index(qi, j))))
        args.append(mask)
    if sinks is not None:
        in_specs.append(pl.BlockSpec((1, LANES), lambda qi, j: (0, 0)))
        args.append(sinks)
    return pl.pallas_call(
        functools.partial(_flash_kernel, units=tuple(units), tq=tq, tk=tk, window=window, dense=dense,
                          n_prev=n_prev, nsteps=nsteps, n_mask=n_mask, has_sink=sinks is not None),
        grid=(s_len // tq, nsteps),
        in_specs=in_specs,
        out_specs=pl.BlockSpec((tq, out_w), lambda qi, j: (qi, 0)),
        out_shape=jax.ShapeDtypeStruct((s_len, out_w), F32),
        scratch_shapes=[pltpu.VMEM((nu, rows, 2 * HEAD_DIM), BF16), pltpu.VMEM((nu, rows, 1), F32),
                        pltpu.VMEM((nu, rows, dv), F32), pltpu.VMEM((max(n_mask, 1), tq, tk), F32)],
        compiler_params=_cparams(("arbitrary", "arbitrary")),
        name=name,
    )(*args)


def _diff_final_kernel(o_ref, lam_ref, g_ref, out_ref, *, lambda_init):
    lf = lam_ref[0]
    lam = (jnp.exp(jnp.sum(lf[0:1] * lf[1:2])) - jnp.exp(jnp.sum(lf[2:3] * lf[3:4])) + lambda_init)
    w = 2 * HEAD_DIM
    for h in range(DA_HEADS):
        o = o_ref[:, (2 * h) * w:(2 * h + 1) * w] - lam * o_ref[:, (2 * h + 1) * w:(2 * h + 2) * w]
        o = o * lax.rsqrt(jnp.mean(o * o, axis=-1, keepdims=True) + 1e-6) * g_ref[...]
        out_ref[:, h * w:(h + 1) * w] = o * (1.0 - lambda_init)


def _diff_final(o, diff_lambda, layer, subln_g, lambda_init, *, tm):
    m = o.shape[0]
    w = 2 * HEAD_DIM
    return pl.pallas_call(
        functools.partial(_diff_final_kernel, lambda_init=lambda_init),
        grid=(m // tm,),
        in_specs=[pl.BlockSpec((tm, 2 * DA_HEADS * w), lambda i: (i, 0)),
                  pl.BlockSpec((1, 4, HEAD_DIM), lambda i: (layer, 0, 0)),
                  pl.BlockSpec((1, w), lambda i: (0, 0))],
        out_specs=pl.BlockSpec((tm, DA_HEADS * w), lambda i: (i, 0)),
        out_shape=jax.ShapeDtypeStruct((m, DA_HEADS * w), F32),
        compiler_params=_cparams(("arbitrary",)),
        name="diff_final",
    )(o, diff_lambda, subln_g.reshape(1, w))


def _f32_key_const(x):
    b = int(np.array(x, np.float32).view(np.int32))
    return b ^ ((b >> 31) & 0x7FFFFFFF)


I16_MIN = -(2 ** 15)


def _dsa_select_kernel(qi_ref, w_ref, kidx_ref, mask_ref, key_scr, half_scr, j_scr, *, tq, ch, nch, topk,
                       s_len):
    i = pl.program_id(0)
    q0 = i * tq
    n_need = (q0 + tq + ch - 1) // ch
    qpos = q0 + lax.broadcasted_iota(jnp.int32, (tq, 1), 0)
    lane = lax.broadcasted_iota(jnp.int32, (1, ch), 1)
    w = w_ref[:, 0:IDX_HEADS]
    q_all = jnp.concatenate([qi_ref[:, h * HEAD_DIM:(h + 1) * HEAD_DIM] for h in range(IDX_HEADS)],
                            axis=0).astype(BF16)

    def score_chunk(c, carry):
        kc = kidx_ref[pl.ds(pl.multiple_of(c * ch, ch), ch), 0:HEAD_DIM].astype(BF16)
        lg = _dot_nt(q_all, kc)
        acc = jnp.zeros((tq, ch), F32)
        for h in range(IDX_HEADS):
            acc = acc + w[:, h:h + 1] * jnp.maximum(lg[h * tq:(h + 1) * tq], 0.0)
        acc = jnp.where(c * ch + lane <= qpos, acc, NEG) + 0.0
        bits = pltpu.bitcast(acc, jnp.int32)
        key = bits ^ ((bits >> 31) & 0x7FFFFFFF)
        key_scr[c] = key
        half_scr[c] = (key >> 16).astype(jnp.int16)
        return carry

    lax.fori_loop(0, n_need, score_chunk, 0)

    def count16(cand, strict):
        cand16 = jnp.broadcast_to(cand, (tq, LANES)).astype(jnp.int16)
        one, zero = jnp.int16(1), jnp.int16(0)

        def body(c, acc):
            blk = half_scr[c]
            for t in range(ch // LANES):
                tile = blk[:, t * LANES:(t + 1) * LANES]
                acc = acc + jnp.where(tile > cand16 if strict else tile >= cand16, one, zero)
            return acc
        acc = lax.fori_loop(0, n_need, body, jnp.zeros((tq, LANES), jnp.int16))
        return jnp.sum(acc.astype(jnp.int32), axis=1, keepdims=True)

    def search16(need_cnt):
        def bit_step(b, t):
            cand = t + jnp.left_shift(jnp.int32(1), 15 - b)
            return jnp.where(count16(cand, False) >= need_cnt, cand, t)
        return lax.fori_loop(0, 16, bit_step, jnp.full((tq, 1), I16_MIN, jnp.int32))

    t_hi = search16(topk)
    need_lo = topk - count16(t_hi, True)

    def low_chunk(c, carry):
        key = key_scr[c]
        low = (key & 0xFFFF) + I16_MIN
        half_scr[c] = jnp.where((key >> 16) == t_hi, low, I16_MIN).astype(jnp.int16)
        return carry

    lax.fori_loop(0, n_need, low_chunk, 0)
    t_lo = search16(need_lo)
    thr = jnp.left_shift(t_hi, 16) + (t_lo - I16_MIN)

    def count(pred):
        def body(c, acc):
            m = jnp.where(pred(key_scr[c], c), 1, 0)
            part = m[:, 0:LANES]
            for t in range(1, ch // LANES):
                part = part + m[:, t * LANES:(t + 1) * LANES]
            return acc + part
        acc = lax.fori_loop(0, n_need, body, jnp.zeros((tq, LANES), jnp.int32))
        return jnp.sum(acc, axis=1, keepdims=True)

    cnt_gt = count(lambda blk, c: blk > thr)
    cnt_ge = count(lambda blk, c: blk >= thr)
    need = topk - cnt_gt
    tie_rows = (cnt_ge > topk) & (thr > _f32_key_const(NEG))
    j_scr[...] = jnp.full((tq, 1), s_len, jnp.int32)
    any_tie = jnp.max(jnp.where(tie_rows, 1, 0)) > 0

    @pl.when(any_tie)
    def _ties():
        def tie_chunk(c, carry):
            half_scr[c] = jnp.where(key_scr[c] == thr, -1 - (c * ch + lane), I16_MIN).astype(jnp.int16)
            return carry

        lax.fori_loop(0, n_need, tie_chunk, 0)
        j_scr[...] = jnp.where(tie_rows, -1 - search16(need), s_len)

    jv = j_scr[...]
    for c in range(nch):
        @pl.when((c < n_need) & any_tie)
        def _w():
            key = key_scr[c]
            sel = (key > thr) | ((key == thr) & (c * ch + lane <= jv))
            mask_ref[:, c * ch:(c + 1) * ch] = jnp.where(sel, 0.0, NEG).astype(mask_ref.dtype)

        @pl.when((c < n_need) & jnp.logical_not(any_tie))
        def _wf():
            mask_ref[:, c * ch:(c + 1) * ch] = jnp.where(key_scr[c] >= thr, 0.0, NEG).astype(mask_ref.dtype)

        @pl.when(c >= n_need)
        def _z():
            mask_ref[:, c * ch:(c + 1) * ch] = jnp.full((tq, ch), NEG, mask_ref.dtype)


def _dsa_select(proj, *, topk, tq=128):
    s_len = proj.shape[0]
    assert s_len < -I16_MIN
    ch = min(1024, s_len)
    nch = s_len // ch
    return pl.pallas_call(
        functools.partial(_dsa_select_kernel, tq=tq, ch=ch, nch=nch, topk=topk, s_len=s_len),
        grid=(s_len // tq,),
        in_specs=[pl.BlockSpec((tq, 512), lambda i: (i, BLK512["b_iq"])),
                  pl.BlockSpec((tq, LANES), lambda i: (i, BLK128["small"])),
                  pl.BlockSpec((s_len, LANES), lambda i: (0, BLK128["b_ik"]))],
        out_specs=pl.BlockSpec((tq, s_len), lambda i: (i, 0)),
        out_shape=jax.ShapeDtypeStruct((s_len, s_len), BF16),
        scratch_shapes=[pltpu.VMEM((nch, tq, ch), jnp.int32), pltpu.VMEM((nch, tq, ch), jnp.int16),
                        pltpu.VMEM((tq, 1), jnp.int32)],
        compiler_params=_cparams(("arbitrary",)),
        name="dsa_select",
    )(proj, proj, proj)


def _nsa_compress_kernel(x_ref, pos_ref, w1_ref, w2_ref, o_ref):
    x = (x_ref[0] + pos_ref[0, 0]).astype(BF16)
    hdn = _silu(_dot(x, w1_ref[0, 0].astype(BF16)))
    o_ref[0] = _dot(hdn.astype(BF16), w2_ref[0, 0].astype(BF16))


def _nsa_compress(xc, pos, w1, w2, layer):
    _, ncp, kdim = xc.shape
    return pl.pallas_call(
        _nsa_compress_kernel,
        grid=(4,),
        in_specs=[pl.BlockSpec((1, ncp, kdim), lambda i: (i, 0, 0)),
                  pl.BlockSpec((1, 1, 1, kdim), lambda i: (layer, i // 2, 0, 0)),
                  pl.BlockSpec((1, 1, kdim, NSA_CMP_HID), lambda i: (layer, i // 2, 0, 0)),
                  pl.BlockSpec((1, 1, NSA_CMP_HID, HEAD_DIM), lambda i: (layer, i // 2, 0, 0))],
        out_specs=pl.BlockSpec((1, ncp, HEAD_DIM), lambda i: (i, 0, 0)),
        out_shape=jax.ShapeDtypeStruct((4, ncp, HEAD_DIM), F32),
        compiler_params=_cparams(("arbitrary",)),
        name="nsa_compress",
    )(xc, pos, w1, w2)


def _nsa_cmp_kernel(q_ref, kv_ref, ov_ref, ex_ref, o_ref, mask_ref, *, tq, ncp, n_slc, topn, ch, nch):
    i = pl.program_id(0)
    q0 = i * tq
    n_need = (q0 + tq + ch - 1) // ch
    rpg = NSA_HEADS // NSA_GROUPS
    slopes = _alibi(NSA_HEADS)
    scale = HEAD_DIM ** -0.5
    qpos_c = q0 + lax.broadcasted_iota(jnp.int32, (tq, ncp), 0)
    cend = lax.broadcasted_iota(jnp.int32, (tq, ncp), 1) * NSA_CMP_STRIDE + (NSA_CMP_LEN - 1)
    dist_c = qpos_c - cend
    valid_c = dist_c >= 0
    distf = dist_c.astype(F32)
    qpos = q0 + lax.broadcasted_iota(jnp.int32, (tq, n_slc), 0)
    blk = lax.broadcasted_iota(jnp.int32, (tq, n_slc), 1)
    cur = qpos // NSA_SLC_LEN
    forced = (blk == 0) | (blk == cur) | (blk == cur - 1)
    blk_ok = blk * NSA_SLC_LEN <= qpos
    ov = ov_ref[...]
    imps = []
    for g in range(NSA_GROUPS):
        kc = kv_ref[g].astype(BF16)
        vc = kv_ref[NSA_GROUPS + g].astype(BF16)
        psum = jnp.zeros((tq, ncp), F32)
        for r in range(rpg):
            h = g * rpg + r
            qh = q_ref[:, h * HEAD_DIM:(h + 1) * HEAD_DIM].astype(BF16)
            s = _dot_nt(qh, kc) * scale - slopes[h] * distf
            s = jnp.where(valid_c, s, NEG)
            e = jnp.where(valid_c, jnp.exp(s - jnp.max(s, axis=1, keepdims=True)), 0.0)
            p = e / jnp.maximum(jnp.sum(e, axis=1, keepdims=True), 1e-30)
            o_ref[:, h * HEAD_DIM:(h + 1) * HEAD_DIM] = _dot(p.astype(BF16), vc)
            psum = psum + p
        p_hi = psum.astype(BF16)
        p_lo = (psum - p_hi.astype(F32)).astype(BF16)
        imp = _dot(p_hi, ov) + _dot(p_lo, ov)
        imp = jnp.where(forced, NSA_FORCE, imp)
        imps.append(jnp.where(blk_ok, imp, NEG))
    imps = [imp.T for imp in imps]
    blk_t = lax.broadcasted_iota(jnp.int32, (n_slc, tq), 0)
    sels = [jnp.full((n_slc, tq), NEG, F32) for _ in range(NSA_GROUPS)]
    for _ in range(topn):
        for g in range(NSA_GROUPS):
            mx = jnp.max(imps[g], axis=0, keepdims=True)
            first = jnp.min(jnp.where(imps[g] == mx, blk_t, n_slc), axis=0, keepdims=True)
            hit = blk_t == first
            sels[g] = jnp.where(hit, 0.0, sels[g])
            imps[g] = jnp.where(hit, -jnp.inf, imps[g])
    sels = [sel.T for sel in sels]
    for g in range(NSA_GROUPS):
        selb = sels[g].astype(BF16)
        for c in range(nch):
            @pl.when(c < n_need)
            def _w():
                tok = _dot(selb, ex_ref[:, c * ch:(c + 1) * ch])
                mask_ref[g, :, c * ch:(c + 1) * ch] = tok.astype(mask_ref.dtype)

            @pl.when(c >= n_need)
            def _z():
                mask_ref[g, :, c * ch:(c + 1) * ch] = jnp.full((tq, ch), NEG, mask_ref.dtype)


def _nsa_cmp(proj, kv_cmp, *, tq=128):
    s_len = proj.shape[0]
    ncp = kv_cmp.shape[1]
    n_slc = s_len // NSA_SLC_LEN
    topn = min(NSA_TOPN, n_slc)
    ch = min(1024, s_len)
    nch = s_len // ch
    starts = np.arange(ncp) * NSA_CMP_STRIDE
    slc_start = np.arange(n_slc) * NSA_SLC_LEN
    overlap = ((starts[:, None] < slc_start[None, :] + NSA_SLC_LEN)
               & (starts[:, None] + NSA_CMP_LEN > slc_start[None, :])).astype(np.float32)
    expand = (np.arange(s_len)[None, :] // NSA_SLC_LEN == np.arange(n_slc)[:, None]).astype(np.float32)
    return pl.pallas_call(
        functools.partial(_nsa_cmp_kernel, tq=tq, ncp=ncp, n_slc=n_slc, topn=topn, ch=ch, nch=nch),
        grid=(s_len // tq,),
        in_specs=[pl.BlockSpec((tq, 512), lambda i: (i, BLK512["d_q"])),
                  pl.BlockSpec((4, ncp, HEAD_DIM), lambda i: (0, 0, 0)),
                  pl.BlockSpec((ncp, n_slc), lambda i: (0, 0)),
                  pl.BlockSpec((n_slc, s_len), lambda i: (0, 0))],
        out_specs=[pl.BlockSpec((tq, 512), lambda i: (i, 0)),
                   pl.BlockSpec((NSA_GROUPS, tq, s_len), lambda i: (0, i, 0))],
        out_shape=[jax.ShapeDtypeStruct((s_len, 512), F32),
                   jax.ShapeDtypeStruct((NSA_GROUPS, s_len, s_len), BF16)],
        compiler_params=_cparams(("arbitrary",)),
        name="nsa_cmp",
    )(proj, kv_cmp, jnp.asarray(overlap, BF16), jnp.asarray(expand, BF16))


def _nsa_combine_kernel(g_ref, oc_ref, os_ref, ow_ref, o_ref):
    gt = _sigmoid(g_ref[...])
    for h in range(NSA_HEADS):
        sl = slice(h * HEAD_DIM, (h + 1) * HEAD_DIM)
        c0 = IDX_HEADS + 3 * h
        o_ref[:, sl] = (gt[:, c0:c0 + 1] * oc_ref[:, sl] + gt[:, c0 + 1:c0 + 2] * os_ref[:, sl]
                        + gt[:, c0 + 2:c0 + 3] * ow_ref[:, sl])


def _nsa_combine(proj, o_cmp, o_slc, o_win, *, tm):
    m = proj.shape[0]
    spec = pl.BlockSpec((tm, 512), lambda i: (i, 0))
    return pl.pallas_call(
        _nsa_combine_kernel,
        grid=(m // tm,),
        in_specs=[pl.BlockSpec((tm, LANES), lambda i: (i, BLK128["small"])), spec, spec, spec],
        out_specs=spec,
        out_shape=jax.ShapeDtypeStruct((m, 512), F32),
        compiler_params=_cparams(("arbitrary",)),
        name="nsa_combine",
    )(proj, o_cmp, o_slc, o_win)


def _merge_kernel(u_ref, oa_ref, ob_ref, oc_ref, od_ref, wg0, wg1, wg2, wg3, wb_ref, o_ref,
                  wgb_ref, wbb_ref):
    wgs = (wg0, wg1, wg2, wg3)

    @pl.when(pl.program_id(1) == 0)
    def _():
        for mch in range(N_BRANCH):
            wgb_ref[mch] = wgs[mch][0].astype(BF16)
            wbb_ref[mch] = wb_ref[0, mch].astype(BF16)

    u = u_ref[...]
    acc = None
    for mch, o_ref_m in enumerate((oa_ref, ob_ref, oc_ref, od_ref)):
        gte = _sigmoid(_dot(u, wgb_ref[mch]))
        z = _dot(o_ref_m[...].astype(BF16), wbb_ref[mch])
        acc = gte * z if acc is None else acc + gte * z
    o_ref[...] = acc.astype(o_ref.dtype)


def _merge(u, branches, w_gate, w_branch, layer, *, tm, tn):
    m, d = u.shape
    nj = d // tn
    bspec = pl.BlockSpec((tm, BRANCH_W), lambda j, i: (i, 0))
    wg_specs = [pl.BlockSpec((1, d, tn),
                             functools.partial(lambda j, i, mch: (layer, 0, mch * nj + j), mch=mch))
                for mch in range(N_BRANCH)]
    return pl.pallas_call(
        _merge_kernel,
        grid=(nj, m // tm),
        in_specs=[pl.BlockSpec((tm, d), lambda j, i: (i, 0)), bspec, bspec, bspec, bspec,
                  *wg_specs,
                  pl.BlockSpec((1, N_BRANCH, BRANCH_W, tn), lambda j, i: (layer, 0, 0, j))],
        out_specs=pl.BlockSpec((tm, tn), lambda j, i: (i, j)),
        out_shape=jax.ShapeDtypeStruct((m, d), BF16),
        scratch_shapes=[pltpu.VMEM((N_BRANCH, d, tn), BF16), pltpu.VMEM((N_BRANCH, BRANCH_W, tn), BF16)],
        compiler_params=_cparams(("arbitrary", "arbitrary")),
        name="merge",
    )(u, *branches, w_gate, w_gate, w_gate, w_gate, w_branch)


def _router_kernel(u_ref, r_ref, o_ref):
    logits = _dot(u_ref[...], r_ref[0].astype(BF16))
    lane = lax.broadcasted_iota(jnp.int32, logits.shape, 1)
    lg = jnp.where(lane < N_EXPERTS, logits, -jnp.inf)
    m1 = jnp.max(lg, axis=1, keepdims=True)
    i1 = jnp.min(jnp.where(lg == m1, lane, LANES), axis=1, keepdims=True)
    lg2 = jnp.where(lane == i1, -jnp.inf, lg)
    m2 = jnp.max(lg2, axis=1, keepdims=True)
    i2 = jnp.min(jnp.where(lg2 == m2, lane, LANES), axis=1, keepdims=True)
    e2 = jnp.exp(m2 - m1)
    w1 = 1.0 / (1.0 + e2)
    w2 = e2 / (1.0 + e2)
    o_ref[...] = jnp.where(lane == 0, i1.astype(F32),
                           jnp.where(lane == 1, i2.astype(F32),
                                     jnp.where(lane == 2, w1, jnp.where(lane == 3, w2, 0.0))))


def _router(u, router_padded, layer, *, tm):
    m, d = u.shape
    return pl.pallas_call(
        _router_kernel,
        grid=(m // tm,),
        in_specs=[pl.BlockSpec((tm, d), lambda i: (i, 0)),
                  pl.BlockSpec((1, d, LANES), lambda i: (layer, 0, 0))],
        out_specs=pl.BlockSpec((tm, LANES), lambda i: (i, 0)),
        out_shape=jax.ShapeDtypeStruct((m, LANES), F32),
        compiler_params=_cparams(("arbitrary",)),
        name="router",
    )(u, router_padded)


MOE_ROW_TILE = 256
MOE_TOK_CHUNK = 256


def _moe_plan(ridx, rw, s_len):
    tm, ct, n_e = MOE_ROW_TILE, MOE_TOK_CHUNK, N_EXPERTS
    i32 = jnp.int32
    e_a = ridx.reshape(-1).astype(i32)
    oh = (e_a[:, None] == jnp.arange(n_e, dtype=i32)[None, :]).astype(i32)
    csum = jnp.cumsum(oh, axis=0)
    rank_a = jnp.sum((csum - oh) * oh, axis=1)
    ntile_e = (csum[-1] + tm - 1) // tm
    tile_end = jnp.cumsum(ntile_e)
    pos_a = jnp.take(tile_end - ntile_e, e_a) * tm + rank_a
    n_rows = 2 * s_len + n_e * tm
    n_tiles = n_rows // tm
    n_chunks = s_len // ct
    row_tok = jnp.full((n_rows,), -1, i32).at[pos_a].set(jnp.arange(2 * s_len, dtype=i32) // 2)
    row_w = jnp.zeros((n_rows,), F32).at[pos_a].set(rw.reshape(-1))
    tile_e = jnp.minimum(jnp.searchsorted(tile_end, jnp.arange(n_tiles, dtype=i32), side="right"),
                         n_e - 1).astype(i32)
    rt = row_tok.reshape(n_tiles, tm)
    lo = jnp.min(jnp.where(rt >= 0, rt, s_len - 1), axis=1) // ct
    hi = jnp.maximum(jnp.max(jnp.where(rt >= 0, rt, 0), axis=1) // ct, lo)
    n_i = hi - lo + 1
    end = jnp.cumsum(n_i)
    n_work = n_tiles + n_e * n_chunks
    w = jnp.arange(n_work, dtype=i32)
    wt = jnp.minimum(jnp.searchsorted(end, w, side="right"), n_tiles - 1).astype(i32)
    wc = jnp.clip(jnp.take(lo, wt) + w - jnp.take(end - n_i, wt), 0, n_chunks - 1).astype(i32)
    wa = (w < end[-1]).astype(i32)
    order = jnp.argsort(jnp.where(wa > 0, wc * n_tiles + wt, n_chunks * n_tiles + w))
    vc = jnp.where(wa > 0, wc, n_chunks - 1)[order]
    return dict(row_tok=row_tok, row_w=row_w, tile_e=tile_e, n_tiles=n_tiles, n_work=n_work,
                gather=(wt, wc, wa), combine=(vc, wt[order], wa[order]))


def _moe_gather_kernel(wt_ref, wc_ref, wa_ref, tok_ref, u_ref, o_ref):
    w = pl.program_id(0)

    @pl.when((w == 0) | (wt_ref[jnp.maximum(w - 1, 0)] != wt_ref[w]))
    def _():
        o_ref[...] = jnp.zeros_like(o_ref)

    @pl.when(wa_ref[w] > 0)
    def _():
        ct = u_ref.shape[0]
        cols = wc_ref[w] * ct + lax.broadcasted_iota(jnp.int32, (1, ct), 1)
        onehot = jnp.where(tok_ref[...] == cols, 1.0, 0.0).astype(BF16)
        o_ref[...] += _dot(onehot, u_ref[...]).astype(o_ref.dtype)


def _moe_gather(u, plan):
    s_len, d = u.shape
    tm, ct = MOE_ROW_TILE, MOE_TOK_CHUNK
    n_rows = plan["row_tok"].shape[0]
    return pl.pallas_call(
        _moe_gather_kernel,
        grid_spec=pltpu.PrefetchScalarGridSpec(
            num_scalar_prefetch=3, grid=(plan["n_work"],),
            in_specs=[pl.BlockSpec((tm, 1), lambda w, wt, wc, wa: (wt[w], 0)),
                      pl.BlockSpec((ct, d), lambda w, wt, wc, wa: (wc[w], 0))],
            out_specs=pl.BlockSpec((tm, d), lambda w, wt, wc, wa: (wt[w], 0))),
        out_shape=jax.ShapeDtypeStruct((n_rows, d), BF16),
        compiler_params=_cparams(("arbitrary",)),
        name="moe_gather",
    )(*plan["gather"], plan["row_tok"].reshape(n_rows, 1), u)


def _moe_combine_kernel(vc_ref, vt_ref, va_ref, tok_ref, y0_ref, y1_ref, y2_ref, o_ref):
    w = pl.program_id(0)
    chunk = vc_ref[w]

    @pl.when((w == 0) | (vc_ref[jnp.maximum(w - 1, 0)] != chunk))
    def _():
        o_ref[...] = jnp.zeros_like(o_ref)

    @pl.when(va_ref[w] > 0)
    def _():
        ct = o_ref.shape[0]
        rows = chunk * ct + lax.broadcasted_iota(jnp.int32, (ct, 1), 0)
        onehot_t = jnp.where(rows == tok_ref[0], 1.0, 0.0).astype(BF16)
        o_ref[...] += (_dot(onehot_t, y0_ref[...]) + _dot(onehot_t, y1_ref[...])
                       + _dot(onehot_t, y2_ref[...]))


def _moe_combine(ys3, plan, s_len):
    n_rows, d = ys3[0].shape
    tm, ct = MOE_ROW_TILE, MOE_TOK_CHUNK
    yspec = pl.BlockSpec((tm, d), lambda w, vc, vt, va: (vt[w], 0))
    return pl.pallas_call(
        _moe_combine_kernel,
        grid_spec=pltpu.PrefetchScalarGridSpec(
            num_scalar_prefetch=3, grid=(plan["n_work"],),
            in_specs=[pl.BlockSpec((1, 1, tm), lambda w, vc, vt, va: (vt[w], 0, 0)), yspec, yspec, yspec],
            out_specs=pl.BlockSpec((ct, d), lambda w, vc, vt, va: (vc[w], 0))),
        out_shape=jax.ShapeDtypeStruct((s_len, d), F32),
        compiler_params=_cparams(("arbitrary",)),
        name="moe_combine",
    )(*plan["combine"], plan["row_tok"].reshape(plan["n_tiles"], 1, tm), *ys3)


def _gmm_kernel(te_ref, *refs, swiglu):
    it = iter(refs)
    a_ref = next(it)
    w_refs = [next(it), next(it)] if swiglu else [next(it)]
    rw_ref = next(it) if swiglu else None
    o_refs = [next(it)] if swiglu else [next(it), next(it), next(it)]
    wb_refs = [next(it) for _ in w_refs]
    i = pl.program_id(1)

    @pl.when((i == 0) | (te_ref[i] != te_ref[jnp.maximum(i - 1, 0)]))
    def _():
        for w_ref, wb_ref in zip(w_refs, wb_refs):
            wb_ref[...] = w_ref[0, 0].astype(BF16)

    a = a_ref[...]
    if swiglu:
        h = _silu(_dot(a, wb_refs[0][...])) * _dot(a, wb_refs[1][...]) * rw_ref[...]
        o_refs[0][...] = h.astype(BF16)
    else:
        y = _dot(a, wb_refs[0][...])
        hi = y.astype(BF16)
        r1 = y - hi.astype(F32)
        mid = r1.astype(BF16)
        o_refs[0][...] = hi
        o_refs[1][...] = mid
        o_refs[2][...] = (r1 - mid.astype(F32)).astype(BF16)


def _gmm(a, ws, layer, plan, *, tn, row_w=None, name="gmm"):
    n_rows, k = a.shape
    n = ws[0].shape[3]
    tm = MOE_ROW_TILE
    swiglu = len(ws) == 2
    wspec = pl.BlockSpec((1, 1, k, tn), lambda j, i, te: (layer, te[i], 0, j))
    in_specs = [pl.BlockSpec((tm, k), lambda j, i, te: (i, 0))] + [wspec] * len(ws)
    args = [a, *ws]
    if swiglu:
        in_specs.append(pl.BlockSpec((tm, 1), lambda j, i, te: (i, 0)))
        args.append(row_w.reshape(n_rows, 1))
    ospec = pl.BlockSpec((tm, tn), lambda j, i, te: (i, j))
    oshape = jax.ShapeDtypeStruct((n_rows, n), BF16)
    return pl.pallas_call(
        functools.partial(_gmm_kernel, swiglu=swiglu),
        grid_spec=pltpu.PrefetchScalarGridSpec(
            num_scalar_prefetch=1, grid=(n // tn, n_rows // tm),
            in_specs=in_specs,
            out_specs=ospec if swiglu else [ospec] * 3,
            scratch_shapes=[pltpu.VMEM((k, tn), BF16) for _ in ws]),
        out_shape=oshape if swiglu else [oshape] * 3,
        compiler_params=_cparams(("arbitrary", "arbitrary")),
        name=name,
    )(plan["tile_e"], *args)


def _permute_w_in(w):
    cols = []
    for nm in _NEW_ORDER:
        if nm.startswith("pad"):
            cols.append(jnp.zeros(w.shape[:2] + (int(nm[3:]),), w.dtype))
        else:
            o, n = _ORIG[nm]
            cols.append(w[:, :, o:o + n])
    out = jnp.concatenate(cols, axis=2)
    assert out.shape[2] == PROJ_W
    return out


def _nsa_cmp_inputs(proj):
    s_len = proj.shape[0]
    n_cmp = (s_len - NSA_CMP_LEN) // NSA_CMP_STRIDE + 1
    ncp = s_len // NSA_CMP_STRIDE
    xs = []
    for jj in range(2):
        for g in range(NSA_GROUPS):
            c0 = COL_DKV + jj * 128 + g * HEAD_DIM
            r = proj[:, c0:c0 + HEAD_DIM].reshape(ncp, NSA_CMP_STRIDE * HEAD_DIM)
            x = jnp.concatenate([r[:-1], r[1:]], axis=1)
            xs.append(jnp.pad(x, ((0, ncp - n_cmp), (0, 0))))
    return jnp.stack(xs)


def _token_mixers(u, layer, p, cfg):
    s_len = u.shape[0]
    tm = cfg["tm"]
    proj = _mm(u, p["w_in"], layer, tm=tm, tn=512, name="in_proj")
    kv_b = _mm(proj, p["dsa_w_ukv"], layer, tm=tm, tn=512, a_blk=BLK128["b_kv"], k=DSA_KV_RANK,
               prologue="rms", gain=p["dsa_kv_norm_g"][layer], name="dsa_kv")
    kv = _kv_pack(proj, kv_b, tm=cfg["tm_ln"])

    lambda_init = 0.8 - 0.6 * math.exp(-0.3 * layer)
    sl_a = _alibi(DA_HEADS)
    units_a = [(2 * h + mp, h * 256, 256, 0,
                ((h * 128 + mp * 64, sl_a[h], ((2 * h + mp) * 128, 128), 0),))
               for h in range(DA_HEADS) for mp in range(2)]
    o_a2 = _flash(proj, kv["a_k"], kv["a_v"], units=units_a,
                  q_spec=(512, BLK512["a_q"]), out_w=1024, tq=cfg["tq"], tk=cfg["tk"], name="diff_attn")
    o_a = _diff_final(o_a2, p["diff_lambda"], layer, p["diff_subln_g"][layer], lambda_init, tm=tm)

    topk = min(DSA_TOPK_MAX, s_len // 4)
    mask_b = _dsa_select(proj, topk=topk)
    sl8 = _alibi(8)
    units_b = [(h, h * 128, 128, 0, ((h * 64, sl8[h], (h * 64, 64), 0),)) for h in range(DSA_HEADS)]
    o_b = _flash(proj, kv["b_k"], kv["b_v"],
                 units=units_b, q_spec=(512, BLK512["b_q"]), out_w=512, tq=cfg["tq"], tk=cfg["tk"],
                 mask=mask_b.reshape(1, s_len, s_len), name="dsa_attn")

    def gqa_units(masked):
        return [(g, g * 128, 128, g if masked else 0,
                 tuple(((g * 4 + r) * 64, sl8[g * 4 + r], ((g * 4 + r) * 64, 64), g * 4 + r)
                       for r in range(4)))
                for g in range(2)]

    def gqa_kv(k_name, v_name):
        return kv[k_name], kv[v_name]

    sinks = jnp.pad(p["swa_sinks"][layer].reshape(1, SWA_HEADS), ((0, 0), (0, LANES - SWA_HEADS)))
    o_c = _flash(proj, *gqa_kv("c_k", "c_v"), units=gqa_units(False), q_spec=(512, BLK512["c_q"]),
                 out_w=512, tq=cfg["tb"], tk=cfg["tb"], window=SWA_WINDOW, sinks=sinks, name="swa_attn")

    kv_cmp = _nsa_compress(_nsa_cmp_inputs(proj), p["nsa_cmp_pos"], p["nsa_cmp_w1"], p["nsa_cmp_w2"],
                           layer)
    o_cmp, mask_d = _nsa_cmp(proj, kv_cmp)
    o_slc = _flash(proj, *gqa_kv("d_ks", "d_vs"), units=gqa_units(True), q_spec=(512, BLK512["d_q"]),
                   out_w=512, tq=cfg["tq"], tk=cfg["tk"], mask=mask_d, name="nsa_slc_attn")
    o_win = _flash(proj, *gqa_kv("d_kw", "d_vw"), units=gqa_units(False), q_spec=(512, BLK512["d_q"]),
                   out_w=512, tq=cfg["tb"], tk=cfg["tb"], window=NSA_WINDOW, name="nsa_win_attn")
    o_d = _nsa_combine(proj, o_cmp, o_slc, o_win, tm=tm)

    merged = _merge(u, (o_a, o_b, o_c, o_d), p["w_gate"], p["w_branch"], layer,
                    tm=cfg["tm_merge"], tn=256)
    return _mm(merged, p["w_o"], layer, tm=tm, tn=512, name="out_proj")


def _config(s_len):
    return dict(tm=min(1024, s_len), tm_merge=min(512, s_len), tm_ln=min(512, s_len),
                tq=min(256, s_len), tk=min(1024, s_len), tb=min(256, s_len))


def kernel(x, c, cond_w, cond_b, w_in, diff_lambda, diff_subln_g, dsa_kv_norm_g, dsa_w_uk, dsa_w_uv,
           swa_sinks, nsa_cmp_pos, nsa_cmp_w1, nsa_cmp_w2, w_branch, w_gate, w_o,
           ln1_g, ln1_b, ln2_g, ln2_b, ffn_w_gate, ffn_w_up, ffn_w_down,
           moe_router, moe_w_gate, moe_w_up, moe_w_down):
    bsz, s_len, d = x.shape
    assert bsz == 1 and d == D_MODEL
    depth = cond_w.shape[0]
    cfg = _config(s_len)
    xs = x.reshape(s_len, d)
    c8 = jnp.broadcast_to(c.reshape(1, d), (8, d))
    p = dict(w_in=_permute_w_in(w_in), diff_lambda=diff_lambda, diff_subln_g=diff_subln_g,
             dsa_kv_norm_g=dsa_kv_norm_g, dsa_w_ukv=jnp.concatenate([dsa_w_uk, dsa_w_uv], axis=2),
             swa_sinks=swa_sinks,
             nsa_cmp_pos=nsa_cmp_pos.reshape(depth, 2, 1, NSA_CMP_LEN * HEAD_DIM),
             nsa_cmp_w1=nsa_cmp_w1, nsa_cmp_w2=nsa_cmp_w2, w_branch=w_branch, w_gate=w_gate, w_o=w_o)
    router_p = jnp.pad(moe_router, ((0, 0), (0, 0), (0, LANES - N_EXPERTS)))
    mods = [_mm(c8, cond_w, l, tm=8, tn=512, prologue="silu", bias=cond_b[l], name="cond")[0:1]
            for l in range(depth)]
    u = _modulate(xs, mods[0], 1, 0, tm=cfg["tm_ln"])
    for l in range(depth):
        y = _token_mixers(u, l, p, cfg)
        xs, u = _resid_ln(xs, y, mods[l], 2, ln1_g[l], ln1_b[l], mods[l], 4, 3, tm=cfg["tm_ln"])
        jx = l // 2
        if l % 2 == 0:
            hdn = _swiglu_up(u, ffn_w_gate, ffn_w_up, jx, tm=cfg["tm"], tn=512, name="ffn_up")
            y = _mmk(hdn, ffn_w_down, jx, tm=cfg["tm"], tn=d, tk=512, name="ffn_down")
        else:
            rt = _router(u, router_p, jx, tm=cfg["tm"])
            plan = _moe_plan(rt[:, 0:2], rt[:, 2:4], s_len)
            hdn = _gmm(_moe_gather(u, plan), (moe_w_gate, moe_w_up), jx, plan, tn=512,
                       row_w=plan["row_w"], name="moe_up")
            y = _moe_combine(_gmm(hdn, (moe_w_down,), jx, plan, tn=512, name="moe_down"), plan, s_len)
        nxt = min(l + 1, depth - 1)
        xs, u = _resid_ln(xs, y, mods[l], 5, ln2_g[l], ln2_b[l], mods[nxt], 1, 0, tm=cfg["tm_ln"])
    return xs.reshape(bsz, s_len, d)
```

```python
import functools
import math

import numpy as np
import jax
import jax.numpy as jnp
from jax import lax
from jax.experimental import pallas as pl
from jax.experimental.pallas import tpu as pltpu

F32 = jnp.float32
BF16 = jnp.bfloat16
NEG = -1e30

D_MODEL = 2048
DEPTH = 4
HEAD_DIM = 64
DA_HEADS = 4
DSA_HEADS = 8
DSA_KV_RANK = 128
IDX_HEADS = 8
DSA_TOPK_MAX = 256
SWA_HEADS = 8
SWA_WINDOW = 128
NSA_HEADS = 8
NSA_GROUPS = 2
NSA_CMP_LEN = 32
NSA_CMP_STRIDE = 16
NSA_CMP_HID = 256
NSA_SLC_LEN = 64
NSA_TOPN = 16
NSA_WINDOW = 512
NSA_FORCE = 1e9
N_BRANCH = 4
BRANCH_W = 512
N_EXPERTS = 8
ALPHA = (2.0 * DEPTH) ** 0.25

VMEM_LIMIT_BYTES = 56 * 1024 * 1024
LANES = 128

_ORIG = dict(a_q=(0, 512), a_k=(512, 512), a_v=(1024, 512), b_q=(1536, 512), b_kv=(2048, 128),
             b_iq=(2176, 512), b_ik=(2688, 64), b_iw=(2752, 8), c_q=(2760, 512), c_k=(3272, 128),
             c_v=(3400, 128), d_q=(3528, 512), d_kv=(4040, 768), d_g=(4808, 24))
_NEW_ORDER = ("a_q", "a_k", "a_v", "b_q", "b_iq", "c_q", "d_q", "b_kv", "c_k", "c_v", "d_kv",
              "b_ik", "pad64", "b_iw", "d_g", "pad96", "pad128")
PROJ_W = 5120
BLK512 = dict(a_q=0, a_k=1, a_v=2, b_q=3, b_iq=4, c_q=5, d_q=6)
BLK128 = dict(b_kv=28, c_k=29, c_v=30, d_kc=31, d_vc=32, d_ks=33, d_vs=34, d_kw=35, d_vw=36,
              b_ik=37, small=38)
COL_DKV = 3968


def _cparams(sem):
    return pltpu.CompilerParams(dimension_semantics=sem, vmem_limit_bytes=VMEM_LIMIT_BYTES)


def _sigmoid(x):
    return 1.0 / (1.0 + jnp.exp(-x))


def _silu(x):
    return x * _sigmoid(x)


def _alibi(n_heads):
    return [2.0 ** (-8.0 * (h + 1) / n_heads) for h in range(n_heads)]


def _dot(a, b):
    return jnp.dot(a, b, preferred_element_type=F32)


def _dot_nt(a, b):
    return lax.dot_general(a, b, (((1,), (1,)), ((), ())), preferred_element_type=F32)


def _mm_kernel(*refs, prologue, has_bias, eps):
    it = iter(refs)
    a_ref = next(it)
    g_ref = next(it) if prologue == "rms" else None
    w_ref = next(it)
    b_ref = next(it) if has_bias else None
    o_ref = next(it)
    wb_ref = next(it)

    @pl.when(pl.program_id(1) == 0)
    def _():
        wb_ref[...] = w_ref[0].astype(BF16)

    a = a_ref[...]
    if prologue == "silu":
        a = _silu(a.astype(F32))
    elif prologue == "rms":
        a = a.astype(F32)
        a = a * lax.rsqrt(jnp.mean(a * a, axis=-1, keepdims=True) + eps) * g_ref[...]
    acc = _dot(a.astype(BF16), wb_ref[...])
    if has_bias:
        acc = acc + b_ref[...]
    o_ref[...] = acc.astype(o_ref.dtype)


def _mm(a, w, layer, *, tm, tn, out_dtype=F32, a_blk=0, k=None, prologue=None, gain=None, bias=None,
        eps=1e-6, name="mm"):
    m = a.shape[0]
    k = a.shape[1] if k is None else k
    n = w.shape[2]
    assert w.shape[1] == k and m % tm == 0 and n % tn == 0
    in_specs = [pl.BlockSpec((tm, k), lambda j, i: (i, a_blk))]
    args = [a]
    if prologue == "rms":
        in_specs.append(pl.BlockSpec((1, k), lambda j, i: (0, 0)))
        args.append(gain.reshape(1, k))
    in_specs.append(pl.BlockSpec((1, k, tn), lambda j, i: (layer, 0, j)))
    args.append(w)
    if bias is not None:
        in_specs.append(pl.BlockSpec((1, tn), lambda j, i: (0, j)))
        args.append(bias.reshape(1, n))
    return pl.pallas_call(
        functools.partial(_mm_kernel, prologue=prologue, has_bias=bias is not None, eps=eps),
        grid=(n // tn, m // tm),
        in_specs=in_specs,
        out_specs=pl.BlockSpec((tm, tn), lambda j, i: (i, j)),
        out_shape=jax.ShapeDtypeStruct((m, n), out_dtype),
        scratch_shapes=[pltpu.VMEM((k, tn), BF16)],
        compiler_params=_cparams(("arbitrary", "arbitrary")),
        name=name,
    )(*args)


def _mmk_kernel(a_ref, w_ref, o_ref, acc_ref, *, nk):
    kk = pl.program_id(2)

    @pl.when(kk == 0)
    def _():
        acc_ref[...] = jnp.zeros_like(acc_ref)

    acc_ref[...] += _dot(a_ref[...], w_ref[0].astype(BF16))

    @pl.when(kk == nk - 1)
    def _():
        o_ref[...] = acc_ref[...]


def _mmk(a, w, layer, *, tm, tn, tk, name="mmk"):
    m, k = a.shape
    n = w.shape[2]
    assert w.shape[1] == k and m % tm == 0 and n % tn == 0 and k % tk == 0
    nk = k // tk
    return pl.pallas_call(
        functools.partial(_mmk_kernel, nk=nk),
        grid=(m // tm, n // tn, nk),
        in_specs=[pl.BlockSpec((tm, tk), lambda i, j, kk: (i, kk)),
                  pl.BlockSpec((1, tk, tn), lambda i, j, kk: (layer, kk, j))],
        out_specs=pl.BlockSpec((tm, tn), lambda i, j, kk: (i, j)),
        out_shape=jax.ShapeDtypeStruct((m, n), F32),
        scratch_shapes=[pltpu.VMEM((tm, tn), F32)],
        compiler_params=_cparams(("arbitrary", "arbitrary", "arbitrary")),
        name=name,
    )(a, w)


def _swiglu_kernel(a_ref, wg_ref, wu_ref, o_ref, wgb_ref, wub_ref):
    @pl.when(pl.program_id(1) == 0)
    def _():
        wgb_ref[...] = wg_ref[0].astype(BF16)
        wub_ref[...] = wu_ref[0].astype(BF16)

    a = a_ref[...]
    o_ref[...] = (_silu(_dot(a, wgb_ref[...])) * _dot(a, wub_ref[...])).astype(o_ref.dtype)


def _swiglu_up(u, wg, wu, layer, *, tm, tn, name="swiglu_up"):
    m, k = u.shape
    f = wg.shape[2]
    assert f % tn == 0 and m % tm == 0
    wspec = pl.BlockSpec((1, k, tn), lambda j, i: (layer, 0, j))
    return pl.pallas_call(
        _swiglu_kernel,
        grid=(f // tn, m // tm),
        in_specs=[pl.BlockSpec((tm, k), lambda j, i: (i, 0)), wspec, wspec],
        out_specs=pl.BlockSpec((tm, tn), lambda j, i: (i, j)),
        out_shape=jax.ShapeDtypeStruct((m, f), BF16),
        scratch_shapes=[pltpu.VMEM((k, tn), BF16), pltpu.VMEM((k, tn), BF16)],
        compiler_params=_cparams(("arbitrary", "arbitrary")),
        name=name,
    )(u, wg, wu)


def _modulate_kernel(x_ref, sc_ref, sh_ref, u_ref):
    u_ref[...] = (x_ref[...] * (1.0 + sc_ref[...]) + sh_ref[...]).astype(u_ref.dtype)


def _modulate(x, mod, sc_blk, sh_blk, *, tm):
    m, d = x.shape
    return pl.pallas_call(
        _modulate_kernel,
        grid=(m // tm,),
        in_specs=[pl.BlockSpec((tm, d), lambda i: (i, 0)),
                  pl.BlockSpec((1, d), lambda i: (0, sc_blk)),
                  pl.BlockSpec((1, d), lambda i: (0, sh_blk))],
        out_specs=pl.BlockSpec((tm, d), lambda i: (i, 0)),
        out_shape=jax.ShapeDtypeStruct((m, d), BF16),
        compiler_params=_cparams(("arbitrary",)),
        name="modulate",
    )(x, mod, mod)


def _resid_ln_kernel(x_ref, y_ref, gate_ref, g_ref, b_ref, sc_ref, sh_ref, xo_ref, u_ref):
    z = ALPHA * x_ref[...] + gate_ref[...] * y_ref[...]
    mu = jnp.mean(z, axis=-1, keepdims=True)
    zc = z - mu
    var = jnp.mean(zc * zc, axis=-1, keepdims=True)
    xn = zc * lax.rsqrt(var + 1e-5) * g_ref[...] + b_ref[...]
    xo_ref[...] = xn
    u_ref[...] = (xn * (1.0 + sc_ref[...]) + sh_ref[...]).astype(u_ref.dtype)


def _resid_ln(x, y, mod, gate_blk, g, b, mod_next, sc_blk, sh_blk, *, tm):
    m, d = x.shape
    row = lambda blk: pl.BlockSpec((1, d), lambda i: (0, blk))
    return pl.pallas_call(
        _resid_ln_kernel,
        grid=(m // tm,),
        in_specs=[pl.BlockSpec((tm, d), lambda i: (i, 0)),
                  pl.BlockSpec((tm, d), lambda i: (i, 0)),
                  row(gate_blk), row(0), row(0), row(sc_blk), row(sh_blk)],
        out_specs=[pl.BlockSpec((tm, d), lambda i: (i, 0)),
                   pl.BlockSpec((tm, d), lambda i: (i, 0))],
        out_shape=[jax.ShapeDtypeStruct((m, d), F32), jax.ShapeDtypeStruct((m, d), BF16)],
        compiler_params=_cparams(("arbitrary",)),
        name="resid_ln",
    )(x, y, mod, g.reshape(1, d), b.reshape(1, d), mod_next, mod_next)


FLASH_ROW_CHUNK = 32
POS_SPLIT = 128


_KV_PACK = (("a_k", "a_k", 0, 2 * DA_HEADS, 64, "k"), ("a_v", "a_v", 0, DA_HEADS, 128, "v"),
            ("b_k", "kv_b", 0, DSA_HEADS, 64, "k"), ("b_v", "kv_b", 512, DSA_HEADS, 64, "v"),
            ("c_k", "c_k", 0, 2, 64, "k"), ("c_v", "c_v", 0, 2, 64, "v"),
            ("d_ks", "d_ks", 0, 2, 64, "k"), ("d_vs", "d_vs", 0, 2, 64, "v"),
            ("d_kw", "d_kw", 0, 2, 64, "k"), ("d_vw", "d_vw", 0, 2, 64, "v"),
            ("b_ik", "b_ik", 0, 1, 128, "cast"))
_KV_SOURCES = ("a_k", "a_v", "c_k", "c_v", "d_ks", "d_vs", "d_kw", "d_vw", "b_ik", "kv_b")


def _kv_pack_kernel(*refs, tm):
    src = dict(zip(_KV_SOURCES, refs[:len(_KV_SOURCES)]))
    outs = refs[len(_KV_SOURCES):]
    i = pl.program_id(0)
    tails = {}
    for w in (64, 128):
        lane = lax.broadcasted_iota(jnp.int32, (tm, w), 1)
        pos = i * tm + lax.broadcasted_iota(jnp.int32, (tm, w), 0)
        tails[("v", w)] = jnp.where(lane == 0, 1.0, 0.0)
        tails[("k", w)] = jnp.where(lane == 0, (pos // POS_SPLIT).astype(F32),
                                    jnp.where(lane == 1, (pos % POS_SPLIT).astype(F32),
                                              jnp.where(lane < 4, 1.0, 0.0)))
    for (_, sname, c0, n_heads, w, kind), o_ref in zip(_KV_PACK, outs):
        if kind == "cast":
            o_ref[...] = src[sname][...].astype(BF16)
            continue
        for h in range(n_heads):
            x = src[sname][:, c0 + h * w:c0 + (h + 1) * w]
            o_ref[:, 2 * h * w:2 * (h + 1) * w] = jnp.concatenate([x, tails[(kind, w)]], axis=1).astype(BF16)


def _kv_pack(proj, kv_b, *, tm):
    s_len = proj.shape[0]
    in_specs, args = [], []
    for sname in _KV_SOURCES:
        if sname == "kv_b":
            in_specs.append(pl.BlockSpec((tm, kv_b.shape[1]), lambda i: (i, 0)))
            args.append(kv_b)
        else:
            wblk, blk = (512, BLK512[sname]) if sname in BLK512 else (128, BLK128[sname])
            in_specs.append(pl.BlockSpec((tm, wblk), functools.partial(lambda i, blk: (i, blk), blk=blk)))
            args.append(proj)
    widths = [(1 if kind == "cast" else 2) * n_heads * w for (_, _, _, n_heads, w, kind) in _KV_PACK]
    outs = pl.pallas_call(
        functools.partial(_kv_pack_kernel, tm=tm),
        grid=(s_len // tm,),
        in_specs=in_specs,
        out_specs=[pl.BlockSpec((tm, wd), lambda i: (i, 0)) for wd in widths],
        out_shape=[jax.ShapeDtypeStruct((s_len, wd), BF16) for wd in widths],
        compiler_params=_cparams(("arbitrary",)),
        name="kv_pack",
    )(*args)
    return {name: o for (name, *_), o in zip(_KV_PACK, outs)}


def _flash_kernel(*refs, units, tq, tk, window, dense, n_prev, nsteps, n_mask, has_sink):
    it = iter(refs)
    q_ref, k_ref, v_ref = next(it), next(it), next(it)
    mask_ref = next(it) if n_mask else None
    sink_ref = next(it) if has_sink else None
    o_ref = next(it)
    q_scr, m_scr, acc_scr, bias_scr = (next(it) for _ in range(4))
    dv = acc_scr.shape[2]
    lcol = dv // 2

    qi = pl.program_id(0)
    j = pl.program_id(1)
    last_kb = (qi * tq + tq - 1) // tk
    kb = j if dense else qi - n_prev + j
    rows = q_scr.shape[1]
    rb = FLASH_ROW_CHUNK
    kw = 2 * HEAD_DIM

    @pl.when(j == 0)
    def _init():
        lane = lax.broadcasted_iota(jnp.int32, (tq, HEAD_DIM), 1)
        qpos = qi * tq + lax.broadcasted_iota(jnp.int32, (tq, HEAD_DIM), 0)
        qhi = (qpos // POS_SPLIT).astype(F32)
        qlo = (qpos % POS_SPLIT).astype(F32)
        for ui, (_, _, _, _, hds) in enumerate(units):
            for r, (qo, slope, _, sink_idx) in enumerate(hds):
                rsl = slice(r * tq, (r + 1) * tq)
                tail = jnp.where(lane == 0, POS_SPLIT * slope,
                                 jnp.where(lane == 1, slope,
                                           jnp.where(lane == 2, -POS_SPLIT * slope * qhi,
                                                     jnp.where(lane == 3, -slope * qlo, 0.0))))
                qs = q_ref[:, qo:qo + HEAD_DIM] * HEAD_DIM ** -0.5
                q_scr[ui, rsl] = jnp.concatenate([qs, tail], axis=1).astype(BF16)
                if has_sink:
                    m_scr[ui, rsl] = jnp.broadcast_to(sink_ref[:, sink_idx:sink_idx + 1], (tq, 1))
                else:
                    m_scr[ui, rsl] = jnp.full((tq, 1), NEG, F32)
            alane = lax.broadcasted_iota(jnp.int32, acc_scr.shape[1:], 1)
            acc_scr[ui] = jnp.where(alane == lcol, 1.0 if has_sink else 0.0, 0.0)

    def scores(ui):
        ku = units[ui][0]
        return _dot_nt(q_scr[ui], k_ref[:, ku * kw:(ku + 1) * kw])

    def step(masked):
        if masked:
            qpos = qi * tq + lax.broadcasted_iota(jnp.int32, (tq, tk), 0)
            kpos = kb * tk + lax.broadcasted_iota(jnp.int32, (tq, tk), 1)
            dist = qpos - kpos
            valid = dist >= 0
            if not dense:
                valid = valid & (dist < window)
            if n_mask:
                for g in range(n_mask):
                    bias_scr[g] = jnp.where(valid, mask_ref[g].astype(F32), NEG)
            else:
                bias_scr[0] = jnp.where(valid, 0.0, NEG)

        def chunk(s, mg, c):
            r0 = c * rb
            sc = s[r0:r0 + rb]
            if masked:
                rw = r0 % tq
                sc = sc + bias_scr[mg, rw:rw + rb]
            return sc

        s_next = scores(0)
        for ui, (_, vo, _, mg, _) in enumerate(units):
            s = s_next
            if ui + 1 < len(units):
                s_next = scores(ui + 1)
            nchunk = rows // rb
            m_old = m_scr[ui]
            m_cur = jnp.concatenate([jnp.max(chunk(s, mg, c), axis=1, keepdims=True) for c in range(nchunk)],
                                    axis=0)
            m_new = jnp.maximum(m_old, m_cur)
            alpha = jnp.exp(m_old - m_new)
            m_scr[ui] = m_new
            p_all = jnp.concatenate(
                [jnp.exp(chunk(s, mg, c) - m_new[c * rb:(c + 1) * rb]).astype(BF16) for c in range(nchunk)],
                axis=0)
            acc_scr[ui] = alpha * acc_scr[ui] + _dot(p_all, v_ref[:, vo:vo + dv])

    needed = (kb >= 0) & (kb <= last_kb)
    if dense and not n_mask:
        interior = kb * tk + tk - 1 <= qi * tq
        pl.when(needed & interior)(lambda: step(False))
        pl.when(needed & jnp.logical_not(interior))(lambda: step(True))
    else:
        pl.when(needed)(lambda: step(True))

    @pl.when(j == nsteps - 1)
    def _fin():
        for ui, (_, _, _, _, hds) in enumerate(units):
            for r, (_, _, (oo, ow), _) in enumerate(hds):
                rsl = slice(r * tq, (r + 1) * tq)
                acc = acc_scr[ui, rsl]
                o_ref[:, oo:oo + ow] = acc[:, 0:ow] / acc[:, lcol:lcol + 1]


def _flash(q_arr, k_arr, v_arr, *, units, q_spec, out_w, tq, tk, window=None,
           mask=None, sinks=None, name="flash"):
    s_len = q_arr.shape[0]
    dense = window is None
    if dense:
        n_prev, nsteps = 0, s_len // tk
    else:
        assert tq == tk
        n_prev = -(-(window - 1) // tk)
        nsteps = n_prev + 1
    n_mask = 0 if mask is None else mask.shape[0]
    dv = units[0][2]
    nu = len(units)
    rows = len(units[0][4]) * tq
    assert all(len(un[4]) * tq == rows and un[2] == dv for un in units) and rows % FLASH_ROW_CHUNK == 0

    def kv_index(qi, j):
        last_kb = (qi * tq + tq - 1) // tk
        kb = j if dense else qi - n_prev + j
        return jnp.clip(kb, 0, last_kb)

    in_specs = [pl.BlockSpec((tq, q_spec[0]), lambda qi, j: (qi, q_spec[1])),
                pl.BlockSpec((tk, k_arr.shape[1]), lambda qi, j: (kv_index(qi, j), 0)),
                pl.BlockSpec((tk, v_arr.shape[1]), lambda qi, j: (kv_index(qi, j), 0))]
    args = [q_arr, k_arr, v_arr]
    if n_mask:
        in_specs.append(pl.BlockSpec((n_mask, tq, tk), lambda qi, j: (0, qi, kv_index(qi, j))))
        args.append(mask)
    if sinks is not None:
        in_specs.append(pl.BlockSpec((1, LANES), lambda qi, j: (0, 0)))
        args.append(sinks)
    return pl.pallas_call(
        functools.partial(_flash_kernel, units=tuple(units), tq=tq, tk=tk, window=window, dense=dense,
                          n_prev=n_prev, nsteps=nsteps, n_mask=n_mask, has_sink=sinks is not None),
        grid=(s_len // tq, nsteps),
        in_specs=in_specs,
        out_specs=pl.BlockSpec((tq, out_w), lambda qi, j: (qi, 0)),
        out_shape=jax.ShapeDtypeStruct((s_len, out_w), F32),
        scratch_shapes=[pltpu.VMEM((nu, rows, 2 * HEAD_DIM), BF16), pltpu.VMEM((nu, rows, 1), F32),
                        pltpu.VMEM((nu, rows, dv), F32), pltpu.VMEM((max(n_mask, 1), tq, tk), F32)],
        compiler_params=_cparams(("arbitrary", "arbitrary")),
        name=name,
    )(*args)


def _diff_final_kernel(o_ref, lam_ref, g_ref, out_ref, *, lambda_init):
    lf = lam_ref[0]
    lam = (jnp.exp(jnp.sum(lf[0:1] * lf[1:2])) - jnp.exp(jnp.sum(lf[2:3] * lf[3:4])) + lambda_init)
    w = 2 * HEAD_DIM
    for h in range(DA_HEADS):
        o = o_ref[:, (2 * h) * w:(2 * h + 1) * w] - lam * o_ref[:, (2 * h + 1) * w:(2 * h + 2) * w]
        o = o * lax.rsqrt(jnp.mean(o * o, axis=-1, keepdims=True) + 1e-6) * g_ref[...]
        out_ref[:, h * w:(h + 1) * w] = o * (1.0 - lambda_init)


def _diff_final(o, diff_lambda, layer, subln_g, lambda_init, *, tm):
    m = o.shape[0]
    w = 2 * HEAD_DIM
    return pl.pallas_call(
        functools.partial(_diff_final_kernel, lambda_init=lambda_init),
        grid=(m // tm,),
        in_specs=[pl.BlockSpec((tm, 2 * DA_HEADS * w), lambda i: (i, 0)),
                  pl.BlockSpec((1, 4, HEAD_DIM), lambda i: (layer, 0, 0)),
                  pl.BlockSpec((1, w), lambda i: (0, 0))],
        out_specs=pl.BlockSpec((tm, DA_HEADS * w), lambda i: (i, 0)),
        out_shape=jax.ShapeDtypeStruct((m, DA_HEADS * w), F32),
        compiler_params=_cparams(("arbitrary",)),
        name="diff_final",
    )(o, diff_lambda, subln_g.reshape(1, w))


def _f32_key_const(x):
    b = int(np.array(x, np.float32).view(np.int32))
    return b ^ ((b >> 31) & 0x7FFFFFFF)


I16_MIN = -(2 ** 15)


def _dsa_select_kernel(qi_ref, w_ref, kidx_ref, mask_ref, key_scr, half_scr, j_scr, *, tq, ch, nch, topk,
                       s_len):
    i = pl.program_id(0)
    q0 = i * tq
    n_need = (q0 + tq + ch - 1) // ch
    qpos = q0 + lax.broadcasted_iota(jnp.int32, (tq, 1), 0)
    lane = lax.broadcasted_iota(jnp.int32, (1, ch), 1)
    w = w_ref[:, 0:IDX_HEADS]
    q_all = jnp.concatenate([qi_ref[:, h * HEAD_DIM:(h + 1) * HEAD_DIM] for h in range(IDX_HEADS)],
                            axis=0).astype(BF16)

    def score_chunk(c, carry):
        kc = kidx_ref[pl.ds(pl.multiple_of(c * ch, ch), ch), 0:HEAD_DIM].astype(BF16)
        lg = _dot_nt(q_all, kc)
        acc = jnp.zeros((tq, ch), F32)
        for h in range(IDX_HEADS):
            acc = acc + w[:, h:h + 1] * jnp.maximum(lg[h * tq:(h + 1) * tq], 0.0)
        acc = jnp.where(c * ch + lane <= qpos, acc, NEG) + 0.0
        bits = pltpu.bitcast(acc, jnp.int32)
        key = bits ^ ((bits >> 31) & 0x7FFFFFFF)
        key_scr[c] = key
        half_scr[c] = (key >> 16).astype(jnp.int16)
        return carry

    lax.fori_loop(0, n_need, score_chunk, 0)

    def count16(cand, strict):
        cand16 = jnp.broadcast_to(cand, (tq, LANES)).astype(jnp.int16)
        one, zero = jnp.int16(1), jnp.int16(0)

        def body(c, acc):
            blk = half_scr[c]
            for t in range(ch // LANES):
                tile = blk[:, t * LANES:(t + 1) * LANES]
                acc = acc + jnp.where(tile > cand16 if strict else tile >= cand16, one, zero)
            return acc
        acc = lax.fori_loop(0, n_need, body, jnp.zeros((tq, LANES), jnp.int16))
        return jnp.sum(acc.astype(jnp.int32), axis=1, keepdims=True)

    def search16(need_cnt):
        def bit_step(b, t):
            cand = t + jnp.left_shift(jnp.int32(1), 15 - b)
            return jnp.where(count16(cand, False) >= need_cnt, cand, t)
        return lax.fori_loop(0, 16, bit_step, jnp.full((tq, 1), I16_MIN, jnp.int32))

    t_hi = search16(topk)
    need_lo = topk - count16(t_hi, True)

    def low_chunk(c, carry):
        key = key_scr[c]
        low = (key & 0xFFFF) + I16_MIN
        half_scr[c] = jnp.where((key >> 16) == t_hi, low, I16_MIN).astype(jnp.int16)
        return carry

    lax.fori_loop(0, n_need, low_chunk, 0)
    t_lo = search16(need_lo)
    thr = jnp.left_shift(t_hi, 16) + (t_lo - I16_MIN)

    def count(pred):
        def body(c, acc):
            m = jnp.where(pred(key_scr[c], c), 1, 0)
            part = m[:, 0:LANES]
            for t in range(1, ch // LANES):
                part = part + m[:, t * LANES:(t + 1) * LANES]
            return acc + part
        acc = lax.fori_loop(0, n_need, body, jnp.zeros((tq, LANES), jnp.int32))
        return jnp.sum(acc, axis=1, keepdims=True)

    cnt_gt = count(lambda blk, c: blk > thr)
    cnt_ge = count(lambda blk, c: blk >= thr)
    need = topk - cnt_gt
    tie_rows = (cnt_ge > topk) & (thr > _f32_key_const(NEG))
    j_scr[...] = jnp.full((tq, 1), s_len, jnp.int32)
    any_tie = jnp.max(jnp.where(tie_rows, 1, 0)) > 0

    @pl.when(any_tie)
    def _ties():
        def tie_chunk(c, carry):
            half_scr[c] = jnp.where(key_scr[c] == thr, -1 - (c * ch + lane), I16_MIN).astype(jnp.int16)
            return carry

        lax.fori_loop(0, n_need, tie_chunk, 0)
        j_scr[...] = jnp.where(tie_rows, -1 - search16(need), s_len)

    jv = j_scr[...]
    for c in range(nch):
        @pl.when((c < n_need) & any_tie)
        def _w():
            key = key_scr[c]
            sel = (key > thr) | ((key == thr) & (c * ch + lane <= jv))
            mask_ref[:, c * ch:(c + 1) * ch] = jnp.where(sel, 0.0, NEG).astype(mask_ref.dtype)

        @pl.when((c < n_need) & jnp.logical_not(any_tie))
        def _wf():
            mask_ref[:, c * ch:(c + 1) * ch] = jnp.where(key_scr[c] >= thr, 0.0, NEG).astype(mask_ref.dtype)

        @pl.when(c >= n_need)
        def _z():
            mask_ref[:, c * ch:(c + 1) * ch] = jnp.full((tq, ch), NEG, mask_ref.dtype)


def _dsa_select(proj, kidx, *, topk, tq=128):
    s_len = proj.shape[0]
    assert s_len < -I16_MIN
    ch = min(1024, s_len)
    nch = s_len // ch
    return pl.pallas_call(
        functools.partial(_dsa_select_kernel, tq=tq, ch=ch, nch=nch, topk=topk, s_len=s_len),
        grid=(s_len // tq,),
        in_specs=[pl.BlockSpec((tq, 512), lambda i: (i, BLK512["b_iq"])),
                  pl.BlockSpec((tq, LANES), lambda i: (i, BLK128["small"])),
                  pl.BlockSpec((s_len, LANES), lambda i: (0, 0))],
        out_specs=pl.BlockSpec((tq, s_len), lambda i: (i, 0)),
        out_shape=jax.ShapeDtypeStruct((s_len, s_len), BF16),
        scratch_shapes=[pltpu.VMEM((nch, tq, ch), jnp.int32), pltpu.VMEM((nch, tq, ch), jnp.int16),
                        pltpu.VMEM((tq, 1), jnp.int32)],
        compiler_params=_cparams(("arbitrary",)),
        name="dsa_select",
    )(proj, proj, kidx)


def _nsa_compress_kernel(x_ref, pos_ref, w1_ref, w2_ref, o_ref):
    x = (x_ref[0] + pos_ref[0, 0]).astype(BF16)
    hdn = _silu(_dot(x, w1_ref[0, 0].astype(BF16)))
    o_ref[0] = _dot(hdn.astype(BF16), w2_ref[0, 0].astype(BF16))


def _nsa_compress(xc, pos, w1, w2, layer):
    _, ncp, kdim = xc.shape
    return pl.pallas_call(
        _nsa_compress_kernel,
        grid=(4,),
        in_specs=[pl.BlockSpec((1, ncp, kdim), lambda i: (i, 0, 0)),
                  pl.BlockSpec((1, 1, 1, kdim), lambda i: (layer, i // 2, 0, 0)),
                  pl.BlockSpec((1, 1, kdim, NSA_CMP_HID), lambda i: (layer, i // 2, 0, 0)),
                  pl.BlockSpec((1, 1, NSA_CMP_HID, HEAD_DIM), lambda i: (layer, i // 2, 0, 0))],
        out_specs=pl.BlockSpec((1, ncp, HEAD_DIM), lambda i: (i, 0, 0)),
        out_shape=jax.ShapeDtypeStruct((4, ncp, HEAD_DIM), F32),
        compiler_params=_cparams(("arbitrary",)),
        name="nsa_compress",
    )(xc, pos, w1, w2)


def _nsa_cmp_kernel(q_ref, kv_ref, ov_ref, ex_ref, o_ref, mask_ref, *, tq, ncp, n_slc, topn, ch, nch):
    i = pl.program_id(0)
    q0 = i * tq
    n_need = (q0 + tq + ch - 1) // ch
    rpg = NSA_HEADS // NSA_GROUPS
    slopes = _alibi(NSA_HEADS)
    scale = HEAD_DIM ** -0.5
    qpos_c = q0 + lax.broadcasted_iota(jnp.int32, (tq, ncp), 0)
    cend = lax.broadcasted_iota(jnp.int32, (tq, ncp), 1) * NSA_CMP_STRIDE + (NSA_CMP_LEN - 1)
    dist_c = qpos_c - cend
    valid_c = dist_c >= 0
    distf = dist_c.astype(F32)
    qpos = q0 + lax.broadcasted_iota(jnp.int32, (tq, n_slc), 0)
    blk = lax.broadcasted_iota(jnp.int32, (tq, n_slc), 1)
    cur = qpos // NSA_SLC_LEN
    forced = (blk == 0) | (blk == cur) | (blk == cur - 1)
    blk_ok = blk * NSA_SLC_LEN <= qpos
    ov = ov_ref[...]
    imps = []
    for g in range(NSA_GROUPS):
        kc = kv_ref[g].astype(BF16)
        vc = kv_ref[NSA_GROUPS + g].astype(BF16)
        psum = jnp.zeros((tq, ncp), F32)
        for r in range(rpg):
            h = g * rpg + r
            qh = q_ref[:, h * HEAD_DIM:(h + 1) * HEAD_DIM].astype(BF16)
            s = _dot_nt(qh, kc) * scale - slopes[h] * distf
            s = jnp.where(valid_c, s, NEG)
            e = jnp.where(valid_c, jnp.exp(s - jnp.max(s, axis=1, keepdims=True)), 0.0)
            p = e / jnp.maximum(jnp.sum(e, axis=1, keepdims=True), 1e-30)
            o_ref[:, h * HEAD_DIM:(h + 1) * HEAD_DIM] = _dot(p.astype(BF16), vc)
            psum = psum + p
        p_hi = psum.astype(BF16)
        p_lo = (psum - p_hi.astype(F32)).astype(BF16)
        imp = _dot(p_hi, ov) + _dot(p_lo, ov)
        imp = jnp.where(forced, NSA_FORCE, imp)
        imps.append(jnp.where(blk_ok, imp, NEG))
    imps = [imp.T for imp in imps]
    blk_t = lax.broadcasted_iota(jnp.int32, (n_slc, tq), 0)
    sels = [jnp.full((n_slc, tq), NEG, F32) for _ in range(NSA_GROUPS)]
    for _ in range(topn):
        for g in range(NSA_GROUPS):
            mx = jnp.max(imps[g], axis=0, keepdims=True)
            first = jnp.min(jnp.where(imps[g] == mx, blk_t, n_slc), axis=0, keepdims=True)
            hit = blk_t == first
            sels[g] = jnp.where(hit, 0.0, sels[g])
            imps[g] = jnp.where(hit, -jnp.inf, imps[g])
    sels = [sel.T for sel in sels]
    for g in range(NSA_GROUPS):
        selb = sels[g].astype(BF16)
        for c in range(nch):
            @pl.when(c < n_need)
            def _w():
                tok = _dot(selb, ex_ref[:, c * ch:(c + 1) * ch])
                mask_ref[g, :, c * ch:(c + 1) * ch] = tok.astype(mask_ref.dtype)

            @pl.when(c >= n_need)
            def _z():
                mask_ref[g, :, c * ch:(c + 1) * ch] = jnp.full((tq, ch), NEG, mask_ref.dtype)


def _nsa_cmp(proj, kv_cmp, *, tq=128):
    s_len = proj.shape[0]
    ncp = kv_cmp.shape[1]
    n_slc = s_len // NSA_SLC_LEN
    topn = min(NSA_TOPN, n_slc)
    ch = min(1024, s_len)
    nch = s_len // ch
    starts = np.arange(ncp) * NSA_CMP_STRIDE
    slc_start = np.arange(n_slc) * NSA_SLC_LEN
    overlap = ((starts[:, None] < slc_start[None, :] + NSA_SLC_LEN)
               & (starts[:, None] + NSA_CMP_LEN > slc_start[None, :])).astype(np.float32)
    expand = (np.arange(s_len)[None, :] // NSA_SLC_LEN == np.arange(n_slc)[:, None]).astype(np.float32)
    return pl.pallas_call(
        functools.partial(_nsa_cmp_kernel, tq=tq, ncp=ncp, n_slc=n_slc, topn=topn, ch=ch, nch=nch),
        grid=(s_len // tq,),
        in_specs=[pl.BlockSpec((tq, 512), lambda i: (i, BLK512["d_q"])),
                  pl.BlockSpec((4, ncp, HEAD_DIM), lambda i: (0, 0, 0)),
                  pl.BlockSpec((ncp, n_slc), lambda i: (0, 0)),
                  pl.BlockSpec((n_slc, s_len), lambda i: (0, 0))],
        out_specs=[pl.BlockSpec((tq, 512), lambda i: (i, 0)),
                   pl.BlockSpec((NSA_GROUPS, tq, s_len), lambda i: (0, i, 0))],
        out_shape=[jax.ShapeDtypeStruct((s_len, 512), F32),
                   jax.ShapeDtypeStruct((NSA_GROUPS, s_len, s_len), BF16)],
        compiler_params=_cparams(("arbitrary",)),
        name="nsa_cmp",
    )(proj, kv_cmp, jnp.asarray(overlap, BF16), jnp.asarray(expand, BF16))


def _nsa_combine_kernel(g_ref, oc_ref, os_ref, ow_ref, o_ref):
    gt = _sigmoid(g_ref[...])
    for h in range(NSA_HEADS):
        sl = slice(h * HEAD_DIM, (h + 1) * HEAD_DIM)
        c0 = IDX_HEADS + 3 * h
        o_ref[:, sl] = (gt[:, c0:c0 + 1] * oc_ref[:, sl] + gt[:, c0 + 1:c0 + 2] * os_ref[:, sl]
                        + gt[:, c0 + 2:c0 + 3] * ow_ref[:, sl])


def _nsa_combine(proj, o_cmp, o_slc, o_win, *, tm):
    m = proj.shape[0]
    spec = pl.BlockSpec((tm, 512), lambda i: (i, 0))
    return pl.pallas_call(
        _nsa_combine_kernel,
        grid=(m // tm,),
        in_specs=[pl.BlockSpec((tm, LANES), lambda i: (i, BLK128["small"])), spec, spec, spec],
        out_specs=spec,
        out_shape=jax.ShapeDtypeStruct((m, 512), F32),
        compiler_params=_cparams(("arbitrary",)),
        name="nsa_combine",
    )(proj, o_cmp, o_slc, o_win)


def _merge_kernel(u_ref, oa_ref, ob_ref, oc_ref, od_ref, wg0, wg1, wg2, wg3, wb_ref, o_ref,
                  wgb_ref, wbb_ref):
    wgs = (wg0, wg1, wg2, wg3)

    @pl.when(pl.program_id(1) == 0)
    def _():
        for mch in range(N_BRANCH):
            wgb_ref[mch] = wgs[mch][0].astype(BF16)
            wbb_ref[mch] = wb_ref[0, mch].astype(BF16)

    u = u_ref[...]
    acc = None
    for mch, o_ref_m in enumerate((oa_ref, ob_ref, oc_ref, od_ref)):
        gte = _sigmoid(_dot(u, wgb_ref[mch]))
        z = _dot(o_ref_m[...].astype(BF16), wbb_ref[mch])
        acc = gte * z if acc is None else acc + gte * z
    o_ref[...] = acc.astype(o_ref.dtype)


def _merge(u, branches, w_gate, w_branch, layer, *, tm, tn):
    m, d = u.shape
    nj = d // tn
    bspec = pl.BlockSpec((tm, BRANCH_W), lambda j, i: (i, 0))
    wg_specs = [pl.BlockSpec((1, d, tn),
                             functools.partial(lambda j, i, mch: (layer, 0, mch * nj + j), mch=mch))
                for mch in range(N_BRANCH)]
    return pl.pallas_call(
        _merge_kernel,
        grid=(nj, m // tm),
        in_specs=[pl.BlockSpec((tm, d), lambda j, i: (i, 0)), bspec, bspec, bspec, bspec,
                  *wg_specs,
                  pl.BlockSpec((1, N_BRANCH, BRANCH_W, tn), lambda j, i: (layer, 0, 0, j))],
        out_specs=pl.BlockSpec((tm, tn), lambda j, i: (i, j)),
        out_shape=jax.ShapeDtypeStruct((m, d), BF16),
        scratch_shapes=[pltpu.VMEM((N_BRANCH, d, tn), BF16), pltpu.VMEM((N_BRANCH, BRANCH_W, tn), BF16)],
        compiler_params=_cparams(("arbitrary", "arbitrary")),
        name="merge",
    )(u, *branches, w_gate, w_gate, w_gate, w_gate, w_branch)


def _router_kernel(u_ref, r_ref, o_ref):
    logits = _dot(u_ref[...], r_ref[0].astype(BF16))
    lane = lax.broadcasted_iota(jnp.int32, logits.shape, 1)
    lg = jnp.where(lane < N_EXPERTS, logits, -jnp.inf)
    m1 = jnp.max(lg, axis=1, keepdims=True)
    i1 = jnp.min(jnp.where(lg == m1, lane, LANES), axis=1, keepdims=True)
    lg2 = jnp.where(lane == i1, -jnp.inf, lg)
    m2 = jnp.max(lg2, axis=1, keepdims=True)
    i2 = jnp.min(jnp.where(lg2 == m2, lane, LANES), axis=1, keepdims=True)
    e2 = jnp.exp(m2 - m1)
    w1 = 1.0 / (1.0 + e2)
    w2 = e2 / (1.0 + e2)
    o_ref[...] = jnp.where(lane == 0, i1.astype(F32),
                           jnp.where(lane == 1, i2.astype(F32),
                                     jnp.where(lane == 2, w1, jnp.where(lane == 3, w2, 0.0))))


def _router(u, router_padded, layer, *, tm):
    m, d = u.shape
    return pl.pallas_call(
        _router_kernel,
        grid=(m // tm,),
        in_specs=[pl.BlockSpec((tm, d), lambda i: (i, 0)),
                  pl.BlockSpec((1, d, LANES), lambda i: (layer, 0, 0))],
        out_specs=pl.BlockSpec((tm, LANES), lambda i: (i, 0)),
        out_shape=jax.ShapeDtypeStruct((m, LANES), F32),
        compiler_params=_cparams(("arbitrary",)),
        name="router",
    )(u, router_padded)


MOE_ROW_TILE = 256
MOE_TOK_CHUNK = 256


def _moe_plan(ridx, rw, s_len):
    tm, ct, n_e = MOE_ROW_TILE, MOE_TOK_CHUNK, N_EXPERTS
    i32 = jnp.int32
    e_a = ridx.reshape(-1).astype(i32)
    oh = (e_a[:, None] == jnp.arange(n_e, dtype=i32)[None, :]).astype(i32)
    csum = jnp.cumsum(oh, axis=0)
    rank_a = jnp.sum((csum - oh) * oh, axis=1)
    ntile_e = (csum[-1] + tm - 1) // tm
    tile_end = jnp.cumsum(ntile_e)
    pos_a = jnp.take(tile_end - ntile_e, e_a) * tm + rank_a
    n_rows = 2 * s_len + n_e * tm
    n_tiles = n_rows // tm
    n_chunks = s_len // ct
    row_tok = jnp.full((n_rows,), -1, i32).at[pos_a].set(jnp.arange(2 * s_len, dtype=i32) // 2)
    row_w = jnp.zeros((n_rows,), F32).at[pos_a].set(rw.reshape(-1))
    tile_e = jnp.minimum(jnp.searchsorted(tile_end, jnp.arange(n_tiles, dtype=i32), side="right"),
                         n_e - 1).astype(i32)
    rt = row_tok.reshape(n_tiles, tm)
    lo = jnp.min(jnp.where(rt >= 0, rt, s_len - 1), axis=1) // ct
    hi = jnp.maximum(jnp.max(jnp.where(rt >= 0, rt, 0), axis=1) // ct, lo)
    n_i = hi - lo + 1
    end = jnp.cumsum(n_i)
    n_work = n_tiles + n_e * n_chunks
    w = jnp.arange(n_work, dtype=i32)
    wt = jnp.minimum(jnp.searchsorted(end, w, side="right"), n_tiles - 1).astype(i32)
    wc = jnp.clip(jnp.take(lo, wt) + w - jnp.take(end - n_i, wt), 0, n_chunks - 1).astype(i32)
    wa = (w < end[-1]).astype(i32)
    order = jnp.argsort(jnp.where(wa > 0, wc * n_tiles + wt, n_chunks * n_tiles + w))
    vc = jnp.where(wa > 0, wc, n_chunks - 1)[order]
    return dict(row_tok=row_tok, row_w=row_w, tile_e=tile_e, n_tiles=n_tiles, n_work=n_work,
                gather=(wt, wc, wa), combine=(vc, wt[order], wa[order]))


def _moe_gather_kernel(wt_ref, wc_ref, wa_ref, tok_ref, u_ref, o_ref):
    w = pl.program_id(0)

    @pl.when((w == 0) | (wt_ref[jnp.maximum(w - 1, 0)] != wt_ref[w]))
    def _():
        o_ref[...] = jnp.zeros_like(o_ref)

    @pl.when(wa_ref[w] > 0)
    def _():
        ct = u_ref.shape[0]
        cols = wc_ref[w] * ct + lax.broadcasted_iota(jnp.int32, (1, ct), 1)
        onehot = jnp.where(tok_ref[...] == cols, 1.0, 0.0).astype(BF16)
        o_ref[...] += _dot(onehot, u_ref[...]).astype(o_ref.dtype)


def _moe_gather(u, plan):
    s_len, d = u.shape
    tm, ct = MOE_ROW_TILE, MOE_TOK_CHUNK
    n_rows = plan["row_tok"].shape[0]
    return pl.pallas_call(
        _moe_gather_kernel,
        grid_spec=pltpu.PrefetchScalarGridSpec(
            num_scalar_prefetch=3, grid=(plan["n_work"],),
            in_specs=[pl.BlockSpec((tm, 1), lambda w, wt, wc, wa: (wt[w], 0)),
                      pl.BlockSpec((ct, d), lambda w, wt, wc, wa: (wc[w], 0))],
            out_specs=pl.BlockSpec((tm, d), lambda w, wt, wc, wa: (wt[w], 0))),
        out_shape=jax.ShapeDtypeStruct((n_rows, d), BF16),
        compiler_params=_cparams(("arbitrary",)),
        name="moe_gather",
    )(*plan["gather"], plan["row_tok"].reshape(n_rows, 1), u)


def _moe_combine_kernel(vc_ref, vt_ref, va_ref, tok_ref, y0_ref, y1_ref, y2_ref, o_ref):
    w = pl.program_id(0)
    chunk = vc_ref[w]

    @pl.when((w == 0) | (vc_ref[jnp.maximum(w - 1, 0)] != chunk))
    def _():
        o_ref[...] = jnp.zeros_like(o_ref)

    @pl.when(va_ref[w] > 0)
    def _():
        ct = o_ref.shape[0]
        rows = chunk * ct + lax.broadcasted_iota(jnp.int32, (ct, 1), 0)
        onehot_t = jnp.where(rows == tok_ref[0], 1.0, 0.0).astype(BF16)
        o_ref[...] += (_dot(onehot_t, y0_ref[...]) + _dot(onehot_t, y1_ref[...])
                       + _dot(onehot_t, y2_ref[...]))


def _moe_combine(ys3, plan, s_len):
    n_rows, d = ys3[0].shape
    tm, ct = MOE_ROW_TILE, MOE_TOK_CHUNK
    yspec = pl.BlockSpec((tm, d), lambda w, vc, vt, va: (vt[w], 0))
    return pl.pallas_call(
        _moe_combine_kernel,
        grid_spec=pltpu.PrefetchScalarGridSpec(
            num_scalar_prefetch=3, grid=(plan["n_work"],),
            in_specs=[pl.BlockSpec((1, 1, tm), lambda w, vc, vt, va: (vt[w], 0, 0)), yspec, yspec, yspec],
            out_specs=pl.BlockSpec((ct, d), lambda w, vc, vt, va: (vc[w], 0))),
        out_shape=jax.ShapeDtypeStruct((s_len, d), F32),
        compiler_params=_cparams(("arbitrary",)),
        name="moe_combine",
    )(*plan["combine"], plan["row_tok"].reshape(plan["n_tiles"], 1, tm), *ys3)


def _gmm_kernel(te_ref, *refs, swiglu):
    it = iter(refs)
    a_ref = next(it)
    w_refs = [next(it), next(it)] if swiglu else [next(it)]
    rw_ref = next(it) if swiglu else None
    o_refs = [next(it)] if swiglu else [next(it), next(it), next(it)]
    wb_refs = [next(it) for _ in w_refs]
    i = pl.program_id(1)

    @pl.when((i == 0) | (te_ref[i] != te_ref[jnp.maximum(i - 1, 0)]))
    def _():
        for w_ref, wb_ref in zip(w_refs, wb_refs):
            wb_ref[...] = w_ref[0, 0].astype(BF16)

    a = a_ref[...]
    if swiglu:
        h = _silu(_dot(a, wb_refs[0][...])) * _dot(a, wb_refs[1][...]) * rw_ref[...]
        o_refs[0][...] = h.astype(BF16)
    else:
        y = _dot(a, wb_refs[0][...])
        hi = y.astype(BF16)
        r1 = y - hi.astype(F32)
        mid = r1.astype(BF16)
        o_refs[0][...] = hi
        o_refs[1][...] = mid
        o_refs[2][...] = (r1 - mid.astype(F32)).astype(BF16)


def _gmm(a, ws, layer, plan, *, tn, row_w=None, name="gmm"):
    n_rows, k = a.shape
    n = ws[0].shape[3]
    tm = MOE_ROW_TILE
    swiglu = len(ws) == 2
    wspec = pl.BlockSpec((1, 1, k, tn), lambda j, i, te: (layer, te[i], 0, j))
    in_specs = [pl.BlockSpec((tm, k), lambda j, i, te: (i, 0))] + [wspec] * len(ws)
    args = [a, *ws]
    if swiglu:
        in_specs.append(pl.BlockSpec((tm, 1), lambda j, i, te: (i, 0)))
        args.append(row_w.reshape(n_rows, 1))
    ospec = pl.BlockSpec((tm, tn), lambda j, i, te: (i, j))
    oshape = jax.ShapeDtypeStruct((n_rows, n), BF16)
    return pl.pallas_call(
        functools.partial(_gmm_kernel, swiglu=swiglu),
        grid_spec=pltpu.PrefetchScalarGridSpec(
            num_scalar_prefetch=1, grid=(n // tn, n_rows // tm),
            in_specs=in_specs,
            out_specs=ospec if swiglu else [ospec] * 3,
            scratch_shapes=[pltpu.VMEM((k, tn), BF16) for _ in ws]),
        out_shape=oshape if swiglu else [oshape] * 3,
        compiler_params=_cparams(("arbitrary", "arbitrary")),
        name=name,
    )(plan["tile_e"], *args)


def _permute_w_in(w):
    cols = []
    for nm in _NEW_ORDER:
        if nm.startswith("pad"):
            cols.append(jnp.zeros(w.shape[:2] + (int(nm[3:]),), w.dtype))
        else:
            o, n = _ORIG[nm]
            cols.append(w[:, :, o:o + n])
    out = jnp.concatenate(cols, axis=2)
    assert out.shape[2] == PROJ_W
    return out


def _nsa_cmp_inputs(proj):
    s_len = proj.shape[0]
    n_cmp = (s_len - NSA_CMP_LEN) // NSA_CMP_STRIDE + 1
    ncp = s_len // NSA_CMP_STRIDE
    xs = []
    for jj in range(2):
        for g in range(NSA_GROUPS):
            c0 = COL_DKV + jj * 128 + g * HEAD_DIM
            r = proj[:, c0:c0 + HEAD_DIM].reshape(ncp, NSA_CMP_STRIDE * HEAD_DIM)
            x = jnp.concatenate([r[:-1], r[1:]], axis=1)
            xs.append(jnp.pad(x, ((0, ncp - n_cmp), (0, 0))))
    return jnp.stack(xs)


def _token_mixers(u, layer, p, cfg):
    s_len = u.shape[0]
    tm = cfg["tm"]
    proj = _mm(u, p["w_in"], layer, tm=tm, tn=512, name="in_proj")
    kv_b = _mm(proj, p["dsa_w_ukv"], layer, tm=tm, tn=512, a_blk=BLK128["b_kv"], k=DSA_KV_RANK,
               prologue="rms", gain=p["dsa_kv_norm_g"][layer], name="dsa_kv")
    kv = _kv_pack(proj, kv_b, tm=cfg["tm_ln"])

    lambda_init = 0.8 - 0.6 * math.exp(-0.3 * layer)
    sl_a = _alibi(DA_HEADS)
    units_a = [(2 * h + mp, h * 256, 256, 0,
                ((h * 128 + mp * 64, sl_a[h], ((2 * h + mp) * 128, 128), 0),))
               for h in range(DA_HEADS) for mp in range(2)]
    o_a2 = _flash(proj, kv["a_k"], kv["a_v"], units=units_a,
                  q_spec=(512, BLK512["a_q"]), out_w=1024, tq=cfg["tq"], tk=cfg["tk"], name="diff_attn")
    o_a = _diff_final(o_a2, p["diff_lambda"], layer, p["diff_subln_g"][layer], lambda_init, tm=tm)

    topk = min(DSA_TOPK_MAX, s_len // 4)
    mask_b = _dsa_select(proj, kv["b_ik"], topk=topk)
    sl8 = _alibi(8)
    units_b = [(h, h * 128, 128, 0, ((h * 64, sl8[h], (h * 64, 64), 0),)) for h in range(DSA_HEADS)]
    o_b = _flash(proj, kv["b_k"], kv["b_v"],
                 units=units_b, q_spec=(512, BLK512["b_q"]), out_w=512, tq=cfg["tq"], tk=cfg["tk"],
                 mask=mask_b.reshape(1, s_len, s_len), name="dsa_attn")

    def gqa_units(masked):
        return [(g, g * 128, 128, g if masked else 0,
                 tuple(((g * 4 + r) * 64, sl8[g * 4 + r], ((g * 4 + r) * 64, 64), g * 4 + r)
                       for r in range(4)))
                for g in range(2)]

    def gqa_kv(k_name, v_name):
        return kv[k_name], kv[v_name]

    sinks = jnp.pad(p["swa_sinks"][layer].reshape(1, SWA_HEADS), ((0, 0), (0, LANES - SWA_HEADS)))
    o_c = _flash(proj, *gqa_kv("c_k", "c_v"), units=gqa_units(False), q_spec=(512, BLK512["c_q"]),
                 out_w=512, tq=cfg["tb"], tk=cfg["tb"], window=SWA_WINDOW, sinks=sinks, name="swa_attn")

    kv_cmp = _nsa_compress(_nsa_cmp_inputs(proj), p["nsa_cmp_pos"], p["nsa_cmp_w1"], p["nsa_cmp_w2"],
                           layer)
    o_cmp, mask_d = _nsa_cmp(proj, kv_cmp)
    o_slc = _flash(proj, *gqa_kv("d_ks", "d_vs"), units=gqa_units(True), q_spec=(512, BLK512["d_q"]),
                   out_w=512, tq=cfg["tq"], tk=cfg["tk"], mask=mask_d, name="nsa_slc_attn")
    o_win = _flash(proj, *gqa_kv("d_kw", "d_vw"), units=gqa_units(False), q_spec=(512, BLK512["d_q"]),
                   out_w=512, tq=cfg["tb"], tk=cfg["tb"], window=NSA_WINDOW, name="nsa_win_attn")
    o_d = _nsa_combine(proj, o_cmp, o_slc, o_win, tm=tm)

    merged = _merge(u, (o_a, o_b, o_c, o_d), p["w_gate"], p["w_branch"], layer,
                    tm=cfg["tm_merge"], tn=256)
    return _mm(merged, p["w_o"], layer, tm=tm, tn=512, name="out_proj")


def _config(s_len):
    return dict(tm=min(1024, s_len), tm_merge=min(512, s_len), tm_ln=min(512, s_len),
                tq=min(256, s_len), tk=min(1024, s_len), tb=min(256, s_len))


def kernel(x, c, cond_w, cond_b, w_in, diff_lambda, diff_subln_g, dsa_kv_norm_g, dsa_w_uk, dsa_w_uv,
           swa_sinks, nsa_cmp_pos, nsa_cmp_w1, nsa_cmp_w2, w_branch, w_gate, w_o,
           ln1_g, ln1_b, ln2_g, ln2_b, ffn_w_gate, ffn_w_up, ffn_w_down,
           moe_router, moe_w_gate, moe_w_up, moe_w_down):
    bsz, s_len, d = x.shape
    assert bsz == 1 and d == D_MODEL
    depth = cond_w.shape[0]
    cfg = _config(s_len)
    xs = x.reshape(s_len, d)
    c8 = jnp.broadcast_to(c.reshape(1, d), (8, d))
    p = dict(w_in=_permute_w_in(w_in), diff_lambda=diff_lambda, diff_subln_g=diff_subln_g,
             dsa_kv_norm_g=dsa_kv_norm_g, dsa_w_ukv=jnp.concatenate([dsa_w_uk, dsa_w_uv], axis=2),
             swa_sinks=swa_sinks,
             nsa_cmp_pos=nsa_cmp_pos.reshape(depth, 2, 1, NSA_CMP_LEN * HEAD_DIM),
             nsa_cmp_w1=nsa_cmp_w1, nsa_cmp_w2=nsa_cmp_w2, w_branch=w_branch, w_gate=w_gate, w_o=w_o)
    router_p = jnp.pad(moe_router, ((0, 0), (0, 0), (0, LANES - N_EXPERTS)))
    mods = [_mm(c8, cond_w, l, tm=8, tn=512, prologue="silu", bias=cond_b[l], name="cond")[0:1]
            for l in range(depth)]
    u = _modulate(xs, mods[0], 1, 0, tm=cfg["tm_ln"])
    for l in range(depth):
        y = _token_mixers(u, l, p, cfg)
        xs, u = _resid_ln(xs, y, mods[l], 2, ln1_g[l], ln1_b[l], mods[l], 4, 3, tm=cfg["tm_ln"])
        jx = l // 2
        if l % 2 == 0:
            hdn = _swiglu_up(u, ffn_w_gate, ffn_w_up, jx, tm=cfg["tm"], tn=512, name="ffn_up")
            y = _mmk(hdn, ffn_w_down, jx, tm=cfg["tm"], tn=d, tk=512, name="ffn_down")
        else:
            rt = _router(u, router_p, jx, tm=cfg["tm"])
            plan = _moe_plan(rt[:, 0:2], rt[:, 2:4], s_len)
            hdn = _gmm(_moe_gather(u, plan), (moe_w_gate, moe_w_up), jx, plan, tn=512,
                       row_w=plan["row_w"], name="moe_up")
            y = _moe_combine(_gmm(hdn, (moe_w_down,), jx, plan, tn=512, name="moe_down"), plan, s_len)
        nxt = min(l + 1, depth - 1)
        xs, u = _resid_ln(xs, y, mods[l], 5, ln2_g[l], ln2_b[l], mods[nxt], 1, 0, tm=cfg["tm_ln"])
    return xs.reshape(bsz, s_len, d)
```

```python
import functools
import math

import numpy as np
import jax
import jax.numpy as jnp
from jax import lax
from jax.experimental import pallas as pl
from jax.experimental.pallas import tpu as pltpu

F32 = jnp.float32
BF16 = jnp.bfloat16
NEG = -1e30

D_MODEL = 2048
DEPTH = 4
HEAD_DIM = 64
DA_HEADS = 4
DSA_HEADS = 8
DSA_KV_RANK = 128
IDX_HEADS = 8
DSA_TOPK_MAX = 256
SWA_HEADS = 8
SWA_WINDOW = 128
NSA_HEADS = 8
NSA_GROUPS = 2
NSA_CMP_LEN = 32
NSA_CMP_STRIDE = 16
NSA_CMP_HID = 256
NSA_SLC_LEN = 64
NSA_TOPN = 16
NSA_WINDOW = 512
NSA_FORCE = 1e9
N_BRANCH = 4
BRANCH_W = 512
N_EXPERTS = 8
ALPHA = (2.0 * DEPTH) ** 0.25

VMEM_LIMIT_BYTES = 56 * 1024 * 1024
LANES = 128

_ORIG = dict(a_q=(0, 512), a_k=(512, 512), a_v=(1024, 512), b_q=(1536, 512), b_kv=(2048, 128),
             b_iq=(2176, 512), b_ik=(2688, 64), b_iw=(2752, 8), c_q=(2760, 512), c_k=(3272, 128),
             c_v=(3400, 128), d_q=(3528, 512), d_kv=(4040, 768), d_g=(4808, 24))
_NEW_ORDER = ("a_q", "a_k", "a_v", "b_q", "b_iq", "c_q", "d_q", "b_kv", "c_k", "c_v", "d_kv",
              "b_ik", "pad64", "b_iw", "d_g", "pad96", "pad128")
PROJ_W = 5120
BLK512 = dict(a_q=0, a_k=1, a_v=2, b_q=3, b_iq=4, c_q=5, d_q=6)
BLK128 = dict(b_kv=28, c_k=29, c_v=30, d_kc=31, d_vc=32, d_ks=33, d_vs=34, d_kw=35, d_vw=36,
              b_ik=37, small=38)
COL_DKV = 3968


def _cparams(sem):
    return pltpu.CompilerParams(dimension_semantics=sem, vmem_limit_bytes=VMEM_LIMIT_BYTES)


def _sigmoid(x):
    return 1.0 / (1.0 + jnp.exp(-x))


def _silu(x):
    return x * _sigmoid(x)


def _alibi(n_heads):
    return [2.0 ** (-8.0 * (h + 1) / n_heads) for h in range(n_heads)]


def _dot(a, b):
    return jnp.dot(a, b, preferred_element_type=F32)


def _dot_nt(a, b):
    return lax.dot_general(a, b, (((1,), (1,)), ((), ())), preferred_element_type=F32)


def _mm_kernel(*refs, prologue, has_bias, eps):
    it = iter(refs)
    a_ref = next(it)
    g_ref = next(it) if prologue == "rms" else None
    w_ref = next(it)
    b_ref = next(it) if has_bias else None
    o_ref = next(it)
    wb_ref = next(it)

    @pl.when(pl.program_id(1) == 0)
    def _():
        wb_ref[...] = w_ref[0].astype(BF16)

    a = a_ref[...]
    if prologue == "silu":
        a = _silu(a.astype(F32))
    elif prologue == "rms":
        a = a.astype(F32)
        a = a * lax.rsqrt(jnp.mean(a * a, axis=-1, keepdims=True) + eps) * g_ref[...]
    acc = _dot(a.astype(BF16), wb_ref[...])
    if has_bias:
        acc = acc + b_ref[...]
    o_ref[...] = acc.astype(o_ref.dtype)


def _mm(a, w, layer, *, tm, tn, out_dtype=F32, a_blk=0, k=None, prologue=None, gain=None, bias=None,
        eps=1e-6, name="mm"):
    m = a.shape[0]
    k = a.shape[1] if k is None else k
    n = w.shape[2]
    assert w.shape[1] == k and m % tm == 0 and n % tn == 0
    in_specs = [pl.BlockSpec((tm, k), lambda j, i: (i, a_blk))]
    args = [a]
    if prologue == "rms":
        in_specs.append(pl.BlockSpec((1, k), lambda j, i: (0, 0)))
        args.append(gain.reshape(1, k))
    in_specs.append(pl.BlockSpec((1, k, tn), lambda j, i: (layer, 0, j)))
    args.append(w)
    if bias is not None:
        in_specs.append(pl.BlockSpec((1, tn), lambda j, i: (0, j)))
        args.append(bias.reshape(1, n))
    return pl.pallas_call(
        functools.partial(_mm_kernel, prologue=prologue, has_bias=bias is not None, eps=eps),
        grid=(n // tn, m // tm),
        in_specs=in_specs,
        out_specs=pl.BlockSpec((tm, tn), lambda j, i: (i, j)),
        out_shape=jax.ShapeDtypeStruct((m, n), out_dtype),
        scratch_shapes=[pltpu.VMEM((k, tn), BF16)],
        compiler_params=_cparams(("arbitrary", "arbitrary")),
        name=name,
    )(*args)


def _mmk_kernel(a_ref, w_ref, o_ref, acc_ref, *, nk):
    kk = pl.program_id(2)

    @pl.when(kk == 0)
    def _():
        acc_ref[...] = jnp.zeros_like(acc_ref)

    acc_ref[...] += _dot(a_ref[...], w_ref[0].astype(BF16))

    @pl.when(kk == nk - 1)
    def _():
        o_ref[...] = acc_ref[...]


def _mmk(a, w, layer, *, tm, tn, tk, name="mmk"):
    m, k = a.shape
    n = w.shape[2]
    assert w.shape[1] == k and m % tm == 0 and n % tn == 0 and k % tk == 0
    nk = k // tk
    return pl.pallas_call(
        functools.partial(_mmk_kernel, nk=nk),
        grid=(m // tm, n // tn, nk),
        in_specs=[pl.BlockSpec((tm, tk), lambda i, j, kk: (i, kk)),
                  pl.BlockSpec((1, tk, tn), lambda i, j, kk: (layer, kk, j))],
        out_specs=pl.BlockSpec((tm, tn), lambda i, j, kk: (i, j)),
        out_shape=jax.ShapeDtypeStruct((m, n), F32),
        scratch_shapes=[pltpu.VMEM((tm, tn), F32)],
        compiler_params=_cparams(("arbitrary", "arbitrary", "arbitrary")),
        name=name,
    )(a, w)


def _swiglu_kernel(a_ref, wg_ref, wu_ref, o_ref, wgb_ref, wub_ref):
    @pl.when(pl.program_id(1) == 0)
    def _():
        wgb_ref[...] = wg_ref[0].astype(BF16)
        wub_ref[...] = wu_ref[0].astype(BF16)

    a = a_ref[...]
    o_ref[...] = (_silu(_dot(a, wgb_ref[...])) * _dot(a, wub_ref[...])).astype(o_ref.dtype)


def _swiglu_up(u, wg, wu, layer, *, tm, tn, name="swiglu_up"):
    m, k = u.shape
    f = wg.shape[2]
    assert f % tn == 0 and m % tm == 0
    wspec = pl.BlockSpec((1, k, tn), lambda j, i: (layer, 0, j))
    return pl.pallas_call(
        _swiglu_kernel,
        grid=(f // tn, m // tm),
        in_specs=[pl.BlockSpec((tm, k), lambda j, i: (i, 0)), wspec, wspec],
        out_specs=pl.BlockSpec((tm, tn), lambda j, i: (i, j)),
        out_shape=jax.ShapeDtypeStruct((m, f), BF16),
        scratch_shapes=[pltpu.VMEM((k, tn), BF16), pltpu.VMEM((k, tn), BF16)],
        compiler_params=_cparams(("arbitrary", "arbitrary")),
        name=name,
    )(u, wg, wu)


def _modulate_kernel(x_ref, sc_ref, sh_ref, u_ref):
    u_ref[...] = (x_ref[...] * (1.0 + sc_ref[...]) + sh_ref[...]).astype(u_ref.dtype)


def _modulate(x, mod, sc_blk, sh_blk, *, tm):
    m, d = x.shape
    return pl.pallas_call(
        _modulate_kernel,
        grid=(m // tm,),
        in_specs=[pl.BlockSpec((tm, d), lambda i: (i, 0)),
                  pl.BlockSpec((1, d), lambda i: (0, sc_blk)),
                  pl.BlockSpec((1, d), lambda i: (0, sh_blk))],
        out_specs=pl.BlockSpec((tm, d), lambda i: (i, 0)),
        out_shape=jax.ShapeDtypeStruct((m, d), BF16),
        compiler_params=_cparams(("arbitrary",)),
        name="modulate",
    )(x, mod, mod)


def _resid_ln_kernel(x_ref, y_ref, gate_ref, g_ref, b_ref, sc_ref, sh_ref, xo_ref, u_ref):
    z = ALPHA * x_ref[...] + gate_ref[...] * y_ref[...]
    mu = jnp.mean(z, axis=-1, keepdims=True)
    zc = z - mu
    var = jnp.mean(zc * zc, axis=-1, keepdims=True)
    xn = zc * lax.rsqrt(var + 1e-5) * g_ref[...] + b_ref[...]
    xo_ref[...] = xn
    u_ref[...] = (xn * (1.0 + sc_ref[...]) + sh_ref[...]).astype(u_ref.dtype)


def _resid_ln(x, y, mod, gate_blk, g, b, mod_next, sc_blk, sh_blk, *, tm):
    m, d = x.shape
    row = lambda blk: pl.BlockSpec((1, d), lambda i: (0, blk))
    return pl.pallas_call(
        _resid_ln_kernel,
        grid=(m // tm,),
        in_specs=[pl.BlockSpec((tm, d), lambda i: (i, 0)),
                  pl.BlockSpec((tm, d), lambda i: (i, 0)),
                  row(gate_blk), row(0), row(0), row(sc_blk), row(sh_blk)],
        out_specs=[pl.BlockSpec((tm, d), lambda i: (i, 0)),
                   pl.BlockSpec((tm, d), lambda i: (i, 0))],
        out_shape=[jax.ShapeDtypeStruct((m, d), F32), jax.ShapeDtypeStruct((m, d), BF16)],
        compiler_params=_cparams(("arbitrary",)),
        name="resid_ln",
    )(x, y, mod, g.reshape(1, d), b.reshape(1, d), mod_next, mod_next)


FLASH_ROW_CHUNK = 32
POS_SPLIT = 128


_KV_PACK = (("a_k", "a_k", 0, 2 * DA_HEADS, 64, "k"), ("a_v", "a_v", 0, DA_HEADS, 128, "v"),
            ("b_k", "kv_b", 0, DSA_HEADS, 64, "k"), ("b_v", "kv_b", 512, DSA_HEADS, 64, "v"),
            ("c_k", "c_k", 0, 2, 64, "k"), ("c_v", "c_v", 0, 2, 64, "v"),
            ("d_ks", "d_ks", 0, 2, 64, "k"), ("d_vs", "d_vs", 0, 2, 64, "v"),
            ("d_kw", "d_kw", 0, 2, 64, "k"), ("d_vw", "d_vw", 0, 2, 64, "v"),
            ("b_ik", "b_ik", 0, 1, 128, "cast"))
_KV_SOURCES = ("a_k", "a_v", "c_k", "c_v", "d_ks", "d_vs", "d_kw", "d_vw", "b_ik", "kv_b")


def _kv_pack_kernel(*refs, tm):
    src = dict(zip(_KV_SOURCES, refs[:len(_KV_SOURCES)]))
    outs = refs[len(_KV_SOURCES):]
    i = pl.program_id(0)
    tails = {}
    for w in (64, 128):
        lane = lax.broadcasted_iota(jnp.int32, (tm, w), 1)
        pos = i * tm + lax.broadcasted_iota(jnp.int32, (tm, w), 0)
        tails[("v", w)] = jnp.where(lane == 0, 1.0, 0.0)
        tails[("k", w)] = jnp.where(lane == 0, (pos // POS_SPLIT).astype(F32),
                                    jnp.where(lane == 1, (pos % POS_SPLIT).astype(F32),
                                              jnp.where(lane < 4, 1.0, 0.0)))
    for (_, sname, c0, n_heads, w, kind), o_ref in zip(_KV_PACK, outs):
        if kind == "cast":
            o_ref[...] = src[sname][...].astype(BF16)
            continue
        for h in range(n_heads):
            x = src[sname][:, c0 + h * w:c0 + (h + 1) * w]
            o_ref[:, 2 * h * w:2 * (h + 1) * w] = jnp.concatenate([x, tails[(kind, w)]], axis=1).astype(BF16)


def _kv_pack(proj, kv_b, *, tm):
    s_len = proj.shape[0]
    in_specs, args = [], []
    for sname in _KV_SOURCES:
        if sname == "kv_b":
            in_specs.append(pl.BlockSpec((tm, kv_b.shape[1]), lambda i: (i, 0)))
            args.append(kv_b)
        else:
            wblk, blk = (512, BLK512[sname]) if sname in BLK512 else (128, BLK128[sname])
            in_specs.append(pl.BlockSpec((tm, wblk), functools.partial(lambda i, blk: (i, blk), blk=blk)))
            args.append(proj)
    widths = [(1 if kind == "cast" else 2) * n_heads * w for (_, _, _, n_heads, w, kind) in _KV_PACK]
    outs = pl.pallas_call(
        functools.partial(_kv_pack_kernel, tm=tm),
        grid=(s_len // tm,),
        in_specs=in_specs,
        out_specs=[pl.BlockSpec((tm, wd), lambda i: (i, 0)) for wd in widths],
        out_shape=[jax.ShapeDtypeStruct((s_len, wd), BF16) for wd in widths],
        compiler_params=_cparams(("arbitrary",)),
        name="kv_pack",
    )(*args)
    return {name: o for (name, *_), o in zip(_KV_PACK, outs)}


def _flash_kernel(qi_ref, kb_ref, first_ref, last_ref, *refs, units, tq, tk, window, dense, n_mask,
                  has_sink):
    it = iter(refs)
    q_ref, k_ref, v_ref = next(it), next(it), next(it)
    mask_ref = next(it) if n_mask else None
    sink_ref = next(it) if has_sink else None
    o_ref = next(it)
    q_scr, m_scr, acc_scr, bias_scr = (next(it) for _ in range(4))
    dv = acc_scr.shape[2]
    lcol = dv // 2

    w = pl.program_id(0)
    qi = qi_ref[w]
    kb = kb_ref[w]
    rows = q_scr.shape[1]
    rb = FLASH_ROW_CHUNK
    kw = 2 * HEAD_DIM

    @pl.when(first_ref[w] > 0)
    def _init():
        lane = lax.broadcasted_iota(jnp.int32, (tq, HEAD_DIM), 1)
        qpos = qi * tq + lax.broadcasted_iota(jnp.int32, (tq, HEAD_DIM), 0)
        qhi = (qpos // POS_SPLIT).astype(F32)
        qlo = (qpos % POS_SPLIT).astype(F32)
        for ui, (_, _, _, _, hds) in enumerate(units):
            for r, (qo, slope, _, sink_idx) in enumerate(hds):
                rsl = slice(r * tq, (r + 1) * tq)
                tail = jnp.where(lane == 0, POS_SPLIT * slope,
                                 jnp.where(lane == 1, slope,
                                           jnp.where(lane == 2, -POS_SPLIT * slope * qhi,
                                                     jnp.where(lane == 3, -slope * qlo, 0.0))))
                qs = q_ref[:, qo:qo + HEAD_DIM] * HEAD_DIM ** -0.5
                q_scr[ui, rsl] = jnp.concatenate([qs, tail], axis=1).astype(BF16)
                if has_sink:
                    m_scr[ui, rsl] = jnp.broadcast_to(sink_ref[:, sink_idx:sink_idx + 1], (tq, 1))
                else:
                    m_scr[ui, rsl] = jnp.full((tq, 1), NEG, F32)
            alane = lax.broadcasted_iota(jnp.int32, acc_scr.shape[1:], 1)
            acc_scr[ui] = jnp.where(alane == lcol, 1.0 if has_sink else 0.0, 0.0)

    def scores(ui):
        ku = units[ui][0]
        return _dot_nt(q_scr[ui], k_ref[:, ku * kw:(ku + 1) * kw])

    def step(masked):
        if masked:
            qpos = qi * tq + lax.broadcasted_iota(jnp.int32, (tq, tk), 0)
            kpos = kb * tk + lax.broadcasted_iota(jnp.int32, (tq, tk), 1)
            dist = qpos - kpos
            valid = dist >= 0
            if not dense:
                valid = valid & (dist < window)
            if n_mask:
                for g in range(n_mask):
                    bias_scr[g] = jnp.where(valid, mask_ref[g].astype(F32), NEG)
            else:
                bias_scr[0] = jnp.where(valid, 0.0, NEG)

        def chunk(s, mg, c):
            r0 = c * rb
            sc = s[r0:r0 + rb]
            if masked:
                rw = r0 % tq
                sc = sc + bias_scr[mg, rw:rw + rb]
            return sc

        s_next = scores(0)
        for ui, (_, vo, _, mg, _) in enumerate(units):
            s = s_next
            if ui + 1 < len(units):
                s_next = scores(ui + 1)
            nchunk = rows // rb
            m_old = m_scr[ui]
            m_cur = jnp.concatenate([jnp.max(chunk(s, mg, c), axis=1, keepdims=True) for c in range(nchunk)],
                                    axis=0)
            m_new = jnp.maximum(m_old, m_cur)
            alpha = jnp.exp(m_old - m_new)
            m_scr[ui] = m_new
            p_all = jnp.concatenate(
                [jnp.exp(chunk(s, mg, c) - m_new[c * rb:(c + 1) * rb]).astype(BF16) for c in range(nchunk)],
                axis=0)
            acc_scr[ui] = alpha * acc_scr[ui] + _dot(p_all, v_ref[:, vo:vo + dv])

    if dense and not n_mask:
        interior = kb * tk + tk - 1 <= qi * tq
        pl.when(interior)(lambda: step(False))
        pl.when(jnp.logical_not(interior))(lambda: step(True))
    else:
        step(True)

    @pl.when(last_ref[w] > 0)
    def _fin():
        for ui, (_, _, _, _, hds) in enumerate(units):
            for r, (_, _, (oo, ow), _) in enumerate(hds):
                rsl = slice(r * tq, (r + 1) * tq)
                acc = acc_scr[ui, rsl]
                o_ref[:, oo:oo + ow] = acc[:, 0:ow] / acc[:, lcol:lcol + 1]


def _flash(q_arr, k_arr, v_arr, *, units, q_spec, out_w, tq, tk, window=None,
           mask=None, sinks=None, name="flash"):
    s_len = q_arr.shape[0]
    dense = window is None
    if not dense:
        assert tq == tk
    n_mask = 0 if mask is None else mask.shape[0]
    dv = units[0][2]
    nu = len(units)
    rows = len(units[0][4]) * tq
    assert all(len(un[4]) * tq == rows and un[2] == dv for un in units) and rows % FLASH_ROW_CHUNK == 0

    pairs = []
    for qi in range(s_len // tq):
        last_kb = (qi * tq + tq - 1) // tk
        first_kb = 0 if dense else max(qi - (-(-(window - 1) // tk)), 0)
        pairs += [(qi, kb, int(kb == first_kb), int(kb == last_kb)) for kb in range(first_kb, last_kb + 1)]
    tables = [jnp.asarray(np.array(col, np.int32)) for col in zip(*pairs)]

    in_specs = [pl.BlockSpec((tq, q_spec[0]), lambda w, qi, kb, fi, la: (qi[w], q_spec[1])),
                pl.BlockSpec((tk, k_arr.shape[1]), lambda w, qi, kb, fi, la: (kb[w], 0)),
                pl.BlockSpec((tk, v_arr.shape[1]), lambda w, qi, kb, fi, la: (kb[w], 0))]
    args = [q_arr, k_arr, v_arr]
    if n_mask:
        in_specs.append(pl.BlockSpec((n_mask, tq, tk), lambda w, qi, kb, fi, la: (0, qi[w], kb[w])))
        args.append(mask)
    if sinks is not None:
        in_specs.append(pl.BlockSpec((1, LANES), lambda w, qi, kb, fi, la: (0, 0)))
        args.append(sinks)
    return pl.pallas_call(
        functools.partial(_flash_kernel, units=tuple(units), tq=tq, tk=tk, window=window, dense=dense,
                          n_mask=n_mask, has_sink=sinks is not None),
        grid_spec=pltpu.PrefetchScalarGridSpec(
            num_scalar_prefetch=4, grid=(len(pairs),),
            in_specs=in_specs,
            out_specs=pl.BlockSpec((tq, out_w), lambda w, qi, kb, fi, la: (qi[w], 0)),
            scratch_shapes=[pltpu.VMEM((nu, rows, 2 * HEAD_DIM), BF16), pltpu.VMEM((nu, rows, 1), F32),
                            pltpu.VMEM((nu, rows, dv), F32), pltpu.VMEM((max(n_mask, 1), tq, tk), F32)]),
        out_shape=jax.ShapeDtypeStruct((s_len, out_w), F32),
        compiler_params=_cparams(("arbitrary",)),
        name=name,
    )(*tables, *args)


def _diff_final_kernel(o_ref, lam_ref, g_ref, out_ref, *, lambda_init):
    lf = lam_ref[0]
    lam = (jnp.exp(jnp.sum(lf[0:1] * lf[1:2])) - jnp.exp(jnp.sum(lf[2:3] * lf[3:4])) + lambda_init)
    w = 2 * HEAD_DIM
    for h in range(DA_HEADS):
        o = o_ref[:, (2 * h) * w:(2 * h + 1) * w] - lam * o_ref[:, (2 * h + 1) * w:(2 * h + 2) * w]
        o = o * lax.rsqrt(jnp.mean(o * o, axis=-1, keepdims=True) + 1e-6) * g_ref[...]
        out_ref[:, h * w:(h + 1) * w] = o * (1.0 - lambda_init)


def _diff_final(o, diff_lambda, layer, subln_g, lambda_init, *, tm):
    m = o.shape[0]
    w = 2 * HEAD_DIM
    return pl.pallas_call(
        functools.partial(_diff_final_kernel, lambda_init=lambda_init),
        grid=(m // tm,),
        in_specs=[pl.BlockSpec((tm, 2 * DA_HEADS * w), lambda i: (i, 0)),
                  pl.BlockSpec((1, 4, HEAD_DIM), lambda i: (layer, 0, 0)),
                  pl.BlockSpec((1, w), lambda i: (0, 0))],
        out_specs=pl.BlockSpec((tm, DA_HEADS * w), lambda i: (i, 0)),
        out_shape=jax.ShapeDtypeStruct((m, DA_HEADS * w), F32),
        compiler_params=_cparams(("arbitrary",)),
        name="diff_final",
    )(o, diff_lambda, subln_g.reshape(1, w))


def _f32_key_const(x):
    b = int(np.array(x, np.float32).view(np.int32))
    return b ^ ((b >> 31) & 0x7FFFFFFF)


I16_MIN = -(2 ** 15)


def _dsa_select_kernel(qi_ref, w_ref, kidx_ref, mask_ref, key_scr, half_scr, j_scr, *, tq, ch, nch, topk,
                       s_len):
    i = pl.program_id(0)
    q0 = i * tq
    n_need = (q0 + tq + ch - 1) // ch
    qpos = q0 + lax.broadcasted_iota(jnp.int32, (tq, 1), 0)
    lane = lax.broadcasted_iota(jnp.int32, (1, ch), 1)
    w = w_ref[:, 0:IDX_HEADS]
    q_all = jnp.concatenate([qi_ref[:, h * HEAD_DIM:(h + 1) * HEAD_DIM] for h in range(IDX_HEADS)],
                            axis=0).astype(BF16)

    def score_chunk(c, carry):
        kc = kidx_ref[pl.ds(pl.multiple_of(c * ch, ch), ch), 0:HEAD_DIM].astype(BF16)
        lg = _dot_nt(q_all, kc)
        acc = jnp.zeros((tq, ch), F32)
        for h in range(IDX_HEADS):
            acc = acc + w[:, h:h + 1] * jnp.maximum(lg[h * tq:(h + 1) * tq], 0.0)
        acc = jnp.where(c * ch + lane <= qpos, acc, NEG) + 0.0
        bits = pltpu.bitcast(acc, jnp.int32)
        key = bits ^ ((bits >> 31) & 0x7FFFFFFF)
        key_scr[c] = key
        half_scr[c] = (key >> 16).astype(jnp.int16)
        return carry

    lax.fori_loop(0, n_need, score_chunk, 0)

    def count16(cand, strict):
        cand16 = jnp.broadcast_to(cand, (tq, LANES)).astype(jnp.int16)
        one, zero = jnp.int16(1), jnp.int16(0)

        def body(c, acc):
            blk = half_scr[c]
            for t in range(ch // LANES):
                tile = blk[:, t * LANES:(t + 1) * LANES]
                acc = acc + jnp.where(tile > cand16 if strict else tile >= cand16, one, zero)
            return acc
        acc = lax.fori_loop(0, n_need, body, jnp.zeros((tq, LANES), jnp.int16))
        return jnp.sum(acc.astype(jnp.int32), axis=1, keepdims=True)

    def search16(need_cnt):
        def bit_step(b, t):
            cand = t + jnp.left_shift(jnp.int32(1), 15 - b)
            return jnp.where(count16(cand, False) >= need_cnt, cand, t)
        return lax.fori_loop(0, 16, bit_step, jnp.full((tq, 1), I16_MIN, jnp.int32))

    t_hi = search16(topk)
    need_lo = topk - count16(t_hi, True)

    def low_chunk(c, carry):
        key = key_scr[c]
        low = (key & 0xFFFF) + I16_MIN
        half_scr[c] = jnp.where((key >> 16) == t_hi, low, I16_MIN).astype(jnp.int16)
        return carry

    lax.fori_loop(0, n_need, low_chunk, 0)
    t_lo = search16(need_lo)
    thr = jnp.left_shift(t_hi, 16) + (t_lo - I16_MIN)

    def count(pred):
        def body(c, acc):
            m = jnp.where(pred(key_scr[c], c), 1, 0)
            part = m[:, 0:LANES]
            for t in range(1, ch // LANES):
                part = part + m[:, t * LANES:(t + 1) * LANES]
            return acc + part
        acc = lax.fori_loop(0, n_need, body, jnp.zeros((tq, LANES), jnp.int32))
        return jnp.sum(acc, axis=1, keepdims=True)

    cnt_gt = count(lambda blk, c: blk > thr)
    cnt_ge = count(lambda blk, c: blk >= thr)
    need = topk - cnt_gt
    tie_rows = (cnt_ge > topk) & (thr > _f32_key_const(NEG))
    j_scr[...] = jnp.full((tq, 1), s_len, jnp.int32)
    any_tie = jnp.max(jnp.where(tie_rows, 1, 0)) > 0

    @pl.when(any_tie)
    def _ties():
        def tie_chunk(c, carry):
            half_scr[c] = jnp.where(key_scr[c] == thr, -1 - (c * ch + lane), I16_MIN).astype(jnp.int16)
            return carry

        lax.fori_loop(0, n_need, tie_chunk, 0)
        j_scr[...] = jnp.where(tie_rows, -1 - search16(need), s_len)

    jv = j_scr[...]
    for c in range(nch):
        @pl.when((c < n_need) & any_tie)
        def _w():
            key = key_scr[c]
            sel = (key > thr) | ((key == thr) & (c * ch + lane <= jv))
            mask_ref[:, c * ch:(c + 1) * ch] = jnp.where(sel, 0.0, NEG).astype(mask_ref.dtype)

        @pl.when((c < n_need) & jnp.logical_not(any_tie))
        def _wf():
            mask_ref[:, c * ch:(c + 1) * ch] = jnp.where(key_scr[c] >= thr, 0.0, NEG).astype(mask_ref.dtype)

        @pl.when(c >= n_need)
        def _z():
            mask_ref[:, c * ch:(c + 1) * ch] = jnp.full((tq, ch), NEG, mask_ref.dtype)


def _dsa_select(proj, kidx, *, topk, tq=128):
    s_len = proj.shape[0]
    assert s_len < -I16_MIN
    ch = min(1024, s_len)
    nch = s_len // ch
    return pl.pallas_call(
        functools.partial(_dsa_select_kernel, tq=tq, ch=ch, nch=nch, topk=topk, s_len=s_len),
        grid=(s_len // tq,),
        in_specs=[pl.BlockSpec((tq, 512), lambda i: (i, BLK512["b_iq"])),
                  pl.BlockSpec((tq, LANES), lambda i: (i, BLK128["small"])),
                  pl.BlockSpec((s_len, LANES), lambda i: (0, 0))],
        out_specs=pl.BlockSpec((tq, s_len), lambda i: (i, 0)),
        out_shape=jax.ShapeDtypeStruct((s_len, s_len), BF16),
        scratch_shapes=[pltpu.VMEM((nch, tq, ch), jnp.int32), pltpu.VMEM((nch, tq, ch), jnp.int16),
                        pltpu.VMEM((tq, 1), jnp.int32)],
        compiler_params=_cparams(("arbitrary",)),
        name="dsa_select",
    )(proj, proj, kidx)


def _nsa_compress_kernel(x_ref, pos_ref, w1_ref, w2_ref, o_ref):
    x = (x_ref[0] + pos_ref[0, 0]).astype(BF16)
    hdn = _silu(_dot(x, w1_ref[0, 0].astype(BF16)))
    o_ref[0] = _dot(hdn.astype(BF16), w2_ref[0, 0].astype(BF16))


def _nsa_compress(xc, pos, w1, w2, layer):
    _, ncp, kdim = xc.shape
    return pl.pallas_call(
        _nsa_compress_kernel,
        grid=(4,),
        in_specs=[pl.BlockSpec((1, ncp, kdim), lambda i: (i, 0, 0)),
                  pl.BlockSpec((1, 1, 1, kdim), lambda i: (layer, i // 2, 0, 0)),
                  pl.BlockSpec((1, 1, kdim, NSA_CMP_HID), lambda i: (layer, i // 2, 0, 0)),
                  pl.BlockSpec((1, 1, NSA_CMP_HID, HEAD_DIM), lambda i: (layer, i // 2, 0, 0))],
        out_specs=pl.BlockSpec((1, ncp, HEAD_DIM), lambda i: (i, 0, 0)),
        out_shape=jax.ShapeDtypeStruct((4, ncp, HEAD_DIM), F32),
        compiler_params=_cparams(("arbitrary",)),
        name="nsa_compress",
    )(xc, pos, w1, w2)


def _nsa_cmp_kernel(q_ref, kv_ref, ov_ref, ex_ref, o_ref, mask_ref, *, tq, ncp, n_slc, topn, ch, nch):
    i = pl.program_id(0)
    q0 = i * tq
    n_need = (q0 + tq + ch - 1) // ch
    rpg = NSA_HEADS // NSA_GROUPS
    slopes = _alibi(NSA_HEADS)
    scale = HEAD_DIM ** -0.5
    qpos_c = q0 + lax.broadcasted_iota(jnp.int32, (tq, ncp), 0)
    cend = lax.broadcasted_iota(jnp.int32, (tq, ncp), 1) * NSA_CMP_STRIDE + (NSA_CMP_LEN - 1)
    dist_c = qpos_c - cend
    valid_c = dist_c >= 0
    distf = dist_c.astype(F32)
    qpos = q0 + lax.broadcasted_iota(jnp.int32, (tq, n_slc), 0)
    blk = lax.broadcasted_iota(jnp.int32, (tq, n_slc), 1)
    cur = qpos // NSA_SLC_LEN
    forced = (blk == 0) | (blk == cur) | (blk == cur - 1)
    blk_ok = blk * NSA_SLC_LEN <= qpos
    ov = ov_ref[...]
    imps = []
    for g in range(NSA_GROUPS):
        kc = kv_ref[g].astype(BF16)
        vc = kv_ref[NSA_GROUPS + g].astype(BF16)
        psum = jnp.zeros((tq, ncp), F32)
        for r in range(rpg):
            h = g * rpg + r
            qh = q_ref[:, h * HEAD_DIM:(h + 1) * HEAD_DIM].astype(BF16)
            s = _dot_nt(qh, kc) * scale - slopes[h] * distf
            s = jnp.where(valid_c, s, NEG)
            e = jnp.where(valid_c, jnp.exp(s - jnp.max(s, axis=1, keepdims=True)), 0.0)
            p = e / jnp.maximum(jnp.sum(e, axis=1, keepdims=True), 1e-30)
            o_ref[:, h * HEAD_DIM:(h + 1) * HEAD_DIM] = _dot(p.astype(BF16), vc)
            psum = psum + p
        p_hi = psum.astype(BF16)
        p_lo = (psum - p_hi.astype(F32)).astype(BF16)
        imp = _dot(p_hi, ov) + _dot(p_lo, ov)
        imp = jnp.where(forced, NSA_FORCE, imp)
        imps.append(jnp.where(blk_ok, imp, NEG))
    imps = [imp.T for imp in imps]
    blk_t = lax.broadcasted_iota(jnp.int32, (n_slc, tq), 0)
    sels = [jnp.full((n_slc, tq), NEG, F32) for _ in range(NSA_GROUPS)]
    for _ in range(topn):
        for g in range(NSA_GROUPS):
            mx = jnp.max(imps[g], axis=0, keepdims=True)
            first = jnp.min(jnp.where(imps[g] == mx, blk_t, n_slc), axis=0, keepdims=True)
            hit = blk_t == first
            sels[g] = jnp.where(hit, 0.0, sels[g])
            imps[g] = jnp.where(hit, -jnp.inf, imps[g])
    sels = [sel.T for sel in sels]
    for g in range(NSA_GROUPS):
        selb = sels[g].astype(BF16)
        for c in range(nch):
            @pl.when(c < n_need)
            def _w():
                tok = _dot(selb, ex_ref[:, c * ch:(c + 1) * ch])
                mask_ref[g, :, c * ch:(c + 1) * ch] = tok.astype(mask_ref.dtype)

            @pl.when(c >= n_need)
            def _z():
                mask_ref[g, :, c * ch:(c + 1) * ch] = jnp.full((tq, ch), NEG, mask_ref.dtype)


def _nsa_cmp(proj, kv_cmp, *, tq=128):
    s_len = proj.shape[0]
    ncp = kv_cmp.shape[1]
    n_slc = s_len // NSA_SLC_LEN
    topn = min(NSA_TOPN, n_slc)
    ch = min(1024, s_len)
    nch = s_len // ch
    starts = np.arange(ncp) * NSA_CMP_STRIDE
    slc_start = np.arange(n_slc) * NSA_SLC_LEN
    overlap = ((starts[:, None] < slc_start[None, :] + NSA_SLC_LEN)
               & (starts[:, None] + NSA_CMP_LEN > slc_start[None, :])).astype(np.float32)
    expand = (np.arange(s_len)[None, :] // NSA_SLC_LEN == np.arange(n_slc)[:, None]).astype(np.float32)
    return pl.pallas_call(
        functools.partial(_nsa_cmp_kernel, tq=tq, ncp=ncp, n_slc=n_slc, topn=topn, ch=ch, nch=nch),
        grid=(s_len // tq,),
        in_specs=[pl.BlockSpec((tq, 512), lambda i: (i, BLK512["d_q"])),
                  pl.BlockSpec((4, ncp, HEAD_DIM), lambda i: (0, 0, 0)),
                  pl.BlockSpec((ncp, n_slc), lambda i: (0, 0)),
                  pl.BlockSpec((n_slc, s_len), lambda i: (0, 0))],
        out_specs=[pl.BlockSpec((tq, 512), lambda i: (i, 0)),
                   pl.BlockSpec((NSA_GROUPS, tq, s_len), lambda i: (0, i, 0))],
        out_shape=[jax.ShapeDtypeStruct((s_len, 512), F32),
                   jax.ShapeDtypeStruct((NSA_GROUPS, s_len, s_len), BF16)],
        compiler_params=_cparams(("arbitrary",)),
        name="nsa_cmp",
    )(proj, kv_cmp, jnp.asarray(overlap, BF16), jnp.asarray(expand, BF16))


def _nsa_combine_kernel(g_ref, oc_ref, os_ref, ow_ref, o_ref):
    gt = _sigmoid(g_ref[...])
    for h in range(NSA_HEADS):
        sl = slice(h * HEAD_DIM, (h + 1) * HEAD_DIM)
        c0 = IDX_HEADS + 3 * h
        o_ref[:, sl] = (gt[:, c0:c0 + 1] * oc_ref[:, sl] + gt[:, c0 + 1:c0 + 2] * os_ref[:, sl]
                        + gt[:, c0 + 2:c0 + 3] * ow_ref[:, sl])


def _nsa_combine(proj, o_cmp, o_slc, o_win, *, tm):
    m = proj.shape[0]
    spec = pl.BlockSpec((tm, 512), lambda i: (i, 0))
    return pl.pallas_call(
        _nsa_combine_kernel,
        grid=(m // tm,),
        in_specs=[pl.BlockSpec((tm, LANES), lambda i: (i, BLK128["small"])), spec, spec, spec],
        out_specs=spec,
        out_shape=jax.ShapeDtypeStruct((m, 512), F32),
        compiler_params=_cparams(("arbitrary",)),
        name="nsa_combine",
    )(proj, o_cmp, o_slc, o_win)


def _merge_kernel(u_ref, oa_ref, ob_ref, oc_ref, od_ref, wg0, wg1, wg2, wg3, wb_ref, o_ref,
                  wgb_ref, wbb_ref):
    wgs = (wg0, wg1, wg2, wg3)

    @pl.when(pl.program_id(1) == 0)
    def _():
        for mch in range(N_BRANCH):
            wgb_ref[mch] = wgs[mch][0].astype(BF16)
            wbb_ref[mch] = wb_ref[0, mch].astype(BF16)

    u = u_ref[...]
    acc = None
    for mch, o_ref_m in enumerate((oa_ref, ob_ref, oc_ref, od_ref)):
        gte = _sigmoid(_dot(u, wgb_ref[mch]))
        z = _dot(o_ref_m[...].astype(BF16), wbb_ref[mch])
        acc = gte * z if acc is None else acc + gte * z
    o_ref[...] = acc.astype(o_ref.dtype)


def _merge(u, branches, w_gate, w_branch, layer, *, tm, tn):
    m, d = u.shape
    nj = d // tn
    bspec = pl.BlockSpec((tm, BRANCH_W), lambda j, i: (i, 0))
    wg_specs = [pl.BlockSpec((1, d, tn),
                             functools.partial(lambda j, i, mch: (layer, 0, mch * nj + j), mch=mch))
                for mch in range(N_BRANCH)]
    return pl.pallas_call(
        _merge_kernel,
        grid=(nj, m // tm),
        in_specs=[pl.BlockSpec((tm, d), lambda j, i: (i, 0)), bspec, bspec, bspec, bspec,
                  *wg_specs,
                  pl.BlockSpec((1, N_BRANCH, BRANCH_W, tn), lambda j, i: (layer, 0, 0, j))],
        out_specs=pl.BlockSpec((tm, tn), lambda j, i: (i, j)),
        out_shape=jax.ShapeDtypeStruct((m, d), BF16),
        scratch_shapes=[pltpu.VMEM((N_BRANCH, d, tn), BF16), pltpu.VMEM((N_BRANCH, BRANCH_W, tn), BF16)],
        compiler_params=_cparams(("arbitrary", "arbitrary")),
        name="merge",
    )(u, *branches, w_gate, w_gate, w_gate, w_gate, w_branch)


def _router_kernel(u_ref, r_ref, o_ref):
    logits = _dot(u_ref[...], r_ref[0].astype(BF16))
    lane = lax.broadcasted_iota(jnp.int32, logits.shape, 1)
    lg = jnp.where(lane < N_EXPERTS, logits, -jnp.inf)
    m1 = jnp.max(lg, axis=1, keepdims=True)
    i1 = jnp.min(jnp.where(lg == m1, lane, LANES), axis=1, keepdims=True)
    lg2 = jnp.where(lane == i1, -jnp.inf, lg)
    m2 = jnp.max(lg2, axis=1, keepdims=True)
    i2 = jnp.min(jnp.where(lg2 == m2, lane, LANES), axis=1, keepdims=True)
    e2 = jnp.exp(m2 - m1)
    w1 = 1.0 / (1.0 + e2)
    w2 = e2 / (1.0 + e2)
    o_ref[...] = jnp.where(lane == 0, i1.astype(F32),
                           jnp.where(lane == 1, i2.astype(F32),
                                     jnp.where(lane == 2, w1, jnp.where(lane == 3, w2, 0.0))))


def _router(u, router_padded, layer, *, tm):
    m, d = u.shape
    return pl.pallas_call(
        _router_kernel,
        grid=(m // tm,),
        in_specs=[pl.BlockSpec((tm, d), lambda i: (i, 0)),
                  pl.BlockSpec((1, d, LANES), lambda i: (layer, 0, 0))],
        out_specs=pl.BlockSpec((tm, LANES), lambda i: (i, 0)),
        out_shape=jax.ShapeDtypeStruct((m, LANES), F32),
        compiler_params=_cparams(("arbitrary",)),
        name="router",
    )(u, router_padded)


MOE_GROUP_TILE = 512
MOE_ROW_TILE = 256
MOE_TOK_CHUNK = 256


def _moe_plan(ridx, rw, s_len):
    gm, tm, ct, n_e = MOE_GROUP_TILE, MOE_ROW_TILE, MOE_TOK_CHUNK, N_EXPERTS
    i32 = jnp.int32
    e_a = ridx.reshape(-1).astype(i32)
    oh = (e_a[:, None] == jnp.arange(n_e, dtype=i32)[None, :]).astype(i32)
    csum = jnp.cumsum(oh, axis=0)
    rank_a = jnp.sum((csum - oh) * oh, axis=1)
    ntile_e = (csum[-1] + gm - 1) // gm
    tile_end = jnp.cumsum(ntile_e)
    pos_a = jnp.take(tile_end - ntile_e, e_a) * gm + rank_a
    n_rows = 2 * s_len + n_e * gm
    n_tiles = n_rows // tm
    n_chunks = s_len // ct
    row_tok = jnp.full((n_rows,), -1, i32).at[pos_a].set(jnp.arange(2 * s_len, dtype=i32) // 2)
    row_w = jnp.zeros((n_rows,), F32).at[pos_a].set(rw.reshape(-1))
    tile_e = jnp.minimum(jnp.searchsorted(tile_end, jnp.arange(n_rows // gm, dtype=i32), side="right"),
                         n_e - 1).astype(i32)
    rt = row_tok.reshape(n_tiles, tm)
    lo = jnp.min(jnp.where(rt >= 0, rt, s_len - 1), axis=1) // ct
    hi = jnp.maximum(jnp.max(jnp.where(rt >= 0, rt, 0), axis=1) // ct, lo)
    n_i = hi - lo + 1
    end = jnp.cumsum(n_i)
    n_work = n_tiles + n_e * n_chunks
    w = jnp.arange(n_work, dtype=i32)
    wt = jnp.minimum(jnp.searchsorted(end, w, side="right"), n_tiles - 1).astype(i32)
    wc = jnp.clip(jnp.take(lo, wt) + w - jnp.take(end - n_i, wt), 0, n_chunks - 1).astype(i32)
    wa = (w < end[-1]).astype(i32)
    order = jnp.argsort(jnp.where(wa > 0, wc * n_tiles + wt, n_chunks * n_tiles + w))
    vc = jnp.where(wa > 0, wc, n_chunks - 1)[order]
    return dict(row_tok=row_tok, row_w=row_w, tile_e=tile_e, n_tiles=n_tiles, n_work=n_work,
                gather=(wt, wc, wa), combine=(vc, wt[order], wa[order]))


def _moe_gather_kernel(wt_ref, wc_ref, wa_ref, tok_ref, u_ref, o_ref):
    w = pl.program_id(0)

    @pl.when((w == 0) | (wt_ref[jnp.maximum(w - 1, 0)] != wt_ref[w]))
    def _():
        o_ref[...] = jnp.zeros_like(o_ref)

    @pl.when(wa_ref[w] > 0)
    def _():
        ct = u_ref.shape[0]
        cols = wc_ref[w] * ct + lax.broadcasted_iota(jnp.int32, (1, ct), 1)
        onehot = jnp.where(tok_ref[...] == cols, 1.0, 0.0).astype(BF16)
        o_ref[...] += _dot(onehot, u_ref[...]).astype(o_ref.dtype)


def _moe_gather(u, plan):
    s_len, d = u.shape
    tm, ct = MOE_ROW_TILE, MOE_TOK_CHUNK
    n_rows = plan["row_tok"].shape[0]
    return pl.pallas_call(
        _moe_gather_kernel,
        grid_spec=pltpu.PrefetchScalarGridSpec(
            num_scalar_prefetch=3, grid=(plan["n_work"],),
            in_specs=[pl.BlockSpec((tm, 1), lambda w, wt, wc, wa: (wt[w], 0)),
                      pl.BlockSpec((ct, d), lambda w, wt, wc, wa: (wc[w], 0))],
            out_specs=pl.BlockSpec((tm, d), lambda w, wt, wc, wa: (wt[w], 0))),
        out_shape=jax.ShapeDtypeStruct((n_rows, d), BF16),
        compiler_params=_cparams(("arbitrary",)),
        name="moe_gather",
    )(*plan["gather"], plan["row_tok"].reshape(n_rows, 1), u)


def _moe_combine_kernel(vc_ref, vt_ref, va_ref, tok_ref, y0_ref, y1_ref, y2_ref, o_ref):
    w = pl.program_id(0)
    chunk = vc_ref[w]

    @pl.when((w == 0) | (vc_ref[jnp.maximum(w - 1, 0)] != chunk))
    def _():
        o_ref[...] = jnp.zeros_like(o_ref)

    @pl.when(va_ref[w] > 0)
    def _():
        ct = o_ref.shape[0]
        rows = chunk * ct + lax.broadcasted_iota(jnp.int32, (ct, 1), 0)
        onehot_t = jnp.where(rows == tok_ref[0], 1.0, 0.0).astype(BF16)
        o_ref[...] += (_dot(onehot_t, y0_ref[...]) + _dot(onehot_t, y1_ref[...])
                       + _dot(onehot_t, y2_ref[...]))


def _moe_combine(ys3, plan, s_len):
    n_rows, d = ys3[0].shape
    tm, ct = MOE_ROW_TILE, MOE_TOK_CHUNK
    yspec = pl.BlockSpec((tm, d), lambda w, vc, vt, va: (vt[w], 0))
    return pl.pallas_call(
        _moe_combine_kernel,
        grid_spec=pltpu.PrefetchScalarGridSpec(
            num_scalar_prefetch=3, grid=(plan["n_work"],),
            in_specs=[pl.BlockSpec((1, 1, tm), lambda w, vc, vt, va: (vt[w], 0, 0)), yspec, yspec, yspec],
            out_specs=pl.BlockSpec((ct, d), lambda w, vc, vt, va: (vc[w], 0))),
        out_shape=jax.ShapeDtypeStruct((s_len, d), F32),
        compiler_params=_cparams(("arbitrary",)),
        name="moe_combine",
    )(*plan["combine"], plan["row_tok"].reshape(plan["n_tiles"], 1, tm), *ys3)


def _gmm_kernel(te_ref, *refs, swiglu):
    it = iter(refs)
    a_ref = next(it)
    w_refs = [next(it), next(it)] if swiglu else [next(it)]
    rw_ref = next(it) if swiglu else None
    o_refs = [next(it)] if swiglu else [next(it), next(it), next(it)]
    wb_refs = [next(it) for _ in w_refs]
    i = pl.program_id(1)

    @pl.when((i == 0) | (te_ref[i] != te_ref[jnp.maximum(i - 1, 0)]))
    def _():
        for w_ref, wb_ref in zip(w_refs, wb_refs):
            wb_ref[...] = w_ref[0, 0].astype(BF16)

    a = a_ref[...]
    if swiglu:
        h = _silu(_dot(a, wb_refs[0][...])) * _dot(a, wb_refs[1][...]) * rw_ref[...]
        o_refs[0][...] = h.astype(BF16)
    else:
        y = _dot(a, wb_refs[0][...])
        hi = y.astype(BF16)
        r1 = y - hi.astype(F32)
        mid = r1.astype(BF16)
        o_refs[0][...] = hi
        o_refs[1][...] = mid
        o_refs[2][...] = (r1 - mid.astype(F32)).astype(BF16)


def _gmm(a, ws, layer, plan, *, tn, row_w=None, name="gmm"):
    n_rows, k = a.shape
    n = ws[0].shape[3]
    tm = MOE_GROUP_TILE
    swiglu = len(ws) == 2
    wspec = pl.BlockSpec((1, 1, k, tn), lambda j, i, te: (layer, te[i], 0, j))
    in_specs = [pl.BlockSpec((tm, k), lambda j, i, te: (i, 0))] + [wspec] * len(ws)
    args = [a, *ws]
    if swiglu:
        in_specs.append(pl.BlockSpec((tm, 1), lambda j, i, te: (i, 0)))
        args.append(row_w.reshape(n_rows, 1))
    ospec = pl.BlockSpec((tm, tn), lambda j, i, te: (i, j))
    oshape = jax.ShapeDtypeStruct((n_rows, n), BF16)
    return pl.pallas_call(
        functools.partial(_gmm_kernel, swiglu=swiglu),
        grid_spec=pltpu.PrefetchScalarGridSpec(
            num_scalar_prefetch=1, grid=(n // tn, n_rows // tm),
            in_specs=in_specs,
            out_specs=ospec if swiglu else [ospec] * 3,
            scratch_shapes=[pltpu.VMEM((k, tn), BF16) for _ in ws]),
        out_shape=oshape if swiglu else [oshape] * 3,
        compiler_params=_cparams(("arbitrary", "arbitrary")),
        name=name,
    )(plan["tile_e"], *args)


def _permute_w_in(w):
    cols = []
    for nm in _NEW_ORDER:
        if nm.startswith("pad"):
            cols.append(jnp.zeros(w.shape[:2] + (int(nm[3:]),), w.dtype))
        else:
            o, n = _ORIG[nm]
            cols.append(w[:, :, o:o + n])
    out = jnp.concatenate(cols, axis=2)
    assert out.shape[2] == PROJ_W
    return out


def _nsa_cmp_inputs(proj):
    s_len = proj.shape[0]
    n_cmp = (s_len - NSA_CMP_LEN) // NSA_CMP_STRIDE + 1
    ncp = s_len // NSA_CMP_STRIDE
    xs = []
    for jj in range(2):
        for g in range(NSA_GROUPS):
            c0 = COL_DKV + jj * 128 + g * HEAD_DIM
            r = proj[:, c0:c0 + HEAD_DIM].reshape(ncp, NSA_CMP_STRIDE * HEAD_DIM)
            x = jnp.concatenate([r[:-1], r[1:]], axis=1)
            xs.append(jnp.pad(x, ((0, ncp - n_cmp), (0, 0))))
    return jnp.stack(xs)


def _token_mixers(u, layer, p, cfg):
    s_len = u.shape[0]
    tm = cfg["tm"]
    proj = _mm(u, p["w_in"], layer, tm=tm, tn=512, name="in_proj")
    kv_b = _mm(proj, p["dsa_w_ukv"], layer, tm=tm, tn=512, a_blk=BLK128["b_kv"], k=DSA_KV_RANK,
               prologue="rms", gain=p["dsa_kv_norm_g"][layer], name="dsa_kv")
    kv = _kv_pack(proj, kv_b, tm=cfg["tm_ln"])

    lambda_init = 0.8 - 0.6 * math.exp(-0.3 * layer)
    sl_a = _alibi(DA_HEADS)
    units_a = [(2 * h + mp, h * 256, 256, 0,
                ((h * 128 + mp * 64, sl_a[h], ((2 * h + mp) * 128, 128), 0),))
               for h in range(DA_HEADS) for mp in range(2)]
    o_a2 = _flash(proj, kv["a_k"], kv["a_v"], units=units_a,
                  q_spec=(512, BLK512["a_q"]), out_w=1024, tq=cfg["tq"], tk=cfg["tk"], name="diff_attn")
    o_a = _diff_final(o_a2, p["diff_lambda"], layer, p["diff_subln_g"][layer], lambda_init, tm=tm)

    topk = min(DSA_TOPK_MAX, s_len // 4)
    mask_b = _dsa_select(proj, kv["b_ik"], topk=topk)
    sl8 = _alibi(8)
    units_b = [(h, h * 128, 128, 0, ((h * 64, sl8[h], (h * 64, 64), 0),)) for h in range(DSA_HEADS)]
    o_b = _flash(proj, kv["b_k"], kv["b_v"],
                 units=units_b, q_spec=(512, BLK512["b_q"]), out_w=512, tq=cfg["tq"], tk=cfg["tk"],
                 mask=mask_b.reshape(1, s_len, s_len), name="dsa_attn")

    def gqa_units(masked):
        return [(g, g * 128, 128, g if masked else 0,
                 tuple(((g * 4 + r) * 64, sl8[g * 4 + r], ((g * 4 + r) * 64, 64), g * 4 + r)
                       for r in range(4)))
                for g in range(2)]

    def gqa_kv(k_name, v_name):
        return kv[k_name], kv[v_name]

    sinks = jnp.pad(p["swa_sinks"][layer].reshape(1, SWA_HEADS), ((0, 0), (0, LANES - SWA_HEADS)))
    o_c = _flash(proj, *gqa_kv("c_k", "c_v"), units=gqa_units(False), q_spec=(512, BLK512["c_q"]),
                 out_w=512, tq=cfg["tb"], tk=cfg["tb"], window=SWA_WINDOW, sinks=sinks, name="swa_attn")

    kv_cmp = _nsa_compress(_nsa_cmp_inputs(proj), p["nsa_cmp_pos"], p["nsa_cmp_w1"], p["nsa_cmp_w2"],
                           layer)
    o_cmp, mask_d = _nsa_cmp(proj, kv_cmp)
    o_slc = _flash(proj, *gqa_kv("d_ks", "d_vs"), units=gqa_units(True), q_spec=(512, BLK512["d_q"]),
                   out_w=512, tq=cfg["tq"], tk=cfg["tk"], mask=mask_d, name="nsa_slc_attn")
    o_win = _flash(proj, *gqa_kv("d_kw", "d_vw"), units=gqa_units(False), q_spec=(512, BLK512["d_q"]),
                   out_w=512, tq=cfg["tb"], tk=cfg["tb"], window=NSA_WINDOW, name="nsa_win_attn")
    o_d = _nsa_combine(proj, o_cmp, o_slc, o_win, tm=tm)

    merged = _merge(u, (o_a, o_b, o_c, o_d), p["w_gate"], p["w_branch"], layer,
                    tm=cfg["tm_merge"], tn=256)
    return _mm(merged, p["w_o"], layer, tm=tm, tn=512, name="out_proj")


def _config(s_len):
    return dict(tm=min(1024, s_len), tm_merge=min(512, s_len), tm_ln=min(512, s_len),
                tq=min(256, s_len), tk=min(1024, s_len), tb=min(256, s_len))


def kernel(x, c, cond_w, cond_b, w_in, diff_lambda, diff_subln_g, dsa_kv_norm_g, dsa_w_uk, dsa_w_uv,
           swa_sinks, nsa_cmp_pos, nsa_cmp_w1, nsa_cmp_w2, w_branch, w_gate, w_o,
           ln1_g, ln1_b, ln2_g, ln2_b, ffn_w_gate, ffn_w_up, ffn_w_down,
           moe_router, moe_w_gate, moe_w_up, moe_w_down):
    bsz, s_len, d = x.shape
    assert bsz == 1 and d == D_MODEL
    depth = cond_w.shape[0]
    cfg = _config(s_len)
    xs = x.reshape(s_len, d)
    c8 = jnp.broadcast_to(c.reshape(1, d), (8, d))
    p = dict(w_in=_permute_w_in(w_in), diff_lambda=diff_lambda, diff_subln_g=diff_subln_g,
             dsa_kv_norm_g=dsa_kv_norm_g, dsa_w_ukv=jnp.concatenate([dsa_w_uk, dsa_w_uv], axis=2),
             swa_sinks=swa_sinks,
             nsa_cmp_pos=nsa_cmp_pos.reshape(depth, 2, 1, NSA_CMP_LEN * HEAD_DIM),
             nsa_cmp_w1=nsa_cmp_w1, nsa_cmp_w2=nsa_cmp_w2, w_branch=w_branch, w_gate=w_gate, w_o=w_o)
    router_p = jnp.pad(moe_router, ((0, 0), (0, 0), (0, LANES - N_EXPERTS)))
    mods = [_mm(c8, cond_w, l, tm=8, tn=512, prologue="silu", bias=cond_b[l], name="cond")[0:1]
            for l in range(depth)]
    u = _modulate(xs, mods[0], 1, 0, tm=cfg["tm_ln"])
    for l in range(depth):
        y = _token_mixers(u, l, p, cfg)
        xs, u = _resid_ln(xs, y, mods[l], 2, ln1_g[l], ln1_b[l], mods[l], 4, 3, tm=cfg["tm_ln"])
        jx = l // 2
        if l % 2 == 0:
            hdn = _swiglu_up(u, ffn_w_gate, ffn_w_up, jx, tm=cfg["tm"], tn=512, name="ffn_up")
            y = _mmk(hdn, ffn_w_down, jx, tm=cfg["tm"], tn=d, tk=512, name="ffn_down")
        else:
            rt = _router(u, router_p, jx, tm=cfg["tm"])
            plan = _moe_plan(rt[:, 0:2], rt[:, 2:4], s_len)
            hdn = _gmm(_moe_gather(u, plan), (moe_w_gate, moe_w_up), jx, plan, tn=512,
                       row_w=plan["row_w"], name="moe_up")
            y = _moe_combine(_gmm(hdn, (moe_w_down,), jx, plan, tn=512, name="moe_down"), plan, s_len)
        nxt = min(l + 1, depth - 1)
        xs, u = _resid_ln(xs, y, mods[l], 5, ln2_g[l], ln2_b[l], mods[nxt], 1, 0, tm=cfg["tm_ln"])
    return xs.reshape(bsz, s_len, d)
```

```python
import functools
import math

import numpy as np
import jax
import jax.numpy as jnp
from jax import lax
from jax.experimental import pallas as pl
from jax.experimental.pallas import tpu as pltpu

F32 = jnp.float32
BF16 = jnp.bfloat16
NEG = -1e30

D_MODEL = 2048
DEPTH = 4
HEAD_DIM = 64
DA_HEADS = 4
DSA_HEADS = 8
DSA_KV_RANK = 128
IDX_HEADS = 8
DSA_TOPK_MAX = 256
SWA_HEADS = 8
SWA_WINDOW = 128
NSA_HEADS = 8
NSA_GROUPS = 2
NSA_CMP_LEN = 32
NSA_CMP_STRIDE = 16
NSA_CMP_HID = 256
NSA_SLC_LEN = 64
NSA_TOPN = 16
NSA_WINDOW = 512
NSA_FORCE = 1e9
N_BRANCH = 4
BRANCH_W = 512
N_EXPERTS = 8
ALPHA = (2.0 * DEPTH) ** 0.25

VMEM_LIMIT_BYTES = 56 * 1024 * 1024
LANES = 128

_ORIG = dict(a_q=(0, 512), a_k=(512, 512), a_v=(1024, 512), b_q=(1536, 512), b_kv=(2048, 128),
             b_iq=(2176, 512), b_ik=(2688, 64), b_iw=(2752, 8), c_q=(2760, 512), c_k=(3272, 128),
             c_v=(3400, 128), d_q=(3528, 512), d_kv=(4040, 768), d_g=(4808, 24))
_NEW_ORDER = ("a_q", "a_k", "a_v", "b_q", "b_iq", "c_q", "d_q", "b_kv", "c_k", "c_v", "d_kv",
              "b_ik", "pad64", "b_iw", "d_g", "pad96", "pad128")
PROJ_W = 5120
BLK512 = dict(a_q=0, a_k=1, a_v=2, b_q=3, b_iq=4, c_q=5, d_q=6)
BLK128 = dict(b_kv=28, c_k=29, c_v=30, d_kc=31, d_vc=32, d_ks=33, d_vs=34, d_kw=35, d_vw=36,
              b_ik=37, small=38)
COL_DKV = 3968


def _cparams(sem):
    return pltpu.CompilerParams(dimension_semantics=sem, vmem_limit_bytes=VMEM_LIMIT_BYTES)


def _sigmoid(x):
    return 1.0 / (1.0 + jnp.exp(-x))


def _silu(x):
    return x * _sigmoid(x)


def _alibi(n_heads):
    return [2.0 ** (-8.0 * (h + 1) / n_heads) for h in range(n_heads)]


def _dot(a, b):
    return jnp.dot(a, b, preferred_element_type=F32)


def _dot_nt(a, b):
    return lax.dot_general(a, b, (((1,), (1,)), ((), ())), preferred_element_type=F32)


def _mm_kernel(*refs, prologue, has_bias, eps):
    it = iter(refs)
    a_ref = next(it)
    g_ref = next(it) if prologue == "rms" else None
    w_ref = next(it)
    b_ref = next(it) if has_bias else None
    o_ref = next(it)
    wb_ref = next(it)

    @pl.when(pl.program_id(1) == 0)
    def _():
        wb_ref[...] = w_ref[0].astype(BF16)

    a = a_ref[...]
    if prologue == "silu":
        a = _silu(a.astype(F32))
    elif prologue == "rms":
        a = a.astype(F32)
        a = a * lax.rsqrt(jnp.mean(a * a, axis=-1, keepdims=True) + eps) * g_ref[...]
    acc = _dot(a.astype(BF16), wb_ref[...])
    if has_bias:
        acc = acc + b_ref[...]
    o_ref[...] = acc.astype(o_ref.dtype)


def _mm(a, w, layer, *, tm, tn, out_dtype=F32, a_blk=0, k=None, prologue=None, gain=None, bias=None,
        eps=1e-6, name="mm"):
    m = a.shape[0]
    k = a.shape[1] if k is None else k
    n = w.shape[2]
    assert w.shape[1] == k and m % tm == 0 and n % tn == 0
    in_specs = [pl.BlockSpec((tm, k), lambda j, i: (i, a_blk))]
    args = [a]
    if prologue == "rms":
        in_specs.append(pl.BlockSpec((1, k), lambda j, i: (0, 0)))
        args.append(gain.reshape(1, k))
    in_specs.append(pl.BlockSpec((1, k, tn), lambda j, i: (layer, 0, j)))
    args.append(w)
    if bias is not None:
        in_specs.append(pl.BlockSpec((1, tn), lambda j, i: (0, j)))
        args.append(bias.reshape(1, n))
    return pl.pallas_call(
        functools.partial(_mm_kernel, prologue=prologue, has_bias=bias is not None, eps=eps),
        grid=(n // tn, m // tm),
        in_specs=in_specs,
        out_specs=pl.BlockSpec((tm, tn), lambda j, i: (i, j)),
        out_shape=jax.ShapeDtypeStruct((m, n), out_dtype),
        scratch_shapes=[pltpu.VMEM((k, tn), BF16)],
        compiler_params=_cparams(("arbitrary", "arbitrary")),
        name=name,
    )(*args)


def _mmk_kernel(a_ref, w_ref, o_ref, acc_ref, *, nk):
    kk = pl.program_id(2)

    @pl.when(kk == 0)
    def _():
        acc_ref[...] = jnp.zeros_like(acc_ref)

    acc_ref[...] += _dot(a_ref[...], w_ref[0].astype(BF16))

    @pl.when(kk == nk - 1)
    def _():
        o_ref[...] = acc_ref[...]


def _mmk(a, w, layer, *, tm, tn, tk, name="mmk"):
    m, k = a.shape
    n = w.shape[2]
    assert w.shape[1] == k and m % tm == 0 and n % tn == 0 and k % tk == 0
    nk = k // tk
    return pl.pallas_call(
        functools.partial(_mmk_kernel, nk=nk),
        grid=(m // tm, n // tn, nk),
        in_specs=[pl.BlockSpec((tm, tk), lambda i, j, kk: (i, kk)),
                  pl.BlockSpec((1, tk, tn), lambda i, j, kk: (layer, kk, j))],
        out_specs=pl.BlockSpec((tm, tn), lambda i, j, kk: (i, j)),
        out_shape=jax.ShapeDtypeStruct((m, n), F32),
        scratch_shapes=[pltpu.VMEM((tm, tn), F32)],
        compiler_params=_cparams(("arbitrary", "arbitrary", "arbitrary")),
        name=name,
    )(a, w)


def _swiglu_kernel(a_ref, wg_ref, wu_ref, o_ref, wgb_ref, wub_ref):
    @pl.when(pl.program_id(1) == 0)
    def _():
        wgb_ref[...] = wg_ref[0].astype(BF16)
        wub_ref[...] = wu_ref[0].astype(BF16)

    a = a_ref[...]
    o_ref[...] = (_silu(_dot(a, wgb_ref[...])) * _dot(a, wub_ref[...])).astype(o_ref.dtype)


def _swiglu_up(u, wg, wu, layer, *, tm, tn, name="swiglu_up"):
    m, k = u.shape
    f = wg.shape[2]
    assert f % tn == 0 and m % tm == 0
    wspec = pl.BlockSpec((1, k, tn), lambda j, i: (layer, 0, j))
    return pl.pallas_call(
        _swiglu_kernel,
        grid=(f // tn, m // tm),
        in_specs=[pl.BlockSpec((tm, k), lambda j, i: (i, 0)), wspec, wspec],
        out_specs=pl.BlockSpec((tm, tn), lambda j, i: (i, j)),
        out_shape=jax.ShapeDtypeStruct((m, f), BF16),
        scratch_shapes=[pltpu.VMEM((k, tn), BF16), pltpu.VMEM((k, tn), BF16)],
        compiler_params=_cparams(("arbitrary", "arbitrary")),
        name=name,
    )(u, wg, wu)


def _modulate_kernel(x_ref, sc_ref, sh_ref, u_ref):
    u_ref[...] = (x_ref[...] * (1.0 + sc_ref[...]) + sh_ref[...]).astype(u_ref.dtype)


def _modulate(x, mod, sc_blk, sh_blk, *, tm):
    m, d = x.shape
    return pl.pallas_call(
        _modulate_kernel,
        grid=(m // tm,),
        in_specs=[pl.BlockSpec((tm, d), lambda i: (i, 0)),
                  pl.BlockSpec((1, d), lambda i: (0, sc_blk)),
                  pl.BlockSpec((1, d), lambda i: (0, sh_blk))],
        out_specs=pl.BlockSpec((tm, d), lambda i: (i, 0)),
        out_shape=jax.ShapeDtypeStruct((m, d), BF16),
        compiler_params=_cparams(("arbitrary",)),
        name="modulate",
    )(x, mod, mod)


def _resid_ln_kernel(x_ref, y_ref, gate_ref, g_ref, b_ref, sc_ref, sh_ref, xo_ref, u_ref):
    z = ALPHA * x_ref[...] + gate_ref[...] * y_ref[...]
    mu = jnp.mean(z, axis=-1, keepdims=True)
    zc = z - mu
    var = jnp.mean(zc * zc, axis=-1, keepdims=True)
    xn = zc * lax.rsqrt(var + 1e-5) * g_ref[...] + b_ref[...]
    xo_ref[...] = xn
    u_ref[...] = (xn * (1.0 + sc_ref[...]) + sh_ref[...]).astype(u_ref.dtype)


def _resid_ln(x, y, mod, gate_blk, g, b, mod_next, sc_blk, sh_blk, *, tm):
    m, d = x.shape
    row = lambda blk: pl.BlockSpec((1, d), lambda i: (0, blk))
    return pl.pallas_call(
        _resid_ln_kernel,
        grid=(m // tm,),
        in_specs=[pl.BlockSpec((tm, d), lambda i: (i, 0)),
                  pl.BlockSpec((tm, d), lambda i: (i, 0)),
                  row(gate_blk), row(0), row(0), row(sc_blk), row(sh_blk)],
        out_specs=[pl.BlockSpec((tm, d), lambda i: (i, 0)),
                   pl.BlockSpec((tm, d), lambda i: (i, 0))],
        out_shape=[jax.ShapeDtypeStruct((m, d), F32), jax.ShapeDtypeStruct((m, d), BF16)],
        compiler_params=_cparams(("arbitrary",)),
        name="resid_ln",
    )(x, y, mod, g.reshape(1, d), b.reshape(1, d), mod_next, mod_next)


FLASH_ROW_CHUNK = 32
POS_SPLIT = 128


_KV_PACK = (("a_k", "a_k", 0, 2 * DA_HEADS, 64, "k"), ("a_v", "a_v", 0, DA_HEADS, 128, "v"),
            ("b_k", "kv_b", 0, DSA_HEADS, 64, "k"), ("b_v", "kv_b", 512, DSA_HEADS, 64, "v"),
            ("c_k", "c_k", 0, 2, 64, "k"), ("c_v", "c_v", 0, 2, 64, "v"),
            ("d_ks", "d_ks", 0, 2, 64, "k"), ("d_vs", "d_vs", 0, 2, 64, "v"),
            ("d_kw", "d_kw", 0, 2, 64, "k"), ("d_vw", "d_vw", 0, 2, 64, "v"),
            ("b_ik", "b_ik", 0, 1, 128, "cast"))
_KV_SOURCES = ("a_k", "a_v", "c_k", "c_v", "d_ks", "d_vs", "d_kw", "d_vw", "b_ik", "kv_b")


def _kv_pack_kernel(*refs, tm):
    src = dict(zip(_KV_SOURCES, refs[:len(_KV_SOURCES)]))
    outs = refs[len(_KV_SOURCES):]
    i = pl.program_id(0)
    tails = {}
    for w in (64, 128):
        lane = lax.broadcasted_iota(jnp.int32, (tm, w), 1)
        pos = i * tm + lax.broadcasted_iota(jnp.int32, (tm, w), 0)
        tails[("v", w)] = jnp.where(lane == 0, 1.0, 0.0)
        tails[("k", w)] = jnp.where(lane == 0, (pos // POS_SPLIT).astype(F32),
                                    jnp.where(lane == 1, (pos % POS_SPLIT).astype(F32),
                                              jnp.where(lane < 4, 1.0, 0.0)))
    for (_, sname, c0, n_heads, w, kind), o_ref in zip(_KV_PACK, outs):
        if kind == "cast":
            o_ref[...] = src[sname][...].astype(BF16)
            continue
        for h in range(n_heads):
            x = src[sname][:, c0 + h * w:c0 + (h + 1) * w]
            o_ref[:, 2 * h * w:2 * (h + 1) * w] = jnp.concatenate([x, tails[(kind, w)]], axis=1).astype(BF16)


def _kv_pack(proj, kv_b, *, tm):
    s_len = proj.shape[0]
    in_specs, args = [], []
    for sname in _KV_SOURCES:
        if sname == "kv_b":
            in_specs.append(pl.BlockSpec((tm, kv_b.shape[1]), lambda i: (i, 0)))
            args.append(kv_b)
        else:
            wblk, blk = (512, BLK512[sname]) if sname in BLK512 else (128, BLK128[sname])
            in_specs.append(pl.BlockSpec((tm, wblk), functools.partial(lambda i, blk: (i, blk), blk=blk)))
            args.append(proj)
    widths = [(1 if kind == "cast" else 2) * n_heads * w for (_, _, _, n_heads, w, kind) in _KV_PACK]
    outs = pl.pallas_call(
        functools.partial(_kv_pack_kernel, tm=tm),
        grid=(s_len // tm,),
        in_specs=in_specs,
        out_specs=[pl.BlockSpec((tm, wd), lambda i: (i, 0)) for wd in widths],
        out_shape=[jax.ShapeDtypeStruct((s_len, wd), BF16) for wd in widths],
        compiler_params=_cparams(("arbitrary",)),
        name="kv_pack",
    )(*args)
    return {name: o for (name, *_), o in zip(_KV_PACK, outs)}


def _flash_kernel(qi_ref, kb_ref, first_ref, last_ref, *refs, units, tq, tk, window, dense, n_mask,
                  has_sink):
    it = iter(refs)
    q_ref, k_ref, v_ref = next(it), next(it), next(it)
    mask_ref = next(it) if n_mask else None
    sink_ref = next(it) if has_sink else None
    o_ref = next(it)
    q_scr, m_scr, acc_scr, bias_scr = (next(it) for _ in range(4))
    dv = acc_scr.shape[2]
    lcol = dv // 2

    w = pl.program_id(0)
    qi = qi_ref[w]
    kb = kb_ref[w]
    rows = q_scr.shape[1]
    rb = FLASH_ROW_CHUNK
    kw = 2 * HEAD_DIM

    @pl.when(first_ref[w] > 0)
    def _init():
        lane = lax.broadcasted_iota(jnp.int32, (tq, HEAD_DIM), 1)
        qpos = qi * tq + lax.broadcasted_iota(jnp.int32, (tq, HEAD_DIM), 0)
        qhi = (qpos // POS_SPLIT).astype(F32)
        qlo = (qpos % POS_SPLIT).astype(F32)
        for ui, (_, _, _, _, hds) in enumerate(units):
            for r, (qo, slope, _, sink_idx) in enumerate(hds):
                rsl = slice(r * tq, (r + 1) * tq)
                tail = jnp.where(lane == 0, POS_SPLIT * slope,
                                 jnp.where(lane == 1, slope,
                                           jnp.where(lane == 2, -POS_SPLIT * slope * qhi,
                                                     jnp.where(lane == 3, -slope * qlo, 0.0))))
                qs = q_ref[:, qo:qo + HEAD_DIM] * HEAD_DIM ** -0.5
                q_scr[ui, rsl] = jnp.concatenate([qs, tail], axis=1).astype(BF16)
                if has_sink:
                    m_scr[ui, rsl] = jnp.broadcast_to(sink_ref[:, sink_idx:sink_idx + 1], (tq, 1))
                else:
                    m_scr[ui, rsl] = jnp.full((tq, 1), NEG, F32)
            alane = lax.broadcasted_iota(jnp.int32, acc_scr.shape[1:], 1)
            acc_scr[ui] = jnp.where(alane == lcol, 1.0 if has_sink else 0.0, 0.0)

    def scores(ui):
        ku = units[ui][0]
        return _dot_nt(q_scr[ui], k_ref[:, ku * kw:(ku + 1) * kw])

    def step(masked):
        if masked:
            qpos = qi * tq + lax.broadcasted_iota(jnp.int32, (tq, tk), 0)
            kpos = kb * tk + lax.broadcasted_iota(jnp.int32, (tq, tk), 1)
            dist = qpos - kpos
            valid = dist >= 0
            if not dense:
                valid = valid & (dist < window)
            if n_mask:
                for g in range(n_mask):
                    bias_scr[g] = jnp.where(valid, mask_ref[g].astype(F32), NEG)
            else:
                bias_scr[0] = jnp.where(valid, 0.0, NEG)

        def chunk(s, mg, c):
            r0 = c * rb
            sc = s[r0:r0 + rb]
            if masked:
                rw = r0 % tq
                sc = sc + bias_scr[mg, rw:rw + rb]
            return sc

        s_next = scores(0)
        for ui, (_, vo, _, mg, _) in enumerate(units):
            s = s_next
            if ui + 1 < len(units):
                s_next = scores(ui + 1)
            nchunk = rows // rb
            m_old = m_scr[ui]
            m_cur = jnp.concatenate([jnp.max(chunk(s, mg, c), axis=1, keepdims=True) for c in range(nchunk)],
                                    axis=0)
            m_new = jnp.maximum(m_old, m_cur)
            alpha = jnp.exp(m_old - m_new)
            m_scr[ui] = m_new
            p_all = jnp.concatenate(
                [jnp.exp(chunk(s, mg, c) - m_new[c * rb:(c + 1) * rb]).astype(BF16) for c in range(nchunk)],
                axis=0)
            acc_scr[ui] = alpha * acc_scr[ui] + _dot(p_all, v_ref[:, vo:vo + dv])

    if dense and not n_mask:
        interior = kb * tk + tk - 1 <= qi * tq
        pl.when(interior)(lambda: step(False))
        pl.when(jnp.logical_not(interior))(lambda: step(True))
    else:
        step(True)

    @pl.when(last_ref[w] > 0)
    def _fin():
        for ui, (_, _, _, _, hds) in enumerate(units):
            for r, (_, _, (oo, ow), _) in enumerate(hds):
                rsl = slice(r * tq, (r + 1) * tq)
                acc = acc_scr[ui, rsl]
                o_ref[:, oo:oo + ow] = acc[:, 0:ow] / acc[:, lcol:lcol + 1]


def _flash(q_arr, k_arr, v_arr, *, units, q_spec, out_w, tq, tk, window=None,
           mask=None, sinks=None, name="flash"):
    s_len = q_arr.shape[0]
    dense = window is None
    if not dense:
        assert tq == tk
    n_mask = 0 if mask is None else mask.shape[0]
    dv = units[0][2]
    nu = len(units)
    rows = len(units[0][4]) * tq
    assert all(len(un[4]) * tq == rows and un[2] == dv for un in units) and rows % FLASH_ROW_CHUNK == 0

    pairs = []
    for qi in range(s_len // tq):
        last_kb = (qi * tq + tq - 1) // tk
        first_kb = 0 if dense else max(qi - (-(-(window - 1) // tk)), 0)
        pairs += [(qi, kb, int(kb == first_kb), int(kb == last_kb)) for kb in range(first_kb, last_kb + 1)]
    tables = [jnp.asarray(np.array(col, np.int32)) for col in zip(*pairs)]

    in_specs = [pl.BlockSpec((tq, q_spec[0]), lambda w, qi, kb, fi, la: (qi[w], q_spec[1])),
                pl.BlockSpec((tk, k_arr.shape[1]), lambda w, qi, kb, fi, la: (kb[w], 0)),
                pl.BlockSpec((tk, v_arr.shape[1]), lambda w, qi, kb, fi, la: (kb[w], 0))]
    args = [q_arr, k_arr, v_arr]
    if n_mask:
        in_specs.append(pl.BlockSpec((n_mask, tq, tk), lambda w, qi, kb, fi, la: (0, qi[w], kb[w])))
        args.append(mask)
    if sinks is not None:
        in_specs.append(pl.BlockSpec((1, LANES), lambda w, qi, kb, fi, la: (0, 0)))
        args.append(sinks)
    return pl.pallas_call(
        functools.partial(_flash_kernel, units=tuple(units), tq=tq, tk=tk, window=window, dense=dense,
                          n_mask=n_mask, has_sink=sinks is not None),
        grid_spec=pltpu.PrefetchScalarGridSpec(
            num_scalar_prefetch=4, grid=(len(pairs),),
            in_specs=in_specs,
            out_specs=pl.BlockSpec((tq, out_w), lambda w, qi, kb, fi, la: (qi[w], 0)),
            scratch_shapes=[pltpu.VMEM((nu, rows, 2 * HEAD_DIM), BF16), pltpu.VMEM((nu, rows, 1), F32),
                            pltpu.VMEM((nu, rows, dv), F32), pltpu.VMEM((max(n_mask, 1), tq, tk), F32)]),
        out_shape=jax.ShapeDtypeStruct((s_len, out_w), F32),
        compiler_params=_cparams(("arbitrary",)),
        name=name,
    )(*tables, *args)


def _diff_final_kernel(o_ref, lam_ref, g_ref, out_ref, *, lambda_init):
    lf = lam_ref[0]
    lam = (jnp.exp(jnp.sum(lf[0:1] * lf[1:2])) - jnp.exp(jnp.sum(lf[2:3] * lf[3:4])) + lambda_init)
    w = 2 * HEAD_DIM
    for h in range(DA_HEADS):
        o = o_ref[:, (2 * h) * w:(2 * h + 1) * w] - lam * o_ref[:, (2 * h + 1) * w:(2 * h + 2) * w]
        o = o * lax.rsqrt(jnp.mean(o * o, axis=-1, keepdims=True) + 1e-6) * g_ref[...]
        out_ref[:, h * w:(h + 1) * w] = o * (1.0 - lambda_init)


def _diff_final(o, diff_lambda, layer, subln_g, lambda_init, *, tm):
    m = o.shape[0]
    w = 2 * HEAD_DIM
    return pl.pallas_call(
        functools.partial(_diff_final_kernel, lambda_init=lambda_init),
        grid=(m // tm,),
        in_specs=[pl.BlockSpec((tm, 2 * DA_HEADS * w), lambda i: (i, 0)),
                  pl.BlockSpec((1, 4, HEAD_DIM), lambda i: (layer, 0, 0)),
                  pl.BlockSpec((1, w), lambda i: (0, 0))],
        out_specs=pl.BlockSpec((tm, DA_HEADS * w), lambda i: (i, 0)),
        out_shape=jax.ShapeDtypeStruct((m, DA_HEADS * w), F32),
        compiler_params=_cparams(("arbitrary",)),
        name="diff_final",
    )(o, diff_lambda, subln_g.reshape(1, w))


def _f32_key_const(x):
    b = int(np.array(x, np.float32).view(np.int32))
    return b ^ ((b >> 31) & 0x7FFFFFFF)


I16_MIN = -(2 ** 15)


def _dsa_select_kernel(qi_ref, w_ref, kidx_ref, mask_ref, key_scr, half_scr, j_scr, *, tq, ch, nch, topk,
                       s_len):
    i = pl.program_id(0)
    q0 = i * tq
    n_need = (q0 + tq + ch - 1) // ch
    qpos = q0 + lax.broadcasted_iota(jnp.int32, (tq, 1), 0)
    lane = lax.broadcasted_iota(jnp.int32, (1, ch), 1)
    w = w_ref[:, 0:IDX_HEADS]
    q_all = jnp.concatenate([qi_ref[:, h * HEAD_DIM:(h + 1) * HEAD_DIM] for h in range(IDX_HEADS)],
                            axis=0).astype(BF16)

    def score_chunk(c, carry):
        kc = kidx_ref[pl.ds(pl.multiple_of(c * ch, ch), ch), 0:HEAD_DIM].astype(BF16)
        lg = _dot_nt(q_all, kc)
        acc = jnp.zeros((tq, ch), F32)
        for h in range(IDX_HEADS):
            acc = acc + w[:, h:h + 1] * jnp.maximum(lg[h * tq:(h + 1) * tq], 0.0)
        acc = jnp.where(c * ch + lane <= qpos, acc, NEG) + 0.0
        bits = pltpu.bitcast(acc, jnp.int32)
        key = bits ^ ((bits >> 31) & 0x7FFFFFFF)
        key_scr[c] = key
        half_scr[c] = (key >> 16).astype(jnp.int16)
        return carry

    lax.fori_loop(0, n_need, score_chunk, 0)

    def count16(cand, strict):
        cand16 = jnp.broadcast_to(cand, (tq, LANES)).astype(jnp.int16)
        one, zero = jnp.int16(1), jnp.int16(0)

        def body(c, acc):
            blk = half_scr[c]
            for t in range(ch // LANES):
                tile = blk[:, t * LANES:(t + 1) * LANES]
                acc = acc + jnp.where(tile > cand16 if strict else tile >= cand16, one, zero)
            return acc
        acc = lax.fori_loop(0, n_need, body, jnp.zeros((tq, LANES), jnp.int16))
        return jnp.sum(acc.astype(jnp.int32), axis=1, keepdims=True)

    def search16(need_cnt):
        def bit_step(b, t):
            cand = t + jnp.left_shift(jnp.int32(1), 15 - b)
            return jnp.where(count16(cand, False) >= need_cnt, cand, t)
        return lax.fori_loop(0, 16, bit_step, jnp.full((tq, 1), I16_MIN, jnp.int32))

    t_hi = search16(topk)
    need_lo = topk - count16(t_hi, True)

    def low_chunk(c, carry):
        key = key_scr[c]
        low = (key & 0xFFFF) + I16_MIN
        half_scr[c] = jnp.where((key >> 16) == t_hi, low, I16_MIN).astype(jnp.int16)
        return carry

    lax.fori_loop(0, n_need, low_chunk, 0)
    t_lo = search16(need_lo)
    thr = jnp.left_shift(t_hi, 16) + (t_lo - I16_MIN)

    def count(pred):
        def body(c, acc):
            m = jnp.where(pred(key_scr[c], c), 1, 0)
            part = m[:, 0:LANES]
            for t in range(1, ch // LANES):
                part = part + m[:, t * LANES:(t + 1) * LANES]
            return acc + part
        acc = lax.fori_loop(0, n_need, body, jnp.zeros((tq, LANES), jnp.int32))
        return jnp.sum(acc, axis=1, keepdims=True)

    cnt_gt = count(lambda blk, c: blk > thr)
    cnt_ge = count(lambda blk, c: blk >= thr)
    need = topk - cnt_gt
    tie_rows = (cnt_ge > topk) & (thr > _f32_key_const(NEG))
    j_scr[...] = jnp.full((tq, 1), s_len, jnp.int32)
    any_tie = jnp.max(jnp.where(tie_rows, 1, 0)) > 0

    @pl.when(any_tie)
    def _ties():
        def tie_chunk(c, carry):
            half_scr[c] = jnp.where(key_scr[c] == thr, -1 - (c * ch + lane), I16_MIN).astype(jnp.int16)
            return carry

        lax.fori_loop(0, n_need, tie_chunk, 0)
        j_scr[...] = jnp.where(tie_rows, -1 - search16(need), s_len)

    jv = j_scr[...]
    for c in range(nch):
        @pl.when((c < n_need) & any_tie)
        def _w():
            key = key_scr[c]
            sel = (key > thr) | ((key == thr) & (c * ch + lane <= jv))
            mask_ref[:, c * ch:(c + 1) * ch] = jnp.where(sel, 0.0, NEG).astype(mask_ref.dtype)

        @pl.when((c < n_need) & jnp.logical_not(any_tie))
        def _wf():
            mask_ref[:, c * ch:(c + 1) * ch] = jnp.where(key_scr[c] >= thr, 0.0, NEG).astype(mask_ref.dtype)

        @pl.when(c >= n_need)
        def _z():
            mask_ref[:, c * ch:(c + 1) * ch] = jnp.full((tq, ch), NEG, mask_ref.dtype)


def _dsa_select(proj, kidx, *, topk, tq=256):
    s_len = proj.shape[0]
    assert s_len < -I16_MIN
    ch = min(1024, s_len)
    nch = s_len // ch
    return pl.pallas_call(
        functools.partial(_dsa_select_kernel, tq=tq, ch=ch, nch=nch, topk=topk, s_len=s_len),
        grid=(s_len // tq,),
        in_specs=[pl.BlockSpec((tq, 512), lambda i: (i, BLK512["b_iq"])),
                  pl.BlockSpec((tq, LANES), lambda i: (i, BLK128["small"])),
                  pl.BlockSpec((s_len, LANES), lambda i: (0, 0))],
        out_specs=pl.BlockSpec((tq, s_len), lambda i: (i, 0)),
        out_shape=jax.ShapeDtypeStruct((s_len, s_len), BF16),
        scratch_shapes=[pltpu.VMEM((nch, tq, ch), jnp.int32), pltpu.VMEM((nch, tq, ch), jnp.int16),
                        pltpu.VMEM((tq, 1), jnp.int32)],
        compiler_params=_cparams(("arbitrary",)),
        name="dsa_select",
    )(proj, proj, kidx)


def _nsa_compress_kernel(x_ref, pos_ref, w1_ref, w2_ref, o_ref):
    x = (x_ref[0] + pos_ref[0, 0]).astype(BF16)
    hdn = _silu(_dot(x, w1_ref[0, 0].astype(BF16)))
    o_ref[0] = _dot(hdn.astype(BF16), w2_ref[0, 0].astype(BF16))


def _nsa_compress(xc, pos, w1, w2, layer):
    _, ncp, kdim = xc.shape
    return pl.pallas_call(
        _nsa_compress_kernel,
        grid=(4,),
        in_specs=[pl.BlockSpec((1, ncp, kdim), lambda i: (i, 0, 0)),
                  pl.BlockSpec((1, 1, 1, kdim), lambda i: (layer, i // 2, 0, 0)),
                  pl.BlockSpec((1, 1, kdim, NSA_CMP_HID), lambda i: (layer, i // 2, 0, 0)),
                  pl.BlockSpec((1, 1, NSA_CMP_HID, HEAD_DIM), lambda i: (layer, i // 2, 0, 0))],
        out_specs=pl.BlockSpec((1, ncp, HEAD_DIM), lambda i: (i, 0, 0)),
        out_shape=jax.ShapeDtypeStruct((4, ncp, HEAD_DIM), F32),
        compiler_params=_cparams(("arbitrary",)),
        name="nsa_compress",
    )(xc, pos, w1, w2)


def _nsa_cmp_kernel(q_ref, kv_ref, ov_ref, ex_ref, o_ref, mask_ref, *, tq, ncp, n_slc, topn, ch, nch):
    i = pl.program_id(0)
    q0 = i * tq
    n_need = (q0 + tq + ch - 1) // ch
    rpg = NSA_HEADS // NSA_GROUPS
    slopes = _alibi(NSA_HEADS)
    scale = HEAD_DIM ** -0.5
    qpos_c = q0 + lax.broadcasted_iota(jnp.int32, (tq, ncp), 0)
    cend = lax.broadcasted_iota(jnp.int32, (tq, ncp), 1) * NSA_CMP_STRIDE + (NSA_CMP_LEN - 1)
    dist_c = qpos_c - cend
    valid_c = dist_c >= 0
    distf = dist_c.astype(F32)
    qpos = q0 + lax.broadcasted_iota(jnp.int32, (tq, n_slc), 0)
    blk = lax.broadcasted_iota(jnp.int32, (tq, n_slc), 1)
    cur = qpos // NSA_SLC_LEN
    forced = (blk == 0) | (blk == cur) | (blk == cur - 1)
    blk_ok = blk * NSA_SLC_LEN <= qpos
    ov = ov_ref[...]
    imps = []
    for g in range(NSA_GROUPS):
        kc = kv_ref[g].astype(BF16)
        vc = kv_ref[NSA_GROUPS + g].astype(BF16)
        psum = jnp.zeros((tq, ncp), F32)
        for r in range(rpg):
            h = g * rpg + r
            qh = q_ref[:, h * HEAD_DIM:(h + 1) * HEAD_DIM].astype(BF16)
            s = _dot_nt(qh, kc) * scale - slopes[h] * distf
            s = jnp.where(valid_c, s, NEG)
            e = jnp.where(valid_c, jnp.exp(s - jnp.max(s, axis=1, keepdims=True)), 0.0)
            p = e / jnp.maximum(jnp.sum(e, axis=1, keepdims=True), 1e-30)
            o_ref[:, h * HEAD_DIM:(h + 1) * HEAD_DIM] = _dot(p.astype(BF16), vc)
            psum = psum + p
        p_hi = psum.astype(BF16)
        p_lo = (psum - p_hi.astype(F32)).astype(BF16)
        imp = _dot(p_hi, ov) + _dot(p_lo, ov)
        imp = jnp.where(forced, NSA_FORCE, imp)
        imps.append(jnp.where(blk_ok, imp, NEG))
    imps = [imp.T for imp in imps]
    blk_t = lax.broadcasted_iota(jnp.int32, (n_slc, tq), 0)
    sels = [jnp.full((n_slc, tq), NEG, F32) for _ in range(NSA_GROUPS)]
    for _ in range(topn):
        for g in range(NSA_GROUPS):
            mx = jnp.max(imps[g], axis=0, keepdims=True)
            first = jnp.min(jnp.where(imps[g] == mx, blk_t, n_slc), axis=0, keepdims=True)
            hit = blk_t == first
            sels[g] = jnp.where(hit, 0.0, sels[g])
            imps[g] = jnp.where(hit, -jnp.inf, imps[g])
    sels = [sel.T for sel in sels]
    for g in range(NSA_GROUPS):
        selb = sels[g].astype(BF16)
        for c in range(nch):
            @pl.when(c < n_need)
            def _w():
                tok = _dot(selb, ex_ref[:, c * ch:(c + 1) * ch])
                mask_ref[g, :, c * ch:(c + 1) * ch] = tok.astype(mask_ref.dtype)

            @pl.when(c >= n_need)
            def _z():
                mask_ref[g, :, c * ch:(c + 1) * ch] = jnp.full((tq, ch), NEG, mask_ref.dtype)


def _nsa_cmp(proj, kv_cmp, *, tq=256):
    s_len = proj.shape[0]
    ncp = kv_cmp.shape[1]
    n_slc = s_len // NSA_SLC_LEN
    topn = min(NSA_TOPN, n_slc)
    ch = min(1024, s_len)
    nch = s_len // ch
    starts = np.arange(ncp) * NSA_CMP_STRIDE
    slc_start = np.arange(n_slc) * NSA_SLC_LEN
    overlap = ((starts[:, None] < slc_start[None, :] + NSA_SLC_LEN)
               & (starts[:, None] + NSA_CMP_LEN > slc_start[None, :])).astype(np.float32)
    expand = (np.arange(s_len)[None, :] // NSA_SLC_LEN == np.arange(n_slc)[:, None]).astype(np.float32)
    return pl.pallas_call(
        functools.partial(_nsa_cmp_kernel, tq=tq, ncp=ncp, n_slc=n_slc, topn=topn, ch=ch, nch=nch),
        grid=(s_len // tq,),
        in_specs=[pl.BlockSpec((tq, 512), lambda i: (i, BLK512["d_q"])),
                  pl.BlockSpec((4, ncp, HEAD_DIM), lambda i: (0, 0, 0)),
                  pl.BlockSpec((ncp, n_slc), lambda i: (0, 0)),
                  pl.BlockSpec((n_slc, s_len), lambda i: (0, 0))],
        out_specs=[pl.BlockSpec((tq, 512), lambda i: (i, 0)),
                   pl.BlockSpec((NSA_GROUPS, tq, s_len), lambda i: (0, i, 0))],
        out_shape=[jax.ShapeDtypeStruct((s_len, 512), F32),
                   jax.ShapeDtypeStruct((NSA_GROUPS, s_len, s_len), BF16)],
        compiler_params=_cparams(("arbitrary",)),
        name="nsa_cmp",
    )(proj, kv_cmp, jnp.asarray(overlap, BF16), jnp.asarray(expand, BF16))


def _nsa_combine_kernel(g_ref, oc_ref, os_ref, ow_ref, o_ref):
    gt = _sigmoid(g_ref[...])
    for h in range(NSA_HEADS):
        sl = slice(h * HEAD_DIM, (h + 1) * HEAD_DIM)
        c0 = IDX_HEADS + 3 * h
        o_ref[:, sl] = (gt[:, c0:c0 + 1] * oc_ref[:, sl] + gt[:, c0 + 1:c0 + 2] * os_ref[:, sl]
                        + gt[:, c0 + 2:c0 + 3] * ow_ref[:, sl])


def _nsa_combine(proj, o_cmp, o_slc, o_win, *, tm):
    m = proj.shape[0]
    spec = pl.BlockSpec((tm, 512), lambda i: (i, 0))
    return pl.pallas_call(
        _nsa_combine_kernel,
        grid=(m // tm,),
        in_specs=[pl.BlockSpec((tm, LANES), lambda i: (i, BLK128["small"])), spec, spec, spec],
        out_specs=spec,
        out_shape=jax.ShapeDtypeStruct((m, 512), F32),
        compiler_params=_cparams(("arbitrary",)),
        name="nsa_combine",
    )(proj, o_cmp, o_slc, o_win)


def _merge_kernel(u_ref, oa_ref, ob_ref, oc_ref, od_ref, wg0, wg1, wg2, wg3, wb_ref, o_ref,
                  wgb_ref, wbb_ref):
    wgs = (wg0, wg1, wg2, wg3)

    @pl.when(pl.program_id(1) == 0)
    def _():
        for mch in range(N_BRANCH):
            wgb_ref[mch] = wgs[mch][0].astype(BF16)
            wbb_ref[mch] = wb_ref[0, mch].astype(BF16)

    u = u_ref[...]
    acc = None
    for mch, o_ref_m in enumerate((oa_ref, ob_ref, oc_ref, od_ref)):
        gte = _sigmoid(_dot(u, wgb_ref[mch]))
        z = _dot(o_ref_m[...].astype(BF16), wbb_ref[mch])
        acc = gte * z if acc is None else acc + gte * z
    o_ref[...] = acc.astype(o_ref.dtype)


def _merge(u, branches, w_gate, w_branch, layer, *, tm, tn):
    m, d = u.shape
    nj = d // tn
    bspec = pl.BlockSpec((tm, BRANCH_W), lambda j, i: (i, 0))
    wg_specs = [pl.BlockSpec((1, d, tn),
                             functools.partial(lambda j, i, mch: (layer, 0, mch * nj + j), mch=mch))
                for mch in range(N_BRANCH)]
    return pl.pallas_call(
        _merge_kernel,
        grid=(nj, m // tm),
        in_specs=[pl.BlockSpec((tm, d), lambda j, i: (i, 0)), bspec, bspec, bspec, bspec,
                  *wg_specs,
                  pl.BlockSpec((1, N_BRANCH, BRANCH_W, tn), lambda j, i: (layer, 0, 0, j))],
        out_specs=pl.BlockSpec((tm, tn), lambda j, i: (i, j)),
        out_shape=jax.ShapeDtypeStruct((m, d), BF16),
        scratch_shapes=[pltpu.VMEM((N_BRANCH, d, tn), BF16), pltpu.VMEM((N_BRANCH, BRANCH_W, tn), BF16)],
        compiler_params=_cparams(("arbitrary", "arbitrary")),
        name="merge",
    )(u, *branches, w_gate, w_gate, w_gate, w_gate, w_branch)


def _router_kernel(u_ref, r_ref, o_ref):
    logits = _dot(u_ref[...], r_ref[0].astype(BF16))
    lane = lax.broadcasted_iota(jnp.int32, logits.shape, 1)
    lg = jnp.where(lane < N_EXPERTS, logits, -jnp.inf)
    m1 = jnp.max(lg, axis=1, keepdims=True)
    i1 = jnp.min(jnp.where(lg == m1, lane, LANES), axis=1, keepdims=True)
    lg2 = jnp.where(lane == i1, -jnp.inf, lg)
    m2 = jnp.max(lg2, axis=1, keepdims=True)
    i2 = jnp.min(jnp.where(lg2 == m2, lane, LANES), axis=1, keepdims=True)
    e2 = jnp.exp(m2 - m1)
    w1 = 1.0 / (1.0 + e2)
    w2 = e2 / (1.0 + e2)
    o_ref[...] = jnp.where(lane == 0, i1.astype(F32),
                           jnp.where(lane == 1, i2.astype(F32),
                                     jnp.where(lane == 2, w1, jnp.where(lane == 3, w2, 0.0))))


def _router(u, router_padded, layer, *, tm):
    m, d = u.shape
    return pl.pallas_call(
        _router_kernel,
        grid=(m // tm,),
        in_specs=[pl.BlockSpec((tm, d), lambda i: (i, 0)),
                  pl.BlockSpec((1, d, LANES), lambda i: (layer, 0, 0))],
        out_specs=pl.BlockSpec((tm, LANES), lambda i: (i, 0)),
        out_shape=jax.ShapeDtypeStruct((m, LANES), F32),
        compiler_params=_cparams(("arbitrary",)),
        name="router",
    )(u, router_padded)


MOE_GROUP_TILE = 512
MOE_ROW_TILE = 256
MOE_TOK_CHUNK = 256


def _moe_plan(ridx, rw, s_len):
    gm, tm, ct, n_e = MOE_GROUP_TILE, MOE_ROW_TILE, MOE_TOK_CHUNK, N_EXPERTS
    i32 = jnp.int32
    e_a = ridx.reshape(-1).astype(i32)
    oh = (e_a[:, None] == jnp.arange(n_e, dtype=i32)[None, :]).astype(i32)
    csum = jnp.cumsum(oh, axis=0)
    rank_a = jnp.sum((csum - oh) * oh, axis=1)
    ntile_e = (csum[-1] + gm - 1) // gm
    tile_end = jnp.cumsum(ntile_e)
    pos_a = jnp.take(tile_end - ntile_e, e_a) * gm + rank_a
    n_rows = 2 * s_len + n_e * gm
    n_tiles = n_rows // tm
    n_chunks = s_len // ct
    row_tok = jnp.full((n_rows,), -1, i32).at[pos_a].set(jnp.arange(2 * s_len, dtype=i32) // 2)
    row_w = jnp.zeros((n_rows,), F32).at[pos_a].set(rw.reshape(-1))
    tile_e = jnp.minimum(jnp.searchsorted(tile_end, jnp.arange(n_rows // gm, dtype=i32), side="right"),
                         n_e - 1).astype(i32)
    rt = row_tok.reshape(n_tiles, tm)
    lo = jnp.min(jnp.where(rt >= 0, rt, s_len - 1), axis=1) // ct
    hi = jnp.maximum(jnp.max(jnp.where(rt >= 0, rt, 0), axis=1) // ct, lo)
    n_i = hi - lo + 1
    end = jnp.cumsum(n_i)
    n_work = n_tiles + n_e * n_chunks
    w = jnp.arange(n_work, dtype=i32)
    wt = jnp.minimum(jnp.searchsorted(end, w, side="right"), n_tiles - 1).astype(i32)
    wc = jnp.clip(jnp.take(lo, wt) + w - jnp.take(end - n_i, wt), 0, n_chunks - 1).astype(i32)
    wa = (w < end[-1]).astype(i32)
    order = jnp.argsort(jnp.where(wa > 0, wc * n_tiles + wt, n_chunks * n_tiles + w))
    vc = jnp.where(wa > 0, wc, n_chunks - 1)[order]
    return dict(row_tok=row_tok, row_w=row_w, tile_e=tile_e, n_tiles=n_tiles, n_work=n_work,
                gather=(wt, wc, wa), combine=(vc, wt[order], wa[order]))


def _moe_gather_kernel(wt_ref, wc_ref, wa_ref, tok_ref, u_ref, o_ref):
    w = pl.program_id(0)

    @pl.when((w == 0) | (wt_ref[jnp.maximum(w - 1, 0)] != wt_ref[w]))
    def _():
        o_ref[...] = jnp.zeros_like(o_ref)

    @pl.when(wa_ref[w] > 0)
    def _():
        ct = u_ref.shape[0]
        cols = wc_ref[w] * ct + lax.broadcasted_iota(jnp.int32, (1, ct), 1)
        onehot = jnp.where(tok_ref[...] == cols, 1.0, 0.0).astype(BF16)
        o_ref[...] += _dot(onehot, u_ref[...]).astype(o_ref.dtype)


def _moe_gather(u, plan):
    s_len, d = u.shape
    tm, ct = MOE_ROW_TILE, MOE_TOK_CHUNK
    n_rows = plan["row_tok"].shape[0]
    return pl.pallas_call(
        _moe_gather_kernel,
        grid_spec=pltpu.PrefetchScalarGridSpec(
            num_scalar_prefetch=3, grid=(plan["n_work"],),
            in_specs=[pl.BlockSpec((tm, 1), lambda w, wt, wc, wa: (wt[w], 0)),
                      pl.BlockSpec((ct, d), lambda w, wt, wc, wa: (wc[w], 0))],
            out_specs=pl.BlockSpec((tm, d), lambda w, wt, wc, wa: (wt[w], 0))),
        out_shape=jax.ShapeDtypeStruct((n_rows, d), BF16),
        compiler_params=_cparams(("arbitrary",)),
        name="moe_gather",
    )(*plan["gather"], plan["row_tok"].reshape(n_rows, 1), u)


def _moe_combine_kernel(vc_ref, vt_ref, va_ref, tok_ref, y0_ref, y1_ref, y2_ref, o_ref):
    w = pl.program_id(0)
    chunk = vc_ref[w]

    @pl.when((w == 0) | (vc_ref[jnp.maximum(w - 1, 0)] != chunk))
    def _():
        o_ref[...] = jnp.zeros_like(o_ref)

    @pl.when(va_ref[w] > 0)
    def _():
        ct = o_ref.shape[0]
        rows = chunk * ct + lax.broadcasted_iota(jnp.int32, (ct, 1), 0)
        onehot_t = jnp.where(rows == tok_ref[0], 1.0, 0.0).astype(BF16)
        o_ref[...] += (_dot(onehot_t, y0_ref[...]) + _dot(onehot_t, y1_ref[...])
                       + _dot(onehot_t, y2_ref[...]))


def _moe_combine(ys3, plan, s_len):
    n_rows, d = ys3[0].shape
    tm, ct = MOE_ROW_TILE, MOE_TOK_CHUNK
    yspec = pl.BlockSpec((tm, d), lambda w, vc, vt, va: (vt[w], 0))
    return pl.pallas_call(
        _moe_combine_kernel,
        grid_spec=pltpu.PrefetchScalarGridSpec(
            num_scalar_prefetch=3, grid=(plan["n_work"],),
            in_specs=[pl.BlockSpec((1, 1, tm), lambda w, vc, vt, va: (vt[w], 0, 0)), yspec, yspec, yspec],
            out_specs=pl.BlockSpec((ct, d), lambda w, vc, vt, va: (vc[w], 0))),
        out_shape=jax.ShapeDtypeStruct((s_len, d), F32),
        compiler_params=_cparams(("arbitrary",)),
        name="moe_combine",
    )(*plan["combine"], plan["row_tok"].reshape(plan["n_tiles"], 1, tm), *ys3)


def _gmm_kernel(te_ref, *refs, swiglu):
    it = iter(refs)
    a_ref = next(it)
    w_refs = [next(it), next(it)] if swiglu else [next(it)]
    rw_ref = next(it) if swiglu else None
    o_refs = [next(it)] if swiglu else [next(it), next(it), next(it)]
    wb_refs = [next(it) for _ in w_refs]
    i = pl.program_id(1)

    @pl.when((i == 0) | (te_ref[i] != te_ref[jnp.maximum(i - 1, 0)]))
    def _():
        for w_ref, wb_ref in zip(w_refs, wb_refs):
            wb_ref[...] = w_ref[0, 0].astype(BF16)

    a = a_ref[...]
    if swiglu:
        h = _silu(_dot(a, wb_refs[0][...])) * _dot(a, wb_refs[1][...]) * rw_ref[...]
        o_refs[0][...] = h.astype(BF16)
    else:
        y = _dot(a, wb_refs[0][...])
        hi = y.astype(BF16)
        r1 = y - hi.astype(F32)
        mid = r1.astype(BF16)
        o_refs[0][...] = hi
        o_refs[1][...] = mid
        o_refs[2][...] = (r1 - mid.astype(F32)).astype(BF16)


def _gmm(a, ws, layer, plan, *, tn, row_w=None, name="gmm"):
    n_rows, k = a.shape
    n = ws[0].shape[3]
    tm = MOE_GROUP_TILE
    swiglu = len(ws) == 2
    wspec = pl.BlockSpec((1, 1, k, tn), lambda j, i, te: (layer, te[i], 0, j))
    in_specs = [pl.BlockSpec((tm, k), lambda j, i, te: (i, 0))] + [wspec] * len(ws)
    args = [a, *ws]
    if swiglu:
        in_specs.append(pl.BlockSpec((tm, 1), lambda j, i, te: (i, 0)))
        args.append(row_w.reshape(n_rows, 1))
    ospec = pl.BlockSpec((tm, tn), lambda j, i, te: (i, j))
    oshape = jax.ShapeDtypeStruct((n_rows, n), BF16)
    return pl.pallas_call(
        functools.partial(_gmm_kernel, swiglu=swiglu),
        grid_spec=pltpu.PrefetchScalarGridSpec(
            num_scalar_prefetch=1, grid=(n // tn, n_rows // tm),
            in_specs=in_specs,
            out_specs=ospec if swiglu else [ospec] * 3,
            scratch_shapes=[pltpu.VMEM((k, tn), BF16) for _ in ws]),
        out_shape=oshape if swiglu else [oshape] * 3,
        compiler_params=_cparams(("arbitrary", "arbitrary")),
        name=name,
    )(plan["tile_e"], *args)


def _permute_w_in(w):
    cols = []
    for nm in _NEW_ORDER:
        if nm.startswith("pad"):
            cols.append(jnp.zeros(w.shape[:2] + (int(nm[3:]),), w.dtype))
        else:
            o, n = _ORIG[nm]
            cols.append(w[:, :, o:o + n])
    out = jnp.concatenate(cols, axis=2)
    assert out.shape[2] == PROJ_W
    return out


def _nsa_cmp_inputs(proj):
    s_len = proj.shape[0]
    n_cmp = (s_len - NSA_CMP_LEN) // NSA_CMP_STRIDE + 1
    ncp = s_len // NSA_CMP_STRIDE
    xs = []
    for jj in range(2):
        for g in range(NSA_GROUPS):
            c0 = COL_DKV + jj * 128 + g * HEAD_DIM
            r = proj[:, c0:c0 + HEAD_DIM].reshape(ncp, NSA_CMP_STRIDE * HEAD_DIM)
            x = jnp.concatenate([r[:-1], r[1:]], axis=1)
            xs.append(jnp.pad(x, ((0, ncp - n_cmp), (0, 0))))
    return jnp.stack(xs)


def _token_mixers(u, layer, p, cfg):
    s_len = u.shape[0]
    tm = cfg["tm"]
    proj = _mm(u, p["w_in"], layer, tm=tm, tn=512, name="in_proj")
    kv_b = _mm(proj, p["dsa_w_ukv"], layer, tm=tm, tn=512, a_blk=BLK128["b_kv"], k=DSA_KV_RANK,
               prologue="rms", gain=p["dsa_kv_norm_g"][layer], name="dsa_kv")
    kv = _kv_pack(proj, kv_b, tm=cfg["tm_ln"])

    lambda_init = 0.8 - 0.6 * math.exp(-0.3 * layer)
    sl_a = _alibi(DA_HEADS)
    units_a = [(2 * h + mp, h * 256, 256, 0,
                ((h * 128 + mp * 64, sl_a[h], ((2 * h + mp) * 128, 128), 0),))
               for h in range(DA_HEADS) for mp in range(2)]
    o_a2 = _flash(proj, kv["a_k"], kv["a_v"], units=units_a,
                  q_spec=(512, BLK512["a_q"]), out_w=1024, tq=cfg["tq"], tk=cfg["tk"], name="diff_attn")
    o_a = _diff_final(o_a2, p["diff_lambda"], layer, p["diff_subln_g"][layer], lambda_init, tm=tm)

    topk = min(DSA_TOPK_MAX, s_len // 4)
    mask_b = _dsa_select(proj, kv["b_ik"], topk=topk)
    sl8 = _alibi(8)
    units_b = [(h, h * 128, 128, 0, ((h * 64, sl8[h], (h * 64, 64), 0),)) for h in range(DSA_HEADS)]
    o_b = _flash(proj, kv["b_k"], kv["b_v"],
                 units=units_b, q_spec=(512, BLK512["b_q"]), out_w=512, tq=cfg["tq"], tk=cfg["tk"],
                 mask=mask_b.reshape(1, s_len, s_len), name="dsa_attn")

    def gqa_units(masked):
        return [(g, g * 128, 128, g if masked else 0,
                 tuple(((g * 4 + r) * 64, sl8[g * 4 + r], ((g * 4 + r) * 64, 64), g * 4 + r)
                       for r in range(4)))
                for g in range(2)]

    def gqa_kv(k_name, v_name):
        return kv[k_name], kv[v_name]

    sinks = jnp.pad(p["swa_sinks"][layer].reshape(1, SWA_HEADS), ((0, 0), (0, LANES - SWA_HEADS)))
    o_c = _flash(proj, *gqa_kv("c_k", "c_v"), units=gqa_units(False), q_spec=(512, BLK512["c_q"]),
                 out_w=512, tq=cfg["tb"], tk=cfg["tb"], window=SWA_WINDOW, sinks=sinks, name="swa_attn")

    kv_cmp = _nsa_compress(_nsa_cmp_inputs(proj), p["nsa_cmp_pos"], p["nsa_cmp_w1"], p["nsa_cmp_w2"],
                           layer)
    o_cmp, mask_d = _nsa_cmp(proj, kv_cmp)
    o_slc = _flash(proj, *gqa_kv("d_ks", "d_vs"), units=gqa_units(True), q_spec=(512, BLK512["d_q"]),
                   out_w=512, tq=cfg["tq"], tk=cfg["tk"], mask=mask_d, name="nsa_slc_attn")
    o_win = _flash(proj, *gqa_kv("d_kw", "d_vw"), units=gqa_units(False), q_spec=(512, BLK512["d_q"]),
                   out_w=512, tq=cfg["tw"], tk=cfg["tw"], window=NSA_WINDOW, name="nsa_win_attn")
    o_d = _nsa_combine(proj, o_cmp, o_slc, o_win, tm=tm)

    merged = _merge(u, (o_a, o_b, o_c, o_d), p["w_gate"], p["w_branch"], layer,
                    tm=cfg["tm_merge"], tn=256)
    return _mm(merged, p["w_o"], layer, tm=tm, tn=512, name="out_proj")


def _config(s_len):
    return dict(tm=min(1024, s_len), tm_merge=min(512, s_len), tm_ln=min(512, s_len),
                tq=min(256, s_len), tk=min(1024, s_len), tb=min(256, s_len), tw=min(512, s_len))


def kernel(x, c, cond_w, cond_b, w_in, diff_lambda, diff_subln_g, dsa_kv_norm_g, dsa_w_uk, dsa_w_uv,
           swa_sinks, nsa_cmp_pos, nsa_cmp_w1, nsa_cmp_w2, w_branch, w_gate, w_o,
           ln1_g, ln1_b, ln2_g, ln2_b, ffn_w_gate, ffn_w_up, ffn_w_down,
           moe_router, moe_w_gate, moe_w_up, moe_w_down):
    bsz, s_len, d = x.shape
    assert bsz == 1 and d == D_MODEL
    depth = cond_w.shape[0]
    cfg = _config(s_len)
    xs = x.reshape(s_len, d)
    c8 = jnp.broadcast_to(c.reshape(1, d), (8, d))
    p = dict(w_in=_permute_w_in(w_in), diff_lambda=diff_lambda, diff_subln_g=diff_subln_g,
             dsa_kv_norm_g=dsa_kv_norm_g, dsa_w_ukv=jnp.concatenate([dsa_w_uk, dsa_w_uv], axis=2),
             swa_sinks=swa_sinks,
             nsa_cmp_pos=nsa_cmp_pos.reshape(depth, 2, 1, NSA_CMP_LEN * HEAD_DIM),
             nsa_cmp_w1=nsa_cmp_w1, nsa_cmp_w2=nsa_cmp_w2, w_branch=w_branch, w_gate=w_gate, w_o=w_o)
    router_p = jnp.pad(moe_router, ((0, 0), (0, 0), (0, LANES - N_EXPERTS)))
    mods = [_mm(c8, cond_w, l, tm=8, tn=512, prologue="silu", bias=cond_b[l], name="cond")[0:1]
            for l in range(depth)]
    u = _modulate(xs, mods[0], 1, 0, tm=cfg["tm_ln"])
    for l in range(depth):
        y = _token_mixers(u, l, p, cfg)
        xs, u = _resid_ln(xs, y, mods[l], 2, ln1_g[l], ln1_b[l], mods[l], 4, 3, tm=cfg["tm_ln"])
        jx = l // 2
        if l % 2 == 0:
            hdn = _swiglu_up(u, ffn_w_gate, ffn_w_up, jx, tm=cfg["tm"], tn=512, name="ffn_up")
            y = _mmk(hdn, ffn_w_down, jx, tm=cfg["tm"], tn=d, tk=512, name="ffn_down")
        else:
            rt = _router(u, router_p, jx, tm=cfg["tm"])
            plan = _moe_plan(rt[:, 0:2], rt[:, 2:4], s_len)
            hdn = _gmm(_moe_gather(u, plan), (moe_w_gate, moe_w_up), jx, plan, tn=512,
                       row_w=plan["row_w"], name="moe_up")
            y = _moe_combine(_gmm(hdn, (moe_w_down,), jx, plan, tn=512, name="moe_down"), plan, s_len)
        nxt = min(l + 1, depth - 1)
        xs, u = _resid_ln(xs, y, mods[l], 5, ln2_g[l], ln2_b[l], mods[nxt], 1, 0, tm=cfg["tm_ln"])
    return xs.reshape(bsz, s_len, d)
```

```python
import functools
import math

import numpy as np
import jax
import jax.numpy as jnp
from jax import lax
from jax.experimental import pallas as pl
from jax.experimental.pallas import tpu as pltpu

F32 = jnp.float32
BF16 = jnp.bfloat16
NEG = -1e30

D_MODEL = 2048
DEPTH = 4
HEAD_DIM = 64
DA_HEADS = 4
DSA_HEADS = 8
DSA_KV_RANK = 128
IDX_HEADS = 8
DSA_TOPK_MAX = 256
SWA_HEADS = 8
SWA_WINDOW = 128
NSA_HEADS = 8
NSA_GROUPS = 2
NSA_CMP_LEN = 32
NSA_CMP_STRIDE = 16
NSA_CMP_HID = 256
NSA_SLC_LEN = 64
NSA_TOPN = 16
NSA_WINDOW = 512
NSA_FORCE = 1e9
N_BRANCH = 4
BRANCH_W = 512
N_EXPERTS = 8
ALPHA = (2.0 * DEPTH) ** 0.25

VMEM_LIMIT_BYTES = 56 * 1024 * 1024
LANES = 128

_ORIG = dict(a_q=(0, 512), a_k=(512, 512), a_v=(1024, 512), b_q=(1536, 512), b_kv=(2048, 128),
             b_iq=(2176, 512), b_ik=(2688, 64), b_iw=(2752, 8), c_q=(2760, 512), c_k=(3272, 128),
             c_v=(3400, 128), d_q=(3528, 512), d_kv=(4040, 768), d_g=(4808, 24))
_NEW_ORDER = ("a_q", "a_k", "a_v", "b_q", "b_iq", "c_q", "d_q", "b_kv", "c_k", "c_v", "d_kv",
              "b_ik", "pad64", "b_iw", "d_g", "pad96", "pad128")
PROJ_W = 5120
BLK512 = dict(a_q=0, a_k=1, a_v=2, b_q=3, b_iq=4, c_q=5, d_q=6)
BLK128 = dict(b_kv=28, c_k=29, c_v=30, d_kc=31, d_vc=32, d_ks=33, d_vs=34, d_kw=35, d_vw=36,
              b_ik=37, small=38)
COL_DKV = 3968


def _cparams(sem):
    return pltpu.CompilerParams(dimension_semantics=sem, vmem_limit_bytes=VMEM_LIMIT_BYTES)


def _sigmoid(x):
    return 1.0 / (1.0 + jnp.exp(-x))


def _silu(x):
    return x * _sigmoid(x)


def _alibi(n_heads):
    return [2.0 ** (-8.0 * (h + 1) / n_heads) for h in range(n_heads)]


def _dot(a, b):
    return jnp.dot(a, b, preferred_element_type=F32)


def _dot_nt(a, b):
    return lax.dot_general(a, b, (((1,), (1,)), ((), ())), preferred_element_type=F32)


def _mm_kernel(*refs, prologue, has_bias, eps):
    it = iter(refs)
    a_ref = next(it)
    g_ref = next(it) if prologue == "rms" else None
    w_ref = next(it)
    b_ref = next(it) if has_bias else None
    o_ref = next(it)
    wb_ref = next(it)

    @pl.when(pl.program_id(1) == 0)
    def _():
        wb_ref[...] = w_ref[0].astype(BF16)

    a = a_ref[...]
    if prologue == "silu":
        a = _silu(a.astype(F32))
    elif prologue == "rms":
        a = a.astype(F32)
        a = a * lax.rsqrt(jnp.mean(a * a, axis=-1, keepdims=True) + eps) * g_ref[...]
    acc = _dot(a.astype(BF16), wb_ref[...])
    if has_bias:
        acc = acc + b_ref[...]
    o_ref[...] = acc.astype(o_ref.dtype)


def _mm(a, w, layer, *, tm, tn, out_dtype=F32, a_blk=0, k=None, prologue=None, gain=None, bias=None,
        eps=1e-6, name="mm"):
    m = a.shape[0]
    k = a.shape[1] if k is None else k
    n = w.shape[2]
    assert w.shape[1] == k and m % tm == 0 and n % tn == 0
    in_specs = [pl.BlockSpec((tm, k), lambda j, i: (i, a_blk))]
    args = [a]
    if prologue == "rms":
        in_specs.append(pl.BlockSpec((1, k), lambda j, i: (0, 0)))
        args.append(gain.reshape(1, k))
    in_specs.append(pl.BlockSpec((1, k, tn), lambda j, i: (layer, 0, j)))
    args.append(w)
    if bias is not None:
        in_specs.append(pl.BlockSpec((1, tn), lambda j, i: (0, j)))
        args.append(bias.reshape(1, n))
    return pl.pallas_call(
        functools.partial(_mm_kernel, prologue=prologue, has_bias=bias is not None, eps=eps),
        grid=(n // tn, m // tm),
        in_specs=in_specs,
        out_specs=pl.BlockSpec((tm, tn), lambda j, i: (i, j)),
        out_shape=jax.ShapeDtypeStruct((m, n), out_dtype),
        scratch_shapes=[pltpu.VMEM((k, tn), BF16)],
        compiler_params=_cparams(("arbitrary", "arbitrary")),
        name=name,
    )(*args)


def _mmk_kernel(a_ref, w_ref, o_ref, acc_ref, *, nk):
    kk = pl.program_id(2)

    @pl.when(kk == 0)
    def _():
        acc_ref[...] = jnp.zeros_like(acc_ref)

    acc_ref[...] += _dot(a_ref[...], w_ref[0].astype(BF16))

    @pl.when(kk == nk - 1)
    def _():
        o_ref[...] = acc_ref[...]


def _mmk(a, w, layer, *, tm, tn, tk, name="mmk"):
    m, k = a.shape
    n = w.shape[2]
    assert w.shape[1] == k and m % tm == 0 and n % tn == 0 and k % tk == 0
    nk = k // tk
    return pl.pallas_call(
        functools.partial(_mmk_kernel, nk=nk),
        grid=(m // tm, n // tn, nk),
        in_specs=[pl.BlockSpec((tm, tk), lambda i, j, kk: (i, kk)),
                  pl.BlockSpec((1, tk, tn), lambda i, j, kk: (layer, kk, j))],
        out_specs=pl.BlockSpec((tm, tn), lambda i, j, kk: (i, j)),
        out_shape=jax.ShapeDtypeStruct((m, n), F32),
        scratch_shapes=[pltpu.VMEM((tm, tn), F32)],
        compiler_params=_cparams(("arbitrary", "arbitrary", "arbitrary")),
        name=name,
    )(a, w)


def _swiglu_kernel(a_ref, wg_ref, wu_ref, o_ref, wgb_ref, wub_ref):
    @pl.when(pl.program_id(1) == 0)
    def _():
        wgb_ref[...] = wg_ref[0].astype(BF16)
        wub_ref[...] = wu_ref[0].astype(BF16)

    a = a_ref[...]
    o_ref[...] = (_silu(_dot(a, wgb_ref[...])) * _dot(a, wub_ref[...])).astype(o_ref.dtype)


def _swiglu_up(u, wg, wu, layer, *, tm, tn, name="swiglu_up"):
    m, k = u.shape
    f = wg.shape[2]
    assert f % tn == 0 and m % tm == 0
    wspec = pl.BlockSpec((1, k, tn), lambda j, i: (layer, 0, j))
    return pl.pallas_call(
        _swiglu_kernel,
        grid=(f // tn, m // tm),
        in_specs=[pl.BlockSpec((tm, k), lambda j, i: (i, 0)), wspec, wspec],
        out_specs=pl.BlockSpec((tm, tn), lambda j, i: (i, j)),
        out_shape=jax.ShapeDtypeStruct((m, f), BF16),
        scratch_shapes=[pltpu.VMEM((k, tn), BF16), pltpu.VMEM((k, tn), BF16)],
        compiler_params=_cparams(("arbitrary", "arbitrary")),
        name=name,
    )(u, wg, wu)


def _modulate_kernel(x_ref, sc_ref, sh_ref, u_ref):
    u_ref[...] = (x_ref[...] * (1.0 + sc_ref[...]) + sh_ref[...]).astype(u_ref.dtype)


def _modulate(x, mod, sc_blk, sh_blk, *, tm):
    m, d = x.shape
    return pl.pallas_call(
        _modulate_kernel,
        grid=(m // tm,),
        in_specs=[pl.BlockSpec((tm, d), lambda i: (i, 0)),
                  pl.BlockSpec((1, d), lambda i: (0, sc_blk)),
                  pl.BlockSpec((1, d), lambda i: (0, sh_blk))],
        out_specs=pl.BlockSpec((tm, d), lambda i: (i, 0)),
        out_shape=jax.ShapeDtypeStruct((m, d), BF16),
        compiler_params=_cparams(("arbitrary",)),
        name="modulate",
    )(x, mod, mod)


def _resid_ln_kernel(x_ref, y_ref, gate_ref, g_ref, b_ref, sc_ref, sh_ref, xo_ref, u_ref):
    z = ALPHA * x_ref[...] + gate_ref[...] * y_ref[...]
    mu = jnp.mean(z, axis=-1, keepdims=True)
    zc = z - mu
    var = jnp.mean(zc * zc, axis=-1, keepdims=True)
    xn = zc * lax.rsqrt(var + 1e-5) * g_ref[...] + b_ref[...]
    xo_ref[...] = xn
    u_ref[...] = (xn * (1.0 + sc_ref[...]) + sh_ref[...]).astype(u_ref.dtype)


def _resid_ln(x, y, mod, gate_blk, g, b, mod_next, sc_blk, sh_blk, *, tm):
    m, d = x.shape
    row = lambda blk: pl.BlockSpec((1, d), lambda i: (0, blk))
    return pl.pallas_call(
        _resid_ln_kernel,
        grid=(m // tm,),
        in_specs=[pl.BlockSpec((tm, d), lambda i: (i, 0)),
                  pl.BlockSpec((tm, d), lambda i: (i, 0)),
                  row(gate_blk), row(0), row(0), row(sc_blk), row(sh_blk)],
        out_specs=[pl.BlockSpec((tm, d), lambda i: (i, 0)),
                   pl.BlockSpec((tm, d), lambda i: (i, 0))],
        out_shape=[jax.ShapeDtypeStruct((m, d), F32), jax.ShapeDtypeStruct((m, d), BF16)],
        compiler_params=_cparams(("arbitrary",)),
        name="resid_ln",
    )(x, y, mod, g.reshape(1, d), b.reshape(1, d), mod_next, mod_next)


FLASH_ROW_CHUNK = 32
POS_SPLIT = 128


_KV_PACK = (("a_k", "a_k", 0, 2 * DA_HEADS, 64, "k"), ("a_v", "a_v", 0, DA_HEADS, 128, "v"),
            ("b_k", "kv_b", 0, DSA_HEADS, 64, "k"), ("b_v", "kv_b", 512, DSA_HEADS, 64, "v"),
            ("c_k", "c_k", 0, 2, 64, "k"), ("c_v", "c_v", 0, 2, 64, "v"),
            ("d_ks", "d_ks", 0, 2, 64, "k"), ("d_vs", "d_vs", 0, 2, 64, "v"),
            ("d_kw", "d_kw", 0, 2, 64, "k"), ("d_vw", "d_vw", 0, 2, 64, "v"),
            ("b_ik", "b_ik", 0, 1, 128, "cast"))
_KV_SOURCES = ("a_k", "a_v", "c_k", "c_v", "d_ks", "d_vs", "d_kw", "d_vw", "b_ik", "kv_b")


def _kv_pack_kernel(*refs, tm):
    src = dict(zip(_KV_SOURCES, refs[:len(_KV_SOURCES)]))
    outs = refs[len(_KV_SOURCES):]
    i = pl.program_id(0)
    tails = {}
    for w in (64, 128):
        lane = lax.broadcasted_iota(jnp.int32, (tm, w), 1)
        pos = i * tm + lax.broadcasted_iota(jnp.int32, (tm, w), 0)
        tails[("v", w)] = jnp.where(lane == 0, 1.0, 0.0)
        tails[("k", w)] = jnp.where(lane == 0, (pos // POS_SPLIT).astype(F32),
                                    jnp.where(lane == 1, (pos % POS_SPLIT).astype(F32),
                                              jnp.where(lane < 4, 1.0, 0.0)))
    for (_, sname, c0, n_heads, w, kind), o_ref in zip(_KV_PACK, outs):
        if kind == "cast":
            o_ref[...] = src[sname][...].astype(BF16)
            continue
        for h in range(n_heads):
            x = src[sname][:, c0 + h * w:c0 + (h + 1) * w]
            o_ref[:, 2 * h * w:2 * (h + 1) * w] = jnp.concatenate([x, tails[(kind, w)]], axis=1).astype(BF16)


def _kv_pack(proj, kv_b, *, tm):
    s_len = proj.shape[0]
    in_specs, args = [], []
    for sname in _KV_SOURCES:
        if sname == "kv_b":
            in_specs.append(pl.BlockSpec((tm, kv_b.shape[1]), lambda i: (i, 0)))
            args.append(kv_b)
        else:
            wblk, blk = (512, BLK512[sname]) if sname in BLK512 else (128, BLK128[sname])
            in_specs.append(pl.BlockSpec((tm, wblk), functools.partial(lambda i, blk: (i, blk), blk=blk)))
            args.append(proj)
    widths = [(1 if kind == "cast" else 2) * n_heads * w for (_, _, _, n_heads, w, kind) in _KV_PACK]
    outs = pl.pallas_call(
        functools.partial(_kv_pack_kernel, tm=tm),
        grid=(s_len // tm,),
        in_specs=in_specs,
        out_specs=[pl.BlockSpec((tm, wd), lambda i: (i, 0)) for wd in widths],
        out_shape=[jax.ShapeDtypeStruct((s_len, wd), BF16) for wd in widths],
        compiler_params=_cparams(("arbitrary",)),
        name="kv_pack",
    )(*args)
    return {name: o for (name, *_), o in zip(_KV_PACK, outs)}


def _flash_kernel(qi_ref, kb_ref, first_ref, last_ref, *refs, units, tq, tk, window, dense, n_mask,
                  has_sink):
    it = iter(refs)
    q_ref, k_ref, v_ref = next(it), next(it), next(it)
    mask_ref = next(it) if n_mask else None
    sink_ref = next(it) if has_sink else None
    o_ref = next(it)
    q_scr, m_scr, acc_scr, bias_scr = (next(it) for _ in range(4))
    dv = acc_scr.shape[2]
    lcol = dv // 2

    w = pl.program_id(0)
    qi = qi_ref[w]
    kb = kb_ref[w]
    rows = q_scr.shape[1]
    rb = FLASH_ROW_CHUNK
    kw = 2 * HEAD_DIM

    @pl.when(first_ref[w] > 0)
    def _init():
        lane = lax.broadcasted_iota(jnp.int32, (tq, HEAD_DIM), 1)
        qpos = qi * tq + lax.broadcasted_iota(jnp.int32, (tq, HEAD_DIM), 0)
        qhi = (qpos // POS_SPLIT).astype(F32)
        qlo = (qpos % POS_SPLIT).astype(F32)
        for ui, (_, _, _, _, hds) in enumerate(units):
            for r, (qo, slope, _, sink_idx) in enumerate(hds):
                rsl = slice(r * tq, (r + 1) * tq)
                tail = jnp.where(lane == 0, POS_SPLIT * slope,
                                 jnp.where(lane == 1, slope,
                                           jnp.where(lane == 2, -POS_SPLIT * slope * qhi,
                                                     jnp.where(lane == 3, -slope * qlo, 0.0))))
                qs = q_ref[:, qo:qo + HEAD_DIM] * HEAD_DIM ** -0.5
                q_scr[ui, rsl] = jnp.concatenate([qs, tail], axis=1).astype(BF16)
                if has_sink:
                    m_scr[ui, rsl] = jnp.broadcast_to(sink_ref[:, sink_idx:sink_idx + 1], (tq, 1))
                else:
                    m_scr[ui, rsl] = jnp.full((tq, 1), NEG, F32)
            alane = lax.broadcasted_iota(jnp.int32, acc_scr.shape[1:], 1)
            acc_scr[ui] = jnp.where(alane == lcol, 1.0 if has_sink else 0.0, 0.0)

    def scores(ui):
        ku = units[ui][0]
        return _dot_nt(q_scr[ui], k_ref[:, ku * kw:(ku + 1) * kw])

    def step(masked):
        if masked:
            qpos = qi * tq + lax.broadcasted_iota(jnp.int32, (tq, tk), 0)
            kpos = kb * tk + lax.broadcasted_iota(jnp.int32, (tq, tk), 1)
            dist = qpos - kpos
            valid = dist >= 0
            if not dense:
                valid = valid & (dist < window)
            if n_mask:
                for g in range(n_mask):
                    bias_scr[g] = jnp.where(valid, mask_ref[g].astype(F32), NEG)
            else:
                bias_scr[0] = jnp.where(valid, 0.0, NEG)

        def chunk(s, mg, c):
            r0 = c * rb
            sc = s[r0:r0 + rb]
            if masked:
                rw = r0 % tq
                sc = sc + bias_scr[mg, rw:rw + rb]
            return sc

        s_next = scores(0)
        for ui, (_, vo, _, mg, _) in enumerate(units):
            s = s_next
            if ui + 1 < len(units):
                s_next = scores(ui + 1)
            nchunk = rows // rb
            m_old = m_scr[ui]
            m_cur = jnp.concatenate([jnp.max(chunk(s, mg, c), axis=1, keepdims=True) for c in range(nchunk)],
                                    axis=0)
            m_new = jnp.maximum(m_old, m_cur)
            alpha = jnp.exp(m_old - m_new)
            m_scr[ui] = m_new
            p_all = jnp.concatenate(
                [jnp.exp(chunk(s, mg, c) - m_new[c * rb:(c + 1) * rb]).astype(BF16) for c in range(nchunk)],
                axis=0)
            acc_scr[ui] = alpha * acc_scr[ui] + _dot(p_all, v_ref[:, vo:vo + dv])

    if dense and not n_mask:
        interior = kb * tk + tk - 1 <= qi * tq
        pl.when(interior)(lambda: step(False))
        pl.when(jnp.logical_not(interior))(lambda: step(True))
    else:
        step(True)

    @pl.when(last_ref[w] > 0)
    def _fin():
        for ui, (_, _, _, _, hds) in enumerate(units):
            for r, (_, _, (oo, ow), _) in enumerate(hds):
                rsl = slice(r * tq, (r + 1) * tq)
                acc = acc_scr[ui, rsl]
                o_ref[:, oo:oo + ow] = acc[:, 0:ow] / acc[:, lcol:lcol + 1]


def _flash(q_arr, k_arr, v_arr, *, units, q_spec, out_w, tq, tk, window=None,
           mask=None, sinks=None, name="flash"):
    s_len = q_arr.shape[0]
    dense = window is None
    if not dense:
        assert tq == tk
    n_mask = 0 if mask is None else mask.shape[0]
    dv = units[0][2]
    nu = len(units)
    rows = len(units[0][4]) * tq
    assert all(len(un[4]) * tq == rows and un[2] == dv for un in units) and rows % FLASH_ROW_CHUNK == 0

    pairs = []
    for qi in range(s_len // tq):
        last_kb = (qi * tq + tq - 1) // tk
        first_kb = 0 if dense else max(qi - (-(-(window - 1) // tk)), 0)
        pairs += [(qi, kb, int(kb == first_kb), int(kb == last_kb)) for kb in range(first_kb, last_kb + 1)]
    tables = [jnp.asarray(np.array(col, np.int32)) for col in zip(*pairs)]

    in_specs = [pl.BlockSpec((tq, q_spec[0]), lambda w, qi, kb, fi, la: (qi[w], q_spec[1])),
                pl.BlockSpec((tk, k_arr.shape[1]), lambda w, qi, kb, fi, la: (kb[w], 0)),
                pl.BlockSpec((tk, v_arr.shape[1]), lambda w, qi, kb, fi, la: (kb[w], 0))]
    args = [q_arr, k_arr, v_arr]
    if n_mask:
        in_specs.append(pl.BlockSpec((n_mask, tq, tk), lambda w, qi, kb, fi, la: (0, qi[w], kb[w])))
        args.append(mask)
    if sinks is not None:
        in_specs.append(pl.BlockSpec((1, LANES), lambda w, qi, kb, fi, la: (0, 0)))
        args.append(sinks)
    return pl.pallas_call(
        functools.partial(_flash_kernel, units=tuple(units), tq=tq, tk=tk, window=window, dense=dense,
                          n_mask=n_mask, has_sink=sinks is not None),
        grid_spec=pltpu.PrefetchScalarGridSpec(
            num_scalar_prefetch=4, grid=(len(pairs),),
            in_specs=in_specs,
            out_specs=pl.BlockSpec((tq, out_w), lambda w, qi, kb, fi, la: (qi[w], 0)),
            scratch_shapes=[pltpu.VMEM((nu, rows, 2 * HEAD_DIM), BF16), pltpu.VMEM((nu, rows, 1), F32),
                            pltpu.VMEM((nu, rows, dv), F32), pltpu.VMEM((max(n_mask, 1), tq, tk), F32)]),
        out_shape=jax.ShapeDtypeStruct((s_len, out_w), F32),
        compiler_params=_cparams(("arbitrary",)),
        name=name,
    )(*tables, *args)


def _diff_final_kernel(o_ref, lam_ref, g_ref, out_ref, *, lambda_init):
    lf = lam_ref[0]
    lam = (jnp.exp(jnp.sum(lf[0:1] * lf[1:2])) - jnp.exp(jnp.sum(lf[2:3] * lf[3:4])) + lambda_init)
    w = 2 * HEAD_DIM
    for h in range(DA_HEADS):
        o = o_ref[:, (2 * h) * w:(2 * h + 1) * w] - lam * o_ref[:, (2 * h + 1) * w:(2 * h + 2) * w]
        o = o * lax.rsqrt(jnp.mean(o * o, axis=-1, keepdims=True) + 1e-6) * g_ref[...]
        out_ref[:, h * w:(h + 1) * w] = o * (1.0 - lambda_init)


def _diff_final(o, diff_lambda, layer, subln_g, lambda_init, *, tm):
    m = o.shape[0]
    w = 2 * HEAD_DIM
    return pl.pallas_call(
        functools.partial(_diff_final_kernel, lambda_init=lambda_init),
        grid=(m // tm,),
        in_specs=[pl.BlockSpec((tm, 2 * DA_HEADS * w), lambda i: (i, 0)),
                  pl.BlockSpec((1, 4, HEAD_DIM), lambda i: (layer, 0, 0)),
                  pl.BlockSpec((1, w), lambda i: (0, 0))],
        out_specs=pl.BlockSpec((tm, DA_HEADS * w), lambda i: (i, 0)),
        out_shape=jax.ShapeDtypeStruct((m, DA_HEADS * w), F32),
        compiler_params=_cparams(("arbitrary",)),
        name="diff_final",
    )(o, diff_lambda, subln_g.reshape(1, w))


def _f32_key_const(x):
    b = int(np.array(x, np.float32).view(np.int32))
    return b ^ ((b >> 31) & 0x7FFFFFFF)


I16_MIN = -(2 ** 15)


def _dsa_select_kernel(qi_ref, w_ref, kidx_ref, mask_ref, key_scr, half_scr, j_scr, *, tq, ch, nch, topk,
                       s_len):
    i = pl.program_id(0)
    q0 = i * tq
    n_need = (q0 + tq + ch - 1) // ch
    qpos = q0 + lax.broadcasted_iota(jnp.int32, (tq, 1), 0)
    lane = lax.broadcasted_iota(jnp.int32, (1, ch), 1)
    w = w_ref[:, 0:IDX_HEADS]
    q_all = jnp.concatenate([qi_ref[:, h * HEAD_DIM:(h + 1) * HEAD_DIM] for h in range(IDX_HEADS)],
                            axis=0).astype(BF16)

    def score_chunk(c, carry):
        kc = kidx_ref[pl.ds(pl.multiple_of(c * ch, ch), ch), 0:HEAD_DIM].astype(BF16)
        lg = _dot_nt(q_all, kc)
        acc = jnp.zeros((tq, ch), F32)
        for h in range(IDX_HEADS):
            acc = acc + w[:, h:h + 1] * jnp.maximum(lg[h * tq:(h + 1) * tq], 0.0)
        acc = jnp.where(c * ch + lane <= qpos, acc, NEG) + 0.0
        bits = pltpu.bitcast(acc, jnp.int32)
        key = bits ^ ((bits >> 31) & 0x7FFFFFFF)
        key_scr[c] = key
        half_scr[c] = (key >> 16).astype(jnp.int16)
        return carry

    lax.fori_loop(0, n_need, score_chunk, 0)

    def count16(cand, strict):
        cand16 = jnp.broadcast_to(cand, (tq, LANES)).astype(jnp.int16)
        one, zero = jnp.int16(1), jnp.int16(0)

        def body(c, acc):
            blk = half_scr[c]
            for t in range(ch // LANES):
                tile = blk[:, t * LANES:(t + 1) * LANES]
                acc = acc + jnp.where(tile > cand16 if strict else tile >= cand16, one, zero)
            return acc
        acc = lax.fori_loop(0, n_need, body, jnp.zeros((tq, LANES), jnp.int16))
        return jnp.sum(acc.astype(jnp.int32), axis=1, keepdims=True)

    def search16(need_cnt):
        def bit_step(b, t):
            cand = t + jnp.left_shift(jnp.int32(1), 15 - b)
            return jnp.where(count16(cand, False) >= need_cnt, cand, t)
        return lax.fori_loop(0, 16, bit_step, jnp.full((tq, 1), I16_MIN, jnp.int32))

    t_hi = search16(topk)
    need_lo = topk - count16(t_hi, True)

    def low_chunk(c, carry):
        key = key_scr[c]
        low = (key & 0xFFFF) + I16_MIN
        half_scr[c] = jnp.where((key >> 16) == t_hi, low, I16_MIN).astype(jnp.int16)
        return carry

    lax.fori_loop(0, n_need, low_chunk, 0)
    t_lo = search16(need_lo)
    thr = jnp.left_shift(t_hi, 16) + (t_lo - I16_MIN)

    def count(pred):
        def body(c, acc):
            m = jnp.where(pred(key_scr[c], c), 1, 0)
            part = m[:, 0:LANES]
            for t in range(1, ch // LANES):
                part = part + m[:, t * LANES:(t + 1) * LANES]
            return acc + part
        acc = lax.fori_loop(0, n_need, body, jnp.zeros((tq, LANES), jnp.int32))
        return jnp.sum(acc, axis=1, keepdims=True)

    cnt_gt = count(lambda blk, c: blk > thr)
    cnt_ge = count(lambda blk, c: blk >= thr)
    need = topk - cnt_gt
    tie_rows = (cnt_ge > topk) & (thr > _f32_key_const(NEG))
    j_scr[...] = jnp.full((tq, 1), s_len, jnp.int32)
    any_tie = jnp.max(jnp.where(tie_rows, 1, 0)) > 0

    @pl.when(any_tie)
    def _ties():
        def tie_chunk(c, carry):
            half_scr[c] = jnp.where(key_scr[c] == thr, -1 - (c * ch + lane), I16_MIN).astype(jnp.int16)
            return carry

        lax.fori_loop(0, n_need, tie_chunk, 0)
        j_scr[...] = jnp.where(tie_rows, -1 - search16(need), s_len)

    jv = j_scr[...]
    for c in range(nch):
        @pl.when((c < n_need) & any_tie)
        def _w():
            key = key_scr[c]
            sel = (key > thr) | ((key == thr) & (c * ch + lane <= jv))
            mask_ref[:, c * ch:(c + 1) * ch] = jnp.where(sel, 0.0, NEG).astype(mask_ref.dtype)

        @pl.when((c < n_need) & jnp.logical_not(any_tie))
        def _wf():
            mask_ref[:, c * ch:(c + 1) * ch] = jnp.where(key_scr[c] >= thr, 0.0, NEG).astype(mask_ref.dtype)

        @pl.when(c >= n_need)
        def _z():
            mask_ref[:, c * ch:(c + 1) * ch] = jnp.full((tq, ch), NEG, mask_ref.dtype)


def _dsa_select(proj, kidx, *, topk, tq=128):
    s_len = proj.shape[0]
    assert s_len < -I16_MIN
    ch = min(1024, s_len)
    nch = s_len // ch
    return pl.pallas_call(
        functools.partial(_dsa_select_kernel, tq=tq, ch=ch, nch=nch, topk=topk, s_len=s_len),
        grid=(s_len // tq,),
        in_specs=[pl.BlockSpec((tq, 512), lambda i: (i, BLK512["b_iq"])),
                  pl.BlockSpec((tq, LANES), lambda i: (i, BLK128["small"])),
                  pl.BlockSpec((s_len, LANES), lambda i: (0, 0))],
        out_specs=pl.BlockSpec((tq, s_len), lambda i: (i, 0)),
        out_shape=jax.ShapeDtypeStruct((s_len, s_len), BF16),
        scratch_shapes=[pltpu.VMEM((nch, tq, ch), jnp.int32), pltpu.VMEM((nch, tq, ch), jnp.int16),
                        pltpu.VMEM((tq, 1), jnp.int32)],
        compiler_params=_cparams(("arbitrary",)),
        name="dsa_select",
    )(proj, proj, kidx)


def _nsa_compress_kernel(x_ref, pos_ref, w1_ref, w2_ref, o_ref):
    x = (x_ref[0] + pos_ref[0, 0]).astype(BF16)
    hdn = _silu(_dot(x, w1_ref[0, 0].astype(BF16)))
    o_ref[0] = _dot(hdn.astype(BF16), w2_ref[0, 0].astype(BF16))


def _nsa_compress(xc, pos, w1, w2, layer):
    _, ncp, kdim = xc.shape
    return pl.pallas_call(
        _nsa_compress_kernel,
        grid=(4,),
        in_specs=[pl.BlockSpec((1, ncp, kdim), lambda i: (i, 0, 0)),
                  pl.BlockSpec((1, 1, 1, kdim), lambda i: (layer, i // 2, 0, 0)),
                  pl.BlockSpec((1, 1, kdim, NSA_CMP_HID), lambda i: (layer, i // 2, 0, 0)),
                  pl.BlockSpec((1, 1, NSA_CMP_HID, HEAD_DIM), lambda i: (layer, i // 2, 0, 0))],
        out_specs=pl.BlockSpec((1, ncp, HEAD_DIM), lambda i: (i, 0, 0)),
        out_shape=jax.ShapeDtypeStruct((4, ncp, HEAD_DIM), F32),
        compiler_params=_cparams(("arbitrary",)),
        name="nsa_compress",
    )(xc, pos, w1, w2)


def _nsa_cmp_kernel(q_ref, kv_ref, ov_ref, ex_ref, o_ref, mask_ref, *, tq, ncp, n_slc, topn, ch, nch):
    i = pl.program_id(0)
    q0 = i * tq
    n_need = (q0 + tq + ch - 1) // ch
    rpg = NSA_HEADS // NSA_GROUPS
    slopes = _alibi(NSA_HEADS)
    scale = HEAD_DIM ** -0.5
    qpos_c = q0 + lax.broadcasted_iota(jnp.int32, (tq, ncp), 0)
    cend = lax.broadcasted_iota(jnp.int32, (tq, ncp), 1) * NSA_CMP_STRIDE + (NSA_CMP_LEN - 1)
    dist_c = qpos_c - cend
    valid_c = dist_c >= 0
    distf = dist_c.astype(F32)
    qpos = q0 + lax.broadcasted_iota(jnp.int32, (tq, n_slc), 0)
    blk = lax.broadcasted_iota(jnp.int32, (tq, n_slc), 1)
    cur = qpos // NSA_SLC_LEN
    forced = (blk == 0) | (blk == cur) | (blk == cur - 1)
    blk_ok = blk * NSA_SLC_LEN <= qpos
    ov = ov_ref[...]
    imps = []
    for g in range(NSA_GROUPS):
        kc = kv_ref[g].astype(BF16)
        vc = kv_ref[NSA_GROUPS + g].astype(BF16)
        psum = jnp.zeros((tq, ncp), F32)
        for r in range(rpg):
            h = g * rpg + r
            qh = q_ref[:, h * HEAD_DIM:(h + 1) * HEAD_DIM].astype(BF16)
            s = _dot_nt(qh, kc) * scale - slopes[h] * distf
            s = jnp.where(valid_c, s, NEG)
            e = jnp.where(valid_c, jnp.exp(s - jnp.max(s, axis=1, keepdims=True)), 0.0)
            p = e / jnp.maximum(jnp.sum(e, axis=1, keepdims=True), 1e-30)
            o_ref[:, h * HEAD_DIM:(h + 1) * HEAD_DIM] = _dot(p.astype(BF16), vc)
            psum = psum + p
        p_hi = psum.astype(BF16)
        p_lo = (psum - p_hi.astype(F32)).astype(BF16)
        imp = _dot(p_hi, ov) + _dot(p_lo, ov)
        imp = jnp.where(forced, NSA_FORCE, imp)
        imps.append(jnp.where(blk_ok, imp, NEG))
    imps = [imp.T for imp in imps]
    blk_t = lax.broadcasted_iota(jnp.int32, (n_slc, tq), 0)
    sels = [jnp.full((n_slc, tq), NEG, F32) for _ in range(NSA_GROUPS)]
    for _ in range(topn):
        for g in range(NSA_GROUPS):
            mx = jnp.max(imps[g], axis=0, keepdims=True)
            first = jnp.min(jnp.where(imps[g] == mx, blk_t, n_slc), axis=0, keepdims=True)
            hit = blk_t == first
            sels[g] = jnp.where(hit, 0.0, sels[g])
            imps[g] = jnp.where(hit, -jnp.inf, imps[g])
    sels = [sel.T for sel in sels]
    for g in range(NSA_GROUPS):
        selb = sels[g].astype(BF16)
        for c in range(nch):
            @pl.when(c < n_need)
            def _w():
                tok = _dot(selb, ex_ref[:, c * ch:(c + 1) * ch])
                mask_ref[g, :, c * ch:(c + 1) * ch] = tok.astype(mask_ref.dtype)

            @pl.when(c >= n_need)
            def _z():
                mask_ref[g, :, c * ch:(c + 1) * ch] = jnp.full((tq, ch), NEG, mask_ref.dtype)


def _nsa_cmp(proj, kv_cmp, *, tq=256):
    s_len = proj.shape[0]
    ncp = kv_cmp.shape[1]
    n_slc = s_len // NSA_SLC_LEN
    topn = min(NSA_TOPN, n_slc)
    ch = min(1024, s_len)
    nch = s_len // ch
    starts = np.arange(ncp) * NSA_CMP_STRIDE
    slc_start = np.arange(n_slc) * NSA_SLC_LEN
    overlap = ((starts[:, None] < slc_start[None, :] + NSA_SLC_LEN)
               & (starts[:, None] + NSA_CMP_LEN > slc_start[None, :])).astype(np.float32)
    expand = (np.arange(s_len)[None, :] // NSA_SLC_LEN == np.arange(n_slc)[:, None]).astype(np.float32)
    return pl.pallas_call(
        functools.partial(_nsa_cmp_kernel, tq=tq, ncp=ncp, n_slc=n_slc, topn=topn, ch=ch, nch=nch),
        grid=(s_len // tq,),
        in_specs=[pl.BlockSpec((tq, 512), lambda i: (i, BLK512["d_q"])),
                  pl.BlockSpec((4, ncp, HEAD_DIM), lambda i: (0, 0, 0)),
                  pl.BlockSpec((ncp, n_slc), lambda i: (0, 0)),
                  pl.BlockSpec((n_slc, s_len), lambda i: (0, 0))],
        out_specs=[pl.BlockSpec((tq, 512), lambda i: (i, 0)),
                   pl.BlockSpec((NSA_GROUPS, tq, s_len), lambda i: (0, i, 0))],
        out_shape=[jax.ShapeDtypeStruct((s_len, 512), F32),
                   jax.ShapeDtypeStruct((NSA_GROUPS, s_len, s_len), BF16)],
        compiler_params=_cparams(("arbitrary",)),
        name="nsa_cmp",
    )(proj, kv_cmp, jnp.asarray(overlap, BF16), jnp.asarray(expand, BF16))


def _nsa_combine_kernel(g_ref, oc_ref, os_ref, ow_ref, o_ref):
    gt = _sigmoid(g_ref[...])
    for h in range(NSA_HEADS):
        sl = slice(h * HEAD_DIM, (h + 1) * HEAD_DIM)
        c0 = IDX_HEADS + 3 * h
        o_ref[:, sl] = (gt[:, c0:c0 + 1] * oc_ref[:, sl] + gt[:, c0 + 1:c0 + 2] * os_ref[:, sl]
                        + gt[:, c0 + 2:c0 + 3] * ow_ref[:, sl])


def _nsa_combine(proj, o_cmp, o_slc, o_win, *, tm):
    m = proj.shape[0]
    spec = pl.BlockSpec((tm, 512), lambda i: (i, 0))
    return pl.pallas_call(
        _nsa_combine_kernel,
        grid=(m // tm,),
        in_specs=[pl.BlockSpec((tm, LANES), lambda i: (i, BLK128["small"])), spec, spec, spec],
        out_specs=spec,
        out_shape=jax.ShapeDtypeStruct((m, 512), F32),
        compiler_params=_cparams(("arbitrary",)),
        name="nsa_combine",
    )(proj, o_cmp, o_slc, o_win)


def _merge_kernel(u_ref, oa_ref, ob_ref, oc_ref, od_ref, wg0, wg1, wg2, wg3, wb_ref, o_ref,
                  wgb_ref, wbb_ref):
    wgs = (wg0, wg1, wg2, wg3)

    @pl.when(pl.program_id(1) == 0)
    def _():
        for mch in range(N_BRANCH):
            wgb_ref[mch] = wgs[mch][0].astype(BF16)
            wbb_ref[mch] = wb_ref[0, mch].astype(BF16)

    u = u_ref[...]
    acc = None
    for mch, o_ref_m in enumerate((oa_ref, ob_ref, oc_ref, od_ref)):
        gte = _sigmoid(_dot(u, wgb_ref[mch]))
        z = _dot(o_ref_m[...].astype(BF16), wbb_ref[mch])
        acc = gte * z if acc is None else acc + gte * z
    o_ref[...] = acc.astype(o_ref.dtype)


def _merge(u, branches, w_gate, w_branch, layer, *, tm, tn):
    m, d = u.shape
    nj = d // tn
    bspec = pl.BlockSpec((tm, BRANCH_W), lambda j, i: (i, 0))
    wg_specs = [pl.BlockSpec((1, d, tn),
                             functools.partial(lambda j, i, mch: (layer, 0, mch * nj + j), mch=mch))
                for mch in range(N_BRANCH)]
    return pl.pallas_call(
        _merge_kernel,
        grid=(nj, m // tm),
        in_specs=[pl.BlockSpec((tm, d), lambda j, i: (i, 0)), bspec, bspec, bspec, bspec,
                  *wg_specs,
                  pl.BlockSpec((1, N_BRANCH, BRANCH_W, tn), lambda j, i: (layer, 0, 0, j))],
        out_specs=pl.BlockSpec((tm, tn), lambda j, i: (i, j)),
        out_shape=jax.ShapeDtypeStruct((m, d), BF16),
        scratch_shapes=[pltpu.VMEM((N_BRANCH, d, tn), BF16), pltpu.VMEM((N_BRANCH, BRANCH_W, tn), BF16)],
        compiler_params=_cparams(("arbitrary", "arbitrary")),
        name="merge",
    )(u, *branches, w_gate, w_gate, w_gate, w_gate, w_branch)


def _router_kernel(u_ref, r_ref, o_ref):
    logits = _dot(u_ref[...], r_ref[0].astype(BF16))
    lane = lax.broadcasted_iota(jnp.int32, logits.shape, 1)
    lg = jnp.where(lane < N_EXPERTS, logits, -jnp.inf)
    m1 = jnp.max(lg, axis=1, keepdims=True)
    i1 = jnp.min(jnp.where(lg == m1, lane, LANES), axis=1, keepdims=True)
    lg2 = jnp.where(lane == i1, -jnp.inf, lg)
    m2 = jnp.max(lg2, axis=1, keepdims=True)
    i2 = jnp.min(jnp.where(lg2 == m2, lane, LANES), axis=1, keepdims=True)
    e2 = jnp.exp(m2 - m1)
    w1 = 1.0 / (1.0 + e2)
    w2 = e2 / (1.0 + e2)
    o_ref[...] = jnp.where(lane == 0, i1.astype(F32),
                           jnp.where(lane == 1, i2.astype(F32),
                                     jnp.where(lane == 2, w1, jnp.where(lane == 3, w2, 0.0))))


def _router(u, router_padded, layer, *, tm):
    m, d = u.shape
    return pl.pallas_call(
        _router_kernel,
        grid=(m // tm,),
        in_specs=[pl.BlockSpec((tm, d), lambda i: (i, 0)),
                  pl.BlockSpec((1, d, LANES), lambda i: (layer, 0, 0))],
        out_specs=pl.BlockSpec((tm, LANES), lambda i: (i, 0)),
        out_shape=jax.ShapeDtypeStruct((m, LANES), F32),
        compiler_params=_cparams(("arbitrary",)),
        name="router",
    )(u, router_padded)


MOE_GROUP_TILE = 512
MOE_ROW_TILE = 256
MOE_TOK_CHUNK = 256


def _moe_plan(ridx, rw, s_len):
    gm, tm, ct, n_e = MOE_GROUP_TILE, MOE_ROW_TILE, MOE_TOK_CHUNK, N_EXPERTS
    i32 = jnp.int32
    e_a = ridx.reshape(-1).astype(i32)
    oh = (e_a[:, None] == jnp.arange(n_e, dtype=i32)[None, :]).astype(i32)
    csum = jnp.cumsum(oh, axis=0)
    rank_a = jnp.sum((csum - oh) * oh, axis=1)
    ntile_e = (csum[-1] + gm - 1) // gm
    tile_end = jnp.cumsum(ntile_e)
    pos_a = jnp.take(tile_end - ntile_e, e_a) * gm + rank_a
    n_rows = 2 * s_len + n_e * gm
    n_tiles = n_rows // tm
    n_chunks = s_len // ct
    row_tok = jnp.full((n_rows,), -1, i32).at[pos_a].set(jnp.arange(2 * s_len, dtype=i32) // 2)
    row_w = jnp.zeros((n_rows,), F32).at[pos_a].set(rw.reshape(-1))
    tile_e = jnp.minimum(jnp.searchsorted(tile_end, jnp.arange(n_rows // gm, dtype=i32), side="right"),
                         n_e - 1).astype(i32)
    rt = row_tok.reshape(n_tiles, tm)
    lo = jnp.min(jnp.where(rt >= 0, rt, s_len - 1), axis=1) // ct
    hi = jnp.maximum(jnp.max(jnp.where(rt >= 0, rt, 0), axis=1) // ct, lo)
    n_i = hi - lo + 1
    end = jnp.cumsum(n_i)
    n_work = n_tiles + n_e * n_chunks
    w = jnp.arange(n_work, dtype=i32)
    wt = jnp.minimum(jnp.searchsorted(end, w, side="right"), n_tiles - 1).astype(i32)
    wc = jnp.clip(jnp.take(lo, wt) + w - jnp.take(end - n_i, wt), 0, n_chunks - 1).astype(i32)
    wa = ((w < end[-1]) & jnp.take(jnp.any(rt >= 0, axis=1), wt)).astype(i32)
    order = jnp.argsort(jnp.where(wa > 0, wc * n_tiles + wt, n_chunks * n_tiles + w))
    vc = jnp.where(wa > 0, wc, n_chunks - 1)[order]
    return dict(row_tok=row_tok, row_w=row_w, tile_e=tile_e, n_used=tile_end[n_e - 1:].astype(i32),
                n_tiles=n_tiles, n_work=n_work,
                gather=(wt, wc, wa), combine=(vc, wt[order], wa[order]))


def _moe_gather_kernel(wt_ref, wc_ref, wa_ref, tok_ref, u_ref, o_ref):
    w = pl.program_id(0)

    @pl.when((w == 0) | (wt_ref[jnp.maximum(w - 1, 0)] != wt_ref[w]))
    def _():
        o_ref[...] = jnp.zeros_like(o_ref)

    @pl.when(wa_ref[w] > 0)
    def _():
        ct = u_ref.shape[0]
        cols = wc_ref[w] * ct + lax.broadcasted_iota(jnp.int32, (1, ct), 1)
        onehot = jnp.where(tok_ref[...] == cols, 1.0, 0.0).astype(BF16)
        o_ref[...] += _dot(onehot, u_ref[...]).astype(o_ref.dtype)


def _moe_gather(u, plan):
    s_len, d = u.shape
    tm, ct = MOE_ROW_TILE, MOE_TOK_CHUNK
    n_rows = plan["row_tok"].shape[0]
    return pl.pallas_call(
        _moe_gather_kernel,
        grid_spec=pltpu.PrefetchScalarGridSpec(
            num_scalar_prefetch=3, grid=(plan["n_work"],),
            in_specs=[pl.BlockSpec((tm, 1), lambda w, wt, wc, wa: (wt[w], 0)),
                      pl.BlockSpec((ct, d), lambda w, wt, wc, wa: (wc[w], 0))],
            out_specs=pl.BlockSpec((tm, d), lambda w, wt, wc, wa: (wt[w], 0))),
        out_shape=jax.ShapeDtypeStruct((n_rows, d), BF16),
        compiler_params=_cparams(("arbitrary",)),
        name="moe_gather",
    )(*plan["gather"], plan["row_tok"].reshape(n_rows, 1), u)


def _moe_combine_kernel(vc_ref, vt_ref, va_ref, tok_ref, y0_ref, y1_ref, y2_ref, o_ref):
    w = pl.program_id(0)
    chunk = vc_ref[w]

    @pl.when((w == 0) | (vc_ref[jnp.maximum(w - 1, 0)] != chunk))
    def _():
        o_ref[...] = jnp.zeros_like(o_ref)

    @pl.when(va_ref[w] > 0)
    def _():
        ct = o_ref.shape[0]
        rows = chunk * ct + lax.broadcasted_iota(jnp.int32, (ct, 1), 0)
        onehot_t = jnp.where(rows == tok_ref[0], 1.0, 0.0).astype(BF16)
        o_ref[...] += (_dot(onehot_t, y0_ref[...]) + _dot(onehot_t, y1_ref[...])
                       + _dot(onehot_t, y2_ref[...]))


def _moe_combine(ys3, plan, s_len):
    n_rows, d = ys3[0].shape
    tm, ct = MOE_ROW_TILE, MOE_TOK_CHUNK
    yspec = pl.BlockSpec((tm, d), lambda w, vc, vt, va: (vt[w], 0))
    return pl.pallas_call(
        _moe_combine_kernel,
        grid_spec=pltpu.PrefetchScalarGridSpec(
            num_scalar_prefetch=3, grid=(plan["n_work"],),
            in_specs=[pl.BlockSpec((1, 1, tm), lambda w, vc, vt, va: (vt[w], 0, 0)), yspec, yspec, yspec],
            out_specs=pl.BlockSpec((ct, d), lambda w, vc, vt, va: (vc[w], 0))),
        out_shape=jax.ShapeDtypeStruct((s_len, d), F32),
        compiler_params=_cparams(("arbitrary",)),
        name="moe_combine",
    )(*plan["combine"], plan["row_tok"].reshape(plan["n_tiles"], 1, tm), *ys3)


def _gmm_kernel(te_ref, nu_ref, *refs, swiglu):
    it = iter(refs)
    a_ref = next(it)
    w_refs = [next(it), next(it)] if swiglu else [next(it)]
    rw_ref = next(it) if swiglu else None
    o_refs = [next(it)] if swiglu else [next(it), next(it), next(it)]
    wb_refs = [next(it) for _ in w_refs]
    i = pl.program_id(1)
    used = i < nu_ref[0]

    @pl.when(used & ((i == 0) | (te_ref[i] != te_ref[jnp.maximum(i - 1, 0)])))
    def _():
        for w_ref, wb_ref in zip(w_refs, wb_refs):
            wb_ref[...] = w_ref[0, 0].astype(BF16)

    @pl.when(used)
    def _():
        a = a_ref[...]
        if swiglu:
            h = _silu(_dot(a, wb_refs[0][...])) * _dot(a, wb_refs[1][...]) * rw_ref[...]
            o_refs[0][...] = h.astype(BF16)
        else:
            y = _dot(a, wb_refs[0][...])
            hi = y.astype(BF16)
            r1 = y - hi.astype(F32)
            mid = r1.astype(BF16)
            o_refs[0][...] = hi
            o_refs[1][...] = mid
            o_refs[2][...] = (r1 - mid.astype(F32)).astype(BF16)

    @pl.when(jnp.logical_not(used))
    def _():
        for o_ref in o_refs:
            o_ref[...] = jnp.zeros_like(o_ref)


def _gmm(a, ws, layer, plan, *, tn, row_w=None, name="gmm"):
    n_rows, k = a.shape
    n = ws[0].shape[3]
    tm = MOE_GROUP_TILE
    swiglu = len(ws) == 2
    wspec = pl.BlockSpec((1, 1, k, tn), lambda j, i, te, nu: (layer, te[i], 0, j))
    in_specs = [pl.BlockSpec((tm, k), lambda j, i, te, nu: (i, 0))] + [wspec] * len(ws)
    args = [a, *ws]
    if swiglu:
        in_specs.append(pl.BlockSpec((tm, 1), lambda j, i, te, nu: (i, 0)))
        args.append(row_w.reshape(n_rows, 1))
    ospec = pl.BlockSpec((tm, tn), lambda j, i, te, nu: (i, j))
    oshape = jax.ShapeDtypeStruct((n_rows, n), BF16)
    return pl.pallas_call(
        functools.partial(_gmm_kernel, swiglu=swiglu),
        grid_spec=pltpu.PrefetchScalarGridSpec(
            num_scalar_prefetch=2, grid=(n // tn, n_rows // tm),
            in_specs=in_specs,
            out_specs=ospec if swiglu else [ospec] * 3,
            scratch_shapes=[pltpu.VMEM((k, tn), BF16) for _ in ws]),
        out_shape=oshape if swiglu else [oshape] * 3,
        compiler_params=_cparams(("arbitrary", "arbitrary")),
        name=name,
    )(plan["tile_e"], plan["n_used"], *args)


def _permute_w_in(w):
    cols = []
    for nm in _NEW_ORDER:
        if nm.startswith("pad"):
            cols.append(jnp.zeros(w.shape[:2] + (int(nm[3:]),), w.dtype))
        else:
            o, n = _ORIG[nm]
            cols.append(w[:, :, o:o + n])
    out = jnp.concatenate(cols, axis=2)
    assert out.shape[2] == PROJ_W
    return out


def _nsa_cmp_inputs(proj):
    s_len = proj.shape[0]
    n_cmp = (s_len - NSA_CMP_LEN) // NSA_CMP_STRIDE + 1
    ncp = s_len // NSA_CMP_STRIDE
    xs = []
    for jj in range(2):
        for g in range(NSA_GROUPS):
            c0 = COL_DKV + jj * 128 + g * HEAD_DIM
            r = proj[:, c0:c0 + HEAD_DIM].reshape(ncp, NSA_CMP_STRIDE * HEAD_DIM)
            x = jnp.concatenate([r[:-1], r[1:]], axis=1)
            xs.append(jnp.pad(x, ((0, ncp - n_cmp), (0, 0))))
    return jnp.stack(xs)


def _token_mixers(u, layer, p, cfg):
    s_len = u.shape[0]
    tm = cfg["tm"]
    proj = _mm(u, p["w_in"], layer, tm=tm, tn=512, name="in_proj")
    kv_b = _mm(proj, p["dsa_w_ukv"], layer, tm=tm, tn=512, a_blk=BLK128["b_kv"], k=DSA_KV_RANK,
               prologue="rms", gain=p["dsa_kv_norm_g"][layer], name="dsa_kv")
    kv = _kv_pack(proj, kv_b, tm=cfg["tm_ln"])

    lambda_init = 0.8 - 0.6 * math.exp(-0.3 * layer)
    sl_a = _alibi(DA_HEADS)
    units_a = [(2 * h + mp, h * 256, 256, 0,
                ((h * 128 + mp * 64, sl_a[h], ((2 * h + mp) * 128, 128), 0),))
               for h in range(DA_HEADS) for mp in range(2)]
    o_a2 = _flash(proj, kv["a_k"], kv["a_v"], units=units_a,
                  q_spec=(512, BLK512["a_q"]), out_w=1024, tq=cfg["tq"], tk=cfg["tk"], name="diff_attn")
    o_a = _diff_final(o_a2, p["diff_lambda"], layer, p["diff_subln_g"][layer], lambda_init, tm=tm)

    topk = min(DSA_TOPK_MAX, s_len // 4)
    mask_b = _dsa_select(proj, kv["b_ik"], topk=topk)
    sl8 = _alibi(8)
    units_b = [(h, h * 128, 128, 0, ((h * 64, sl8[h], (h * 64, 64), 0),)) for h in range(DSA_HEADS)]
    o_b = _flash(proj, kv["b_k"], kv["b_v"],
                 units=units_b, q_spec=(512, BLK512["b_q"]), out_w=512, tq=cfg["tq"], tk=cfg["tk"],
                 mask=mask_b.reshape(1, s_len, s_len), name="dsa_attn")

    def gqa_units(masked):
        return [(g, g * 128, 128, g if masked else 0,
                 tuple(((g * 4 + r) * 64, sl8[g * 4 + r], ((g * 4 + r) * 64, 64), g * 4 + r)
                       for r in range(4)))
                for g in range(2)]

    def gqa_kv(k_name, v_name):
        return kv[k_name], kv[v_name]

    sinks = jnp.pad(p["swa_sinks"][layer].reshape(1, SWA_HEADS), ((0, 0), (0, LANES - SWA_HEADS)))
    o_c = _flash(proj, *gqa_kv("c_k", "c_v"), units=gqa_units(False), q_spec=(512, BLK512["c_q"]),
                 out_w=512, tq=cfg["tb"], tk=cfg["tb"], window=SWA_WINDOW, sinks=sinks, name="swa_attn")

    kv_cmp = _nsa_compress(_nsa_cmp_inputs(proj), p["nsa_cmp_pos"], p["nsa_cmp_w1"], p["nsa_cmp_w2"],
                           layer)
    o_cmp, mask_d = _nsa_cmp(proj, kv_cmp)
    o_slc = _flash(proj, *gqa_kv("d_ks", "d_vs"), units=gqa_units(True), q_spec=(512, BLK512["d_q"]),
                   out_w=512, tq=cfg["tq"], tk=cfg["tk"], mask=mask_d, name="nsa_slc_attn")
    o_win = _flash(proj, *gqa_kv("d_kw", "d_vw"), units=gqa_units(False), q_spec=(512, BLK512["d_q"]),
                   out_w=512, tq=cfg["tw"], tk=cfg["tw"], window=NSA_WINDOW, name="nsa_win_attn")
    o_d = _nsa_combine(proj, o_cmp, o_slc, o_win, tm=tm)

    merged = _merge(u, (o_a, o_b, o_c, o_d), p["w_gate"], p["w_branch"], layer,
                    tm=cfg["tm_merge"], tn=256)
    return _mm(merged, p["w_o"], layer, tm=tm, tn=512, name="out_proj")


def _config(s_len):
    return dict(tm=min(1024, s_len), tm_merge=min(512, s_len), tm_ln=min(512, s_len),
                tq=min(256, s_len), tk=min(1024, s_len), tb=min(256, s_len), tw=min(512, s_len))


def kernel(x, c, cond_w, cond_b, w_in, diff_lambda, diff_subln_g, dsa_kv_norm_g, dsa_w_uk, dsa_w_uv,
           swa_sinks, nsa_cmp_pos, nsa_cmp_w1, nsa_cmp_w2, w_branch, w_gate, w_o,
           ln1_g, ln1_b, ln2_g, ln2_b, ffn_w_gate, ffn_w_up, ffn_w_down,
           moe_router, moe_w_gate, moe_w_up, moe_w_down):
    bsz, s_len, d = x.shape
    assert bsz == 1 and d == D_MODEL
    depth = cond_w.shape[0]
    cfg = _config(s_len)
    xs = x.reshape(s_len, d)
    c8 = jnp.broadcast_to(c.reshape(1, d), (8, d))
    p = dict(w_in=_permute_w_in(w_in), diff_lambda=diff_lambda, diff_subln_g=diff_subln_g,
             dsa_kv_norm_g=dsa_kv_norm_g, dsa_w_ukv=jnp.concatenate([dsa_w_uk, dsa_w_uv], axis=2),
             swa_sinks=swa_sinks,
             nsa_cmp_pos=nsa_cmp_pos.reshape(depth, 2, 1, NSA_CMP_LEN * HEAD_DIM),
             nsa_cmp_w1=nsa_cmp_w1, nsa_cmp_w2=nsa_cmp_w2, w_branch=w_branch, w_gate=w_gate, w_o=w_o)
    router_p = jnp.pad(moe_router, ((0, 0), (0, 0), (0, LANES - N_EXPERTS)))
    mods = [_mm(c8, cond_w, l, tm=8, tn=512, prologue="silu", bias=cond_b[l], name="cond")[0:1]
            for l in range(depth)]
    u = _modulate(xs, mods[0], 1, 0, tm=cfg["tm_ln"])
    for l in range(depth):
        y = _token_mixers(u, l, p, cfg)
        xs, u = _resid_ln(xs, y, mods[l], 2, ln1_g[l], ln1_b[l], mods[l], 4, 3, tm=cfg["tm_ln"])
        jx = l // 2
        if l % 2 == 0:
            hdn = _swiglu_up(u, ffn_w_gate, ffn_w_up, jx, tm=cfg["tm"], tn=512, name="ffn_up")
            y = _mmk(hdn, ffn_w_down, jx, tm=cfg["tm"], tn=d, tk=512, name="ffn_down")
        else:
            rt = _router(u, router_p, jx, tm=cfg["tm"])
            plan = _moe_plan(rt[:, 0:2], rt[:, 2:4], s_len)
            hdn = _gmm(_moe_gather(u, plan), (moe_w_gate, moe_w_up), jx, plan, tn=512,
                       row_w=plan["row_w"], name="moe_up")
            y = _moe_combine(_gmm(hdn, (moe_w_down,), jx, plan, tn=512, name="moe_down"), plan, s_len)
        nxt = min(l + 1, depth - 1)
        xs, u = _resid_ln(xs, y, mods[l], 5, ln2_g[l], ln2_b[l], mods[nxt], 1, 0, tm=cfg["tm_ln"])
    return xs.reshape(bsz, s_len, d)
```

```python
import functools
import math

import numpy as np
import jax
import jax.numpy as jnp
from jax import lax
from jax.experimental import pallas as pl
from jax.experimental.pallas import tpu as pltpu

F32 = jnp.float32
BF16 = jnp.bfloat16
NEG = -1e30

D_MODEL = 2048
DEPTH = 4
HEAD_DIM = 64
DA_HEADS = 4
DSA_HEADS = 8
DSA_KV_RANK = 128
IDX_HEADS = 8
DSA_TOPK_MAX = 256
SWA_HEADS = 8
SWA_WINDOW = 128
NSA_HEADS = 8
NSA_GROUPS = 2
NSA_CMP_LEN = 32
NSA_CMP_STRIDE = 16
NSA_CMP_HID = 256
NSA_SLC_LEN = 64
NSA_TOPN = 16
NSA_WINDOW = 512
NSA_FORCE = 1e9
N_BRANCH = 4
BRANCH_W = 512
N_EXPERTS = 8
ALPHA = (2.0 * DEPTH) ** 0.25

VMEM_LIMIT_BYTES = 56 * 1024 * 1024
LANES = 128

_ORIG = dict(a_q=(0, 512), a_k=(512, 512), a_v=(1024, 512), b_q=(1536, 512), b_kv=(2048, 128),
             b_iq=(2176, 512), b_ik=(2688, 64), b_iw=(2752, 8), c_q=(2760, 512), c_k=(3272, 128),
             c_v=(3400, 128), d_q=(3528, 512), d_kv=(4040, 768), d_g=(4808, 24))
_NEW_ORDER = ("a_q", "a_k", "a_v", "b_q", "b_iq", "c_q", "d_q", "b_kv", "c_k", "c_v", "d_kv",
              "b_ik", "pad64", "b_iw", "d_g", "pad96", "pad128")
PROJ_W = 5120
BLK512 = dict(a_q=0, a_k=1, a_v=2, b_q=3, b_iq=4, c_q=5, d_q=6)
BLK128 = dict(b_kv=28, c_k=29, c_v=30, d_kc=31, d_vc=32, d_ks=33, d_vs=34, d_kw=35, d_vw=36,
              b_ik=37, small=38)
COL_DKV = 3968


def _cparams(sem):
    return pltpu.CompilerParams(dimension_semantics=sem, vmem_limit_bytes=VMEM_LIMIT_BYTES)


def _sigmoid(x):
    return 1.0 / (1.0 + jnp.exp(-x))


def _silu(x):
    return x * _sigmoid(x)


def _alibi(n_heads):
    return [2.0 ** (-8.0 * (h + 1) / n_heads) for h in range(n_heads)]


def _dot(a, b):
    return jnp.dot(a, b, preferred_element_type=F32)


def _dot_nt(a, b):
    return lax.dot_general(a, b, (((1,), (1,)), ((), ())), preferred_element_type=F32)


def _mm_kernel(*refs, prologue, has_bias, eps):
    it = iter(refs)
    a_ref = next(it)
    g_ref = next(it) if prologue == "rms" else None
    w_ref = next(it)
    b_ref = next(it) if has_bias else None
    o_ref = next(it)
    wb_ref = next(it)

    @pl.when(pl.program_id(1) == 0)
    def _():
        wb_ref[...] = w_ref[0].astype(BF16)

    a = a_ref[...]
    if prologue == "silu":
        a = _silu(a.astype(F32))
    elif prologue == "rms":
        a = a.astype(F32)
        a = a * lax.rsqrt(jnp.mean(a * a, axis=-1, keepdims=True) + eps) * g_ref[...]
    acc = _dot(a.astype(BF16), wb_ref[...])
    if has_bias:
        acc = acc + b_ref[...]
    o_ref[...] = acc.astype(o_ref.dtype)


def _mm(a, w, layer, *, tm, tn, out_dtype=F32, a_blk=0, k=None, prologue=None, gain=None, bias=None,
        eps=1e-6, name="mm"):
    m = a.shape[0]
    k = a.shape[1] if k is None else k
    n = w.shape[2]
    assert w.shape[1] == k and m % tm == 0 and n % tn == 0
    in_specs = [pl.BlockSpec((tm, k), lambda j, i: (i, a_blk))]
    args = [a]
    if prologue == "rms":
        in_specs.append(pl.BlockSpec((1, k), lambda j, i: (0, 0)))
        args.append(gain.reshape(1, k))
    in_specs.append(pl.BlockSpec((1, k, tn), lambda j, i: (layer, 0, j)))
    args.append(w)
    if bias is not None:
        in_specs.append(pl.BlockSpec((1, tn), lambda j, i: (0, j)))
        args.append(bias.reshape(1, n))
    return pl.pallas_call(
        functools.partial(_mm_kernel, prologue=prologue, has_bias=bias is not None, eps=eps),
        grid=(n // tn, m // tm),
        in_specs=in_specs,
        out_specs=pl.BlockSpec((tm, tn), lambda j, i: (i, j)),
        out_shape=jax.ShapeDtypeStruct((m, n), out_dtype),
        scratch_shapes=[pltpu.VMEM((k, tn), BF16)],
        compiler_params=_cparams(("arbitrary", "arbitrary")),
        name=name,
    )(*args)


def _mmk_kernel(a_ref, w_ref, o_ref, acc_ref, *, nk):
    kk = pl.program_id(2)

    @pl.when(kk == 0)
    def _():
        acc_ref[...] = jnp.zeros_like(acc_ref)

    acc_ref[...] += _dot(a_ref[...], w_ref[0].astype(BF16))

    @pl.when(kk == nk - 1)
    def _():
        o_ref[...] = acc_ref[...]


def _mmk(a, w, layer, *, tm, tn, tk, name="mmk"):
    m, k = a.shape
    n = w.shape[2]
    assert w.shape[1] == k and m % tm == 0 and n % tn == 0 and k % tk == 0
    nk = k // tk
    return pl.pallas_call(
        functools.partial(_mmk_kernel, nk=nk),
        grid=(m // tm, n // tn, nk),
        in_specs=[pl.BlockSpec((tm, tk), lambda i, j, kk: (i, kk)),
                  pl.BlockSpec((1, tk, tn), lambda i, j, kk: (layer, kk, j))],
        out_specs=pl.BlockSpec((tm, tn), lambda i, j, kk: (i, j)),
        out_shape=jax.ShapeDtypeStruct((m, n), F32),
        scratch_shapes=[pltpu.VMEM((tm, tn), F32)],
        compiler_params=_cparams(("arbitrary", "arbitrary", "arbitrary")),
        name=name,
    )(a, w)


def _swiglu_kernel(a_ref, wg_ref, wu_ref, o_ref, wgb_ref, wub_ref):
    @pl.when(pl.program_id(1) == 0)
    def _():
        wgb_ref[...] = wg_ref[0].astype(BF16)
        wub_ref[...] = wu_ref[0].astype(BF16)

    a = a_ref[...]
    o_ref[...] = (_silu(_dot(a, wgb_ref[...])) * _dot(a, wub_ref[...])).astype(o_ref.dtype)


def _swiglu_up(u, wg, wu, layer, *, tm, tn, name="swiglu_up"):
    m, k = u.shape
    f = wg.shape[2]
    assert f % tn == 0 and m % tm == 0
    wspec = pl.BlockSpec((1, k, tn), lambda j, i: (layer, 0, j))
    return pl.pallas_call(
        _swiglu_kernel,
        grid=(f // tn, m // tm),
        in_specs=[pl.BlockSpec((tm, k), lambda j, i: (i, 0)), wspec, wspec],
        out_specs=pl.BlockSpec((tm, tn), lambda j, i: (i, j)),
        out_shape=jax.ShapeDtypeStruct((m, f), BF16),
        scratch_shapes=[pltpu.VMEM((k, tn), BF16), pltpu.VMEM((k, tn), BF16)],
        compiler_params=_cparams(("arbitrary", "arbitrary")),
        name=name,
    )(u, wg, wu)


def _modulate_kernel(x_ref, sc_ref, sh_ref, u_ref):
    u_ref[...] = (x_ref[...] * (1.0 + sc_ref[...]) + sh_ref[...]).astype(u_ref.dtype)


def _modulate(x, mod, sc_blk, sh_blk, *, tm):
    m, d = x.shape
    return pl.pallas_call(
        _modulate_kernel,
        grid=(m // tm,),
        in_specs=[pl.BlockSpec((tm, d), lambda i: (i, 0)),
                  pl.BlockSpec((1, d), lambda i: (0, sc_blk)),
                  pl.BlockSpec((1, d), lambda i: (0, sh_blk))],
        out_specs=pl.BlockSpec((tm, d), lambda i: (i, 0)),
        out_shape=jax.ShapeDtypeStruct((m, d), BF16),
        compiler_params=_cparams(("arbitrary",)),
        name="modulate",
    )(x, mod, mod)


def _resid_ln_kernel(x_ref, y_ref, gate_ref, g_ref, b_ref, sc_ref, sh_ref, xo_ref, u_ref):
    z = ALPHA * x_ref[...] + gate_ref[...] * y_ref[...]
    mu = jnp.mean(z, axis=-1, keepdims=True)
    zc = z - mu
    var = jnp.mean(zc * zc, axis=-1, keepdims=True)
    xn = zc * lax.rsqrt(var + 1e-5) * g_ref[...] + b_ref[...]
    xo_ref[...] = xn
    u_ref[...] = (xn * (1.0 + sc_ref[...]) + sh_ref[...]).astype(u_ref.dtype)


def _resid_ln(x, y, mod, gate_blk, g, b, mod_next, sc_blk, sh_blk, *, tm):
    m, d = x.shape
    row = lambda blk: pl.BlockSpec((1, d), lambda i: (0, blk))
    return pl.pallas_call(
        _resid_ln_kernel,
        grid=(m // tm,),
        in_specs=[pl.BlockSpec((tm, d), lambda i: (i, 0)),
                  pl.BlockSpec((tm, d), lambda i: (i, 0)),
                  row(gate_blk), row(0), row(0), row(sc_blk), row(sh_blk)],
        out_specs=[pl.BlockSpec((tm, d), lambda i: (i, 0)),
                   pl.BlockSpec((tm, d), lambda i: (i, 0))],
        out_shape=[jax.ShapeDtypeStruct((m, d), F32), jax.ShapeDtypeStruct((m, d), BF16)],
        compiler_params=_cparams(("arbitrary",)),
        name="resid_ln",
    )(x, y, mod, g.reshape(1, d), b.reshape(1, d), mod_next, mod_next)


FLASH_ROW_CHUNK = 32
POS_SPLIT = 128


_KV_PACK = (("a_k", "a_k", 0, 2 * DA_HEADS, 64, "k"), ("a_v", "a_v", 0, DA_HEADS, 128, "v"),
            ("b_k", "kv_b", 0, DSA_HEADS, 64, "k"), ("b_v", "kv_b", 512, DSA_HEADS, 64, "v"),
            ("c_k", "c_k", 0, 2, 64, "k"), ("c_v", "c_v", 0, 2, 64, "v"),
            ("d_ks", "d_ks", 0, 2, 64, "k"), ("d_vs", "d_vs", 0, 2, 64, "v"),
            ("d_kw", "d_kw", 0, 2, 64, "k"), ("d_vw", "d_vw", 0, 2, 64, "v"),
            ("b_ik", "b_ik", 0, 1, 128, "cast"))
_KV_SOURCES = ("a_k", "a_v", "c_k", "c_v", "d_ks", "d_vs", "d_kw", "d_vw", "b_ik", "kv_b")


def _kv_pack_kernel(*refs, tm):
    src = dict(zip(_KV_SOURCES, refs[:len(_KV_SOURCES)]))
    outs = refs[len(_KV_SOURCES):]
    i = pl.program_id(0)
    tails = {}
    for w in (64, 128):
        lane = lax.broadcasted_iota(jnp.int32, (tm, w), 1)
        pos = i * tm + lax.broadcasted_iota(jnp.int32, (tm, w), 0)
        tails[("v", w)] = jnp.where(lane == 0, 1.0, 0.0)
        tails[("k", w)] = jnp.where(lane == 0, (pos // POS_SPLIT).astype(F32),
                                    jnp.where(lane == 1, (pos % POS_SPLIT).astype(F32),
                                              jnp.where(lane < 4, 1.0, 0.0)))
    for (_, sname, c0, n_heads, w, kind), o_ref in zip(_KV_PACK, outs):
        if kind == "cast":
            o_ref[...] = src[sname][...].astype(BF16)
            continue
        for h in range(n_heads):
            x = src[sname][:, c0 + h * w:c0 + (h + 1) * w]
            o_ref[:, 2 * h * w:2 * (h + 1) * w] = jnp.concatenate([x, tails[(kind, w)]], axis=1).astype(BF16)


def _kv_pack(proj, kv_b, *, tm):
    s_len = proj.shape[0]
    in_specs, args = [], []
    for sname in _KV_SOURCES:
        if sname == "kv_b":
            in_specs.append(pl.BlockSpec((tm, kv_b.shape[1]), lambda i: (i, 0)))
            args.append(kv_b)
        else:
            wblk, blk = (512, BLK512[sname]) if sname in BLK512 else (128, BLK128[sname])
            in_specs.append(pl.BlockSpec((tm, wblk), functools.partial(lambda i, blk: (i, blk), blk=blk)))
            args.append(proj)
    widths = [(1 if kind == "cast" else 2) * n_heads * w for (_, _, _, n_heads, w, kind) in _KV_PACK]
    outs = pl.pallas_call(
        functools.partial(_kv_pack_kernel, tm=tm),
        grid=(s_len // tm,),
        in_specs=in_specs,
        out_specs=[pl.BlockSpec((tm, wd), lambda i: (i, 0)) for wd in widths],
        out_shape=[jax.ShapeDtypeStruct((s_len, wd), BF16) for wd in widths],
        compiler_params=_cparams(("arbitrary",)),
        name="kv_pack",
    )(*args)
    return {name: o for (name, *_), o in zip(_KV_PACK, outs)}


def _flash_kernel(qi_ref, kb_ref, first_ref, last_ref, *refs, units, tq, tk, window, dense, n_mask,
                  has_sink):
    it = iter(refs)
    q_ref, k_ref, v_ref = next(it), next(it), next(it)
    mask_ref = next(it) if n_mask else None
    sink_ref = next(it) if has_sink else None
    o_ref = next(it)
    q_scr, m_scr, acc_scr, bias_scr = (next(it) for _ in range(4))
    dv = acc_scr.shape[2]
    lcol = dv // 2

    w = pl.program_id(0)
    qi = qi_ref[w]
    kb = kb_ref[w]
    rows = q_scr.shape[1]
    rb = FLASH_ROW_CHUNK
    kw = 2 * HEAD_DIM

    @pl.when(first_ref[w] > 0)
    def _init():
        lane = lax.broadcasted_iota(jnp.int32, (tq, HEAD_DIM), 1)
        qpos = qi * tq + lax.broadcasted_iota(jnp.int32, (tq, HEAD_DIM), 0)
        qhi = (qpos // POS_SPLIT).astype(F32)
        qlo = (qpos % POS_SPLIT).astype(F32)
        for ui, (_, _, _, _, hds) in enumerate(units):
            for r, (qo, slope, _, sink_idx) in enumerate(hds):
                rsl = slice(r * tq, (r + 1) * tq)
                tail = jnp.where(lane == 0, POS_SPLIT * slope,
                                 jnp.where(lane == 1, slope,
                                           jnp.where(lane == 2, -POS_SPLIT * slope * qhi,
                                                     jnp.where(lane == 3, -slope * qlo, 0.0))))
                qs = q_ref[:, qo:qo + HEAD_DIM] * HEAD_DIM ** -0.5
                q_scr[ui, rsl] = jnp.concatenate([qs, tail], axis=1).astype(BF16)
                if has_sink:
                    m_scr[ui, rsl] = jnp.broadcast_to(sink_ref[:, sink_idx:sink_idx + 1], (tq, 1))
                else:
                    m_scr[ui, rsl] = jnp.full((tq, 1), NEG, F32)
            alane = lax.broadcasted_iota(jnp.int32, acc_scr.shape[1:], 1)
            acc_scr[ui] = jnp.where(alane == lcol, 1.0 if has_sink else 0.0, 0.0)

    def step(masked, kt):
        def scores(ui):
            ku = units[ui][0]
            return _dot_nt(q_scr[ui], k_ref[0:kt, ku * kw:(ku + 1) * kw])

        if masked:
            qpos = qi * tq + lax.broadcasted_iota(jnp.int32, (tq, kt), 0)
            kpos = kb * tk + lax.broadcasted_iota(jnp.int32, (tq, kt), 1)
            dist = qpos - kpos
            valid = dist >= 0
            if not dense:
                valid = valid & (dist < window)
            if n_mask:
                for g in range(n_mask):
                    bias_scr[g, :, 0:kt] = jnp.where(valid, mask_ref[g, :, 0:kt].astype(F32), NEG)
            else:
                bias_scr[0, :, 0:kt] = jnp.where(valid, 0.0, NEG)

        def chunk(s, mg, c):
            r0 = c * rb
            sc = s[r0:r0 + rb]
            if masked:
                rw = r0 % tq
                sc = sc + bias_scr[mg, rw:rw + rb, 0:kt]
            return sc

        s_next = scores(0)
        for ui, (_, vo, _, mg, _) in enumerate(units):
            s = s_next
            if ui + 1 < len(units):
                s_next = scores(ui + 1)
            nchunk = rows // rb
            m_old = m_scr[ui]
            m_cur = jnp.concatenate([jnp.max(chunk(s, mg, c), axis=1, keepdims=True) for c in range(nchunk)],
                                    axis=0)
            m_new = jnp.maximum(m_old, m_cur)
            alpha = jnp.exp(m_old - m_new)
            m_scr[ui] = m_new
            p_all = jnp.concatenate(
                [jnp.exp(chunk(s, mg, c) - m_new[c * rb:(c + 1) * rb]).astype(BF16) for c in range(nchunk)],
                axis=0)
            acc_scr[ui] = alpha * acc_scr[ui] + _dot(p_all, v_ref[0:kt, vo:vo + dv])

    if dense:
        assert tk % tq == 0
        sub = tk // tq
        diag = last_ref[w] > 0
        part = qi % sub
        if n_mask:
            pl.when(jnp.logical_not(diag) | (part == sub - 1))(lambda: step(True, tk))
        else:
            pl.when(jnp.logical_not(diag))(lambda: step(False, tk))
            pl.when(diag & (part == sub - 1))(lambda: step(True, tk))
        for c in range(sub - 1):
            pl.when(diag & (part == c))(functools.partial(step, True, (c + 1) * tq))
    else:
        step(True, tk)

    @pl.when(last_ref[w] > 0)
    def _fin():
        for ui, (_, _, _, _, hds) in enumerate(units):
            for r, (_, _, (oo, ow), _) in enumerate(hds):
                rsl = slice(r * tq, (r + 1) * tq)
                acc = acc_scr[ui, rsl]
                o_ref[:, oo:oo + ow] = acc[:, 0:ow] / acc[:, lcol:lcol + 1]


def _flash(q_arr, k_arr, v_arr, *, units, q_spec, out_w, tq, tk, window=None,
           mask=None, sinks=None, name="flash"):
    s_len = q_arr.shape[0]
    dense = window is None
    if not dense:
        assert tq == tk
    n_mask = 0 if mask is None else mask.shape[0]
    dv = units[0][2]
    nu = len(units)
    rows = len(units[0][4]) * tq
    assert all(len(un[4]) * tq == rows and un[2] == dv for un in units) and rows % FLASH_ROW_CHUNK == 0

    pairs = []
    for qi in range(s_len // tq):
        last_kb = (qi * tq + tq - 1) // tk
        first_kb = 0 if dense else max(qi - (-(-(window - 1) // tk)), 0)
        pairs += [(qi, kb, int(kb == first_kb), int(kb == last_kb)) for kb in range(first_kb, last_kb + 1)]
    tables = [jnp.asarray(np.array(col, np.int32)) for col in zip(*pairs)]

    in_specs = [pl.BlockSpec((tq, q_spec[0]), lambda w, qi, kb, fi, la: (qi[w], q_spec[1])),
                pl.BlockSpec((tk, k_arr.shape[1]), lambda w, qi, kb, fi, la: (kb[w], 0)),
                pl.BlockSpec((tk, v_arr.shape[1]), lambda w, qi, kb, fi, la: (kb[w], 0))]
    args = [q_arr, k_arr, v_arr]
    if n_mask:
        in_specs.append(pl.BlockSpec((n_mask, tq, tk), lambda w, qi, kb, fi, la: (0, qi[w], kb[w])))
        args.append(mask)
    if sinks is not None:
        in_specs.append(pl.BlockSpec((1, LANES), lambda w, qi, kb, fi, la: (0, 0)))
        args.append(sinks)
    return pl.pallas_call(
        functools.partial(_flash_kernel, units=tuple(units), tq=tq, tk=tk, window=window, dense=dense,
                          n_mask=n_mask, has_sink=sinks is not None),
        grid_spec=pltpu.PrefetchScalarGridSpec(
            num_scalar_prefetch=4, grid=(len(pairs),),
            in_specs=in_specs,
            out_specs=pl.BlockSpec((tq, out_w), lambda w, qi, kb, fi, la: (qi[w], 0)),
            scratch_shapes=[pltpu.VMEM((nu, rows, 2 * HEAD_DIM), BF16), pltpu.VMEM((nu, rows, 1), F32),
                            pltpu.VMEM((nu, rows, dv), F32), pltpu.VMEM((max(n_mask, 1), tq, tk), F32)]),
        out_shape=jax.ShapeDtypeStruct((s_len, out_w), F32),
        compiler_params=_cparams(("arbitrary",)),
        name=name,
    )(*tables, *args)


def _diff_final_kernel(o_ref, lam_ref, g_ref, out_ref, *, lambda_init):
    lf = lam_ref[0]
    lam = (jnp.exp(jnp.sum(lf[0:1] * lf[1:2])) - jnp.exp(jnp.sum(lf[2:3] * lf[3:4])) + lambda_init)
    w = 2 * HEAD_DIM
    for h in range(DA_HEADS):
        o = o_ref[:, (2 * h) * w:(2 * h + 1) * w] - lam * o_ref[:, (2 * h + 1) * w:(2 * h + 2) * w]
        o = o * lax.rsqrt(jnp.mean(o * o, axis=-1, keepdims=True) + 1e-6) * g_ref[...]
        out_ref[:, h * w:(h + 1) * w] = o * (1.0 - lambda_init)


def _diff_final(o, diff_lambda, layer, subln_g, lambda_init, *, tm):
    m = o.shape[0]
    w = 2 * HEAD_DIM
    return pl.pallas_call(
        functools.partial(_diff_final_kernel, lambda_init=lambda_init),
        grid=(m // tm,),
        in_specs=[pl.BlockSpec((tm, 2 * DA_HEADS * w), lambda i: (i, 0)),
                  pl.BlockSpec((1, 4, HEAD_DIM), lambda i: (layer, 0, 0)),
                  pl.BlockSpec((1, w), lambda i: (0, 0))],
        out_specs=pl.BlockSpec((tm, DA_HEADS * w), lambda i: (i, 0)),
        out_shape=jax.ShapeDtypeStruct((m, DA_HEADS * w), F32),
        compiler_params=_cparams(("arbitrary",)),
        name="diff_final",
    )(o, diff_lambda, subln_g.reshape(1, w))


def _f32_key_const(x):
    b = int(np.array(x, np.float32).view(np.int32))
    return b ^ ((b >> 31) & 0x7FFFFFFF)


I16_MIN = -(2 ** 15)


def _dsa_select_kernel(qi_ref, w_ref, kidx_ref, mask_ref, key_scr, half_scr, j_scr, *, tq, ch, nch, topk,
                       s_len):
    i = pl.program_id(0)
    q0 = i * tq
    n_need = (q0 + tq + ch - 1) // ch
    qpos = q0 + lax.broadcasted_iota(jnp.int32, (tq, 1), 0)
    lane = lax.broadcasted_iota(jnp.int32, (1, ch), 1)
    w = w_ref[:, 0:IDX_HEADS]
    q_all = jnp.concatenate([qi_ref[:, h * HEAD_DIM:(h + 1) * HEAD_DIM] for h in range(IDX_HEADS)],
                            axis=0).astype(BF16)

    def score_chunk(c, carry):
        kc = kidx_ref[pl.ds(pl.multiple_of(c * ch, ch), ch), 0:HEAD_DIM].astype(BF16)
        lg = _dot_nt(q_all, kc)
        acc = jnp.zeros((tq, ch), F32)
        for h in range(IDX_HEADS):
            acc = acc + w[:, h:h + 1] * jnp.maximum(lg[h * tq:(h + 1) * tq], 0.0)
        acc = jnp.where(c * ch + lane <= qpos, acc, NEG) + 0.0
        bits = pltpu.bitcast(acc, jnp.int32)
        key = bits ^ ((bits >> 31) & 0x7FFFFFFF)
        key_scr[c] = key
        half_scr[c] = (key >> 16).astype(jnp.int16)
        return carry

    lax.fori_loop(0, n_need, score_chunk, 0)

    def count16(cand, strict):
        cand16 = jnp.broadcast_to(cand, (tq, LANES)).astype(jnp.int16)
        one, zero = jnp.int16(1), jnp.int16(0)

        def body(c, acc):
            blk = half_scr[c]
            for t in range(ch // LANES):
                tile = blk[:, t * LANES:(t + 1) * LANES]
                acc = acc + jnp.where(tile > cand16 if strict else tile >= cand16, one, zero)
            return acc
        acc = lax.fori_loop(0, n_need, body, jnp.zeros((tq, LANES), jnp.int16))
        return jnp.sum(acc.astype(jnp.int32), axis=1, keepdims=True)

    def search16(need_cnt):
        def bit_step(b, t):
            cand = t + jnp.left_shift(jnp.int32(1), 15 - b)
            return jnp.where(count16(cand, False) >= need_cnt, cand, t)
        return lax.fori_loop(0, 16, bit_step, jnp.full((tq, 1), I16_MIN, jnp.int32))

    t_hi = search16(topk)
    need_lo = topk - count16(t_hi, True)

    def low_chunk(c, carry):
        key = key_scr[c]
        low = (key & 0xFFFF) + I16_MIN
        half_scr[c] = jnp.where((key >> 16) == t_hi, low, I16_MIN).astype(jnp.int16)
        return carry

    lax.fori_loop(0, n_need, low_chunk, 0)
    t_lo = search16(need_lo)
    thr = jnp.left_shift(t_hi, 16) + (t_lo - I16_MIN)

    def count(pred):
        def body(c, acc):
            m = jnp.where(pred(key_scr[c], c), 1, 0)
            part = m[:, 0:LANES]
            for t in range(1, ch // LANES):
                part = part + m[:, t * LANES:(t + 1) * LANES]
            return acc + part
        acc = lax.fori_loop(0, n_need, body, jnp.zeros((tq, LANES), jnp.int32))
        return jnp.sum(acc, axis=1, keepdims=True)

    cnt_gt = count(lambda blk, c: blk > thr)
    cnt_ge = count(lambda blk, c: blk >= thr)
    need = topk - cnt_gt
    tie_rows = (cnt_ge > topk) & (thr > _f32_key_const(NEG))
    j_scr[...] = jnp.full((tq, 1), s_len, jnp.int32)
    any_tie = jnp.max(jnp.where(tie_rows, 1, 0)) > 0

    @pl.when(any_tie)
    def _ties():
        def tie_chunk(c, carry):
            half_scr[c] = jnp.where(key_scr[c] == thr, -1 - (c * ch + lane), I16_MIN).astype(jnp.int16)
            return carry

        lax.fori_loop(0, n_need, tie_chunk, 0)
        j_scr[...] = jnp.where(tie_rows, -1 - search16(need), s_len)

    jv = j_scr[...]
    for c in range(nch):
        @pl.when((c < n_need) & any_tie)
        def _w():
            key = key_scr[c]
            sel = (key > thr) | ((key == thr) & (c * ch + lane <= jv))
            mask_ref[:, c * ch:(c + 1) * ch] = jnp.where(sel, 0.0, NEG).astype(mask_ref.dtype)

        @pl.when((c < n_need) & jnp.logical_not(any_tie))
        def _wf():
            mask_ref[:, c * ch:(c + 1) * ch] = jnp.where(key_scr[c] >= thr, 0.0, NEG).astype(mask_ref.dtype)

        @pl.when(c >= n_need)
        def _z():
            mask_ref[:, c * ch:(c + 1) * ch] = jnp.full((tq, ch), NEG, mask_ref.dtype)


def _dsa_select(proj, kidx, *, topk, tq=128):
    s_len = proj.shape[0]
    assert s_len < -I16_MIN
    ch = min(1024, s_len)
    nch = s_len // ch
    return pl.pallas_call(
        functools.partial(_dsa_select_kernel, tq=tq, ch=ch, nch=nch, topk=topk, s_len=s_len),
        grid=(s_len // tq,),
        in_specs=[pl.BlockSpec((tq, 512), lambda i: (i, BLK512["b_iq"])),
                  pl.BlockSpec((tq, LANES), lambda i: (i, BLK128["small"])),
                  pl.BlockSpec((s_len, LANES), lambda i: (0, 0))],
        out_specs=pl.BlockSpec((tq, s_len), lambda i: (i, 0)),
        out_shape=jax.ShapeDtypeStruct((s_len, s_len), BF16),
        scratch_shapes=[pltpu.VMEM((nch, tq, ch), jnp.int32), pltpu.VMEM((nch, tq, ch), jnp.int16),
                        pltpu.VMEM((tq, 1), jnp.int32)],
        compiler_params=_cparams(("arbitrary",)),
        name="dsa_select",
    )(proj, proj, kidx)


def _nsa_compress_kernel(x_ref, pos_ref, w1_ref, w2_ref, o_ref):
    x = (x_ref[0] + pos_ref[0, 0]).astype(BF16)
    hdn = _silu(_dot(x, w1_ref[0, 0].astype(BF16)))
    o_ref[0] = _dot(hdn.astype(BF16), w2_ref[0, 0].astype(BF16))


def _nsa_compress(xc, pos, w1, w2, layer):
    _, ncp, kdim = xc.shape
    return pl.pallas_call(
        _nsa_compress_kernel,
        grid=(4,),
        in_specs=[pl.BlockSpec((1, ncp, kdim), lambda i: (i, 0, 0)),
                  pl.BlockSpec((1, 1, 1, kdim), lambda i: (layer, i // 2, 0, 0)),
                  pl.BlockSpec((1, 1, kdim, NSA_CMP_HID), lambda i: (layer, i // 2, 0, 0)),
                  pl.BlockSpec((1, 1, NSA_CMP_HID, HEAD_DIM), lambda i: (layer, i // 2, 0, 0))],
        out_specs=pl.BlockSpec((1, ncp, HEAD_DIM), lambda i: (i, 0, 0)),
        out_shape=jax.ShapeDtypeStruct((4, ncp, HEAD_DIM), F32),
        compiler_params=_cparams(("arbitrary",)),
        name="nsa_compress",
    )(xc, pos, w1, w2)


def _nsa_cmp_kernel(q_ref, kv_ref, ov_ref, ex_ref, o_ref, mask_ref, *, tq, ncp, n_slc, topn, ch, nch):
    i = pl.program_id(0)
    q0 = i * tq
    n_need = (q0 + tq + ch - 1) // ch
    rpg = NSA_HEADS // NSA_GROUPS
    slopes = _alibi(NSA_HEADS)
    scale = HEAD_DIM ** -0.5
    qpos_c = q0 + lax.broadcasted_iota(jnp.int32, (tq, ncp), 0)
    cend = lax.broadcasted_iota(jnp.int32, (tq, ncp), 1) * NSA_CMP_STRIDE + (NSA_CMP_LEN - 1)
    dist_c = qpos_c - cend
    valid_c = dist_c >= 0
    distf = dist_c.astype(F32)
    qpos = q0 + lax.broadcasted_iota(jnp.int32, (tq, n_slc), 0)
    blk = lax.broadcasted_iota(jnp.int32, (tq, n_slc), 1)
    cur = qpos // NSA_SLC_LEN
    forced = (blk == 0) | (blk == cur) | (blk == cur - 1)
    blk_ok = blk * NSA_SLC_LEN <= qpos
    ov = ov_ref[...]
    imps = []
    for g in range(NSA_GROUPS):
        kc = kv_ref[g].astype(BF16)
        vc = kv_ref[NSA_GROUPS + g].astype(BF16)
        psum = jnp.zeros((tq, ncp), F32)
        for r in range(rpg):
            h = g * rpg + r
            qh = q_ref[:, h * HEAD_DIM:(h + 1) * HEAD_DIM].astype(BF16)
            s = _dot_nt(qh, kc) * scale - slopes[h] * distf
            s = jnp.where(valid_c, s, NEG)
            e = jnp.where(valid_c, jnp.exp(s - jnp.max(s, axis=1, keepdims=True)), 0.0)
            p = e / jnp.maximum(jnp.sum(e, axis=1, keepdims=True), 1e-30)
            o_ref[:, h * HEAD_DIM:(h + 1) * HEAD_DIM] = _dot(p.astype(BF16), vc)
            psum = psum + p
        p_hi = psum.astype(BF16)
        p_lo = (psum - p_hi.astype(F32)).astype(BF16)
        imp = _dot(p_hi, ov) + _dot(p_lo, ov)
        imp = jnp.where(forced, NSA_FORCE, imp)
        imps.append(jnp.where(blk_ok, imp, NEG))
    imps = [imp.T for imp in imps]
    blk_t = lax.broadcasted_iota(jnp.int32, (n_slc, tq), 0)
    sels = [jnp.full((n_slc, tq), NEG, F32) for _ in range(NSA_GROUPS)]
    for _ in range(topn):
        for g in range(NSA_GROUPS):
            mx = jnp.max(imps[g], axis=0, keepdims=True)
            first = jnp.min(jnp.where(imps[g] == mx, blk_t, n_slc), axis=0, keepdims=True)
            hit = blk_t == first
            sels[g] = jnp.where(hit, 0.0, sels[g])
            imps[g] = jnp.where(hit, -jnp.inf, imps[g])
    sels = [sel.T for sel in sels]
    for g in range(NSA_GROUPS):
        selb = sels[g].astype(BF16)
        for c in range(nch):
            @pl.when(c < n_need)
            def _w():
                tok = _dot(selb, ex_ref[:, c * ch:(c + 1) * ch])
                mask_ref[g, :, c * ch:(c + 1) * ch] = tok.astype(mask_ref.dtype)

            @pl.when(c >= n_need)
            def _z():
                mask_ref[g, :, c * ch:(c + 1) * ch] = jnp.full((tq, ch), NEG, mask_ref.dtype)


def _nsa_cmp(proj, kv_cmp, *, tq=256):
    s_len = proj.shape[0]
    ncp = kv_cmp.shape[1]
    n_slc = s_len // NSA_SLC_LEN
    topn = min(NSA_TOPN, n_slc)
    ch = min(1024, s_len)
    nch = s_len // ch
    starts = np.arange(ncp) * NSA_CMP_STRIDE
    slc_start = np.arange(n_slc) * NSA_SLC_LEN
    overlap = ((starts[:, None] < slc_start[None, :] + NSA_SLC_LEN)
               & (starts[:, None] + NSA_CMP_LEN > slc_start[None, :])).astype(np.float32)
    expand = (np.arange(s_len)[None, :] // NSA_SLC_LEN == np.arange(n_slc)[:, None]).astype(np.float32)
    return pl.pallas_call(
        functools.partial(_nsa_cmp_kernel, tq=tq, ncp=ncp, n_slc=n_slc, topn=topn, ch=ch, nch=nch),
        grid=(s_len // tq,),
        in_specs=[pl.BlockSpec((tq, 512), lambda i: (i, BLK512["d_q"])),
                  pl.BlockSpec((4, ncp, HEAD_DIM), lambda i: (0, 0, 0)),
                  pl.BlockSpec((ncp, n_slc), lambda i: (0, 0)),
                  pl.BlockSpec((n_slc, s_len), lambda i: (0, 0))],
        out_specs=[pl.BlockSpec((tq, 512), lambda i: (i, 0)),
                   pl.BlockSpec((NSA_GROUPS, tq, s_len), lambda i: (0, i, 0))],
        out_shape=[jax.ShapeDtypeStruct((s_len, 512), F32),
                   jax.ShapeDtypeStruct((NSA_GROUPS, s_len, s_len), BF16)],
        compiler_params=_cparams(("arbitrary",)),
        name="nsa_cmp",
    )(proj, kv_cmp, jnp.asarray(overlap, BF16), jnp.asarray(expand, BF16))


def _nsa_combine_kernel(g_ref, oc_ref, os_ref, ow_ref, o_ref):
    gt = _sigmoid(g_ref[...])
    for h in range(NSA_HEADS):
        sl = slice(h * HEAD_DIM, (h + 1) * HEAD_DIM)
        c0 = IDX_HEADS + 3 * h
        o_ref[:, sl] = (gt[:, c0:c0 + 1] * oc_ref[:, sl] + gt[:, c0 + 1:c0 + 2] * os_ref[:, sl]
                        + gt[:, c0 + 2:c0 + 3] * ow_ref[:, sl])


def _nsa_combine(proj, o_cmp, o_slc, o_win, *, tm):
    m = proj.shape[0]
    spec = pl.BlockSpec((tm, 512), lambda i: (i, 0))
    return pl.pallas_call(
        _nsa_combine_kernel,
        grid=(m // tm,),
        in_specs=[pl.BlockSpec((tm, LANES), lambda i: (i, BLK128["small"])), spec, spec, spec],
        out_specs=spec,
        out_shape=jax.ShapeDtypeStruct((m, 512), F32),
        compiler_params=_cparams(("arbitrary",)),
        name="nsa_combine",
    )(proj, o_cmp, o_slc, o_win)


def _merge_kernel(u_ref, oa_ref, ob_ref, oc_ref, od_ref, wg0, wg1, wg2, wg3, wb_ref, o_ref,
                  wgb_ref, wbb_ref):
    wgs = (wg0, wg1, wg2, wg3)

    @pl.when(pl.program_id(1) == 0)
    def _():
        for mch in range(N_BRANCH):
            wgb_ref[mch] = wgs[mch][0].astype(BF16)
            wbb_ref[mch] = wb_ref[0, mch].astype(BF16)

    u = u_ref[...]
    acc = None
    for mch, o_ref_m in enumerate((oa_ref, ob_ref, oc_ref, od_ref)):
        gte = _sigmoid(_dot(u, wgb_ref[mch]))
        z = _dot(o_ref_m[...].astype(BF16), wbb_ref[mch])
        acc = gte * z if acc is None else acc + gte * z
    o_ref[...] = acc.astype(o_ref.dtype)


def _merge(u, branches, w_gate, w_branch, layer, *, tm, tn):
    m, d = u.shape
    nj = d // tn
    bspec = pl.BlockSpec((tm, BRANCH_W), lambda j, i: (i, 0))
    wg_specs = [pl.BlockSpec((1, d, tn),
                             functools.partial(lambda j, i, mch: (layer, 0, mch * nj + j), mch=mch))
                for mch in range(N_BRANCH)]
    return pl.pallas_call(
        _merge_kernel,
        grid=(nj, m // tm),
        in_specs=[pl.BlockSpec((tm, d), lambda j, i: (i, 0)), bspec, bspec, bspec, bspec,
                  *wg_specs,
                  pl.BlockSpec((1, N_BRANCH, BRANCH_W, tn), lambda j, i: (layer, 0, 0, j))],
        out_specs=pl.BlockSpec((tm, tn), lambda j, i: (i, j)),
        out_shape=jax.ShapeDtypeStruct((m, d), BF16),
        scratch_shapes=[pltpu.VMEM((N_BRANCH, d, tn), BF16), pltpu.VMEM((N_BRANCH, BRANCH_W, tn), BF16)],
        compiler_params=_cparams(("arbitrary", "arbitrary")),
        name="merge",
    )(u, *branches, w_gate, w_gate, w_gate, w_gate, w_branch)


def _router_kernel(u_ref, r_ref, o_ref):
    logits = _dot(u_ref[...], r_ref[0].astype(BF16))
    lane = lax.broadcasted_iota(jnp.int32, logits.shape, 1)
    lg = jnp.where(lane < N_EXPERTS, logits, -jnp.inf)
    m1 = jnp.max(lg, axis=1, keepdims=True)
    i1 = jnp.min(jnp.where(lg == m1, lane, LANES), axis=1, keepdims=True)
    lg2 = jnp.where(lane == i1, -jnp.inf, lg)
    m2 = jnp.max(lg2, axis=1, keepdims=True)
    i2 = jnp.min(jnp.where(lg2 == m2, lane, LANES), axis=1, keepdims=True)
    e2 = jnp.exp(m2 - m1)
    w1 = 1.0 / (1.0 + e2)
    w2 = e2 / (1.0 + e2)
    o_ref[...] = jnp.where(lane == 0, i1.astype(F32),
                           jnp.where(lane == 1, i2.astype(F32),
                                     jnp.where(lane == 2, w1, jnp.where(lane == 3, w2, 0.0))))


def _router(u, router_padded, layer, *, tm):
    m, d = u.shape
    return pl.pallas_call(
        _router_kernel,
        grid=(m // tm,),
        in_specs=[pl.BlockSpec((tm, d), lambda i: (i, 0)),
                  pl.BlockSpec((1, d, LANES), lambda i: (layer, 0, 0))],
        out_specs=pl.BlockSpec((tm, LANES), lambda i: (i, 0)),
        out_shape=jax.ShapeDtypeStruct((m, LANES), F32),
        compiler_params=_cparams(("arbitrary",)),
        name="router",
    )(u, router_padded)


MOE_GROUP_TILE = 512
MOE_ROW_TILE = 256
MOE_TOK_CHUNK = 256


def _moe_plan(ridx, rw, s_len):
    gm, tm, ct, n_e = MOE_GROUP_TILE, MOE_ROW_TILE, MOE_TOK_CHUNK, N_EXPERTS
    i32 = jnp.int32
    e_a = ridx.reshape(-1).astype(i32)
    oh = (e_a[:, None] == jnp.arange(n_e, dtype=i32)[None, :]).astype(i32)
    csum = jnp.cumsum(oh, axis=0)
    rank_a = jnp.sum((csum - oh) * oh, axis=1)
    ntile_e = (csum[-1] + gm - 1) // gm
    tile_end = jnp.cumsum(ntile_e)
    pos_a = jnp.take(tile_end - ntile_e, e_a) * gm + rank_a
    n_rows = 2 * s_len + n_e * gm
    n_tiles = n_rows // tm
    n_chunks = s_len // ct
    row_tok = jnp.full((n_rows,), -1, i32).at[pos_a].set(jnp.arange(2 * s_len, dtype=i32) // 2)
    row_w = jnp.zeros((n_rows,), F32).at[pos_a].set(rw.reshape(-1))
    tile_e = jnp.minimum(jnp.searchsorted(tile_end, jnp.arange(n_rows // gm, dtype=i32), side="right"),
                         n_e - 1).astype(i32)
    rt = row_tok.reshape(n_tiles, tm)
    lo = jnp.min(jnp.where(rt >= 0, rt, s_len - 1), axis=1) // ct
    hi = jnp.maximum(jnp.max(jnp.where(rt >= 0, rt, 0), axis=1) // ct, lo)
    n_i = hi - lo + 1
    end = jnp.cumsum(n_i)
    n_work = n_tiles + n_e * n_chunks
    w = jnp.arange(n_work, dtype=i32)
    wt = jnp.minimum(jnp.searchsorted(end, w, side="right"), n_tiles - 1).astype(i32)
    wc = jnp.clip(jnp.take(lo, wt) + w - jnp.take(end - n_i, wt), 0, n_chunks - 1).astype(i32)
    wa = ((w < end[-1]) & jnp.take(jnp.any(rt >= 0, axis=1), wt)).astype(i32)
    order = jnp.argsort(jnp.where(wa > 0, wc * n_tiles + wt, n_chunks * n_tiles + w))
    vc = jnp.where(wa > 0, wc, n_chunks - 1)[order]
    return dict(row_tok=row_tok, row_w=row_w, tile_e=tile_e, n_used=tile_end[n_e - 1:].astype(i32),
                n_tiles=n_tiles, n_work=n_work,
                gather=(wt, wc, wa), combine=(vc, wt[order], wa[order]))


def _moe_gather_kernel(wt_ref, wc_ref, wa_ref, tok_ref, u_ref, o_ref):
    w = pl.program_id(0)

    @pl.when((w == 0) | (wt_ref[jnp.maximum(w - 1, 0)] != wt_ref[w]))
    def _():
        o_ref[...] = jnp.zeros_like(o_ref)

    @pl.when(wa_ref[w] > 0)
    def _():
        ct = u_ref.shape[0]
        cols = wc_ref[w] * ct + lax.broadcasted_iota(jnp.int32, (1, ct), 1)
        onehot = jnp.where(tok_ref[...] == cols, 1.0, 0.0).astype(BF16)
        o_ref[...] += _dot(onehot, u_ref[...]).astype(o_ref.dtype)


def _moe_gather(u, plan):
    s_len, d = u.shape
    tm, ct = MOE_ROW_TILE, MOE_TOK_CHUNK
    n_rows = plan["row_tok"].shape[0]
    return pl.pallas_call(
        _moe_gather_kernel,
        grid_spec=pltpu.PrefetchScalarGridSpec(
            num_scalar_prefetch=3, grid=(plan["n_work"],),
            in_specs=[pl.BlockSpec((tm, 1), lambda w, wt, wc, wa: (wt[w], 0)),
                      pl.BlockSpec((ct, d), lambda w, wt, wc, wa: (wc[w], 0))],
            out_specs=pl.BlockSpec((tm, d), lambda w, wt, wc, wa: (wt[w], 0))),
        out_shape=jax.ShapeDtypeStruct((n_rows, d), BF16),
        compiler_params=_cparams(("arbitrary",)),
        name="moe_gather",
    )(*plan["gather"], plan["row_tok"].reshape(n_rows, 1), u)


def _moe_combine_kernel(vc_ref, vt_ref, va_ref, tok_ref, y0_ref, y1_ref, y2_ref, o_ref):
    w = pl.program_id(0)
    chunk = vc_ref[w]

    @pl.when((w == 0) | (vc_ref[jnp.maximum(w - 1, 0)] != chunk))
    def _():
        o_ref[...] = jnp.zeros_like(o_ref)

    @pl.when(va_ref[w] > 0)
    def _():
        ct = o_ref.shape[0]
        rows = chunk * ct + lax.broadcasted_iota(jnp.int32, (ct, 1), 0)
        onehot_t = jnp.where(rows == tok_ref[0], 1.0, 0.0).astype(BF16)
        o_ref[...] += (_dot(onehot_t, y0_ref[...]) + _dot(onehot_t, y1_ref[...])
                       + _dot(onehot_t, y2_ref[...]))


def _moe_combine(ys3, plan, s_len):
    n_rows, d = ys3[0].shape
    tm, ct = MOE_ROW_TILE, MOE_TOK_CHUNK
    yspec = pl.BlockSpec((tm, d), lambda w, vc, vt, va: (vt[w], 0))
    return pl.pallas_call(
        _moe_combine_kernel,
        grid_spec=pltpu.PrefetchScalarGridSpec(
            num_scalar_prefetch=3, grid=(plan["n_work"],),
            in_specs=[pl.BlockSpec((1, 1, tm), lambda w, vc, vt, va: (vt[w], 0, 0)), yspec, yspec, yspec],
            out_specs=pl.BlockSpec((ct, d), lambda w, vc, vt, va: (vc[w], 0))),
        out_shape=jax.ShapeDtypeStruct((s_len, d), F32),
        compiler_params=_cparams(("arbitrary",)),
        name="moe_combine",
    )(*plan["combine"], plan["row_tok"].reshape(plan["n_tiles"], 1, tm), *ys3)


def _gmm_kernel(te_ref, nu_ref, *refs, swiglu):
    it = iter(refs)
    a_ref = next(it)
    w_refs = [next(it), next(it)] if swiglu else [next(it)]
    rw_ref = next(it) if swiglu else None
    o_refs = [next(it)] if swiglu else [next(it), next(it), next(it)]
    wb_refs = [next(it) for _ in w_refs]
    i = pl.program_id(1)
    used = i < nu_ref[0]

    @pl.when(used & ((i == 0) | (te_ref[i] != te_ref[jnp.maximum(i - 1, 0)])))
    def _():
        for w_ref, wb_ref in zip(w_refs, wb_refs):
            wb_ref[...] = w_ref[0, 0].astype(BF16)

    @pl.when(used)
    def _():
        a = a_ref[...]
        if swiglu:
            h = _silu(_dot(a, wb_refs[0][...])) * _dot(a, wb_refs[1][...]) * rw_ref[...]
            o_refs[0][...] = h.astype(BF16)
        else:
            y = _dot(a, wb_refs[0][...])
            hi = y.astype(BF16)
            r1 = y - hi.astype(F32)
            mid = r1.astype(BF16)
            o_refs[0][...] = hi
            o_refs[1][...] = mid
            o_refs[2][...] = (r1 - mid.astype(F32)).astype(BF16)

    @pl.when(jnp.logical_not(used))
    def _():
        for o_ref in o_refs:
            o_ref[...] = jnp.zeros_like(o_ref)


def _gmm(a, ws, layer, plan, *, tn, row_w=None, name="gmm"):
    n_rows, k = a.shape
    n = ws[0].shape[3]
    tm = MOE_GROUP_TILE
    swiglu = len(ws) == 2
    wspec = pl.BlockSpec((1, 1, k, tn), lambda j, i, te, nu: (layer, te[i], 0, j))
    in_specs = [pl.BlockSpec((tm, k), lambda j, i, te, nu: (i, 0))] + [wspec] * len(ws)
    args = [a, *ws]
    if swiglu:
        in_specs.append(pl.BlockSpec((tm, 1), lambda j, i, te, nu: (i, 0)))
        args.append(row_w.reshape(n_rows, 1))
    ospec = pl.BlockSpec((tm, tn), lambda j, i, te, nu: (i, j))
    oshape = jax.ShapeDtypeStruct((n_rows, n), BF16)
    return pl.pallas_call(
        functools.partial(_gmm_kernel, swiglu=swiglu),
        grid_spec=pltpu.PrefetchScalarGridSpec(
            num_scalar_prefetch=2, grid=(n // tn, n_rows // tm),
            in_specs=in_specs,
            out_specs=ospec if swiglu else [ospec] * 3,
            scratch_shapes=[pltpu.VMEM((k, tn), BF16) for _ in ws]),
        out_shape=oshape if swiglu else [oshape] * 3,
        compiler_params=_cparams(("arbitrary", "arbitrary")),
        name=name,
    )(plan["tile_e"], plan["n_used"], *args)


def _permute_w_in(w):
    cols = []
    for nm in _NEW_ORDER:
        if nm.startswith("pad"):
            cols.append(jnp.zeros(w.shape[:2] + (int(nm[3:]),), w.dtype))
        else:
            o, n = _ORIG[nm]
            cols.append(w[:, :, o:o + n])
    out = jnp.concatenate(cols, axis=2)
    assert out.shape[2] == PROJ_W
    return out


def _nsa_cmp_inputs(proj):
    s_len = proj.shape[0]
    n_cmp = (s_len - NSA_CMP_LEN) // NSA_CMP_STRIDE + 1
    ncp = s_len // NSA_CMP_STRIDE
    xs = []
    for jj in range(2):
        for g in range(NSA_GROUPS):
            c0 = COL_DKV + jj * 128 + g * HEAD_DIM
            r = proj[:, c0:c0 + HEAD_DIM].reshape(ncp, NSA_CMP_STRIDE * HEAD_DIM)
            x = jnp.concatenate([r[:-1], r[1:]], axis=1)
            xs.append(jnp.pad(x, ((0, ncp - n_cmp), (0, 0))))
    return jnp.stack(xs)


def _token_mixers(u, layer, p, cfg):
    s_len = u.shape[0]
    tm = cfg["tm"]
    proj = _mm(u, p["w_in"], layer, tm=tm, tn=512, name="in_proj")
    kv_b = _mm(proj, p["dsa_w_ukv"], layer, tm=tm, tn=512, a_blk=BLK128["b_kv"], k=DSA_KV_RANK,
               prologue="rms", gain=p["dsa_kv_norm_g"][layer], name="dsa_kv")
    kv = _kv_pack(proj, kv_b, tm=cfg["tm_ln"])

    lambda_init = 0.8 - 0.6 * math.exp(-0.3 * layer)
    sl_a = _alibi(DA_HEADS)
    units_a = [(2 * h + mp, h * 256, 256, 0,
                ((h * 128 + mp * 64, sl_a[h], ((2 * h + mp) * 128, 128), 0),))
               for h in range(DA_HEADS) for mp in range(2)]
    o_a2 = _flash(proj, kv["a_k"], kv["a_v"], units=units_a,
                  q_spec=(512, BLK512["a_q"]), out_w=1024, tq=cfg["tq"], tk=cfg["tk"], name="diff_attn")
    o_a = _diff_final(o_a2, p["diff_lambda"], layer, p["diff_subln_g"][layer], lambda_init, tm=tm)

    topk = min(DSA_TOPK_MAX, s_len // 4)
    mask_b = _dsa_select(proj, kv["b_ik"], topk=topk)
    sl8 = _alibi(8)
    units_b = [(h, h * 128, 128, 0, ((h * 64, sl8[h], (h * 64, 64), 0),)) for h in range(DSA_HEADS)]
    o_b = _flash(proj, kv["b_k"], kv["b_v"],
                 units=units_b, q_spec=(512, BLK512["b_q"]), out_w=512, tq=cfg["tq"], tk=cfg["tk"],
                 mask=mask_b.reshape(1, s_len, s_len), name="dsa_attn")

    def gqa_units(masked):
        return [(g, g * 128, 128, g if masked else 0,
                 tuple(((g * 4 + r) * 64, sl8[g * 4 + r], ((g * 4 + r) * 64, 64), g * 4 + r)
                       for r in range(4)))
                for g in range(2)]

    def gqa_kv(k_name, v_name):
        return kv[k_name], kv[v_name]

    sinks = jnp.pad(p["swa_sinks"][layer].reshape(1, SWA_HEADS), ((0, 0), (0, LANES - SWA_HEADS)))
    o_c = _flash(proj, *gqa_kv("c_k", "c_v"), units=gqa_units(False), q_spec=(512, BLK512["c_q"]),
                 out_w=512, tq=cfg["tb"], tk=cfg["tb"], window=SWA_WINDOW, sinks=sinks, name="swa_attn")

    kv_cmp = _nsa_compress(_nsa_cmp_inputs(proj), p["nsa_cmp_pos"], p["nsa_cmp_w1"], p["nsa_cmp_w2"],
                           layer)
    o_cmp, mask_d = _nsa_cmp(proj, kv_cmp)
    o_slc = _flash(proj, *gqa_kv("d_ks", "d_vs"), units=gqa_units(True), q_spec=(512, BLK512["d_q"]),
                   out_w=512, tq=cfg["tq"], tk=cfg["tk"], mask=mask_d, name="nsa_slc_attn")
    o_win = _flash(proj, *gqa_kv("d_kw", "d_vw"), units=gqa_units(False), q_spec=(512, BLK512["d_q"]),
                   out_w=512, tq=cfg["tw"], tk=cfg["tw"], window=NSA_WINDOW, name="nsa_win_attn")
    o_d = _nsa_combine(proj, o_cmp, o_slc, o_win, tm=tm)

    merged = _merge(u, (o_a, o_b, o_c, o_d), p["w_gate"], p["w_branch"], layer,
                    tm=cfg["tm_merge"], tn=256)
    return _mm(merged, p["w_o"], layer, tm=tm, tn=512, name="out_proj")


def _config(s_len):
    return dict(tm=min(1024, s_len), tm_merge=min(512, s_len), tm_ln=min(512, s_len),
                tq=min(256, s_len), tk=min(1024, s_len), tb=min(256, s_len), tw=min(512, s_len))


def kernel(x, c, cond_w, cond_b, w_in, diff_lambda, diff_subln_g, dsa_kv_norm_g, dsa_w_uk, dsa_w_uv,
           swa_sinks, nsa_cmp_pos, nsa_cmp_w1, nsa_cmp_w2, w_branch, w_gate, w_o,
           ln1_g, ln1_b, ln2_g, ln2_b, ffn_w_gate, ffn_w_up, ffn_w_down,
           moe_router, moe_w_gate, moe_w_up, moe_w_down):
    bsz, s_len, d = x.shape
    assert bsz == 1 and d == D_MODEL
    depth = cond_w.shape[0]
    cfg = _config(s_len)
    xs = x.reshape(s_len, d)
    c8 = jnp.broadcast_to(c.reshape(1, d), (8, d))
    p = dict(w_in=_permute_w_in(w_in), diff_lambda=diff_lambda, diff_subln_g=diff_subln_g,
             dsa_kv_norm_g=dsa_kv_norm_g, dsa_w_ukv=jnp.concatenate([dsa_w_uk, dsa_w_uv], axis=2),
             swa_sinks=swa_sinks,
             nsa_cmp_pos=nsa_cmp_pos.reshape(depth, 2, 1, NSA_CMP_LEN * HEAD_DIM),
             nsa_cmp_w1=nsa_cmp_w1, nsa_cmp_w2=nsa_cmp_w2, w_branch=w_branch, w_gate=w_gate, w_o=w_o)
    router_p = jnp.pad(moe_router, ((0, 0), (0, 0), (0, LANES - N_EXPERTS)))
    mods = [_mm(c8, cond_w, l, tm=8, tn=512, prologue="silu", bias=cond_b[l], name="cond")[0:1]
            for l in range(depth)]
    u = _modulate(xs, mods[0], 1, 0, tm=cfg["tm_ln"])
    for l in range(depth):
        y = _token_mixers(u, l, p, cfg)
        xs, u = _resid_ln(xs, y, mods[l], 2, ln1_g[l], ln1_b[l], mods[l], 4, 3, tm=cfg["tm_ln"])
        jx = l // 2
        if l % 2 == 0:
            hdn = _swiglu_up(u, ffn_w_gate, ffn_w_up, jx, tm=cfg["tm"], tn=512, name="ffn_up")
            y = _mmk(hdn, ffn_w_down, jx, tm=cfg["tm"], tn=d, tk=512, name="ffn_down")
        else:
            rt = _router(u, router_p, jx, tm=cfg["tm"])
            plan = _moe_plan(rt[:, 0:2], rt[:, 2:4], s_len)
            hdn = _gmm(_moe_gather(u, plan), (moe_w_gate, moe_w_up), jx, plan, tn=512,
                       row_w=plan["row_w"], name="moe_up")
            y = _moe_combine(_gmm(hdn, (moe_w_down,), jx, plan, tn=512, name="moe_down"), plan, s_len)
        nxt = min(l + 1, depth - 1)
        xs, u = _resid_ln(xs, y, mods[l], 5, ln2_g[l], ln2_b[l], mods[nxt], 1, 0, tm=cfg["tm_ln"])
    return xs.reshape(bsz, s_len, d)
```

```python
import functools
import math

import numpy as np
import jax
import jax.numpy as jnp
from jax import lax
from jax.experimental import pallas as pl
from jax.experimental.pallas import tpu as pltpu

F32 = jnp.float32
BF16 = jnp.bfloat16
NEG = -1e30

D_MODEL = 2048
DEPTH = 4
HEAD_DIM = 64
DA_HEADS = 4
DSA_HEADS = 8
DSA_KV_RANK = 128
IDX_HEADS = 8
DSA_TOPK_MAX = 256
SWA_HEADS = 8
SWA_WINDOW = 128
NSA_HEADS = 8
NSA_GROUPS = 2
NSA_CMP_LEN = 32
NSA_CMP_STRIDE = 16
NSA_CMP_HID = 256
NSA_SLC_LEN = 64
NSA_TOPN = 16
NSA_WINDOW = 512
NSA_FORCE = 1e9
N_BRANCH = 4
BRANCH_W = 512
N_EXPERTS = 8
ALPHA = (2.0 * DEPTH) ** 0.25

VMEM_LIMIT_BYTES = 56 * 1024 * 1024
LANES = 128

_ORIG = dict(a_q=(0, 512), a_k=(512, 512), a_v=(1024, 512), b_q=(1536, 512), b_kv=(2048, 128),
             b_iq=(2176, 512), b_ik=(2688, 64), b_iw=(2752, 8), c_q=(2760, 512), c_k=(3272, 128),
             c_v=(3400, 128), d_q=(3528, 512), d_kv=(4040, 768), d_g=(4808, 24))
_NEW_ORDER = ("a_q", "a_k", "a_v", "b_q", "b_iq", "c_q", "d_q", "b_kv", "c_k", "c_v", "d_kv",
              "b_ik", "pad64", "b_iw", "d_g", "pad96", "pad128")
PROJ_W = 5120
BLK512 = dict(a_q=0, a_k=1, a_v=2, b_q=3, b_iq=4, c_q=5, d_q=6)
BLK128 = dict(b_kv=28, c_k=29, c_v=30, d_kc=31, d_vc=32, d_ks=33, d_vs=34, d_kw=35, d_vw=36,
              b_ik=37, small=38)
COL_DKV = 3968


def _cparams(sem):
    return pltpu.CompilerParams(dimension_semantics=sem, vmem_limit_bytes=VMEM_LIMIT_BYTES)


def _sigmoid(x):
    return 1.0 / (1.0 + jnp.exp(-x))


def _silu(x):
    return x * _sigmoid(x)


def _alibi(n_heads):
    return [2.0 ** (-8.0 * (h + 1) / n_heads) for h in range(n_heads)]


def _dot(a, b):
    return jnp.dot(a, b, preferred_element_type=F32)


def _dot_nt(a, b):
    return lax.dot_general(a, b, (((1,), (1,)), ((), ())), preferred_element_type=F32)


def _mm_kernel(*refs, prologue, has_bias, eps):
    it = iter(refs)
    a_ref = next(it)
    g_ref = next(it) if prologue == "rms" else None
    w_ref = next(it)
    b_ref = next(it) if has_bias else None
    o_ref = next(it)
    wb_ref = next(it)

    @pl.when(pl.program_id(1) == 0)
    def _():
        wb_ref[...] = w_ref[0].astype(BF16)

    a = a_ref[...]
    if prologue == "silu":
        a = _silu(a.astype(F32))
    elif prologue == "rms":
        a = a.astype(F32)
        a = a * lax.rsqrt(jnp.mean(a * a, axis=-1, keepdims=True) + eps) * g_ref[...]
    acc = _dot(a.astype(BF16), wb_ref[...])
    if has_bias:
        acc = acc + b_ref[...]
    o_ref[...] = acc.astype(o_ref.dtype)


def _mm(a, w, layer, *, tm, tn, out_dtype=F32, a_blk=0, k=None, prologue=None, gain=None, bias=None,
        eps=1e-6, name="mm"):
    m = a.shape[0]
    k = a.shape[1] if k is None else k
    n = w.shape[2]
    assert w.shape[1] == k and m % tm == 0 and n % tn == 0
    in_specs = [pl.BlockSpec((tm, k), lambda j, i: (i, a_blk))]
    args = [a]
    if prologue == "rms":
        in_specs.append(pl.BlockSpec((1, k), lambda j, i: (0, 0)))
        args.append(gain.reshape(1, k))
    in_specs.append(pl.BlockSpec((1, k, tn), lambda j, i: (layer, 0, j)))
    args.append(w)
    if bias is not None:
        in_specs.append(pl.BlockSpec((1, tn), lambda j, i: (0, j)))
        args.append(bias.reshape(1, n))
    return pl.pallas_call(
        functools.partial(_mm_kernel, prologue=prologue, has_bias=bias is not None, eps=eps),
        grid=(n // tn, m // tm),
        in_specs=in_specs,
        out_specs=pl.BlockSpec((tm, tn), lambda j, i: (i, j)),
        out_shape=jax.ShapeDtypeStruct((m, n), out_dtype),
        scratch_shapes=[pltpu.VMEM((k, tn), BF16)],
        compiler_params=_cparams(("arbitrary", "arbitrary")),
        name=name,
    )(*args)


def _mmk_kernel(a_ref, w_ref, o_ref, acc_ref, *, nk):
    kk = pl.program_id(2)

    @pl.when(kk == 0)
    def _():
        acc_ref[...] = jnp.zeros_like(acc_ref)

    acc_ref[...] += _dot(a_ref[...], w_ref[0].astype(BF16))

    @pl.when(kk == nk - 1)
    def _():
        o_ref[...] = acc_ref[...]


def _mmk(a, w, layer, *, tm, tn, tk, name="mmk"):
    m, k = a.shape
    n = w.shape[2]
    assert w.shape[1] == k and m % tm == 0 and n % tn == 0 and k % tk == 0
    nk = k // tk
    return pl.pallas_call(
        functools.partial(_mmk_kernel, nk=nk),
        grid=(m // tm, n // tn, nk),
        in_specs=[pl.BlockSpec((tm, tk), lambda i, j, kk: (i, kk)),
                  pl.BlockSpec((1, tk, tn), lambda i, j, kk: (layer, kk, j))],
        out_specs=pl.BlockSpec((tm, tn), lambda i, j, kk: (i, j)),
        out_shape=jax.ShapeDtypeStruct((m, n), F32),
        scratch_shapes=[pltpu.VMEM((tm, tn), F32)],
        compiler_params=_cparams(("arbitrary", "arbitrary", "arbitrary")),
        name=name,
    )(a, w)


def _swiglu_kernel(a_ref, wg_ref, wu_ref, o_ref, wgb_ref, wub_ref):
    @pl.when(pl.program_id(1) == 0)
    def _():
        wgb_ref[...] = wg_ref[0].astype(BF16)
        wub_ref[...] = wu_ref[0].astype(BF16)

    a = a_ref[...]
    o_ref[...] = (_silu(_dot(a, wgb_ref[...])) * _dot(a, wub_ref[...])).astype(o_ref.dtype)


def _swiglu_up(u, wg, wu, layer, *, tm, tn, name="swiglu_up"):
    m, k = u.shape
    f = wg.shape[2]
    assert f % tn == 0 and m % tm == 0
    wspec = pl.BlockSpec((1, k, tn), lambda j, i: (layer, 0, j))
    return pl.pallas_call(
        _swiglu_kernel,
        grid=(f // tn, m // tm),
        in_specs=[pl.BlockSpec((tm, k), lambda j, i: (i, 0)), wspec, wspec],
        out_specs=pl.BlockSpec((tm, tn), lambda j, i: (i, j)),
        out_shape=jax.ShapeDtypeStruct((m, f), BF16),
        scratch_shapes=[pltpu.VMEM((k, tn), BF16), pltpu.VMEM((k, tn), BF16)],
        compiler_params=_cparams(("arbitrary", "arbitrary")),
        name=name,
    )(u, wg, wu)


def _modulate_kernel(x_ref, sc_ref, sh_ref, u_ref):
    u_ref[...] = (x_ref[...] * (1.0 + sc_ref[...]) + sh_ref[...]).astype(u_ref.dtype)


def _modulate(x, mod, sc_blk, sh_blk, *, tm):
    m, d = x.shape
    return pl.pallas_call(
        _modulate_kernel,
        grid=(m // tm,),
        in_specs=[pl.BlockSpec((tm, d), lambda i: (i, 0)),
                  pl.BlockSpec((1, d), lambda i: (0, sc_blk)),
                  pl.BlockSpec((1, d), lambda i: (0, sh_blk))],
        out_specs=pl.BlockSpec((tm, d), lambda i: (i, 0)),
        out_shape=jax.ShapeDtypeStruct((m, d), BF16),
        compiler_params=_cparams(("arbitrary",)),
        name="modulate",
    )(x, mod, mod)


def _resid_ln_kernel(x_ref, y_ref, gate_ref, g_ref, b_ref, sc_ref, sh_ref, xo_ref, u_ref):
    z = ALPHA * x_ref[...] + gate_ref[...] * y_ref[...]
    mu = jnp.mean(z, axis=-1, keepdims=True)
    zc = z - mu
    var = jnp.mean(zc * zc, axis=-1, keepdims=True)
    xn = zc * lax.rsqrt(var + 1e-5) * g_ref[...] + b_ref[...]
    xo_ref[...] = xn
    u_ref[...] = (xn * (1.0 + sc_ref[...]) + sh_ref[...]).astype(u_ref.dtype)


def _resid_ln(x, y, mod, gate_blk, g, b, mod_next, sc_blk, sh_blk, *, tm):
    m, d = x.shape
    row = lambda blk: pl.BlockSpec((1, d), lambda i: (0, blk))
    return pl.pallas_call(
        _resid_ln_kernel,
        grid=(m // tm,),
        in_specs=[pl.BlockSpec((tm, d), lambda i: (i, 0)),
                  pl.BlockSpec((tm, d), lambda i: (i, 0)),
                  row(gate_blk), row(0), row(0), row(sc_blk), row(sh_blk)],
        out_specs=[pl.BlockSpec((tm, d), lambda i: (i, 0)),
                   pl.BlockSpec((tm, d), lambda i: (i, 0))],
        out_shape=[jax.ShapeDtypeStruct((m, d), F32), jax.ShapeDtypeStruct((m, d), BF16)],
        compiler_params=_cparams(("arbitrary",)),
        name="resid_ln",
    )(x, y, mod, g.reshape(1, d), b.reshape(1, d), mod_next, mod_next)


FLASH_ROW_CHUNK = 32
POS_SPLIT = 128


_KV_PACK = (("a_k", "a_k", 0, 2 * DA_HEADS, 64, "k"), ("a_v", "a_v", 0, DA_HEADS, 128, "v"),
            ("b_k", "kv_b", 0, DSA_HEADS, 64, "k"), ("b_v", "kv_b", 512, DSA_HEADS, 64, "v"),
            ("c_k", "c_k", 0, 2, 64, "k"), ("c_v", "c_v", 0, 2, 64, "v"),
            ("d_ks", "d_ks", 0, 2, 64, "k"), ("d_vs", "d_vs", 0, 2, 64, "v"),
            ("d_kw", "d_kw", 0, 2, 64, "k"), ("d_vw", "d_vw", 0, 2, 64, "v"),
            ("b_ik", "b_ik", 0, 1, 128, "cast"))
_KV_SOURCES = ("a_k", "a_v", "c_k", "c_v", "d_ks", "d_vs", "d_kw", "d_vw", "b_ik", "kv_b")


def _kv_pack_kernel(*refs, tm):
    src = dict(zip(_KV_SOURCES, refs[:len(_KV_SOURCES)]))
    outs = refs[len(_KV_SOURCES):]
    i = pl.program_id(0)
    tails = {}
    for w in (64, 128):
        lane = lax.broadcasted_iota(jnp.int32, (tm, w), 1)
        pos = i * tm + lax.broadcasted_iota(jnp.int32, (tm, w), 0)
        tails[("v", w)] = jnp.where(lane == 0, 1.0, 0.0)
        tails[("k", w)] = jnp.where(lane == 0, (pos // POS_SPLIT).astype(F32),
                                    jnp.where(lane == 1, (pos % POS_SPLIT).astype(F32),
                                              jnp.where(lane < 4, 1.0, 0.0)))
    for (_, sname, c0, n_heads, w, kind), o_ref in zip(_KV_PACK, outs):
        if kind == "cast":
            o_ref[...] = src[sname][...].astype(BF16)
            continue
        for h in range(n_heads):
            x = src[sname][:, c0 + h * w:c0 + (h + 1) * w]
            o_ref[:, 2 * h * w:2 * (h + 1) * w] = jnp.concatenate([x, tails[(kind, w)]], axis=1).astype(BF16)


def _kv_pack(proj, kv_b, *, tm):
    s_len = proj.shape[0]
    in_specs, args = [], []
    for sname in _KV_SOURCES:
        if sname == "kv_b":
            in_specs.append(pl.BlockSpec((tm, kv_b.shape[1]), lambda i: (i, 0)))
            args.append(kv_b)
        else:
            wblk, blk = (512, BLK512[sname]) if sname in BLK512 else (128, BLK128[sname])
            in_specs.append(pl.BlockSpec((tm, wblk), functools.partial(lambda i, blk: (i, blk), blk=blk)))
            args.append(proj)
    widths = [(1 if kind == "cast" else 2) * n_heads * w for (_, _, _, n_heads, w, kind) in _KV_PACK]
    outs = pl.pallas_call(
        functools.partial(_kv_pack_kernel, tm=tm),
        grid=(s_len // tm,),
        in_specs=in_specs,
        out_specs=[pl.BlockSpec((tm, wd), lambda i: (i, 0)) for wd in widths],
        out_shape=[jax.ShapeDtypeStruct((s_len, wd), BF16) for wd in widths],
        compiler_params=_cparams(("arbitrary",)),
        name="kv_pack",
    )(*args)
    return {name: o for (name, *_), o in zip(_KV_PACK, outs)}


def _flash_kernel(qi_ref, kb_ref, first_ref, last_ref, *refs, units, tq, tk, window, dense, n_mask,
                  has_sink):
    it = iter(refs)
    q_ref = next(it)
    kp_ref = None if dense else next(it)
    k_ref = next(it)
    vp_ref = None if dense else next(it)
    v_ref = next(it)
    mask_ref = next(it) if n_mask else None
    sink_ref = next(it) if has_sink else None
    o_ref = next(it)
    q_scr, m_scr, acc_scr, bias_scr = (next(it) for _ in range(4))
    dv = acc_scr.shape[2]
    lcol = dv // 2

    w = pl.program_id(0)
    qi = qi_ref[w]
    kb = kb_ref[w]
    rows = q_scr.shape[1]
    rb = FLASH_ROW_CHUNK
    kw = 2 * HEAD_DIM

    @pl.when(first_ref[w] > 0)
    def _init():
        lane = lax.broadcasted_iota(jnp.int32, (tq, HEAD_DIM), 1)
        qpos = qi * tq + lax.broadcasted_iota(jnp.int32, (tq, HEAD_DIM), 0)
        qhi = (qpos // POS_SPLIT).astype(F32)
        qlo = (qpos % POS_SPLIT).astype(F32)
        for ui, (_, _, _, _, hds) in enumerate(units):
            for r, (qo, slope, _, sink_idx) in enumerate(hds):
                rsl = slice(r * tq, (r + 1) * tq)
                tail = jnp.where(lane == 0, POS_SPLIT * slope,
                                 jnp.where(lane == 1, slope,
                                           jnp.where(lane == 2, -POS_SPLIT * slope * qhi,
                                                     jnp.where(lane == 3, -slope * qlo, 0.0))))
                qs = q_ref[:, qo:qo + HEAD_DIM] * HEAD_DIM ** -0.5
                q_scr[ui, rsl] = jnp.concatenate([qs, tail], axis=1).astype(BF16)
                if has_sink:
                    m_scr[ui, rsl] = jnp.broadcast_to(sink_ref[:, sink_idx:sink_idx + 1], (tq, 1))
                else:
                    m_scr[ui, rsl] = jnp.full((tq, 1), NEG, F32)
            alane = lax.broadcasted_iota(jnp.int32, acc_scr.shape[1:], 1)
            acc_scr[ui] = jnp.where(alane == lcol, 1.0 if has_sink else 0.0, 0.0)

    def step(masked, kt):
        def tile(ref, prev_ref, lo, hi):
            if dense:
                return ref[0:kt, lo:hi]
            return jnp.concatenate([prev_ref[:, lo:hi], ref[:, lo:hi]], axis=0)

        def scores(ui):
            ku = units[ui][0]
            return _dot_nt(q_scr[ui], tile(k_ref, kp_ref, ku * kw, (ku + 1) * kw))

        if masked:
            qpos = qi * tq + lax.broadcasted_iota(jnp.int32, (tq, kt), 0)
            kpos = (kb * tk if dense else (qi - 1) * tq) + lax.broadcasted_iota(jnp.int32, (tq, kt), 1)
            dist = qpos - kpos
            valid = dist >= 0
            if not dense:
                valid = valid & (dist < window) & (kpos >= 0)
            if n_mask:
                for g in range(n_mask):
                    bias_scr[g, :, 0:kt] = jnp.where(valid, mask_ref[g, :, 0:kt].astype(F32), NEG)
            else:
                bias_scr[0, :, 0:kt] = jnp.where(valid, 0.0, NEG)

        def chunk(s, mg, c):
            r0 = c * rb
            sc = s[r0:r0 + rb]
            if masked:
                rw = r0 % tq
                sc = sc + bias_scr[mg, rw:rw + rb, 0:kt]
            return sc

        s_next = scores(0)
        for ui, (_, vo, _, mg, _) in enumerate(units):
            s = s_next
            if ui + 1 < len(units):
                s_next = scores(ui + 1)
            nchunk = rows // rb
            m_old = m_scr[ui]
            m_cur = jnp.concatenate([jnp.max(chunk(s, mg, c), axis=1, keepdims=True) for c in range(nchunk)],
                                    axis=0)
            m_new = jnp.maximum(m_old, m_cur)
            alpha = jnp.exp(m_old - m_new)
            m_scr[ui] = m_new
            p_all = jnp.concatenate(
                [jnp.exp(chunk(s, mg, c) - m_new[c * rb:(c + 1) * rb]).astype(BF16) for c in range(nchunk)],
                axis=0)
            acc_scr[ui] = alpha * acc_scr[ui] + _dot(p_all, tile(v_ref, vp_ref, vo, vo + dv))

    if dense:
        assert tk % tq == 0
        sub = tk // tq
        diag = last_ref[w] > 0
        part = qi % sub
        if n_mask:
            pl.when(jnp.logical_not(diag) | (part == sub - 1))(lambda: step(True, tk))
        else:
            pl.when(jnp.logical_not(diag))(lambda: step(False, tk))
            pl.when(diag & (part == sub - 1))(lambda: step(True, tk))
        for c in range(sub - 1):
            pl.when(diag & (part == c))(functools.partial(step, True, (c + 1) * tq))
    else:
        step(True, 2 * tq)

    @pl.when(last_ref[w] > 0)
    def _fin():
        for ui, (_, _, _, _, hds) in enumerate(units):
            for r, (_, _, (oo, ow), _) in enumerate(hds):
                rsl = slice(r * tq, (r + 1) * tq)
                acc = acc_scr[ui, rsl]
                o_ref[:, oo:oo + ow] = acc[:, 0:ow] / acc[:, lcol:lcol + 1]


def _flash(q_arr, k_arr, v_arr, *, units, q_spec, out_w, tq, tk, window=None,
           mask=None, sinks=None, name="flash"):
    s_len = q_arr.shape[0]
    dense = window is None
    n_mask = 0 if mask is None else mask.shape[0]
    if not dense:
        assert tq == tk and window <= tq + 1 and not n_mask
    dv = units[0][2]
    nu = len(units)
    rows = len(units[0][4]) * tq
    assert all(len(un[4]) * tq == rows and un[2] == dv for un in units) and rows % FLASH_ROW_CHUNK == 0

    pairs = []
    for qi in range(s_len // tq):
        last_kb = (qi * tq + tq - 1) // tk
        first_kb = 0 if dense else last_kb
        pairs += [(qi, kb, int(kb == first_kb), int(kb == last_kb)) for kb in range(first_kb, last_kb + 1)]
    tables = [jnp.asarray(np.array(col, np.int32)) for col in zip(*pairs)]

    def kv_specs(arr):
        cur = pl.BlockSpec((tk, arr.shape[1]), lambda w, qi, kb, fi, la: (kb[w], 0))
        if dense:
            return [cur], [arr]
        prev = pl.BlockSpec((tk, arr.shape[1]), lambda w, qi, kb, fi, la: (jnp.maximum(kb[w] - 1, 0), 0))
        return [prev, cur], [arr, arr]

    in_specs = [pl.BlockSpec((tq, q_spec[0]), lambda w, qi, kb, fi, la: (qi[w], q_spec[1]))]
    args = [q_arr]
    for arr in (k_arr, v_arr):
        specs, arrs = kv_specs(arr)
        in_specs += specs
        args += arrs
    if n_mask:
        in_specs.append(pl.BlockSpec((n_mask, tq, tk), lambda w, qi, kb, fi, la: (0, qi[w], kb[w])))
        args.append(mask)
    if sinks is not None:
        in_specs.append(pl.BlockSpec((1, LANES), lambda w, qi, kb, fi, la: (0, 0)))
        args.append(sinks)
    return pl.pallas_call(
        functools.partial(_flash_kernel, units=tuple(units), tq=tq, tk=tk, window=window, dense=dense,
                          n_mask=n_mask, has_sink=sinks is not None),
        grid_spec=pltpu.PrefetchScalarGridSpec(
            num_scalar_prefetch=4, grid=(len(pairs),),
            in_specs=in_specs,
            out_specs=pl.BlockSpec((tq, out_w), lambda w, qi, kb, fi, la: (qi[w], 0)),
            scratch_shapes=[pltpu.VMEM((nu, rows, 2 * HEAD_DIM), BF16), pltpu.VMEM((nu, rows, 1), F32),
                            pltpu.VMEM((nu, rows, dv), F32),
                            pltpu.VMEM((max(n_mask, 1), tq, tk if dense else 2 * tq), F32)]),
        out_shape=jax.ShapeDtypeStruct((s_len, out_w), F32),
        compiler_params=_cparams(("arbitrary",)),
        name=name,
    )(*tables, *args)


def _diff_final_kernel(o_ref, lam_ref, g_ref, out_ref, *, lambda_init):
    lf = lam_ref[0]
    lam = (jnp.exp(jnp.sum(lf[0:1] * lf[1:2])) - jnp.exp(jnp.sum(lf[2:3] * lf[3:4])) + lambda_init)
    w = 2 * HEAD_DIM
    for h in range(DA_HEADS):
        o = o_ref[:, (2 * h) * w:(2 * h + 1) * w] - lam * o_ref[:, (2 * h + 1) * w:(2 * h + 2) * w]
        o = o * lax.rsqrt(jnp.mean(o * o, axis=-1, keepdims=True) + 1e-6) * g_ref[...]
        out_ref[:, h * w:(h + 1) * w] = o * (1.0 - lambda_init)


def _diff_final(o, diff_lambda, layer, subln_g, lambda_init, *, tm):
    m = o.shape[0]
    w = 2 * HEAD_DIM
    return pl.pallas_call(
        functools.partial(_diff_final_kernel, lambda_init=lambda_init),
        grid=(m // tm,),
        in_specs=[pl.BlockSpec((tm, 2 * DA_HEADS * w), lambda i: (i, 0)),
                  pl.BlockSpec((1, 4, HEAD_DIM), lambda i: (layer, 0, 0)),
                  pl.BlockSpec((1, w), lambda i: (0, 0))],
        out_specs=pl.BlockSpec((tm, DA_HEADS * w), lambda i: (i, 0)),
        out_shape=jax.ShapeDtypeStruct((m, DA_HEADS * w), F32),
        compiler_params=_cparams(("arbitrary",)),
        name="diff_final",
    )(o, diff_lambda, subln_g.reshape(1, w))


def _f32_key_const(x):
    b = int(np.array(x, np.float32).view(np.int32))
    return b ^ ((b >> 31) & 0x7FFFFFFF)


I16_MIN = -(2 ** 15)


def _dsa_select_kernel(qi_ref, w_ref, kidx_ref, mask_ref, key_scr, half_scr, j_scr, *, tq, ch, nch, topk,
                       s_len):
    i = pl.program_id(0)
    q0 = i * tq
    n_need = (q0 + tq + ch - 1) // ch
    qpos = q0 + lax.broadcasted_iota(jnp.int32, (tq, 1), 0)
    lane = lax.broadcasted_iota(jnp.int32, (1, ch), 1)
    w = w_ref[:, 0:IDX_HEADS]
    q_all = jnp.concatenate([qi_ref[:, h * HEAD_DIM:(h + 1) * HEAD_DIM] for h in range(IDX_HEADS)],
                            axis=0).astype(BF16)

    def score_chunk(c, carry):
        kc = kidx_ref[pl.ds(pl.multiple_of(c * ch, ch), ch), 0:HEAD_DIM].astype(BF16)
        lg = _dot_nt(q_all, kc)
        acc = jnp.zeros((tq, ch), F32)
        for h in range(IDX_HEADS):
            acc = acc + w[:, h:h + 1] * jnp.maximum(lg[h * tq:(h + 1) * tq], 0.0)
        acc = jnp.where(c * ch + lane <= qpos, acc, NEG) + 0.0
        bits = pltpu.bitcast(acc, jnp.int32)
        key = bits ^ ((bits >> 31) & 0x7FFFFFFF)
        key_scr[c] = key
        half_scr[c] = (key >> 16).astype(jnp.int16)
        return carry

    lax.fori_loop(0, n_need, score_chunk, 0)

    def count16(cand, strict):
        cand16 = jnp.broadcast_to(cand, (tq, LANES)).astype(jnp.int16)
        one, zero = jnp.int16(1), jnp.int16(0)

        def body(c, acc):
            blk = half_scr[c]
            for t in range(ch // LANES):
                tile = blk[:, t * LANES:(t + 1) * LANES]
                acc = acc + jnp.where(tile > cand16 if strict else tile >= cand16, one, zero)
            return acc
        acc = lax.fori_loop(0, n_need, body, jnp.zeros((tq, LANES), jnp.int16))
        return jnp.sum(acc.astype(jnp.int32), axis=1, keepdims=True)

    def search16(need_cnt):
        def bit_step(b, t):
            cand = t + jnp.left_shift(jnp.int32(1), 15 - b)
            return jnp.where(count16(cand, False) >= need_cnt, cand, t)
        return lax.fori_loop(0, 16, bit_step, jnp.full((tq, 1), I16_MIN, jnp.int32))

    t_hi = search16(topk)
    need_lo = topk - count16(t_hi, True)

    def low_chunk(c, carry):
        key = key_scr[c]
        low = (key & 0xFFFF) + I16_MIN
        half_scr[c] = jnp.where((key >> 16) == t_hi, low, I16_MIN).astype(jnp.int16)
        return carry

    lax.fori_loop(0, n_need, low_chunk, 0)
    t_lo = search16(need_lo)
    thr = jnp.left_shift(t_hi, 16) + (t_lo - I16_MIN)

    def count(pred):
        def body(c, acc):
            m = jnp.where(pred(key_scr[c], c), 1, 0)
            part = m[:, 0:LANES]
            for t in range(1, ch // LANES):
                part = part + m[:, t * LANES:(t + 1) * LANES]
            return acc + part
        acc = lax.fori_loop(0, n_need, body, jnp.zeros((tq, LANES), jnp.int32))
        return jnp.sum(acc, axis=1, keepdims=True)

    cnt_gt = count(lambda blk, c: blk > thr)
    cnt_ge = count(lambda blk, c: blk >= thr)
    need = topk - cnt_gt
    tie_rows = (cnt_ge > topk) & (thr > _f32_key_const(NEG))
    j_scr[...] = jnp.full((tq, 1), s_len, jnp.int32)
    any_tie = jnp.max(jnp.where(tie_rows, 1, 0)) > 0

    @pl.when(any_tie)
    def _ties():
        def tie_chunk(c, carry):
            half_scr[c] = jnp.where(key_scr[c] == thr, -1 - (c * ch + lane), I16_MIN).astype(jnp.int16)
            return carry

        lax.fori_loop(0, n_need, tie_chunk, 0)
        j_scr[...] = jnp.where(tie_rows, -1 - search16(need), s_len)

    jv = j_scr[...]
    for c in range(nch):
        @pl.when((c < n_need) & any_tie)
        def _w():
            key = key_scr[c]
            sel = (key > thr) | ((key == thr) & (c * ch + lane <= jv))
            mask_ref[:, c * ch:(c + 1) * ch] = jnp.where(sel, 0.0, NEG).astype(mask_ref.dtype)

        @pl.when((c < n_need) & jnp.logical_not(any_tie))
        def _wf():
            mask_ref[:, c * ch:(c + 1) * ch] = jnp.where(key_scr[c] >= thr, 0.0, NEG).astype(mask_ref.dtype)

        @pl.when(c >= n_need)
        def _z():
            mask_ref[:, c * ch:(c + 1) * ch] = jnp.full((tq, ch), NEG, mask_ref.dtype)


def _dsa_select(proj, kidx, *, topk, tq=128):
    s_len = proj.shape[0]
    assert s_len < -I16_MIN
    ch = min(1024, s_len)
    nch = s_len // ch
    return pl.pallas_call(
        functools.partial(_dsa_select_kernel, tq=tq, ch=ch, nch=nch, topk=topk, s_len=s_len),
        grid=(s_len // tq,),
        in_specs=[pl.BlockSpec((tq, 512), lambda i: (i, BLK512["b_iq"])),
                  pl.BlockSpec((tq, LANES), lambda i: (i, BLK128["small"])),
                  pl.BlockSpec((s_len, LANES), lambda i: (0, 0))],
        out_specs=pl.BlockSpec((tq, s_len), lambda i: (i, 0)),
        out_shape=jax.ShapeDtypeStruct((s_len, s_len), BF16),
        scratch_shapes=[pltpu.VMEM((nch, tq, ch), jnp.int32), pltpu.VMEM((nch, tq, ch), jnp.int16),
                        pltpu.VMEM((tq, 1), jnp.int32)],
        compiler_params=_cparams(("arbitrary",)),
        name="dsa_select",
    )(proj, proj, kidx)


def _nsa_compress_kernel(x_ref, pos_ref, w1_ref, w2_ref, o_ref):
    x = (x_ref[0] + pos_ref[0, 0]).astype(BF16)
    hdn = _silu(_dot(x, w1_ref[0, 0].astype(BF16)))
    o_ref[0] = _dot(hdn.astype(BF16), w2_ref[0, 0].astype(BF16))


def _nsa_compress(xc, pos, w1, w2, layer):
    _, ncp, kdim = xc.shape
    return pl.pallas_call(
        _nsa_compress_kernel,
        grid=(4,),
        in_specs=[pl.BlockSpec((1, ncp, kdim), lambda i: (i, 0, 0)),
                  pl.BlockSpec((1, 1, 1, kdim), lambda i: (layer, i // 2, 0, 0)),
                  pl.BlockSpec((1, 1, kdim, NSA_CMP_HID), lambda i: (layer, i // 2, 0, 0)),
                  pl.BlockSpec((1, 1, NSA_CMP_HID, HEAD_DIM), lambda i: (layer, i // 2, 0, 0))],
        out_specs=pl.BlockSpec((1, ncp, HEAD_DIM), lambda i: (i, 0, 0)),
        out_shape=jax.ShapeDtypeStruct((4, ncp, HEAD_DIM), F32),
        compiler_params=_cparams(("arbitrary",)),
        name="nsa_compress",
    )(xc, pos, w1, w2)


def _nsa_cmp_kernel(q_ref, kv_ref, ov_ref, ex_ref, o_ref, mask_ref, *, tq, ncp, n_slc, topn, ch, nch):
    i = pl.program_id(0)
    q0 = i * tq
    n_need = (q0 + tq + ch - 1) // ch
    rpg = NSA_HEADS // NSA_GROUPS
    slopes = _alibi(NSA_HEADS)
    scale = HEAD_DIM ** -0.5
    qpos_c = q0 + lax.broadcasted_iota(jnp.int32, (tq, ncp), 0)
    cend = lax.broadcasted_iota(jnp.int32, (tq, ncp), 1) * NSA_CMP_STRIDE + (NSA_CMP_LEN - 1)
    dist_c = qpos_c - cend
    valid_c = dist_c >= 0
    distf = dist_c.astype(F32)
    qpos = q0 + lax.broadcasted_iota(jnp.int32, (tq, n_slc), 0)
    blk = lax.broadcasted_iota(jnp.int32, (tq, n_slc), 1)
    cur = qpos // NSA_SLC_LEN
    forced = (blk == 0) | (blk == cur) | (blk == cur - 1)
    blk_ok = blk * NSA_SLC_LEN <= qpos
    ov = ov_ref[...]
    imps = []
    for g in range(NSA_GROUPS):
        kc = kv_ref[g].astype(BF16)
        vc = kv_ref[NSA_GROUPS + g].astype(BF16)
        psum = jnp.zeros((tq, ncp), F32)
        for r in range(rpg):
            h = g * rpg + r
            qh = q_ref[:, h * HEAD_DIM:(h + 1) * HEAD_DIM].astype(BF16)
            s = _dot_nt(qh, kc) * scale - slopes[h] * distf
            s = jnp.where(valid_c, s, NEG)
            e = jnp.where(valid_c, jnp.exp(s - jnp.max(s, axis=1, keepdims=True)), 0.0)
            p = e / jnp.maximum(jnp.sum(e, axis=1, keepdims=True), 1e-30)
            o_ref[:, h * HEAD_DIM:(h + 1) * HEAD_DIM] = _dot(p.astype(BF16), vc)
            psum = psum + p
        p_hi = psum.astype(BF16)
        p_lo = (psum - p_hi.astype(F32)).astype(BF16)
        imp = _dot(p_hi, ov) + _dot(p_lo, ov)
        imp = jnp.where(forced, NSA_FORCE, imp)
        imps.append(jnp.where(blk_ok, imp, NEG))
    imps = [imp.T for imp in imps]
    blk_t = lax.broadcasted_iota(jnp.int32, (n_slc, tq), 0)
    sels = [jnp.full((n_slc, tq), NEG, F32) for _ in range(NSA_GROUPS)]
    for _ in range(topn):
        for g in range(NSA_GROUPS):
            mx = jnp.max(imps[g], axis=0, keepdims=True)
            first = jnp.min(jnp.where(imps[g] == mx, blk_t, n_slc), axis=0, keepdims=True)
            hit = blk_t == first
            sels[g] = jnp.where(hit, 0.0, sels[g])
            imps[g] = jnp.where(hit, -jnp.inf, imps[g])
    sels = [sel.T for sel in sels]
    for g in range(NSA_GROUPS):
        selb = sels[g].astype(BF16)
        for c in range(nch):
            @pl.when(c < n_need)
            def _w():
                tok = _dot(selb, ex_ref[:, c * ch:(c + 1) * ch])
                mask_ref[g, :, c * ch:(c + 1) * ch] = tok.astype(mask_ref.dtype)

            @pl.when(c >= n_need)
            def _z():
                mask_ref[g, :, c * ch:(c + 1) * ch] = jnp.full((tq, ch), NEG, mask_ref.dtype)


def _nsa_cmp(proj, kv_cmp, *, tq=256):
    s_len = proj.shape[0]
    ncp = kv_cmp.shape[1]
    n_slc = s_len // NSA_SLC_LEN
    topn = min(NSA_TOPN, n_slc)
    ch = min(1024, s_len)
    nch = s_len // ch
    starts = np.arange(ncp) * NSA_CMP_STRIDE
    slc_start = np.arange(n_slc) * NSA_SLC_LEN
    overlap = ((starts[:, None] < slc_start[None, :] + NSA_SLC_LEN)
               & (starts[:, None] + NSA_CMP_LEN > slc_start[None, :])).astype(np.float32)
    expand = (np.arange(s_len)[None, :] // NSA_SLC_LEN == np.arange(n_slc)[:, None]).astype(np.float32)
    return pl.pallas_call(
        functools.partial(_nsa_cmp_kernel, tq=tq, ncp=ncp, n_slc=n_slc, topn=topn, ch=ch, nch=nch),
        grid=(s_len // tq,),
        in_specs=[pl.BlockSpec((tq, 512), lambda i: (i, BLK512["d_q"])),
                  pl.BlockSpec((4, ncp, HEAD_DIM), lambda i: (0, 0, 0)),
                  pl.BlockSpec((ncp, n_slc), lambda i: (0, 0)),
                  pl.BlockSpec((n_slc, s_len), lambda i: (0, 0))],
        out_specs=[pl.BlockSpec((tq, 512), lambda i: (i, 0)),
                   pl.BlockSpec((NSA_GROUPS, tq, s_len), lambda i: (0, i, 0))],
        out_shape=[jax.ShapeDtypeStruct((s_len, 512), F32),
                   jax.ShapeDtypeStruct((NSA_GROUPS, s_len, s_len), BF16)],
        compiler_params=_cparams(("arbitrary",)),
        name="nsa_cmp",
    )(proj, kv_cmp, jnp.asarray(overlap, BF16), jnp.asarray(expand, BF16))


def _nsa_combine_kernel(g_ref, oc_ref, os_ref, ow_ref, o_ref):
    gt = _sigmoid(g_ref[...])
    for h in range(NSA_HEADS):
        sl = slice(h * HEAD_DIM, (h + 1) * HEAD_DIM)
        c0 = IDX_HEADS + 3 * h
        o_ref[:, sl] = (gt[:, c0:c0 + 1] * oc_ref[:, sl] + gt[:, c0 + 1:c0 + 2] * os_ref[:, sl]
                        + gt[:, c0 + 2:c0 + 3] * ow_ref[:, sl])


def _nsa_combine(proj, o_cmp, o_slc, o_win, *, tm):
    m = proj.shape[0]
    spec = pl.BlockSpec((tm, 512), lambda i: (i, 0))
    return pl.pallas_call(
        _nsa_combine_kernel,
        grid=(m // tm,),
        in_specs=[pl.BlockSpec((tm, LANES), lambda i: (i, BLK128["small"])), spec, spec, spec],
        out_specs=spec,
        out_shape=jax.ShapeDtypeStruct((m, 512), F32),
        compiler_params=_cparams(("arbitrary",)),
        name="nsa_combine",
    )(proj, o_cmp, o_slc, o_win)


def _merge_kernel(u_ref, oa_ref, ob_ref, oc_ref, od_ref, wg0, wg1, wg2, wg3, wb_ref, o_ref,
                  wgb_ref, wbb_ref):
    wgs = (wg0, wg1, wg2, wg3)

    @pl.when(pl.program_id(1) == 0)
    def _():
        for mch in range(N_BRANCH):
            wgb_ref[mch] = wgs[mch][0].astype(BF16)
            wbb_ref[mch] = wb_ref[0, mch].astype(BF16)

    u = u_ref[...]
    acc = None
    for mch, o_ref_m in enumerate((oa_ref, ob_ref, oc_ref, od_ref)):
        gte = _sigmoid(_dot(u, wgb_ref[mch]))
        z = _dot(o_ref_m[...].astype(BF16), wbb_ref[mch])
        acc = gte * z if acc is None else acc + gte * z
    o_ref[...] = acc.astype(o_ref.dtype)


def _merge(u, branches, w_gate, w_branch, layer, *, tm, tn):
    m, d = u.shape
    nj = d // tn
    bspec = pl.BlockSpec((tm, BRANCH_W), lambda j, i: (i, 0))
    wg_specs = [pl.BlockSpec((1, d, tn),
                             functools.partial(lambda j, i, mch: (layer, 0, mch * nj + j), mch=mch))
                for mch in range(N_BRANCH)]
    return pl.pallas_call(
        _merge_kernel,
        grid=(nj, m // tm),
        in_specs=[pl.BlockSpec((tm, d), lambda j, i: (i, 0)), bspec, bspec, bspec, bspec,
                  *wg_specs,
                  pl.BlockSpec((1, N_BRANCH, BRANCH_W, tn), lambda j, i: (layer, 0, 0, j))],
        out_specs=pl.BlockSpec((tm, tn), lambda j, i: (i, j)),
        out_shape=jax.ShapeDtypeStruct((m, d), BF16),
        scratch_shapes=[pltpu.VMEM((N_BRANCH, d, tn), BF16), pltpu.VMEM((N_BRANCH, BRANCH_W, tn), BF16)],
        compiler_params=_cparams(("arbitrary", "arbitrary")),
        name="merge",
    )(u, *branches, w_gate, w_gate, w_gate, w_gate, w_branch)


def _router_kernel(u_ref, r_ref, o_ref):
    logits = _dot(u_ref[...], r_ref[0].astype(BF16))
    lane = lax.broadcasted_iota(jnp.int32, logits.shape, 1)
    lg = jnp.where(lane < N_EXPERTS, logits, -jnp.inf)
    m1 = jnp.max(lg, axis=1, keepdims=True)
    i1 = jnp.min(jnp.where(lg == m1, lane, LANES), axis=1, keepdims=True)
    lg2 = jnp.where(lane == i1, -jnp.inf, lg)
    m2 = jnp.max(lg2, axis=1, keepdims=True)
    i2 = jnp.min(jnp.where(lg2 == m2, lane, LANES), axis=1, keepdims=True)
    e2 = jnp.exp(m2 - m1)
    w1 = 1.0 / (1.0 + e2)
    w2 = e2 / (1.0 + e2)
    o_ref[...] = jnp.where(lane == 0, i1.astype(F32),
                           jnp.where(lane == 1, i2.astype(F32),
                                     jnp.where(lane == 2, w1, jnp.where(lane == 3, w2, 0.0))))


def _router(u, router_padded, layer, *, tm):
    m, d = u.shape
    return pl.pallas_call(
        _router_kernel,
        grid=(m // tm,),
        in_specs=[pl.BlockSpec((tm, d), lambda i: (i, 0)),
                  pl.BlockSpec((1, d, LANES), lambda i: (layer, 0, 0))],
        out_specs=pl.BlockSpec((tm, LANES), lambda i: (i, 0)),
        out_shape=jax.ShapeDtypeStruct((m, LANES), F32),
        compiler_params=_cparams(("arbitrary",)),
        name="router",
    )(u, router_padded)


MOE_GROUP_TILE = 512
MOE_ROW_TILE = 256
MOE_TOK_CHUNK = 256


def _moe_plan(ridx, rw, s_len):
    gm, tm, ct, n_e = MOE_GROUP_TILE, MOE_ROW_TILE, MOE_TOK_CHUNK, N_EXPERTS
    i32 = jnp.int32
    e_a = ridx.reshape(-1).astype(i32)
    oh = (e_a[:, None] == jnp.arange(n_e, dtype=i32)[None, :]).astype(i32)
    csum = jnp.cumsum(oh, axis=0)
    rank_a = jnp.sum((csum - oh) * oh, axis=1)
    ntile_e = (csum[-1] + gm - 1) // gm
    tile_end = jnp.cumsum(ntile_e)
    pos_a = jnp.take(tile_end - ntile_e, e_a) * gm + rank_a
    n_rows = 2 * s_len + n_e * gm
    n_tiles = n_rows // tm
    n_chunks = s_len // ct
    row_tok = jnp.full((n_rows,), -1, i32).at[pos_a].set(jnp.arange(2 * s_len, dtype=i32) // 2)
    row_w = jnp.zeros((n_rows,), F32).at[pos_a].set(rw.reshape(-1))
    tile_e = jnp.minimum(jnp.searchsorted(tile_end, jnp.arange(n_rows // gm, dtype=i32), side="right"),
                         n_e - 1).astype(i32)
    rt = row_tok.reshape(n_tiles, tm)
    lo = jnp.min(jnp.where(rt >= 0, rt, s_len - 1), axis=1) // ct
    hi = jnp.maximum(jnp.max(jnp.where(rt >= 0, rt, 0), axis=1) // ct, lo)
    n_i = hi - lo + 1
    end = jnp.cumsum(n_i)
    n_work = n_tiles + n_e * n_chunks
    w = jnp.arange(n_work, dtype=i32)
    wt = jnp.minimum(jnp.searchsorted(end, w, side="right"), n_tiles - 1).astype(i32)
    wc = jnp.clip(jnp.take(lo, wt) + w - jnp.take(end - n_i, wt), 0, n_chunks - 1).astype(i32)
    wa = ((w < end[-1]) & jnp.take(jnp.any(rt >= 0, axis=1), wt)).astype(i32)
    order = jnp.argsort(jnp.where(wa > 0, wc * n_tiles + wt, n_chunks * n_tiles + w))
    vc = jnp.where(wa > 0, wc, n_chunks - 1)[order]
    return dict(row_tok=row_tok, row_w=row_w, tile_e=tile_e, n_used=tile_end[n_e - 1:].astype(i32),
                n_tiles=n_tiles, n_work=n_work,
                gather=(wt, wc, wa), combine=(vc, wt[order], wa[order]))


def _moe_gather_kernel(wt_ref, wc_ref, wa_ref, tok_ref, u_ref, o_ref):
    w = pl.program_id(0)

    @pl.when((w == 0) | (wt_ref[jnp.maximum(w - 1, 0)] != wt_ref[w]))
    def _():
        o_ref[...] = jnp.zeros_like(o_ref)

    @pl.when(wa_ref[w] > 0)
    def _():
        ct = u_ref.shape[0]
        cols = wc_ref[w] * ct + lax.broadcasted_iota(jnp.int32, (1, ct), 1)
        onehot = jnp.where(tok_ref[...] == cols, 1.0, 0.0).astype(BF16)
        o_ref[...] += _dot(onehot, u_ref[...]).astype(o_ref.dtype)


def _moe_gather(u, plan):
    s_len, d = u.shape
    tm, ct = MOE_ROW_TILE, MOE_TOK_CHUNK
    n_rows = plan["row_tok"].shape[0]
    return pl.pallas_call(
        _moe_gather_kernel,
        grid_spec=pltpu.PrefetchScalarGridSpec(
            num_scalar_prefetch=3, grid=(plan["n_work"],),
            in_specs=[pl.BlockSpec((tm, 1), lambda w, wt, wc, wa: (wt[w], 0)),
                      pl.BlockSpec((ct, d), lambda w, wt, wc, wa: (wc[w], 0))],
            out_specs=pl.BlockSpec((tm, d), lambda w, wt, wc, wa: (wt[w], 0))),
        out_shape=jax.ShapeDtypeStruct((n_rows, d), BF16),
        compiler_params=_cparams(("arbitrary",)),
        name="moe_gather",
    )(*plan["gather"], plan["row_tok"].reshape(n_rows, 1), u)


def _moe_combine_kernel(vc_ref, vt_ref, va_ref, tok_ref, y0_ref, y1_ref, y2_ref, o_ref):
    w = pl.program_id(0)
    chunk = vc_ref[w]

    @pl.when((w == 0) | (vc_ref[jnp.maximum(w - 1, 0)] != chunk))
    def _():
        o_ref[...] = jnp.zeros_like(o_ref)

    @pl.when(va_ref[w] > 0)
    def _():
        ct = o_ref.shape[0]
        rows = chunk * ct + lax.broadcasted_iota(jnp.int32, (ct, 1), 0)
        onehot_t = jnp.where(rows == tok_ref[0], 1.0, 0.0).astype(BF16)
        o_ref[...] += (_dot(onehot_t, y0_ref[...]) + _dot(onehot_t, y1_ref[...])
                       + _dot(onehot_t, y2_ref[...]))


def _moe_combine(ys3, plan, s_len):
    n_rows, d = ys3[0].shape
    tm, ct = MOE_ROW_TILE, MOE_TOK_CHUNK
    yspec = pl.BlockSpec((tm, d), lambda w, vc, vt, va: (vt[w], 0))
    return pl.pallas_call(
        _moe_combine_kernel,
        grid_spec=pltpu.PrefetchScalarGridSpec(
            num_scalar_prefetch=3, grid=(plan["n_work"],),
            in_specs=[pl.BlockSpec((1, 1, tm), lambda w, vc, vt, va: (vt[w], 0, 0)), yspec, yspec, yspec],
            out_specs=pl.BlockSpec((ct, d), lambda w, vc, vt, va: (vc[w], 0))),
        out_shape=jax.ShapeDtypeStruct((s_len, d), F32),
        compiler_params=_cparams(("arbitrary",)),
        name="moe_combine",
    )(*plan["combine"], plan["row_tok"].reshape(plan["n_tiles"], 1, tm), *ys3)


def _gmm_kernel(te_ref, nu_ref, *refs, swiglu):
    it = iter(refs)
    a_ref = next(it)
    w_refs = [next(it), next(it)] if swiglu else [next(it)]
    rw_ref = next(it) if swiglu else None
    o_refs = [next(it)] if swiglu else [next(it), next(it), next(it)]
    wb_refs = [next(it) for _ in w_refs]
    i = pl.program_id(1)
    used = i < nu_ref[0]

    @pl.when(used & ((i == 0) | (te_ref[i] != te_ref[jnp.maximum(i - 1, 0)])))
    def _():
        for w_ref, wb_ref in zip(w_refs, wb_refs):
            wb_ref[...] = w_ref[0, 0].astype(BF16)

    @pl.when(used)
    def _():
        a = a_ref[...]
        if swiglu:
            h = _silu(_dot(a, wb_refs[0][...])) * _dot(a, wb_refs[1][...]) * rw_ref[...]
            o_refs[0][...] = h.astype(BF16)
        else:
            y = _dot(a, wb_refs[0][...])
            hi = y.astype(BF16)
            r1 = y - hi.astype(F32)
            mid = r1.astype(BF16)
            o_refs[0][...] = hi
            o_refs[1][...] = mid
            o_refs[2][...] = (r1 - mid.astype(F32)).astype(BF16)

    @pl.when(jnp.logical_not(used))
    def _():
        for o_ref in o_refs:
            o_ref[...] = jnp.zeros_like(o_ref)


def _gmm(a, ws, layer, plan, *, tn, row_w=None, name="gmm"):
    n_rows, k = a.shape
    n = ws[0].shape[3]
    tm = MOE_GROUP_TILE
    swiglu = len(ws) == 2
    wspec = pl.BlockSpec((1, 1, k, tn), lambda j, i, te, nu: (layer, te[i], 0, j))
    in_specs = [pl.BlockSpec((tm, k), lambda j, i, te, nu: (i, 0))] + [wspec] * len(ws)
    args = [a, *ws]
    if swiglu:
        in_specs.append(pl.BlockSpec((tm, 1), lambda j, i, te, nu: (i, 0)))
        args.append(row_w.reshape(n_rows, 1))
    ospec = pl.BlockSpec((tm, tn), lambda j, i, te, nu: (i, j))
    oshape = jax.ShapeDtypeStruct((n_rows, n), BF16)
    return pl.pallas_call(
        functools.partial(_gmm_kernel, swiglu=swiglu),
        grid_spec=pltpu.PrefetchScalarGridSpec(
            num_scalar_prefetch=2, grid=(n // tn, n_rows // tm),
            in_specs=in_specs,
            out_specs=ospec if swiglu else [ospec] * 3,
            scratch_shapes=[pltpu.VMEM((k, tn), BF16) for _ in ws]),
        out_shape=oshape if swiglu else [oshape] * 3,
        compiler_params=_cparams(("arbitrary", "arbitrary")),
        name=name,
    )(plan["tile_e"], plan["n_used"], *args)


def _permute_w_in(w):
    cols = []
    for nm in _NEW_ORDER:
        if nm.startswith("pad"):
            cols.append(jnp.zeros(w.shape[:2] + (int(nm[3:]),), w.dtype))
        else:
            o, n = _ORIG[nm]
            cols.append(w[:, :, o:o + n])
    out = jnp.concatenate(cols, axis=2)
    assert out.shape[2] == PROJ_W
    return out


def _nsa_cmp_inputs(proj):
    s_len = proj.shape[0]
    n_cmp = (s_len - NSA_CMP_LEN) // NSA_CMP_STRIDE + 1
    ncp = s_len // NSA_CMP_STRIDE
    xs = []
    for jj in range(2):
        for g in range(NSA_GROUPS):
            c0 = COL_DKV + jj * 128 + g * HEAD_DIM
            r = proj[:, c0:c0 + HEAD_DIM].reshape(ncp, NSA_CMP_STRIDE * HEAD_DIM)
            x = jnp.concatenate([r[:-1], r[1:]], axis=1)
            xs.append(jnp.pad(x, ((0, ncp - n_cmp), (0, 0))))
    return jnp.stack(xs)


def _token_mixers(u, layer, p, cfg):
    s_len = u.shape[0]
    tm = cfg["tm"]
    proj = _mm(u, p["w_in"], layer, tm=tm, tn=512, name="in_proj")
    kv_b = _mm(proj, p["dsa_w_ukv"], layer, tm=tm, tn=512, a_blk=BLK128["b_kv"], k=DSA_KV_RANK,
               prologue="rms", gain=p["dsa_kv_norm_g"][layer], name="dsa_kv")
    kv = _kv_pack(proj, kv_b, tm=cfg["tm_ln"])

    lambda_init = 0.8 - 0.6 * math.exp(-0.3 * layer)
    sl_a = _alibi(DA_HEADS)
    units_a = [(2 * h + mp, h * 256, 256, 0,
                ((h * 128 + mp * 64, sl_a[h], ((2 * h + mp) * 128, 128), 0),))
               for h in range(DA_HEADS) for mp in range(2)]
    o_a2 = _flash(proj, kv["a_k"], kv["a_v"], units=units_a,
                  q_spec=(512, BLK512["a_q"]), out_w=1024, tq=cfg["tq"], tk=cfg["tk"], name="diff_attn")
    o_a = _diff_final(o_a2, p["diff_lambda"], layer, p["diff_subln_g"][layer], lambda_init, tm=tm)

    topk = min(DSA_TOPK_MAX, s_len // 4)
    mask_b = _dsa_select(proj, kv["b_ik"], topk=topk)
    sl8 = _alibi(8)
    units_b = [(h, h * 128, 128, 0, ((h * 64, sl8[h], (h * 64, 64), 0),)) for h in range(DSA_HEADS)]
    o_b = _flash(proj, kv["b_k"], kv["b_v"],
                 units=units_b, q_spec=(512, BLK512["b_q"]), out_w=512, tq=cfg["tq"], tk=cfg["tk"],
                 mask=mask_b.reshape(1, s_len, s_len), name="dsa_attn")

    def gqa_units(masked):
        return [(g, g * 128, 128, g if masked else 0,
                 tuple(((g * 4 + r) * 64, sl8[g * 4 + r], ((g * 4 + r) * 64, 64), g * 4 + r)
                       for r in range(4)))
                for g in range(2)]

    def gqa_kv(k_name, v_name):
        return kv[k_name], kv[v_name]

    sinks = jnp.pad(p["swa_sinks"][layer].reshape(1, SWA_HEADS), ((0, 0), (0, LANES - SWA_HEADS)))
    o_c = _flash(proj, *gqa_kv("c_k", "c_v"), units=gqa_units(False), q_spec=(512, BLK512["c_q"]),
                 out_w=512, tq=cfg["tb"], tk=cfg["tb"], window=SWA_WINDOW, sinks=sinks, name="swa_attn")

    kv_cmp = _nsa_compress(_nsa_cmp_inputs(proj), p["nsa_cmp_pos"], p["nsa_cmp_w1"], p["nsa_cmp_w2"],
                           layer)
    o_cmp, mask_d = _nsa_cmp(proj, kv_cmp)
    o_slc = _flash(proj, *gqa_kv("d_ks", "d_vs"), units=gqa_units(True), q_spec=(512, BLK512["d_q"]),
                   out_w=512, tq=cfg["tq"], tk=cfg["tk"], mask=mask_d, name="nsa_slc_attn")
    o_win = _flash(proj, *gqa_kv("d_kw", "d_vw"), units=gqa_units(False), q_spec=(512, BLK512["d_q"]),
                   out_w=512, tq=cfg["tw"], tk=cfg["tw"], window=NSA_WINDOW, name="nsa_win_attn")
    o_d = _nsa_combine(proj, o_cmp, o_slc, o_win, tm=tm)

    merged = _merge(u, (o_a, o_b, o_c, o_d), p["w_gate"], p["w_branch"], layer,
                    tm=cfg["tm_merge"], tn=256)
    return _mm(merged, p["w_o"], layer, tm=tm, tn=512, name="out_proj")


def _config(s_len):
    return dict(tm=min(1024, s_len), tm_merge=min(512, s_len), tm_ln=min(512, s_len),
                tq=min(256, s_len), tk=min(1024, s_len), tb=min(256, s_len), tw=min(512, s_len))


def kernel(x, c, cond_w, cond_b, w_in, diff_lambda, diff_subln_g, dsa_kv_norm_g, dsa_w_uk, dsa_w_uv,
           swa_sinks, nsa_cmp_pos, nsa_cmp_w1, nsa_cmp_w2, w_branch, w_gate, w_o,
           ln1_g, ln1_b, ln2_g, ln2_b, ffn_w_gate, ffn_w_up, ffn_w_down,
           moe_router, moe_w_gate, moe_w_up, moe_w_down):
    bsz, s_len, d = x.shape
    assert bsz == 1 and d == D_MODEL
    depth = cond_w.shape[0]
    cfg = _config(s_len)
    xs = x.reshape(s_len, d)
    c8 = jnp.broadcast_to(c.reshape(1, d), (8, d))
    p = dict(w_in=_permute_w_in(w_in), diff_lambda=diff_lambda, diff_subln_g=diff_subln_g,
             dsa_kv_norm_g=dsa_kv_norm_g, dsa_w_ukv=jnp.concatenate([dsa_w_uk, dsa_w_uv], axis=2),
             swa_sinks=swa_sinks,
             nsa_cmp_pos=nsa_cmp_pos.reshape(depth, 2, 1, NSA_CMP_LEN * HEAD_DIM),
             nsa_cmp_w1=nsa_cmp_w1, nsa_cmp_w2=nsa_cmp_w2, w_branch=w_branch, w_gate=w_gate, w_o=w_o)
    router_p = jnp.pad(moe_router, ((0, 0), (0, 0), (0, LANES - N_EXPERTS)))
    mods = [_mm(c8, cond_w, l, tm=8, tn=512, prologue="silu", bias=cond_b[l], name="cond")[0:1]
            for l in range(depth)]
    u = _modulate(xs, mods[0], 1, 0, tm=cfg["tm_ln"])
    for l in range(depth):
        y = _token_mixers(u, l, p, cfg)
        xs, u = _resid_ln(xs, y, mods[l], 2, ln1_g[l], ln1_b[l], mods[l], 4, 3, tm=cfg["tm_ln"])
        jx = l // 2
        if l % 2 == 0:
            hdn = _swiglu_up(u, ffn_w_gate, ffn_w_up, jx, tm=cfg["tm"], tn=512, name="ffn_up")
            y = _mmk(hdn, ffn_w_down, jx, tm=cfg["tm"], tn=d, tk=512, name="ffn_down")
        else:
            rt = _router(u, router_p, jx, tm=cfg["tm"])
            plan = _moe_plan(rt[:, 0:2], rt[:, 2:4], s_len)
            hdn = _gmm(_moe_gather(u, plan), (moe_w_gate, moe_w_up), jx, plan, tn=512,
                       row_w=plan["row_w"], name="moe_up")
            y = _moe_combine(_gmm(hdn, (moe_w_down,), jx, plan, tn=512, name="moe_down"), plan, s_len)
        nxt = min(l + 1, depth - 1)
        xs, u = _resid_ln(xs, y, mods[l], 5, ln2_g[l], ln2_b[l], mods[nxt], 1, 0, tm=cfg["tm_ln"])
    return xs.reshape(bsz, s_len, d)
```

```python
import functools
import math

import numpy as np
import jax
import jax.numpy as jnp
from jax import lax
from jax.experimental import pallas as pl
from jax.experimental.pallas import tpu as pltpu

F32 = jnp.float32
BF16 = jnp.bfloat16
NEG = -1e30

D_MODEL = 2048
DEPTH = 4
HEAD_DIM = 64
DA_HEADS = 4
DSA_HEADS = 8
DSA_KV_RANK = 128
IDX_HEADS = 8
DSA_TOPK_MAX = 256
SWA_HEADS = 8
SWA_WINDOW = 128
NSA_HEADS = 8
NSA_GROUPS = 2
NSA_CMP_LEN = 32
NSA_CMP_STRIDE = 16
NSA_CMP_HID = 256
NSA_SLC_LEN = 64
NSA_TOPN = 16
NSA_WINDOW = 512
NSA_FORCE = 1e9
N_BRANCH = 4
BRANCH_W = 512
N_EXPERTS = 8
ALPHA = (2.0 * DEPTH) ** 0.25

VMEM_LIMIT_BYTES = 56 * 1024 * 1024
LANES = 128

_ORIG = dict(a_q=(0, 512), a_k=(512, 512), a_v=(1024, 512), b_q=(1536, 512), b_kv=(2048, 128),
             b_iq=(2176, 512), b_ik=(2688, 64), b_iw=(2752, 8), c_q=(2760, 512), c_k=(3272, 128),
             c_v=(3400, 128), d_q=(3528, 512), d_kv=(4040, 768), d_g=(4808, 24))
_NEW_ORDER = ("a_q", "a_k", "a_v", "b_q", "b_iq", "c_q", "d_q", "b_kv", "c_k", "c_v", "d_kv",
              "b_ik", "pad64", "b_iw", "d_g", "pad96", "pad128")
PROJ_W = 5120
BLK512 = dict(a_q=0, a_k=1, a_v=2, b_q=3, b_iq=4, c_q=5, d_q=6)
BLK128 = dict(b_kv=28, c_k=29, c_v=30, d_kc=31, d_vc=32, d_ks=33, d_vs=34, d_kw=35, d_vw=36,
              b_ik=37, small=38)
COL_DKV = 3968


def _cparams(sem):
    return pltpu.CompilerParams(dimension_semantics=sem, vmem_limit_bytes=VMEM_LIMIT_BYTES)


def _sigmoid(x):
    return 1.0 / (1.0 + jnp.exp(-x))


def _silu(x):
    return x * _sigmoid(x)


def _alibi(n_heads):
    return [2.0 ** (-8.0 * (h + 1) / n_heads) for h in range(n_heads)]


def _dot(a, b):
    return jnp.dot(a, b, preferred_element_type=F32)


def _dot_nt(a, b):
    return lax.dot_general(a, b, (((1,), (1,)), ((), ())), preferred_element_type=F32)


def _mm_kernel(*refs, prologue, has_bias, eps):
    it = iter(refs)
    a_ref = next(it)
    g_ref = next(it) if prologue == "rms" else None
    w_ref = next(it)
    b_ref = next(it) if has_bias else None
    o_ref = next(it)
    wb_ref = next(it)

    @pl.when(pl.program_id(1) == 0)
    def _():
        wb_ref[...] = w_ref[0].astype(BF16)

    a = a_ref[...]
    if prologue == "silu":
        a = _silu(a.astype(F32))
    elif prologue == "rms":
        a = a.astype(F32)
        a = a * lax.rsqrt(jnp.mean(a * a, axis=-1, keepdims=True) + eps) * g_ref[...]
    acc = _dot(a.astype(BF16), wb_ref[...])
    if has_bias:
        acc = acc + b_ref[...]
    o_ref[...] = acc.astype(o_ref.dtype)


def _mm(a, w, layer, *, tm, tn, out_dtype=F32, a_blk=0, k=None, prologue=None, gain=None, bias=None,
        eps=1e-6, name="mm"):
    m = a.shape[0]
    k = a.shape[1] if k is None else k
    n = w.shape[2]
    assert w.shape[1] == k and m % tm == 0 and n % tn == 0
    in_specs = [pl.BlockSpec((tm, k), lambda j, i: (i, a_blk))]
    args = [a]
    if prologue == "rms":
        in_specs.append(pl.BlockSpec((1, k), lambda j, i: (0, 0)))
        args.append(gain.reshape(1, k))
    in_specs.append(pl.BlockSpec((1, k, tn), lambda j, i: (layer, 0, j)))
    args.append(w)
    if bias is not None:
        in_specs.append(pl.BlockSpec((1, tn), lambda j, i: (0, j)))
        args.append(bias.reshape(1, n))
    return pl.pallas_call(
        functools.partial(_mm_kernel, prologue=prologue, has_bias=bias is not None, eps=eps),
        grid=(n // tn, m // tm),
        in_specs=in_specs,
        out_specs=pl.BlockSpec((tm, tn), lambda j, i: (i, j)),
        out_shape=jax.ShapeDtypeStruct((m, n), out_dtype),
        scratch_shapes=[pltpu.VMEM((k, tn), BF16)],
        compiler_params=_cparams(("arbitrary", "arbitrary")),
        name=name,
    )(*args)


def _mmk_kernel(a_ref, w_ref, o_ref, acc_ref, *, nk):
    kk = pl.program_id(2)

    @pl.when(kk == 0)
    def _():
        acc_ref[...] = jnp.zeros_like(acc_ref)

    acc_ref[...] += _dot(a_ref[...], w_ref[0].astype(BF16))

    @pl.when(kk == nk - 1)
    def _():
        o_ref[...] = acc_ref[...]


def _mmk(a, w, layer, *, tm, tn, tk, name="mmk"):
    m, k = a.shape
    n = w.shape[2]
    assert w.shape[1] == k and m % tm == 0 and n % tn == 0 and k % tk == 0
    nk = k // tk
    return pl.pallas_call(
        functools.partial(_mmk_kernel, nk=nk),
        grid=(m // tm, n // tn, nk),
        in_specs=[pl.BlockSpec((tm, tk), lambda i, j, kk: (i, kk)),
                  pl.BlockSpec((1, tk, tn), lambda i, j, kk: (layer, kk, j))],
        out_specs=pl.BlockSpec((tm, tn), lambda i, j, kk: (i, j)),
        out_shape=jax.ShapeDtypeStruct((m, n), F32),
        scratch_shapes=[pltpu.VMEM((tm, tn), F32)],
        compiler_params=_cparams(("arbitrary", "arbitrary", "arbitrary")),
        name=name,
    )(a, w)


def _swiglu_kernel(a_ref, wg_ref, wu_ref, o_ref, wgb_ref, wub_ref):
    @pl.when(pl.program_id(1) == 0)
    def _():
        wgb_ref[...] = wg_ref[0].astype(BF16)
        wub_ref[...] = wu_ref[0].astype(BF16)

    a = a_ref[...]
    o_ref[...] = (_silu(_dot(a, wgb_ref[...])) * _dot(a, wub_ref[...])).astype(o_ref.dtype)


def _swiglu_up(u, wg, wu, layer, *, tm, tn, name="swiglu_up"):
    m, k = u.shape
    f = wg.shape[2]
    assert f % tn == 0 and m % tm == 0
    wspec = pl.BlockSpec((1, k, tn), lambda j, i: (layer, 0, j))
    return pl.pallas_call(
        _swiglu_kernel,
        grid=(f // tn, m // tm),
        in_specs=[pl.BlockSpec((tm, k), lambda j, i: (i, 0)), wspec, wspec],
        out_specs=pl.BlockSpec((tm, tn), lambda j, i: (i, j)),
        out_shape=jax.ShapeDtypeStruct((m, f), BF16),
        scratch_shapes=[pltpu.VMEM((k, tn), BF16), pltpu.VMEM((k, tn), BF16)],
        compiler_params=_cparams(("arbitrary", "arbitrary")),
        name=name,
    )(u, wg, wu)


def _modulate_kernel(x_ref, sc_ref, sh_ref, u_ref):
    u_ref[...] = (x_ref[...] * (1.0 + sc_ref[...]) + sh_ref[...]).astype(u_ref.dtype)


def _modulate(x, mod, sc_blk, sh_blk, *, tm):
    m, d = x.shape
    return pl.pallas_call(
        _modulate_kernel,
        grid=(m // tm,),
        in_specs=[pl.BlockSpec((tm, d), lambda i: (i, 0)),
                  pl.BlockSpec((1, d), lambda i: (0, sc_blk)),
                  pl.BlockSpec((1, d), lambda i: (0, sh_blk))],
        out_specs=pl.BlockSpec((tm, d), lambda i: (i, 0)),
        out_shape=jax.ShapeDtypeStruct((m, d), BF16),
        compiler_params=_cparams(("arbitrary",)),
        name="modulate",
    )(x, mod, mod)


def _resid_ln_kernel(x_ref, y_ref, gate_ref, g_ref, b_ref, sc_ref, sh_ref, xo_ref, u_ref):
    z = ALPHA * x_ref[...] + gate_ref[...] * y_ref[...]
    mu = jnp.mean(z, axis=-1, keepdims=True)
    zc = z - mu
    var = jnp.mean(zc * zc, axis=-1, keepdims=True)
    xn = zc * lax.rsqrt(var + 1e-5) * g_ref[...] + b_ref[...]
    xo_ref[...] = xn
    u_ref[...] = (xn * (1.0 + sc_ref[...]) + sh_ref[...]).astype(u_ref.dtype)


def _resid_ln(x, y, mod, gate_blk, g, b, mod_next, sc_blk, sh_blk, *, tm):
    m, d = x.shape
    row = lambda blk: pl.BlockSpec((1, d), lambda i: (0, blk))
    return pl.pallas_call(
        _resid_ln_kernel,
        grid=(m // tm,),
        in_specs=[pl.BlockSpec((tm, d), lambda i: (i, 0)),
                  pl.BlockSpec((tm, d), lambda i: (i, 0)),
                  row(gate_blk), row(0), row(0), row(sc_blk), row(sh_blk)],
        out_specs=[pl.BlockSpec((tm, d), lambda i: (i, 0)),
                   pl.BlockSpec((tm, d), lambda i: (i, 0))],
        out_shape=[jax.ShapeDtypeStruct((m, d), F32), jax.ShapeDtypeStruct((m, d), BF16)],
        compiler_params=_cparams(("arbitrary",)),
        name="resid_ln",
    )(x, y, mod, g.reshape(1, d), b.reshape(1, d), mod_next, mod_next)


FLASH_ROW_CHUNK = 32
POS_SPLIT = 128


_KV_PACK = (("a_k", "a_k", 0, 2 * DA_HEADS, 64, "k"), ("a_v", "a_v", 0, DA_HEADS, 128, "v"),
            ("b_k", "kv_b", 0, DSA_HEADS, 64, "k"), ("b_v", "kv_b", 512, DSA_HEADS, 64, "v"),
            ("c_k", "c_k", 0, 2, 64, "k"), ("c_v", "c_v", 0, 2, 64, "v"),
            ("d_ks", "d_ks", 0, 2, 64, "k"), ("d_vs", "d_vs", 0, 2, 64, "v"),
            ("d_kw", "d_kw", 0, 2, 64, "k"), ("d_vw", "d_vw", 0, 2, 64, "v"),
            ("b_ik", "b_ik", 0, 1, 128, "cast"))
_KV_SOURCES = ("a_k", "a_v", "c_k", "c_v", "d_ks", "d_vs", "d_kw", "d_vw", "b_ik", "kv_b")


def _kv_pack_kernel(*refs, tm):
    src = dict(zip(_KV_SOURCES, refs[:len(_KV_SOURCES)]))
    outs = refs[len(_KV_SOURCES):]
    i = pl.program_id(0)
    tails = {}
    for w in (64, 128):
        lane = lax.broadcasted_iota(jnp.int32, (tm, w), 1)
        pos = i * tm + lax.broadcasted_iota(jnp.int32, (tm, w), 0)
        tails[("v", w)] = jnp.where(lane == 0, 1.0, 0.0)
        tails[("k", w)] = jnp.where(lane == 0, (pos // POS_SPLIT).astype(F32),
                                    jnp.where(lane == 1, (pos % POS_SPLIT).astype(F32),
                                              jnp.where(lane < 4, 1.0, 0.0)))
    for (_, sname, c0, n_heads, w, kind), o_ref in zip(_KV_PACK, outs):
        if kind == "cast":
            o_ref[...] = src[sname][...].astype(BF16)
            continue
        for h in range(n_heads):
            x = src[sname][:, c0 + h * w:c0 + (h + 1) * w]
            o_ref[:, 2 * h * w:2 * (h + 1) * w] = jnp.concatenate([x, tails[(kind, w)]], axis=1).astype(BF16)


def _kv_pack(proj, kv_b, *, tm):
    s_len = proj.shape[0]
    in_specs, args = [], []
    for sname in _KV_SOURCES:
        if sname == "kv_b":
            in_specs.append(pl.BlockSpec((tm, kv_b.shape[1]), lambda i: (i, 0)))
            args.append(kv_b)
        else:
            wblk, blk = (512, BLK512[sname]) if sname in BLK512 else (128, BLK128[sname])
            in_specs.append(pl.BlockSpec((tm, wblk), functools.partial(lambda i, blk: (i, blk), blk=blk)))
            args.append(proj)
    widths = [(1 if kind == "cast" else 2) * n_heads * w for (_, _, _, n_heads, w, kind) in _KV_PACK]
    outs = pl.pallas_call(
        functools.partial(_kv_pack_kernel, tm=tm),
        grid=(s_len // tm,),
        in_specs=in_specs,
        out_specs=[pl.BlockSpec((tm, wd), lambda i: (i, 0)) for wd in widths],
        out_shape=[jax.ShapeDtypeStruct((s_len, wd), BF16) for wd in widths],
        compiler_params=_cparams(("arbitrary",)),
        name="kv_pack",
    )(*args)
    return {name: o for (name, *_), o in zip(_KV_PACK, outs)}


def _flash_kernel(qi_ref, kb_ref, first_ref, last_ref, *refs, units, tq, tk, window, dense, n_mask,
                  has_sink):
    it = iter(refs)
    q_ref = next(it)
    kp_ref = None if dense else next(it)
    k_ref = next(it)
    vp_ref = None if dense else next(it)
    v_ref = next(it)
    mask_ref = next(it) if n_mask else None
    sink_ref = next(it) if has_sink else None
    o_ref = next(it)
    q_scr, m_scr, acc_scr, bias_scr = (next(it) for _ in range(4))
    dv = acc_scr.shape[2]
    lcol = dv // 2

    w = pl.program_id(0)
    qi = qi_ref[w]
    kb = kb_ref[w]
    rows = q_scr.shape[1]
    rb = FLASH_ROW_CHUNK
    kw = 2 * HEAD_DIM

    @pl.when(first_ref[w] > 0)
    def _init():
        lane = lax.broadcasted_iota(jnp.int32, (tq, HEAD_DIM), 1)
        qpos = qi * tq + lax.broadcasted_iota(jnp.int32, (tq, HEAD_DIM), 0)
        qhi = (qpos // POS_SPLIT).astype(F32)
        qlo = (qpos % POS_SPLIT).astype(F32)
        for ui, (_, _, _, _, hds) in enumerate(units):
            for r, (qo, slope, _, sink_idx) in enumerate(hds):
                rsl = slice(r * tq, (r + 1) * tq)
                tail = jnp.where(lane == 0, POS_SPLIT * slope,
                                 jnp.where(lane == 1, slope,
                                           jnp.where(lane == 2, -POS_SPLIT * slope * qhi,
                                                     jnp.where(lane == 3, -slope * qlo, 0.0))))
                qs = q_ref[:, qo:qo + HEAD_DIM] * HEAD_DIM ** -0.5
                q_scr[ui, rsl] = jnp.concatenate([qs, tail], axis=1).astype(BF16)
                if has_sink:
                    m_scr[ui, rsl] = jnp.broadcast_to(sink_ref[:, sink_idx:sink_idx + 1], (tq, 1))
                else:
                    m_scr[ui, rsl] = jnp.full((tq, 1), NEG, F32)
            alane = lax.broadcasted_iota(jnp.int32, acc_scr.shape[1:], 1)
            acc_scr[ui] = jnp.where(alane == lcol, 1.0 if has_sink else 0.0, 0.0)

    def step(masked, kt):
        def tile(ref, prev_ref, lo, hi):
            if dense:
                return ref[0:kt, lo:hi]
            return jnp.concatenate([prev_ref[:, lo:hi], ref[:, lo:hi]], axis=0)

        def scores(ui):
            ku = units[ui][0]
            return _dot_nt(q_scr[ui], tile(k_ref, kp_ref, ku * kw, (ku + 1) * kw))

        if masked:
            qpos = qi * tq + lax.broadcasted_iota(jnp.int32, (tq, kt), 0)
            kpos = (kb * tk if dense else (qi - 1) * tq) + lax.broadcasted_iota(jnp.int32, (tq, kt), 1)
            dist = qpos - kpos
            valid = dist >= 0
            if not dense:
                valid = valid & (dist < window) & (kpos >= 0)
            if n_mask:
                for g in range(n_mask):
                    bias_scr[g, :, 0:kt] = jnp.where(valid, mask_ref[g, :, 0:kt].astype(F32), NEG)
            else:
                bias_scr[0, :, 0:kt] = jnp.where(valid, 0.0, NEG)

        def chunk(s, mg, c):
            r0 = c * rb
            sc = s[r0:r0 + rb]
            if masked:
                rw = r0 % tq
                sc = sc + bias_scr[mg, rw:rw + rb, 0:kt]
            return sc

        s_next = scores(0)
        for ui, (_, vo, _, mg, _) in enumerate(units):
            s = s_next
            if ui + 1 < len(units):
                s_next = scores(ui + 1)
            nchunk = rows // rb
            m_old = m_scr[ui]
            m_cur = jnp.concatenate([jnp.max(chunk(s, mg, c), axis=1, keepdims=True) for c in range(nchunk)],
                                    axis=0)
            m_new = jnp.maximum(m_old, m_cur)
            alpha = jnp.exp(m_old - m_new)
            m_scr[ui] = m_new
            p_all = jnp.concatenate(
                [jnp.exp(chunk(s, mg, c) - m_new[c * rb:(c + 1) * rb]).astype(BF16) for c in range(nchunk)],
                axis=0)
            acc_scr[ui] = alpha * acc_scr[ui] + _dot(p_all, tile(v_ref, vp_ref, vo, vo + dv))

    if dense:
        assert tk % tq == 0
        sub = tk // tq
        diag = last_ref[w] > 0
        part = qi % sub
        if n_mask:
            pl.when(jnp.logical_not(diag) | (part == sub - 1))(lambda: step(True, tk))
        else:
            pl.when(jnp.logical_not(diag))(lambda: step(False, tk))
            pl.when(diag & (part == sub - 1))(lambda: step(True, tk))
        for c in range(sub - 1):
            pl.when(diag & (part == c))(functools.partial(step, True, (c + 1) * tq))
    else:
        step(True, 2 * tq)

    @pl.when(last_ref[w] > 0)
    def _fin():
        for ui, (_, _, _, _, hds) in enumerate(units):
            for r, (_, _, (oo, ow), _) in enumerate(hds):
                rsl = slice(r * tq, (r + 1) * tq)
                acc = acc_scr[ui, rsl]
                o_ref[:, oo:oo + ow] = acc[:, 0:ow] / acc[:, lcol:lcol + 1]


def _flash(q_arr, k_arr, v_arr, *, units, q_spec, out_w, tq, tk, window=None,
           mask=None, sinks=None, name="flash"):
    s_len = q_arr.shape[0]
    dense = window is None
    n_mask = 0 if mask is None else mask.shape[0]
    if not dense:
        assert tq == tk and window <= tq + 1 and not n_mask
    dv = units[0][2]
    nu = len(units)
    rows = len(units[0][4]) * tq
    assert all(len(un[4]) * tq == rows and un[2] == dv for un in units) and rows % FLASH_ROW_CHUNK == 0

    pairs = []
    for qi in range(s_len // tq):
        last_kb = (qi * tq + tq - 1) // tk
        first_kb = 0 if dense else last_kb
        pairs += [(qi, kb, int(kb == first_kb), int(kb == last_kb)) for kb in range(first_kb, last_kb + 1)]
    tables = [jnp.asarray(np.array(col, np.int32)) for col in zip(*pairs)]

    def kv_specs(arr):
        cur = pl.BlockSpec((tk, arr.shape[1]), lambda w, qi, kb, fi, la: (kb[w], 0))
        if dense:
            return [cur], [arr]
        prev = pl.BlockSpec((tk, arr.shape[1]), lambda w, qi, kb, fi, la: (jnp.maximum(kb[w] - 1, 0), 0))
        return [prev, cur], [arr, arr]

    in_specs = [pl.BlockSpec((tq, q_spec[0]), lambda w, qi, kb, fi, la: (qi[w], q_spec[1]))]
    args = [q_arr]
    for arr in (k_arr, v_arr):
        specs, arrs = kv_specs(arr)
        in_specs += specs
        args += arrs
    if n_mask:
        in_specs.append(pl.BlockSpec((n_mask, tq, tk), lambda w, qi, kb, fi, la: (0, qi[w], kb[w])))
        args.append(mask)
    if sinks is not None:
        in_specs.append(pl.BlockSpec((1, LANES), lambda w, qi, kb, fi, la: (0, 0)))
        args.append(sinks)
    return pl.pallas_call(
        functools.partial(_flash_kernel, units=tuple(units), tq=tq, tk=tk, window=window, dense=dense,
                          n_mask=n_mask, has_sink=sinks is not None),
        grid_spec=pltpu.PrefetchScalarGridSpec(
            num_scalar_prefetch=4, grid=(len(pairs),),
            in_specs=in_specs,
            out_specs=pl.BlockSpec((tq, out_w), lambda w, qi, kb, fi, la: (qi[w], 0)),
            scratch_shapes=[pltpu.VMEM((nu, rows, 2 * HEAD_DIM), BF16), pltpu.VMEM((nu, rows, 1), F32),
                            pltpu.VMEM((nu, rows, dv), F32),
                            pltpu.VMEM((max(n_mask, 1), tq, tk if dense else 2 * tq), F32)]),
        out_shape=jax.ShapeDtypeStruct((s_len, out_w), F32),
        compiler_params=_cparams(("arbitrary",)),
        name=name,
    )(*tables, *args)


def _diff_final_kernel(o_ref, lam_ref, g_ref, out_ref, *, lambda_init):
    lf = lam_ref[0]
    lam = (jnp.exp(jnp.sum(lf[0:1] * lf[1:2])) - jnp.exp(jnp.sum(lf[2:3] * lf[3:4])) + lambda_init)
    w = 2 * HEAD_DIM
    for h in range(DA_HEADS):
        o = o_ref[:, (2 * h) * w:(2 * h + 1) * w] - lam * o_ref[:, (2 * h + 1) * w:(2 * h + 2) * w]
        o = o * lax.rsqrt(jnp.mean(o * o, axis=-1, keepdims=True) + 1e-6) * g_ref[...]
        out_ref[:, h * w:(h + 1) * w] = o * (1.0 - lambda_init)


def _diff_final(o, diff_lambda, layer, subln_g, lambda_init, *, tm):
    m = o.shape[0]
    w = 2 * HEAD_DIM
    return pl.pallas_call(
        functools.partial(_diff_final_kernel, lambda_init=lambda_init),
        grid=(m // tm,),
        in_specs=[pl.BlockSpec((tm, 2 * DA_HEADS * w), lambda i: (i, 0)),
                  pl.BlockSpec((1, 4, HEAD_DIM), lambda i: (layer, 0, 0)),
                  pl.BlockSpec((1, w), lambda i: (0, 0))],
        out_specs=pl.BlockSpec((tm, DA_HEADS * w), lambda i: (i, 0)),
        out_shape=jax.ShapeDtypeStruct((m, DA_HEADS * w), F32),
        compiler_params=_cparams(("arbitrary",)),
        name="diff_final",
    )(o, diff_lambda, subln_g.reshape(1, w))


def _f32_key_const(x):
    b = int(np.array(x, np.float32).view(np.int32))
    return b ^ ((b >> 31) & 0x7FFFFFFF)


I16_MIN = -(2 ** 15)


def _dsa_select_kernel(qi_ref, w_ref, kidx_ref, mask_ref, key_scr, half_scr, j_scr, *, tq, ch, nch, topk,
                       s_len):
    i = pl.program_id(0)
    q0 = i * tq
    n_need = (q0 + tq + ch - 1) // ch
    qpos = q0 + lax.broadcasted_iota(jnp.int32, (tq, 1), 0)
    lane = lax.broadcasted_iota(jnp.int32, (1, ch), 1)
    w = w_ref[:, 0:IDX_HEADS]
    q_all = jnp.concatenate([qi_ref[:, h * HEAD_DIM:(h + 1) * HEAD_DIM] for h in range(IDX_HEADS)],
                            axis=0).astype(BF16)

    def score_chunk(c, carry):
        kc = kidx_ref[pl.ds(pl.multiple_of(c * ch, ch), ch), 0:HEAD_DIM].astype(BF16)
        lg = _dot_nt(q_all, kc)
        acc = jnp.zeros((tq, ch), F32)
        for h in range(IDX_HEADS):
            acc = acc + w[:, h:h + 1] * jnp.maximum(lg[h * tq:(h + 1) * tq], 0.0)
        acc = jnp.where(c * ch + lane <= qpos, acc, NEG) + 0.0
        bits = pltpu.bitcast(acc, jnp.int32)
        key = bits ^ ((bits >> 31) & 0x7FFFFFFF)
        key_scr[c] = key
        half_scr[c] = (key >> 16).astype(jnp.int16)
        return carry

    lax.fori_loop(0, n_need, score_chunk, 0)

    def count16(cand, strict):
        cand16 = jnp.broadcast_to(cand, (tq, LANES)).astype(jnp.int16)
        one, zero = jnp.int16(1), jnp.int16(0)

        def body(c, acc):
            blk = half_scr[c]
            for t in range(ch // LANES):
                tile = blk[:, t * LANES:(t + 1) * LANES]
                acc = acc + jnp.where(tile > cand16 if strict else tile >= cand16, one, zero)
            return acc
        acc = lax.fori_loop(0, n_need, body, jnp.zeros((tq, LANES), jnp.int16))
        return jnp.sum(acc.astype(jnp.int32), axis=1, keepdims=True)

    def search16(need_cnt):
        def bit_step(b, t):
            cand = t + jnp.left_shift(jnp.int32(1), 15 - b)
            return jnp.where(count16(cand, False) >= need_cnt, cand, t)
        return lax.fori_loop(0, 16, bit_step, jnp.full((tq, 1), I16_MIN, jnp.int32))

    t_hi = search16(topk)
    need_lo = topk - count16(t_hi, True)

    def low_chunk(c, carry):
        key = key_scr[c]
        low = (key & 0xFFFF) + I16_MIN
        half_scr[c] = jnp.where((key >> 16) == t_hi, low, I16_MIN).astype(jnp.int16)
        return carry

    lax.fori_loop(0, n_need, low_chunk, 0)
    t_lo = search16(need_lo)
    thr = jnp.left_shift(t_hi, 16) + (t_lo - I16_MIN)

    def count(pred):
        def body(c, acc):
            m = jnp.where(pred(key_scr[c], c), 1, 0)
            part = m[:, 0:LANES]
            for t in range(1, ch // LANES):
                part = part + m[:, t * LANES:(t + 1) * LANES]
            return acc + part
        acc = lax.fori_loop(0, n_need, body, jnp.zeros((tq, LANES), jnp.int32))
        return jnp.sum(acc, axis=1, keepdims=True)

    cnt_gt = count(lambda blk, c: blk > thr)
    cnt_ge = count(lambda blk, c: blk >= thr)
    need = topk - cnt_gt
    tie_rows = (cnt_ge > topk) & (thr > _f32_key_const(NEG))
    j_scr[...] = jnp.full((tq, 1), s_len, jnp.int32)
    any_tie = jnp.max(jnp.where(tie_rows, 1, 0)) > 0

    @pl.when(any_tie)
    def _ties():
        def tie_chunk(c, carry):
            half_scr[c] = jnp.where(key_scr[c] == thr, -1 - (c * ch + lane), I16_MIN).astype(jnp.int16)
            return carry

        lax.fori_loop(0, n_need, tie_chunk, 0)
        j_scr[...] = jnp.where(tie_rows, -1 - search16(need), s_len)

    jv = j_scr[...]
    for c in range(nch):
        @pl.when((c < n_need) & any_tie)
        def _w():
            key = key_scr[c]
            sel = (key > thr) | ((key == thr) & (c * ch + lane <= jv))
            mask_ref[:, c * ch:(c + 1) * ch] = jnp.where(sel, 0.0, NEG).astype(mask_ref.dtype)

        @pl.when((c < n_need) & jnp.logical_not(any_tie))
        def _wf():
            mask_ref[:, c * ch:(c + 1) * ch] = jnp.where(key_scr[c] >= thr, 0.0, NEG).astype(mask_ref.dtype)

        @pl.when(c >= n_need)
        def _z():
            mask_ref[:, c * ch:(c + 1) * ch] = jnp.full((tq, ch), NEG, mask_ref.dtype)


def _dsa_select(proj, kidx, *, topk, tq=64):
    s_len = proj.shape[0]
    assert s_len < -I16_MIN
    ch = min(1024, s_len)
    nch = s_len // ch
    return pl.pallas_call(
        functools.partial(_dsa_select_kernel, tq=tq, ch=ch, nch=nch, topk=topk, s_len=s_len),
        grid=(s_len // tq,),
        in_specs=[pl.BlockSpec((tq, 512), lambda i: (i, BLK512["b_iq"])),
                  pl.BlockSpec((tq, LANES), lambda i: (i, BLK128["small"])),
                  pl.BlockSpec((s_len, LANES), lambda i: (0, 0))],
        out_specs=pl.BlockSpec((tq, s_len), lambda i: (i, 0)),
        out_shape=jax.ShapeDtypeStruct((s_len, s_len), BF16),
        scratch_shapes=[pltpu.VMEM((nch, tq, ch), jnp.int32), pltpu.VMEM((nch, tq, ch), jnp.int16),
                        pltpu.VMEM((tq, 1), jnp.int32)],
        compiler_params=_cparams(("arbitrary",)),
        name="dsa_select",
    )(proj, proj, kidx)


def _nsa_compress_kernel(x_ref, pos_ref, w1_ref, w2_ref, o_ref):
    x = (x_ref[0] + pos_ref[0, 0]).astype(BF16)
    hdn = _silu(_dot(x, w1_ref[0, 0].astype(BF16)))
    o_ref[0] = _dot(hdn.astype(BF16), w2_ref[0, 0].astype(BF16))


def _nsa_compress(xc, pos, w1, w2, layer):
    _, ncp, kdim = xc.shape
    return pl.pallas_call(
        _nsa_compress_kernel,
        grid=(4,),
        in_specs=[pl.BlockSpec((1, ncp, kdim), lambda i: (i, 0, 0)),
                  pl.BlockSpec((1, 1, 1, kdim), lambda i: (layer, i // 2, 0, 0)),
                  pl.BlockSpec((1, 1, kdim, NSA_CMP_HID), lambda i: (layer, i // 2, 0, 0)),
                  pl.BlockSpec((1, 1, NSA_CMP_HID, HEAD_DIM), lambda i: (layer, i // 2, 0, 0))],
        out_specs=pl.BlockSpec((1, ncp, HEAD_DIM), lambda i: (i, 0, 0)),
        out_shape=jax.ShapeDtypeStruct((4, ncp, HEAD_DIM), F32),
        compiler_params=_cparams(("arbitrary",)),
        name="nsa_compress",
    )(xc, pos, w1, w2)


def _nsa_cmp_kernel(q_ref, kv_ref, ov_ref, ex_ref, o_ref, mask_ref, *, tq, ncp, n_slc, topn, ch, nch):
    i = pl.program_id(0)
    q0 = i * tq
    n_need = (q0 + tq + ch - 1) // ch
    rpg = NSA_HEADS // NSA_GROUPS
    slopes = _alibi(NSA_HEADS)
    scale = HEAD_DIM ** -0.5
    qpos_c = q0 + lax.broadcasted_iota(jnp.int32, (tq, ncp), 0)
    cend = lax.broadcasted_iota(jnp.int32, (tq, ncp), 1) * NSA_CMP_STRIDE + (NSA_CMP_LEN - 1)
    dist_c = qpos_c - cend
    valid_c = dist_c >= 0
    distf = dist_c.astype(F32)
    qpos = q0 + lax.broadcasted_iota(jnp.int32, (tq, n_slc), 0)
    blk = lax.broadcasted_iota(jnp.int32, (tq, n_slc), 1)
    cur = qpos // NSA_SLC_LEN
    forced = (blk == 0) | (blk == cur) | (blk == cur - 1)
    blk_ok = blk * NSA_SLC_LEN <= qpos
    ov = ov_ref[...]
    imps = []
    for g in range(NSA_GROUPS):
        kc = kv_ref[g].astype(BF16)
        vc = kv_ref[NSA_GROUPS + g].astype(BF16)
        psum = jnp.zeros((tq, ncp), F32)
        for r in range(rpg):
            h = g * rpg + r
            qh = q_ref[:, h * HEAD_DIM:(h + 1) * HEAD_DIM].astype(BF16)
            s = _dot_nt(qh, kc) * scale - slopes[h] * distf
            s = jnp.where(valid_c, s, NEG)
            e = jnp.where(valid_c, jnp.exp(s - jnp.max(s, axis=1, keepdims=True)), 0.0)
            p = e / jnp.maximum(jnp.sum(e, axis=1, keepdims=True), 1e-30)
            o_ref[:, h * HEAD_DIM:(h + 1) * HEAD_DIM] = _dot(p.astype(BF16), vc)
            psum = psum + p
        p_hi = psum.astype(BF16)
        p_lo = (psum - p_hi.astype(F32)).astype(BF16)
        imp = _dot(p_hi, ov) + _dot(p_lo, ov)
        imp = jnp.where(forced, NSA_FORCE, imp)
        imps.append(jnp.where(blk_ok, imp, NEG))
    imps = [imp.T for imp in imps]
    blk_t = lax.broadcasted_iota(jnp.int32, (n_slc, tq), 0)
    sels = [jnp.full((n_slc, tq), NEG, F32) for _ in range(NSA_GROUPS)]
    for _ in range(topn):
        for g in range(NSA_GROUPS):
            mx = jnp.max(imps[g], axis=0, keepdims=True)
            first = jnp.min(jnp.where(imps[g] == mx, blk_t, n_slc), axis=0, keepdims=True)
            hit = blk_t == first
            sels[g] = jnp.where(hit, 0.0, sels[g])
            imps[g] = jnp.where(hit, -jnp.inf, imps[g])
    sels = [sel.T for sel in sels]
    for g in range(NSA_GROUPS):
        selb = sels[g].astype(BF16)
        for c in range(nch):
            @pl.when(c < n_need)
            def _w():
                tok = _dot(selb, ex_ref[:, c * ch:(c + 1) * ch])
                mask_ref[g, :, c * ch:(c + 1) * ch] = tok.astype(mask_ref.dtype)

            @pl.when(c >= n_need)
            def _z():
                mask_ref[g, :, c * ch:(c + 1) * ch] = jnp.full((tq, ch), NEG, mask_ref.dtype)


def _nsa_cmp(proj, kv_cmp, *, tq=256):
    s_len = proj.shape[0]
    ncp = kv_cmp.shape[1]
    n_slc = s_len // NSA_SLC_LEN
    topn = min(NSA_TOPN, n_slc)
    ch = min(1024, s_len)
    nch = s_len // ch
    starts = np.arange(ncp) * NSA_CMP_STRIDE
    slc_start = np.arange(n_slc) * NSA_SLC_LEN
    overlap = ((starts[:, None] < slc_start[None, :] + NSA_SLC_LEN)
               & (starts[:, None] + NSA_CMP_LEN > slc_start[None, :])).astype(np.float32)
    expand = (np.arange(s_len)[None, :] // NSA_SLC_LEN == np.arange(n_slc)[:, None]).astype(np.float32)
    return pl.pallas_call(
        functools.partial(_nsa_cmp_kernel, tq=tq, ncp=ncp, n_slc=n_slc, topn=topn, ch=ch, nch=nch),
        grid=(s_len // tq,),
        in_specs=[pl.BlockSpec((tq, 512), lambda i: (i, BLK512["d_q"])),
                  pl.BlockSpec((4, ncp, HEAD_DIM), lambda i: (0, 0, 0)),
                  pl.BlockSpec((ncp, n_slc), lambda i: (0, 0)),
                  pl.BlockSpec((n_slc, s_len), lambda i: (0, 0))],
        out_specs=[pl.BlockSpec((tq, 512), lambda i: (i, 0)),
                   pl.BlockSpec((NSA_GROUPS, tq, s_len), lambda i: (0, i, 0))],
        out_shape=[jax.ShapeDtypeStruct((s_len, 512), F32),
                   jax.ShapeDtypeStruct((NSA_GROUPS, s_len, s_len), BF16)],
        compiler_params=_cparams(("arbitrary",)),
        name="nsa_cmp",
    )(proj, kv_cmp, jnp.asarray(overlap, BF16), jnp.asarray(expand, BF16))


def _nsa_combine_kernel(g_ref, oc_ref, os_ref, ow_ref, o_ref):
    gt = _sigmoid(g_ref[...])
    for h in range(NSA_HEADS):
        sl = slice(h * HEAD_DIM, (h + 1) * HEAD_DIM)
        c0 = IDX_HEADS + 3 * h
        o_ref[:, sl] = (gt[:, c0:c0 + 1] * oc_ref[:, sl] + gt[:, c0 + 1:c0 + 2] * os_ref[:, sl]
                        + gt[:, c0 + 2:c0 + 3] * ow_ref[:, sl])


def _nsa_combine(proj, o_cmp, o_slc, o_win, *, tm):
    m = proj.shape[0]
    spec = pl.BlockSpec((tm, 512), lambda i: (i, 0))
    return pl.pallas_call(
        _nsa_combine_kernel,
        grid=(m // tm,),
        in_specs=[pl.BlockSpec((tm, LANES), lambda i: (i, BLK128["small"])), spec, spec, spec],
        out_specs=spec,
        out_shape=jax.ShapeDtypeStruct((m, 512), F32),
        compiler_params=_cparams(("arbitrary",)),
        name="nsa_combine",
    )(proj, o_cmp, o_slc, o_win)


def _merge_kernel(u_ref, oa_ref, ob_ref, oc_ref, od_ref, wg0, wg1, wg2, wg3, wb_ref, o_ref,
                  wgb_ref, wbb_ref):
    wgs = (wg0, wg1, wg2, wg3)

    @pl.when(pl.program_id(1) == 0)
    def _():
        for mch in range(N_BRANCH):
            wgb_ref[mch] = wgs[mch][0].astype(BF16)
            wbb_ref[mch] = wb_ref[0, mch].astype(BF16)

    u = u_ref[...]
    acc = None
    for mch, o_ref_m in enumerate((oa_ref, ob_ref, oc_ref, od_ref)):
        gte = _sigmoid(_dot(u, wgb_ref[mch]))
        z = _dot(o_ref_m[...].astype(BF16), wbb_ref[mch])
        acc = gte * z if acc is None else acc + gte * z
    o_ref[...] = acc.astype(o_ref.dtype)


def _merge(u, branches, w_gate, w_branch, layer, *, tm, tn):
    m, d = u.shape
    nj = d // tn
    bspec = pl.BlockSpec((tm, BRANCH_W), lambda j, i: (i, 0))
    wg_specs = [pl.BlockSpec((1, d, tn),
                             functools.partial(lambda j, i, mch: (layer, 0, mch * nj + j), mch=mch))
                for mch in range(N_BRANCH)]
    return pl.pallas_call(
        _merge_kernel,
        grid=(nj, m // tm),
        in_specs=[pl.BlockSpec((tm, d), lambda j, i: (i, 0)), bspec, bspec, bspec, bspec,
                  *wg_specs,
                  pl.BlockSpec((1, N_BRANCH, BRANCH_W, tn), lambda j, i: (layer, 0, 0, j))],
        out_specs=pl.BlockSpec((tm, tn), lambda j, i: (i, j)),
        out_shape=jax.ShapeDtypeStruct((m, d), BF16),
        scratch_shapes=[pltpu.VMEM((N_BRANCH, d, tn), BF16), pltpu.VMEM((N_BRANCH, BRANCH_W, tn), BF16)],
        compiler_params=_cparams(("arbitrary", "arbitrary")),
        name="merge",
    )(u, *branches, w_gate, w_gate, w_gate, w_gate, w_branch)


def _router_kernel(u_ref, r_ref, o_ref):
    logits = _dot(u_ref[...], r_ref[0].astype(BF16))
    lane = lax.broadcasted_iota(jnp.int32, logits.shape, 1)
    lg = jnp.where(lane < N_EXPERTS, logits, -jnp.inf)
    m1 = jnp.max(lg, axis=1, keepdims=True)
    i1 = jnp.min(jnp.where(lg == m1, lane, LANES), axis=1, keepdims=True)
    lg2 = jnp.where(lane == i1, -jnp.inf, lg)
    m2 = jnp.max(lg2, axis=1, keepdims=True)
    i2 = jnp.min(jnp.where(lg2 == m2, lane, LANES), axis=1, keepdims=True)
    e2 = jnp.exp(m2 - m1)
    w1 = 1.0 / (1.0 + e2)
    w2 = e2 / (1.0 + e2)
    o_ref[...] = jnp.where(lane == 0, i1.astype(F32),
                           jnp.where(lane == 1, i2.astype(F32),
                                     jnp.where(lane == 2, w1, jnp.where(lane == 3, w2, 0.0))))


def _router(u, router_padded, layer, *, tm):
    m, d = u.shape
    return pl.pallas_call(
        _router_kernel,
        grid=(m // tm,),
        in_specs=[pl.BlockSpec((tm, d), lambda i: (i, 0)),
                  pl.BlockSpec((1, d, LANES), lambda i: (layer, 0, 0))],
        out_specs=pl.BlockSpec((tm, LANES), lambda i: (i, 0)),
        out_shape=jax.ShapeDtypeStruct((m, LANES), F32),
        compiler_params=_cparams(("arbitrary",)),
        name="router",
    )(u, router_padded)


MOE_GROUP_TILE = 512
MOE_ROW_TILE = 256
MOE_TOK_CHUNK = 256


def _moe_plan(ridx, rw, s_len):
    gm, tm, ct, n_e = MOE_GROUP_TILE, MOE_ROW_TILE, MOE_TOK_CHUNK, N_EXPERTS
    i32 = jnp.int32
    e_a = ridx.reshape(-1).astype(i32)
    oh = (e_a[:, None] == jnp.arange(n_e, dtype=i32)[None, :]).astype(i32)
    csum = jnp.cumsum(oh, axis=0)
    rank_a = jnp.sum((csum - oh) * oh, axis=1)
    ntile_e = (csum[-1] + gm - 1) // gm
    tile_end = jnp.cumsum(ntile_e)
    pos_a = jnp.take(tile_end - ntile_e, e_a) * gm + rank_a
    n_rows = 2 * s_len + n_e * gm
    n_tiles = n_rows // tm
    n_chunks = s_len // ct
    row_tok = jnp.full((n_rows,), -1, i32).at[pos_a].set(jnp.arange(2 * s_len, dtype=i32) // 2)
    row_w = jnp.zeros((n_rows,), F32).at[pos_a].set(rw.reshape(-1))
    tile_e = jnp.minimum(jnp.searchsorted(tile_end, jnp.arange(n_rows // gm, dtype=i32), side="right"),
                         n_e - 1).astype(i32)
    rt = row_tok.reshape(n_tiles, tm)
    lo = jnp.min(jnp.where(rt >= 0, rt, s_len - 1), axis=1) // ct
    hi = jnp.maximum(jnp.max(jnp.where(rt >= 0, rt, 0), axis=1) // ct, lo)
    n_i = hi - lo + 1
    end = jnp.cumsum(n_i)
    n_work = n_tiles + n_e * n_chunks
    w = jnp.arange(n_work, dtype=i32)
    wt = jnp.minimum(jnp.searchsorted(end, w, side="right"), n_tiles - 1).astype(i32)
    wc = jnp.clip(jnp.take(lo, wt) + w - jnp.take(end - n_i, wt), 0, n_chunks - 1).astype(i32)
    wa = ((w < end[-1]) & jnp.take(jnp.any(rt >= 0, axis=1), wt)).astype(i32)
    order = jnp.argsort(jnp.where(wa > 0, wc * n_tiles + wt, n_chunks * n_tiles + w))
    vc = jnp.where(wa > 0, wc, n_chunks - 1)[order]
    return dict(row_tok=row_tok, row_w=row_w, tile_e=tile_e, n_used=tile_end[n_e - 1:].astype(i32),
                n_tiles=n_tiles, n_work=n_work,
                gather=(wt, wc, wa), combine=(vc, wt[order], wa[order]))


def _moe_gather_kernel(wt_ref, wc_ref, wa_ref, tok_ref, u_ref, o_ref):
    w = pl.program_id(0)

    @pl.when((w == 0) | (wt_ref[jnp.maximum(w - 1, 0)] != wt_ref[w]))
    def _():
        o_ref[...] = jnp.zeros_like(o_ref)

    @pl.when(wa_ref[w] > 0)
    def _():
        ct = u_ref.shape[0]
        cols = wc_ref[w] * ct + lax.broadcasted_iota(jnp.int32, (1, ct), 1)
        onehot = jnp.where(tok_ref[...] == cols, 1.0, 0.0).astype(BF16)
        o_ref[...] += _dot(onehot, u_ref[...]).astype(o_ref.dtype)


def _moe_gather(u, plan):
    s_len, d = u.shape
    tm, ct = MOE_ROW_TILE, MOE_TOK_CHUNK
    n_rows = plan["row_tok"].shape[0]
    return pl.pallas_call(
        _moe_gather_kernel,
        grid_spec=pltpu.PrefetchScalarGridSpec(
            num_scalar_prefetch=3, grid=(plan["n_work"],),
            in_specs=[pl.BlockSpec((tm, 1), lambda w, wt, wc, wa: (wt[w], 0)),
                      pl.BlockSpec((ct, d), lambda w, wt, wc, wa: (wc[w], 0))],
            out_specs=pl.BlockSpec((tm, d), lambda w, wt, wc, wa: (wt[w], 0))),
        out_shape=jax.ShapeDtypeStruct((n_rows, d), BF16),
        compiler_params=_cparams(("arbitrary",)),
        name="moe_gather",
    )(*plan["gather"], plan["row_tok"].reshape(n_rows, 1), u)


def _moe_combine_kernel(vc_ref, vt_ref, va_ref, tok_ref, y0_ref, y1_ref, y2_ref, o_ref):
    w = pl.program_id(0)
    chunk = vc_ref[w]

    @pl.when((w == 0) | (vc_ref[jnp.maximum(w - 1, 0)] != chunk))
    def _():
        o_ref[...] = jnp.zeros_like(o_ref)

    @pl.when(va_ref[w] > 0)
    def _():
        ct = o_ref.shape[0]
        rows = chunk * ct + lax.broadcasted_iota(jnp.int32, (ct, 1), 0)
        onehot_t = jnp.where(rows == tok_ref[0], 1.0, 0.0).astype(BF16)
        o_ref[...] += (_dot(onehot_t, y0_ref[...]) + _dot(onehot_t, y1_ref[...])
                       + _dot(onehot_t, y2_ref[...]))


def _moe_combine(ys3, plan, s_len):
    n_rows, d = ys3[0].shape
    tm, ct = MOE_ROW_TILE, MOE_TOK_CHUNK
    yspec = pl.BlockSpec((tm, d), lambda w, vc, vt, va: (vt[w], 0))
    return pl.pallas_call(
        _moe_combine_kernel,
        grid_spec=pltpu.PrefetchScalarGridSpec(
            num_scalar_prefetch=3, grid=(plan["n_work"],),
            in_specs=[pl.BlockSpec((1, 1, tm), lambda w, vc, vt, va: (vt[w], 0, 0)), yspec, yspec, yspec],
            out_specs=pl.BlockSpec((ct, d), lambda w, vc, vt, va: (vc[w], 0))),
        out_shape=jax.ShapeDtypeStruct((s_len, d), F32),
        compiler_params=_cparams(("arbitrary",)),
        name="moe_combine",
    )(*plan["combine"], plan["row_tok"].reshape(plan["n_tiles"], 1, tm), *ys3)


def _gmm_kernel(te_ref, nu_ref, *refs, swiglu):
    it = iter(refs)
    a_ref = next(it)
    w_refs = [next(it), next(it)] if swiglu else [next(it)]
    rw_ref = next(it) if swiglu else None
    o_refs = [next(it)] if swiglu else [next(it), next(it), next(it)]
    wb_refs = [next(it) for _ in w_refs]
    i = pl.program_id(1)
    used = i < nu_ref[0]

    @pl.when(used & ((i == 0) | (te_ref[i] != te_ref[jnp.maximum(i - 1, 0)])))
    def _():
        for w_ref, wb_ref in zip(w_refs, wb_refs):
            wb_ref[...] = w_ref[0, 0].astype(BF16)

    @pl.when(used)
    def _():
        a = a_ref[...]
        if swiglu:
            h = _silu(_dot(a, wb_refs[0][...])) * _dot(a, wb_refs[1][...]) * rw_ref[...]
            o_refs[0][...] = h.astype(BF16)
        else:
            y = _dot(a, wb_refs[0][...])
            hi = y.astype(BF16)
            r1 = y - hi.astype(F32)
            mid = r1.astype(BF16)
            o_refs[0][...] = hi
            o_refs[1][...] = mid
            o_refs[2][...] = (r1 - mid.astype(F32)).astype(BF16)

    @pl.when(jnp.logical_not(used))
    def _():
        for o_ref in o_refs:
            o_ref[...] = jnp.zeros_like(o_ref)


def _gmm(a, ws, layer, plan, *, tn, row_w=None, name="gmm"):
    n_rows, k = a.shape
    n = ws[0].shape[3]
    tm = MOE_GROUP_TILE
    swiglu = len(ws) == 2
    wspec = pl.BlockSpec((1, 1, k, tn), lambda j, i, te, nu: (layer, te[i], 0, j))
    in_specs = [pl.BlockSpec((tm, k), lambda j, i, te, nu: (i, 0))] + [wspec] * len(ws)
    args = [a, *ws]
    if swiglu:
        in_specs.append(pl.BlockSpec((tm, 1), lambda j, i, te, nu: (i, 0)))
        args.append(row_w.reshape(n_rows, 1))
    ospec = pl.BlockSpec((tm, tn), lambda j, i, te, nu: (i, j))
    oshape = jax.ShapeDtypeStruct((n_rows, n), BF16)
    return pl.pallas_call(
        functools.partial(_gmm_kernel, swiglu=swiglu),
        grid_spec=pltpu.PrefetchScalarGridSpec(
            num_scalar_prefetch=2, grid=(n // tn, n_rows // tm),
            in_specs=in_specs,
            out_specs=ospec if swiglu else [ospec] * 3,
            scratch_shapes=[pltpu.VMEM((k, tn), BF16) for _ in ws]),
        out_shape=oshape if swiglu else [oshape] * 3,
        compiler_params=_cparams(("arbitrary", "arbitrary")),
        name=name,
    )(plan["tile_e"], plan["n_used"], *args)


def _permute_w_in(w):
    cols = []
    for nm in _NEW_ORDER:
        if nm.startswith("pad"):
            cols.append(jnp.zeros(w.shape[:2] + (int(nm[3:]),), w.dtype))
        else:
            o, n = _ORIG[nm]
            cols.append(w[:, :, o:o + n])
    out = jnp.concatenate(cols, axis=2)
    assert out.shape[2] == PROJ_W
    return out


def _nsa_cmp_inputs(proj):
    s_len = proj.shape[0]
    n_cmp = (s_len - NSA_CMP_LEN) // NSA_CMP_STRIDE + 1
    ncp = s_len // NSA_CMP_STRIDE
    xs = []
    for jj in range(2):
        for g in range(NSA_GROUPS):
            c0 = COL_DKV + jj * 128 + g * HEAD_DIM
            r = proj[:, c0:c0 + HEAD_DIM].reshape(ncp, NSA_CMP_STRIDE * HEAD_DIM)
            x = jnp.concatenate([r[:-1], r[1:]], axis=1)
            xs.append(jnp.pad(x, ((0, ncp - n_cmp), (0, 0))))
    return jnp.stack(xs)


def _token_mixers(u, layer, p, cfg):
    s_len = u.shape[0]
    tm = cfg["tm"]
    proj = _mm(u, p["w_in"], layer, tm=tm, tn=512, name="in_proj")
    kv_b = _mm(proj, p["dsa_w_ukv"], layer, tm=tm, tn=512, a_blk=BLK128["b_kv"], k=DSA_KV_RANK,
               prologue="rms", gain=p["dsa_kv_norm_g"][layer], name="dsa_kv")
    kv = _kv_pack(proj, kv_b, tm=cfg["tm_ln"])

    lambda_init = 0.8 - 0.6 * math.exp(-0.3 * layer)
    sl_a = _alibi(DA_HEADS)
    units_a = [(2 * h + mp, h * 256, 256, 0,
                ((h * 128 + mp * 64, sl_a[h], ((2 * h + mp) * 128, 128), 0),))
               for h in range(DA_HEADS) for mp in range(2)]
    o_a2 = _flash(proj, kv["a_k"], kv["a_v"], units=units_a,
                  q_spec=(512, BLK512["a_q"]), out_w=1024, tq=cfg["tq"], tk=cfg["tk"], name="diff_attn")
    o_a = _diff_final(o_a2, p["diff_lambda"], layer, p["diff_subln_g"][layer], lambda_init, tm=tm)

    topk = min(DSA_TOPK_MAX, s_len // 4)
    mask_b = _dsa_select(proj, kv["b_ik"], topk=topk)
    sl8 = _alibi(8)
    units_b = [(h, h * 128, 128, 0, ((h * 64, sl8[h], (h * 64, 64), 0),)) for h in range(DSA_HEADS)]
    o_b = _flash(proj, kv["b_k"], kv["b_v"],
                 units=units_b, q_spec=(512, BLK512["b_q"]), out_w=512, tq=cfg["tq"], tk=cfg["tk"],
                 mask=mask_b.reshape(1, s_len, s_len), name="dsa_attn")

    def gqa_units(masked):
        return [(g, g * 128, 128, g if masked else 0,
                 tuple(((g * 4 + r) * 64, sl8[g * 4 + r], ((g * 4 + r) * 64, 64), g * 4 + r)
                       for r in range(4)))
                for g in range(2)]

    def gqa_kv(k_name, v_name):
        return kv[k_name], kv[v_name]

    sinks = jnp.pad(p["swa_sinks"][layer].reshape(1, SWA_HEADS), ((0, 0), (0, LANES - SWA_HEADS)))
    o_c = _flash(proj, *gqa_kv("c_k", "c_v"), units=gqa_units(False), q_spec=(512, BLK512["c_q"]),
                 out_w=512, tq=cfg["tb"], tk=cfg["tb"], window=SWA_WINDOW, sinks=sinks, name="swa_attn")

    kv_cmp = _nsa_compress(_nsa_cmp_inputs(proj), p["nsa_cmp_pos"], p["nsa_cmp_w1"], p["nsa_cmp_w2"],
                           layer)
    o_cmp, mask_d = _nsa_cmp(proj, kv_cmp)
    o_slc = _flash(proj, *gqa_kv("d_ks", "d_vs"), units=gqa_units(True), q_spec=(512, BLK512["d_q"]),
                   out_w=512, tq=cfg["tq"], tk=cfg["tk"], mask=mask_d, name="nsa_slc_attn")
    o_win = _flash(proj, *gqa_kv("d_kw", "d_vw"), units=gqa_units(False), q_spec=(512, BLK512["d_q"]),
                   out_w=512, tq=cfg["tw"], tk=cfg["tw"], window=NSA_WINDOW, name="nsa_win_attn")
    o_d = _nsa_combine(proj, o_cmp, o_slc, o_win, tm=tm)

    merged = _merge(u, (o_a, o_b, o_c, o_d), p["w_gate"], p["w_branch"], layer,
                    tm=cfg["tm_merge"], tn=256)
    return _mm(merged, p["w_o"], layer, tm=tm, tn=512, name="out_proj")


def _config(s_len):
    return dict(tm=min(1024, s_len), tm_merge=min(512, s_len), tm_ln=min(512, s_len),
                tq=min(256, s_len), tk=min(1024, s_len), tb=min(128, s_len), tw=min(512, s_len))


def kernel(x, c, cond_w, cond_b, w_in, diff_lambda, diff_subln_g, dsa_kv_norm_g, dsa_w_uk, dsa_w_uv,
           swa_sinks, nsa_cmp_pos, nsa_cmp_w1, nsa_cmp_w2, w_branch, w_gate, w_o,
           ln1_g, ln1_b, ln2_g, ln2_b, ffn_w_gate, ffn_w_up, ffn_w_down,
           moe_router, moe_w_gate, moe_w_up, moe_w_down):
    bsz, s_len, d = x.shape
    assert bsz == 1 and d == D_MODEL
    depth = cond_w.shape[0]
    cfg = _config(s_len)
    xs = x.reshape(s_len, d)
    c8 = jnp.broadcast_to(c.reshape(1, d), (8, d))
    p = dict(w_in=_permute_w_in(w_in), diff_lambda=diff_lambda, diff_subln_g=diff_subln_g,
             dsa_kv_norm_g=dsa_kv_norm_g, dsa_w_ukv=jnp.concatenate([dsa_w_uk, dsa_w_uv], axis=2),
             swa_sinks=swa_sinks,
             nsa_cmp_pos=nsa_cmp_pos.reshape(depth, 2, 1, NSA_CMP_LEN * HEAD_DIM),
             nsa_cmp_w1=nsa_cmp_w1, nsa_cmp_w2=nsa_cmp_w2, w_branch=w_branch, w_gate=w_gate, w_o=w_o)
    router_p = jnp.pad(moe_router, ((0, 0), (0, 0), (0, LANES - N_EXPERTS)))
    mods = [_mm(c8, cond_w, l, tm=8, tn=512, prologue="silu", bias=cond_b[l], name="cond")[0:1]
            for l in range(depth)]
    u = _modulate(xs, mods[0], 1, 0, tm=cfg["tm_ln"])
    for l in range(depth):
        y = _token_mixers(u, l, p, cfg)
        xs, u = _resid_ln(xs, y, mods[l], 2, ln1_g[l], ln1_b[l], mods[l], 4, 3, tm=cfg["tm_ln"])
        jx = l // 2
        if l % 2 == 0:
            hdn = _swiglu_up(u, ffn_w_gate, ffn_w_up, jx, tm=cfg["tm"], tn=512, name="ffn_up")
            y = _mmk(hdn, ffn_w_down, jx, tm=cfg["tm"], tn=d, tk=512, name="ffn_down")
        else:
            rt = _router(u, router_p, jx, tm=cfg["tm"])
            plan = _moe_plan(rt[:, 0:2], rt[:, 2:4], s_len)
            hdn = _gmm(_moe_gather(u, plan), (moe_w_gate, moe_w_up), jx, plan, tn=512,
                       row_w=plan["row_w"], name="moe_up")
            y = _moe_combine(_gmm(hdn, (moe_w_down,), jx, plan, tn=512, name="moe_down"), plan, s_len)
        nxt = min(l + 1, depth - 1)
        xs, u = _resid_ln(xs, y, mods[l], 5, ln2_g[l], ln2_b[l], mods[nxt], 1, 0, tm=cfg["tm_ln"])
    return xs.reshape(bsz, s_len, d)
```

```python
import functools
import math

import numpy as np
import jax
import jax.numpy as jnp
from jax import lax
from jax.experimental import pallas as pl
from jax.experimental.pallas import tpu as pltpu

F32 = jnp.float32
BF16 = jnp.bfloat16
NEG = -1e30

D_MODEL = 2048
DEPTH = 4
HEAD_DIM = 64
DA_HEADS = 4
DSA_HEADS = 8
DSA_KV_RANK = 128
IDX_HEADS = 8
DSA_TOPK_MAX = 256
SWA_HEADS = 8
SWA_WINDOW = 128
NSA_HEADS = 8
NSA_GROUPS = 2
NSA_CMP_LEN = 32
NSA_CMP_STRIDE = 16
NSA_CMP_HID = 256
NSA_SLC_LEN = 64
NSA_TOPN = 16
NSA_WINDOW = 512
NSA_FORCE = 1e9
N_BRANCH = 4
BRANCH_W = 512
N_EXPERTS = 8
ALPHA = (2.0 * DEPTH) ** 0.25

VMEM_LIMIT_BYTES = 56 * 1024 * 1024
LANES = 128

_ORIG = dict(a_q=(0, 512), a_k=(512, 512), a_v=(1024, 512), b_q=(1536, 512), b_kv=(2048, 128),
             b_iq=(2176, 512), b_ik=(2688, 64), b_iw=(2752, 8), c_q=(2760, 512), c_k=(3272, 128),
             c_v=(3400, 128), d_q=(3528, 512), d_kv=(4040, 768), d_g=(4808, 24))
_NEW_ORDER = ("a_q", "a_k", "a_v", "b_q", "b_iq", "c_q", "d_q", "b_kv", "c_k", "c_v", "d_kv",
              "b_ik", "pad64", "b_iw", "d_g", "pad96", "pad128")
PROJ_W = 5120
BLK512 = dict(a_q=0, a_k=1, a_v=2, b_q=3, b_iq=4, c_q=5, d_q=6)
BLK128 = dict(b_kv=28, c_k=29, c_v=30, d_kc=31, d_vc=32, d_ks=33, d_vs=34, d_kw=35, d_vw=36,
              b_ik=37, small=38)
COL_DKV = 3968


def _cparams(sem):
    return pltpu.CompilerParams(dimension_semantics=sem, vmem_limit_bytes=VMEM_LIMIT_BYTES)


def _sigmoid(x):
    return 1.0 / (1.0 + jnp.exp(-x))


def _silu(x):
    return x * _sigmoid(x)


def _alibi(n_heads):
    return [2.0 ** (-8.0 * (h + 1) / n_heads) for h in range(n_heads)]


def _dot(a, b):
    return jnp.dot(a, b, preferred_element_type=F32)


def _dot_nt(a, b):
    return lax.dot_general(a, b, (((1,), (1,)), ((), ())), preferred_element_type=F32)


def _mm_kernel(*refs, prologue, has_bias, eps):
    it = iter(refs)
    a_ref = next(it)
    g_ref = next(it) if prologue == "rms" else None
    w_ref = next(it)
    b_ref = next(it) if has_bias else None
    o_ref = next(it)
    wb_ref = next(it)

    @pl.when(pl.program_id(1) == 0)
    def _():
        wb_ref[...] = w_ref[0].astype(BF16)

    a = a_ref[...]
    if prologue == "silu":
        a = _silu(a.astype(F32))
    elif prologue == "rms":
        a = a.astype(F32)
        a = a * lax.rsqrt(jnp.mean(a * a, axis=-1, keepdims=True) + eps) * g_ref[...]
    acc = _dot(a.astype(BF16), wb_ref[...])
    if has_bias:
        acc = acc + b_ref[...]
    o_ref[...] = acc.astype(o_ref.dtype)


def _mm(a, w, layer, *, tm, tn, out_dtype=F32, a_blk=0, k=None, prologue=None, gain=None, bias=None,
        eps=1e-6, name="mm"):
    m = a.shape[0]
    k = a.shape[1] if k is None else k
    n = w.shape[2]
    assert w.shape[1] == k and m % tm == 0 and n % tn == 0
    in_specs = [pl.BlockSpec((tm, k), lambda j, i: (i, a_blk))]
    args = [a]
    if prologue == "rms":
        in_specs.append(pl.BlockSpec((1, k), lambda j, i: (0, 0)))
        args.append(gain.reshape(1, k))
    in_specs.append(pl.BlockSpec((1, k, tn), lambda j, i: (layer, 0, j)))
    args.append(w)
    if bias is not None:
        in_specs.append(pl.BlockSpec((1, tn), lambda j, i: (0, j)))
        args.append(bias.reshape(1, n))
    return pl.pallas_call(
        functools.partial(_mm_kernel, prologue=prologue, has_bias=bias is not None, eps=eps),
        grid=(n // tn, m // tm),
        in_specs=in_specs,
        out_specs=pl.BlockSpec((tm, tn), lambda j, i: (i, j)),
        out_shape=jax.ShapeDtypeStruct((m, n), out_dtype),
        scratch_shapes=[pltpu.VMEM((k, tn), BF16)],
        compiler_params=_cparams(("arbitrary", "arbitrary")),
        name=name,
    )(*args)


def _mmk_kernel(a_ref, w_ref, o_ref, acc_ref, *, nk):
    kk = pl.program_id(2)

    @pl.when(kk == 0)
    def _():
        acc_ref[...] = jnp.zeros_like(acc_ref)

    acc_ref[...] += _dot(a_ref[...], w_ref[0].astype(BF16))

    @pl.when(kk == nk - 1)
    def _():
        o_ref[...] = acc_ref[...]


def _mmk(a, w, layer, *, tm, tn, tk, name="mmk"):
    m, k = a.shape
    n = w.shape[2]
    assert w.shape[1] == k and m % tm == 0 and n % tn == 0 and k % tk == 0
    nk = k // tk
    return pl.pallas_call(
        functools.partial(_mmk_kernel, nk=nk),
        grid=(m // tm, n // tn, nk),
        in_specs=[pl.BlockSpec((tm, tk), lambda i, j, kk: (i, kk)),
                  pl.BlockSpec((1, tk, tn), lambda i, j, kk: (layer, kk, j))],
        out_specs=pl.BlockSpec((tm, tn), lambda i, j, kk: (i, j)),
        out_shape=jax.ShapeDtypeStruct((m, n), F32),
        scratch_shapes=[pltpu.VMEM((tm, tn), F32)],
        compiler_params=_cparams(("arbitrary", "arbitrary", "arbitrary")),
        name=name,
    )(a, w)


def _swiglu_kernel(a_ref, wg_ref, wu_ref, o_ref, wgb_ref, wub_ref):
    @pl.when(pl.program_id(1) == 0)
    def _():
        wgb_ref[...] = wg_ref[0].astype(BF16)
        wub_ref[...] = wu_ref[0].astype(BF16)

    a = a_ref[...]
    o_ref[...] = (_silu(_dot(a, wgb_ref[...])) * _dot(a, wub_ref[...])).astype(o_ref.dtype)


def _swiglu_up(u, wg, wu, layer, *, tm, tn, name="swiglu_up"):
    m, k = u.shape
    f = wg.shape[2]
    assert f % tn == 0 and m % tm == 0
    wspec = pl.BlockSpec((1, k, tn), lambda j, i: (layer, 0, j))
    return pl.pallas_call(
        _swiglu_kernel,
        grid=(f // tn, m // tm),
        in_specs=[pl.BlockSpec((tm, k), lambda j, i: (i, 0)), wspec, wspec],
        out_specs=pl.BlockSpec((tm, tn), lambda j, i: (i, j)),
        out_shape=jax.ShapeDtypeStruct((m, f), BF16),
        scratch_shapes=[pltpu.VMEM((k, tn), BF16), pltpu.VMEM((k, tn), BF16)],
        compiler_params=_cparams(("arbitrary", "arbitrary")),
        name=name,
    )(u, wg, wu)


def _modulate_kernel(x_ref, sc_ref, sh_ref, u_ref):
    u_ref[...] = (x_ref[...] * (1.0 + sc_ref[...]) + sh_ref[...]).astype(u_ref.dtype)


def _modulate(x, mod, sc_blk, sh_blk, *, tm):
    m, d = x.shape
    return pl.pallas_call(
        _modulate_kernel,
        grid=(m // tm,),
        in_specs=[pl.BlockSpec((tm, d), lambda i: (i, 0)),
                  pl.BlockSpec((1, d), lambda i: (0, sc_blk)),
                  pl.BlockSpec((1, d), lambda i: (0, sh_blk))],
        out_specs=pl.BlockSpec((tm, d), lambda i: (i, 0)),
        out_shape=jax.ShapeDtypeStruct((m, d), BF16),
        compiler_params=_cparams(("arbitrary",)),
        name="modulate",
    )(x, mod, mod)


def _resid_ln_kernel(x_ref, y_ref, gate_ref, g_ref, b_ref, sc_ref, sh_ref, xo_ref, u_ref):
    z = ALPHA * x_ref[...] + gate_ref[...] * y_ref[...]
    mu = jnp.mean(z, axis=-1, keepdims=True)
    zc = z - mu
    var = jnp.mean(zc * zc, axis=-1, keepdims=True)
    xn = zc * lax.rsqrt(var + 1e-5) * g_ref[...] + b_ref[...]
    xo_ref[...] = xn
    u_ref[...] = (xn * (1.0 + sc_ref[...]) + sh_ref[...]).astype(u_ref.dtype)


def _resid_ln(x, y, mod, gate_blk, g, b, mod_next, sc_blk, sh_blk, *, tm):
    m, d = x.shape
    row = lambda blk: pl.BlockSpec((1, d), lambda i: (0, blk))
    return pl.pallas_call(
        _resid_ln_kernel,
        grid=(m // tm,),
        in_specs=[pl.BlockSpec((tm, d), lambda i: (i, 0)),
                  pl.BlockSpec((tm, d), lambda i: (i, 0)),
                  row(gate_blk), row(0), row(0), row(sc_blk), row(sh_blk)],
        out_specs=[pl.BlockSpec((tm, d), lambda i: (i, 0)),
                   pl.BlockSpec((tm, d), lambda i: (i, 0))],
        out_shape=[jax.ShapeDtypeStruct((m, d), F32), jax.ShapeDtypeStruct((m, d), BF16)],
        compiler_params=_cparams(("arbitrary",)),
        name="resid_ln",
    )(x, y, mod, g.reshape(1, d), b.reshape(1, d), mod_next, mod_next)


FLASH_ROW_CHUNK = 32
POS_SPLIT = 128


_KV_PACK = (("a_k", "a_k", 0, 2 * DA_HEADS, 64, "k"), ("a_v", "a_v", 0, DA_HEADS, 128, "v"),
            ("b_k", "kv_b", 0, DSA_HEADS, 64, "k"), ("b_v", "kv_b", 512, DSA_HEADS, 64, "v"),
            ("c_k", "c_k", 0, 2, 64, "k"), ("c_v", "c_v", 0, 2, 64, "v"),
            ("d_ks", "d_ks", 0, 2, 64, "k"), ("d_vs", "d_vs", 0, 2, 64, "v"),
            ("d_kw", "d_kw", 0, 2, 64, "k"), ("d_vw", "d_vw", 0, 2, 64, "v"),
            ("b_ik", "b_ik", 0, 1, 128, "cast"))
_KV_SOURCES = ("a_k", "a_v", "c_k", "c_v", "d_ks", "d_vs", "d_kw", "d_vw", "b_ik", "kv_b")


def _kv_pack_kernel(*refs, tm):
    src = dict(zip(_KV_SOURCES, refs[:len(_KV_SOURCES)]))
    outs = refs[len(_KV_SOURCES):]
    i = pl.program_id(0)
    tails = {}
    for w in (64, 128):
        lane = lax.broadcasted_iota(jnp.int32, (tm, w), 1)
        pos = i * tm + lax.broadcasted_iota(jnp.int32, (tm, w), 0)
        tails[("v", w)] = jnp.where(lane == 0, 1.0, 0.0)
        tails[("k", w)] = jnp.where(lane == 0, (pos // POS_SPLIT).astype(F32),
                                    jnp.where(lane == 1, (pos % POS_SPLIT).astype(F32),
                                              jnp.where(lane < 4, 1.0, 0.0)))
    for (_, sname, c0, n_heads, w, kind), o_ref in zip(_KV_PACK, outs):
        if kind == "cast":
            o_ref[...] = src[sname][...].astype(BF16)
            continue
        for h in range(n_heads):
            x = src[sname][:, c0 + h * w:c0 + (h + 1) * w]
            o_ref[:, 2 * h * w:2 * (h + 1) * w] = jnp.concatenate([x, tails[(kind, w)]], axis=1).astype(BF16)


def _kv_pack(proj, kv_b, *, tm):
    s_len = proj.shape[0]
    in_specs, args = [], []
    for sname in _KV_SOURCES:
        if sname == "kv_b":
            in_specs.append(pl.BlockSpec((tm, kv_b.shape[1]), lambda i: (i, 0)))
            args.append(kv_b)
        else:
            wblk, blk = (512, BLK512[sname]) if sname in BLK512 else (128, BLK128[sname])
            in_specs.append(pl.BlockSpec((tm, wblk), functools.partial(lambda i, blk: (i, blk), blk=blk)))
            args.append(proj)
    widths = [(1 if kind == "cast" else 2) * n_heads * w for (_, _, _, n_heads, w, kind) in _KV_PACK]
    outs = pl.pallas_call(
        functools.partial(_kv_pack_kernel, tm=tm),
        grid=(s_len // tm,),
        in_specs=in_specs,
        out_specs=[pl.BlockSpec((tm, wd), lambda i: (i, 0)) for wd in widths],
        out_shape=[jax.ShapeDtypeStruct((s_len, wd), BF16) for wd in widths],
        compiler_params=_cparams(("arbitrary",)),
        name="kv_pack",
    )(*args)
    return {name: o for (name, *_), o in zip(_KV_PACK, outs)}


def _flash_kernel(qi_ref, kb_ref, first_ref, last_ref, *refs, units, tq, tk, window, dense, n_mask,
                  has_sink):
    it = iter(refs)
    q_ref = next(it)
    kp_ref = None if dense else next(it)
    k_ref = next(it)
    vp_ref = None if dense else next(it)
    v_ref = next(it)
    mask_ref = next(it) if n_mask else None
    sink_ref = next(it) if has_sink else None
    o_ref = next(it)
    q_scr, m_scr, acc_scr, bias_scr = (next(it) for _ in range(4))
    dv = acc_scr.shape[2]
    lcol = dv // 2

    w = pl.program_id(0)
    qi = qi_ref[w]
    kb = kb_ref[w]
    rows = q_scr.shape[1]
    rb = FLASH_ROW_CHUNK
    kw = 2 * HEAD_DIM

    @pl.when(first_ref[w] > 0)
    def _init():
        lane = lax.broadcasted_iota(jnp.int32, (tq, HEAD_DIM), 1)
        qpos = qi * tq + lax.broadcasted_iota(jnp.int32, (tq, HEAD_DIM), 0)
        qhi = (qpos // POS_SPLIT).astype(F32)
        qlo = (qpos % POS_SPLIT).astype(F32)
        for ui, (_, _, _, _, hds) in enumerate(units):
            for r, (qo, slope, _, sink_idx) in enumerate(hds):
                rsl = slice(r * tq, (r + 1) * tq)
                tail = jnp.where(lane == 0, POS_SPLIT * slope,
                                 jnp.where(lane == 1, slope,
                                           jnp.where(lane == 2, -POS_SPLIT * slope * qhi,
                                                     jnp.where(lane == 3, -slope * qlo, 0.0))))
                qs = q_ref[:, qo:qo + HEAD_DIM] * HEAD_DIM ** -0.5
                q_scr[ui, rsl] = jnp.concatenate([qs, tail], axis=1).astype(BF16)
                if has_sink:
                    m_scr[ui, rsl] = jnp.broadcast_to(sink_ref[:, sink_idx:sink_idx + 1], (tq, 1))
                else:
                    m_scr[ui, rsl] = jnp.full((tq, 1), NEG, F32)
            alane = lax.broadcasted_iota(jnp.int32, acc_scr.shape[1:], 1)
            acc_scr[ui] = jnp.where(alane == lcol, 1.0 if has_sink else 0.0, 0.0)

    def step(masked, kt):
        def tile(ref, prev_ref, lo, hi):
            if dense:
                return ref[0:kt, lo:hi]
            return jnp.concatenate([prev_ref[:, lo:hi], ref[:, lo:hi]], axis=0)

        def scores(ui):
            ku = units[ui][0]
            return _dot_nt(q_scr[ui], tile(k_ref, kp_ref, ku * kw, (ku + 1) * kw))

        if masked:
            qpos = qi * tq + lax.broadcasted_iota(jnp.int32, (tq, kt), 0)
            kpos = (kb * tk if dense else (qi - 1) * tq) + lax.broadcasted_iota(jnp.int32, (tq, kt), 1)
            dist = qpos - kpos
            valid = dist >= 0
            if not dense:
                valid = valid & (dist < window) & (kpos >= 0)
            if n_mask:
                for g in range(n_mask):
                    bias_scr[g, :, 0:kt] = jnp.where(valid, mask_ref[g, :, 0:kt].astype(F32), NEG)
            else:
                bias_scr[0, :, 0:kt] = jnp.where(valid, 0.0, NEG)

        def chunk(s, mg, c):
            r0 = c * rb
            sc = s[r0:r0 + rb]
            if masked:
                rw = r0 % tq
                sc = sc + bias_scr[mg, rw:rw + rb, 0:kt]
            return sc

        s_next = scores(0)
        for ui, (_, vo, _, mg, _) in enumerate(units):
            s = s_next
            if ui + 1 < len(units):
                s_next = scores(ui + 1)
            nchunk = rows // rb
            m_old = m_scr[ui]
            m_cur = jnp.concatenate([jnp.max(chunk(s, mg, c), axis=1, keepdims=True) for c in range(nchunk)],
                                    axis=0)
            m_new = jnp.maximum(m_old, m_cur)
            alpha = jnp.exp(m_old - m_new)
            m_scr[ui] = m_new
            p_all = jnp.concatenate(
                [jnp.exp(chunk(s, mg, c) - m_new[c * rb:(c + 1) * rb]).astype(BF16) for c in range(nchunk)],
                axis=0)
            acc_scr[ui] = alpha * acc_scr[ui] + _dot(p_all, tile(v_ref, vp_ref, vo, vo + dv))

    if dense:
        assert tk % tq == 0
        sub = tk // tq
        diag = last_ref[w] > 0
        part = qi % sub
        if n_mask:
            pl.when(jnp.logical_not(diag) | (part == sub - 1))(lambda: step(True, tk))
        else:
            pl.when(jnp.logical_not(diag))(lambda: step(False, tk))
            pl.when(diag & (part == sub - 1))(lambda: step(True, tk))
        for c in range(sub - 1):
            pl.when(diag & (part == c))(functools.partial(step, True, (c + 1) * tq))
    else:
        step(True, 2 * tq)

    @pl.when(last_ref[w] > 0)
    def _fin():
        for ui, (_, _, _, _, hds) in enumerate(units):
            for r, (_, _, (oo, ow), _) in enumerate(hds):
                rsl = slice(r * tq, (r + 1) * tq)
                acc = acc_scr[ui, rsl]
                o_ref[:, oo:oo + ow] = acc[:, 0:ow] / acc[:, lcol:lcol + 1]


def _flash(q_arr, k_arr, v_arr, *, units, q_spec, out_w, tq, tk, window=None,
           mask=None, sinks=None, name="flash"):
    s_len = q_arr.shape[0]
    dense = window is None
    n_mask = 0 if mask is None else mask.shape[0]
    if not dense:
        assert tq == tk and window <= tq + 1 and not n_mask
    dv = units[0][2]
    nu = len(units)
    rows = len(units[0][4]) * tq
    assert all(len(un[4]) * tq == rows and un[2] == dv for un in units) and rows % FLASH_ROW_CHUNK == 0

    pairs = []
    for qi in range(s_len // tq):
        last_kb = (qi * tq + tq - 1) // tk
        first_kb = 0 if dense else last_kb
        pairs += [(qi, kb, int(kb == first_kb), int(kb == last_kb)) for kb in range(first_kb, last_kb + 1)]
    tables = [jnp.asarray(np.array(col, np.int32)) for col in zip(*pairs)]

    def kv_specs(arr):
        cur = pl.BlockSpec((tk, arr.shape[1]), lambda w, qi, kb, fi, la: (kb[w], 0))
        if dense:
            return [cur], [arr]
        prev = pl.BlockSpec((tk, arr.shape[1]), lambda w, qi, kb, fi, la: (jnp.maximum(kb[w] - 1, 0), 0))
        return [prev, cur], [arr, arr]

    in_specs = [pl.BlockSpec((tq, q_spec[0]), lambda w, qi, kb, fi, la: (qi[w], q_spec[1]))]
    args = [q_arr]
    for arr in (k_arr, v_arr):
        specs, arrs = kv_specs(arr)
        in_specs += specs
        args += arrs
    if n_mask:
        in_specs.append(pl.BlockSpec((n_mask, tq, tk), lambda w, qi, kb, fi, la: (0, qi[w], kb[w])))
        args.append(mask)
    if sinks is not None:
        in_specs.append(pl.BlockSpec((1, LANES), lambda w, qi, kb, fi, la: (0, 0)))
        args.append(sinks)
    return pl.pallas_call(
        functools.partial(_flash_kernel, units=tuple(units), tq=tq, tk=tk, window=window, dense=dense,
                          n_mask=n_mask, has_sink=sinks is not None),
        grid_spec=pltpu.PrefetchScalarGridSpec(
            num_scalar_prefetch=4, grid=(len(pairs),),
            in_specs=in_specs,
            out_specs=pl.BlockSpec((tq, out_w), lambda w, qi, kb, fi, la: (qi[w], 0)),
            scratch_shapes=[pltpu.VMEM((nu, rows, 2 * HEAD_DIM), BF16), pltpu.VMEM((nu, rows, 1), F32),
                            pltpu.VMEM((nu, rows, dv), F32),
                            pltpu.VMEM((max(n_mask, 1), tq, tk if dense else 2 * tq), F32)]),
        out_shape=jax.ShapeDtypeStruct((s_len, out_w), F32),
        compiler_params=_cparams(("arbitrary",)),
        name=name,
    )(*tables, *args)


def _diff_final_kernel(o_ref, lam_ref, g_ref, out_ref, *, lambda_init):
    lf = lam_ref[0]
    lam = (jnp.exp(jnp.sum(lf[0:1] * lf[1:2])) - jnp.exp(jnp.sum(lf[2:3] * lf[3:4])) + lambda_init)
    w = 2 * HEAD_DIM
    for h in range(DA_HEADS):
        o = o_ref[:, (2 * h) * w:(2 * h + 1) * w] - lam * o_ref[:, (2 * h + 1) * w:(2 * h + 2) * w]
        o = o * lax.rsqrt(jnp.mean(o * o, axis=-1, keepdims=True) + 1e-6) * g_ref[...]
        out_ref[:, h * w:(h + 1) * w] = o * (1.0 - lambda_init)


def _diff_final(o, diff_lambda, layer, subln_g, lambda_init, *, tm):
    m = o.shape[0]
    w = 2 * HEAD_DIM
    return pl.pallas_call(
        functools.partial(_diff_final_kernel, lambda_init=lambda_init),
        grid=(m // tm,),
        in_specs=[pl.BlockSpec((tm, 2 * DA_HEADS * w), lambda i: (i, 0)),
                  pl.BlockSpec((1, 4, HEAD_DIM), lambda i: (layer, 0, 0)),
                  pl.BlockSpec((1, w), lambda i: (0, 0))],
        out_specs=pl.BlockSpec((tm, DA_HEADS * w), lambda i: (i, 0)),
        out_shape=jax.ShapeDtypeStruct((m, DA_HEADS * w), F32),
        compiler_params=_cparams(("arbitrary",)),
        name="diff_final",
    )(o, diff_lambda, subln_g.reshape(1, w))


def _f32_key_const(x):
    b = int(np.array(x, np.float32).view(np.int32))
    return b ^ ((b >> 31) & 0x7FFFFFFF)


I16_MIN = -(2 ** 15)


def _dsa_select_kernel(qi_ref, w_ref, kidx_ref, mask_ref, key_scr, half_scr, j_scr, *, tq, ch, nch, topk,
                       s_len):
    i = pl.program_id(0)
    q0 = i * tq
    n_need = (q0 + tq + ch - 1) // ch
    qpos = q0 + lax.broadcasted_iota(jnp.int32, (tq, 1), 0)
    lane = lax.broadcasted_iota(jnp.int32, (1, ch), 1)
    w = w_ref[:, 0:IDX_HEADS]
    q_all = jnp.concatenate([qi_ref[:, h * HEAD_DIM:(h + 1) * HEAD_DIM] for h in range(IDX_HEADS)],
                            axis=0).astype(BF16)

    def score_chunk(c, carry):
        kc = kidx_ref[pl.ds(pl.multiple_of(c * ch, ch), ch), 0:HEAD_DIM].astype(BF16)
        lg = _dot_nt(q_all, kc)
        acc = jnp.zeros((tq, ch), F32)
        for h in range(IDX_HEADS):
            acc = acc + w[:, h:h + 1] * jnp.maximum(lg[h * tq:(h + 1) * tq], 0.0)
        acc = jnp.where(c * ch + lane <= qpos, acc, NEG) + 0.0
        bits = pltpu.bitcast(acc, jnp.int32)
        key = bits ^ ((bits >> 31) & 0x7FFFFFFF)
        key_scr[c] = key
        half_scr[c] = (key >> 16).astype(jnp.int16)
        return carry

    lax.fori_loop(0, n_need, score_chunk, 0)

    def count16(cand, strict):
        cand16 = jnp.broadcast_to(cand, (tq, LANES)).astype(jnp.int16)
        one, zero = jnp.int16(1), jnp.int16(0)

        def body(c, acc):
            blk = half_scr[c]
            for t in range(ch // LANES):
                tile = blk[:, t * LANES:(t + 1) * LANES]
                acc = acc + jnp.where(tile > cand16 if strict else tile >= cand16, one, zero)
            return acc
        acc = lax.fori_loop(0, n_need, body, jnp.zeros((tq, LANES), jnp.int16))
        return jnp.sum(acc.astype(jnp.int32), axis=1, keepdims=True)

    def search16(need_cnt):
        def bit_step(b, t):
            cand = t + jnp.left_shift(jnp.int32(1), 15 - b)
            return jnp.where(count16(cand, False) >= need_cnt, cand, t)
        return lax.fori_loop(0, 16, bit_step, jnp.full((tq, 1), I16_MIN, jnp.int32))

    t_hi = search16(topk)
    need_lo = topk - count16(t_hi, True)

    def low_chunk(c, carry):
        key = key_scr[c]
        low = (key & 0xFFFF) + I16_MIN
        half_scr[c] = jnp.where((key >> 16) == t_hi, low, I16_MIN).astype(jnp.int16)
        return carry

    lax.fori_loop(0, n_need, low_chunk, 0)
    t_lo = search16(need_lo)
    thr = jnp.left_shift(t_hi, 16) + (t_lo - I16_MIN)

    def count(pred):
        def body(c, acc):
            m = jnp.where(pred(key_scr[c], c), 1, 0)
            part = m[:, 0:LANES]
            for t in range(1, ch // LANES):
                part = part + m[:, t * LANES:(t + 1) * LANES]
            return acc + part
        acc = lax.fori_loop(0, n_need, body, jnp.zeros((tq, LANES), jnp.int32))
        return jnp.sum(acc, axis=1, keepdims=True)

    cnt_gt = count(lambda blk, c: blk > thr)
    cnt_ge = count(lambda blk, c: blk >= thr)
    need = topk - cnt_gt
    tie_rows = (cnt_ge > topk) & (thr > _f32_key_const(NEG))
    j_scr[...] = jnp.full((tq, 1), s_len, jnp.int32)
    any_tie = jnp.max(jnp.where(tie_rows, 1, 0)) > 0

    @pl.when(any_tie)
    def _ties():
        def tie_chunk(c, carry):
            half_scr[c] = jnp.where(key_scr[c] == thr, -1 - (c * ch + lane), I16_MIN).astype(jnp.int16)
            return carry

        lax.fori_loop(0, n_need, tie_chunk, 0)
        j_scr[...] = jnp.where(tie_rows, -1 - search16(need), s_len)

    jv = j_scr[...]
    for c in range(nch):
        @pl.when((c < n_need) & any_tie)
        def _w():
            key = key_scr[c]
            sel = (key > thr) | ((key == thr) & (c * ch + lane <= jv))
            mask_ref[:, c * ch:(c + 1) * ch] = jnp.where(sel, 0.0, NEG).astype(mask_ref.dtype)

        @pl.when((c < n_need) & jnp.logical_not(any_tie))
        def _wf():
            mask_ref[:, c * ch:(c + 1) * ch] = jnp.where(key_scr[c] >= thr, 0.0, NEG).astype(mask_ref.dtype)

        @pl.when(c >= n_need)
        def _z():
            mask_ref[:, c * ch:(c + 1) * ch] = jnp.full((tq, ch), NEG, mask_ref.dtype)


def _dsa_select(proj, kidx, *, topk, tq=128):
    s_len = proj.shape[0]
    assert s_len < -I16_MIN
    ch = min(1024, s_len)
    nch = s_len // ch
    return pl.pallas_call(
        functools.partial(_dsa_select_kernel, tq=tq, ch=ch, nch=nch, topk=topk, s_len=s_len),
        grid=(s_len // tq,),
        in_specs=[pl.BlockSpec((tq, 512), lambda i: (i, BLK512["b_iq"])),
                  pl.BlockSpec((tq, LANES), lambda i: (i, BLK128["small"])),
                  pl.BlockSpec((s_len, LANES), lambda i: (0, 0))],
        out_specs=pl.BlockSpec((tq, s_len), lambda i: (i, 0)),
        out_shape=jax.ShapeDtypeStruct((s_len, s_len), BF16),
        scratch_shapes=[pltpu.VMEM((nch, tq, ch), jnp.int32), pltpu.VMEM((nch, tq, ch), jnp.int16),
                        pltpu.VMEM((tq, 1), jnp.int32)],
        compiler_params=_cparams(("arbitrary",)),
        name="dsa_select",
    )(proj, proj, kidx)


def _nsa_compress_kernel(x_ref, pos_ref, w1_ref, w2_ref, o_ref):
    x = (x_ref[0] + pos_ref[0, 0]).astype(BF16)
    hdn = _silu(_dot(x, w1_ref[0, 0].astype(BF16)))
    o_ref[0] = _dot(hdn.astype(BF16), w2_ref[0, 0].astype(BF16))


def _nsa_compress(xc, pos, w1, w2, layer):
    _, ncp, kdim = xc.shape
    return pl.pallas_call(
        _nsa_compress_kernel,
        grid=(4,),
        in_specs=[pl.BlockSpec((1, ncp, kdim), lambda i: (i, 0, 0)),
                  pl.BlockSpec((1, 1, 1, kdim), lambda i: (layer, i // 2, 0, 0)),
                  pl.BlockSpec((1, 1, kdim, NSA_CMP_HID), lambda i: (layer, i // 2, 0, 0)),
                  pl.BlockSpec((1, 1, NSA_CMP_HID, HEAD_DIM), lambda i: (layer, i // 2, 0, 0))],
        out_specs=pl.BlockSpec((1, ncp, HEAD_DIM), lambda i: (i, 0, 0)),
        out_shape=jax.ShapeDtypeStruct((4, ncp, HEAD_DIM), F32),
        compiler_params=_cparams(("arbitrary",)),
        name="nsa_compress",
    )(xc, pos, w1, w2)


def _nsa_cmp_kernel(q_ref, kv_ref, ov_ref, ex_ref, o_ref, mask_ref, *, tq, ncp, n_slc, topn, ch, nch):
    i = pl.program_id(0)
    q0 = i * tq
    n_need = (q0 + tq + ch - 1) // ch
    rpg = NSA_HEADS // NSA_GROUPS
    slopes = _alibi(NSA_HEADS)
    scale = HEAD_DIM ** -0.5
    qpos_c = q0 + lax.broadcasted_iota(jnp.int32, (tq, ncp), 0)
    cend = lax.broadcasted_iota(jnp.int32, (tq, ncp), 1) * NSA_CMP_STRIDE + (NSA_CMP_LEN - 1)
    dist_c = qpos_c - cend
    valid_c = dist_c >= 0
    distf = dist_c.astype(F32)
    qpos = q0 + lax.broadcasted_iota(jnp.int32, (tq, n_slc), 0)
    blk = lax.broadcasted_iota(jnp.int32, (tq, n_slc), 1)
    cur = qpos // NSA_SLC_LEN
    forced = (blk == 0) | (blk == cur) | (blk == cur - 1)
    blk_ok = blk * NSA_SLC_LEN <= qpos
    ov = ov_ref[...]
    imps = []
    for g in range(NSA_GROUPS):
        kc = kv_ref[g].astype(BF16)
        vc = kv_ref[NSA_GROUPS + g].astype(BF16)
        psum = jnp.zeros((tq, ncp), F32)
        for r in range(rpg):
            h = g * rpg + r
            qh = q_ref[:, h * HEAD_DIM:(h + 1) * HEAD_DIM].astype(BF16)
            s = _dot_nt(qh, kc) * scale - slopes[h] * distf
            s = jnp.where(valid_c, s, NEG)
            e = jnp.where(valid_c, jnp.exp(s - jnp.max(s, axis=1, keepdims=True)), 0.0)
            p = e / jnp.maximum(jnp.sum(e, axis=1, keepdims=True), 1e-30)
            o_ref[:, h * HEAD_DIM:(h + 1) * HEAD_DIM] = _dot(p.astype(BF16), vc)
            psum = psum + p
        p_hi = psum.astype(BF16)
        p_lo = (psum - p_hi.astype(F32)).astype(BF16)
        imp = _dot(p_hi, ov) + _dot(p_lo, ov)
        imp = jnp.where(forced, NSA_FORCE, imp)
        imps.append(jnp.where(blk_ok, imp, NEG))
    imps = [imp.T for imp in imps]
    blk_t = lax.broadcasted_iota(jnp.int32, (n_slc, tq), 0)
    sels = [jnp.full((n_slc, tq), NEG, F32) for _ in range(NSA_GROUPS)]
    for _ in range(topn):
        for g in range(NSA_GROUPS):
            mx = jnp.max(imps[g], axis=0, keepdims=True)
            first = jnp.min(jnp.where(imps[g] == mx, blk_t, n_slc), axis=0, keepdims=True)
            hit = blk_t == first
            sels[g] = jnp.where(hit, 0.0, sels[g])
            imps[g] = jnp.where(hit, -jnp.inf, imps[g])
    sels = [sel.T for sel in sels]
    for g in range(NSA_GROUPS):
        selb = sels[g].astype(BF16)
        for c in range(nch):
            @pl.when(c < n_need)
            def _w():
                tok = _dot(selb, ex_ref[:, c * ch:(c + 1) * ch])
                mask_ref[g, :, c * ch:(c + 1) * ch] = tok.astype(mask_ref.dtype)

            @pl.when(c >= n_need)
            def _z():
                mask_ref[g, :, c * ch:(c + 1) * ch] = jnp.full((tq, ch), NEG, mask_ref.dtype)


def _nsa_cmp(proj, kv_cmp, *, tq=256):
    s_len = proj.shape[0]
    ncp = kv_cmp.shape[1]
    n_slc = s_len // NSA_SLC_LEN
    topn = min(NSA_TOPN, n_slc)
    ch = min(1024, s_len)
    nch = s_len // ch
    starts = np.arange(ncp) * NSA_CMP_STRIDE
    slc_start = np.arange(n_slc) * NSA_SLC_LEN
    overlap = ((starts[:, None] < slc_start[None, :] + NSA_SLC_LEN)
               & (starts[:, None] + NSA_CMP_LEN > slc_start[None, :])).astype(np.float32)
    expand = (np.arange(s_len)[None, :] // NSA_SLC_LEN == np.arange(n_slc)[:, None]).astype(np.float32)
    return pl.pallas_call(
        functools.partial(_nsa_cmp_kernel, tq=tq, ncp=ncp, n_slc=n_slc, topn=topn, ch=ch, nch=nch),
        grid=(s_len // tq,),
        in_specs=[pl.BlockSpec((tq, 512), lambda i: (i, BLK512["d_q"])),
                  pl.BlockSpec((4, ncp, HEAD_DIM), lambda i: (0, 0, 0)),
                  pl.BlockSpec((ncp, n_slc), lambda i: (0, 0)),
                  pl.BlockSpec((n_slc, s_len), lambda i: (0, 0))],
        out_specs=[pl.BlockSpec((tq, 512), lambda i: (i, 0)),
                   pl.BlockSpec((NSA_GROUPS, tq, s_len), lambda i: (0, i, 0))],
        out_shape=[jax.ShapeDtypeStruct((s_len, 512), F32),
                   jax.ShapeDtypeStruct((NSA_GROUPS, s_len, s_len), BF16)],
        compiler_params=_cparams(("arbitrary",)),
        name="nsa_cmp",
    )(proj, kv_cmp, jnp.asarray(overlap, BF16), jnp.asarray(expand, BF16))


def _nsa_combine_kernel(g_ref, oc_ref, os_ref, ow_ref, o_ref):
    gt = _sigmoid(g_ref[...])
    for h in range(NSA_HEADS):
        sl = slice(h * HEAD_DIM, (h + 1) * HEAD_DIM)
        c0 = IDX_HEADS + 3 * h
        o_ref[:, sl] = (gt[:, c0:c0 + 1] * oc_ref[:, sl] + gt[:, c0 + 1:c0 + 2] * os_ref[:, sl]
                        + gt[:, c0 + 2:c0 + 3] * ow_ref[:, sl])


def _nsa_combine(proj, o_cmp, o_slc, o_win, *, tm):
    m = proj.shape[0]
    spec = pl.BlockSpec((tm, 512), lambda i: (i, 0))
    return pl.pallas_call(
        _nsa_combine_kernel,
        grid=(m // tm,),
        in_specs=[pl.BlockSpec((tm, LANES), lambda i: (i, BLK128["small"])), spec, spec, spec],
        out_specs=spec,
        out_shape=jax.ShapeDtypeStruct((m, 512), F32),
        compiler_params=_cparams(("arbitrary",)),
        name="nsa_combine",
    )(proj, o_cmp, o_slc, o_win)


def _merge_kernel(u_ref, oa_ref, ob_ref, oc_ref, od_ref, wg0, wg1, wg2, wg3, wb_ref, o_ref,
                  wgb_ref, wbb_ref):
    wgs = (wg0, wg1, wg2, wg3)

    @pl.when(pl.program_id(1) == 0)
    def _():
        for mch in range(N_BRANCH):
            wgb_ref[mch] = wgs[mch][0].astype(BF16)
            wbb_ref[mch] = wb_ref[0, mch].astype(BF16)

    u = u_ref[...]
    acc = None
    for mch, o_ref_m in enumerate((oa_ref, ob_ref, oc_ref, od_ref)):
        gte = _sigmoid(_dot(u, wgb_ref[mch]))
        z = _dot(o_ref_m[...].astype(BF16), wbb_ref[mch])
        acc = gte * z if acc is None else acc + gte * z
    o_ref[...] = acc.astype(o_ref.dtype)


def _merge(u, branches, w_gate, w_branch, layer, *, tm, tn):
    m, d = u.shape
    nj = d // tn
    bspec = pl.BlockSpec((tm, BRANCH_W), lambda j, i: (i, 0))
    wg_specs = [pl.BlockSpec((1, d, tn),
                             functools.partial(lambda j, i, mch: (layer, 0, mch * nj + j), mch=mch))
                for mch in range(N_BRANCH)]
    return pl.pallas_call(
        _merge_kernel,
        grid=(nj, m // tm),
        in_specs=[pl.BlockSpec((tm, d), lambda j, i: (i, 0)), bspec, bspec, bspec, bspec,
                  *wg_specs,
                  pl.BlockSpec((1, N_BRANCH, BRANCH_W, tn), lambda j, i: (layer, 0, 0, j))],
        out_specs=pl.BlockSpec((tm, tn), lambda j, i: (i, j)),
        out_shape=jax.ShapeDtypeStruct((m, d), BF16),
        scratch_shapes=[pltpu.VMEM((N_BRANCH, d, tn), BF16), pltpu.VMEM((N_BRANCH, BRANCH_W, tn), BF16)],
        compiler_params=_cparams(("arbitrary", "arbitrary")),
        name="merge",
    )(u, *branches, w_gate, w_gate, w_gate, w_gate, w_branch)


def _router_kernel(u_ref, r_ref, o_ref):
    logits = _dot(u_ref[...], r_ref[0].astype(BF16))
    lane = lax.broadcasted_iota(jnp.int32, logits.shape, 1)
    lg = jnp.where(lane < N_EXPERTS, logits, -jnp.inf)
    m1 = jnp.max(lg, axis=1, keepdims=True)
    i1 = jnp.min(jnp.where(lg == m1, lane, LANES), axis=1, keepdims=True)
    lg2 = jnp.where(lane == i1, -jnp.inf, lg)
    m2 = jnp.max(lg2, axis=1, keepdims=True)
    i2 = jnp.min(jnp.where(lg2 == m2, lane, LANES), axis=1, keepdims=True)
    e2 = jnp.exp(m2 - m1)
    w1 = 1.0 / (1.0 + e2)
    w2 = e2 / (1.0 + e2)
    o_ref[...] = jnp.where(lane == 0, i1.astype(F32),
                           jnp.where(lane == 1, i2.astype(F32),
                                     jnp.where(lane == 2, w1, jnp.where(lane == 3, w2, 0.0))))


def _router(u, router_padded, layer, *, tm):
    m, d = u.shape
    return pl.pallas_call(
        _router_kernel,
        grid=(m // tm,),
        in_specs=[pl.BlockSpec((tm, d), lambda i: (i, 0)),
                  pl.BlockSpec((1, d, LANES), lambda i: (layer, 0, 0))],
        out_specs=pl.BlockSpec((tm, LANES), lambda i: (i, 0)),
        out_shape=jax.ShapeDtypeStruct((m, LANES), F32),
        compiler_params=_cparams(("arbitrary",)),
        name="router",
    )(u, router_padded)


MOE_GROUP_TILE = 512
MOE_ROW_TILE = 256
MOE_TOK_CHUNK = 256


def _moe_plan(ridx, rw, s_len):
    gm, tm, ct, n_e = MOE_GROUP_TILE, MOE_ROW_TILE, MOE_TOK_CHUNK, N_EXPERTS
    i32 = jnp.int32
    e_a = ridx.reshape(-1).astype(i32)
    oh = (e_a[:, None] == jnp.arange(n_e, dtype=i32)[None, :]).astype(i32)
    csum = jnp.cumsum(oh, axis=0)
    rank_a = jnp.sum((csum - oh) * oh, axis=1)
    ntile_e = (csum[-1] + gm - 1) // gm
    tile_end = jnp.cumsum(ntile_e)
    pos_a = jnp.take(tile_end - ntile_e, e_a) * gm + rank_a
    n_rows = 2 * s_len + n_e * gm
    n_tiles = n_rows // tm
    n_chunks = s_len // ct
    row_tok = jnp.full((n_rows,), -1, i32).at[pos_a].set(jnp.arange(2 * s_len, dtype=i32) // 2)
    row_w = jnp.zeros((n_rows,), F32).at[pos_a].set(rw.reshape(-1))
    tile_e = jnp.minimum(jnp.searchsorted(tile_end, jnp.arange(n_rows // gm, dtype=i32), side="right"),
                         n_e - 1).astype(i32)
    rt = row_tok.reshape(n_tiles, tm)
    lo = jnp.min(jnp.where(rt >= 0, rt, s_len - 1), axis=1) // ct
    hi = jnp.maximum(jnp.max(jnp.where(rt >= 0, rt, 0), axis=1) // ct, lo)
    n_i = hi - lo + 1
    end = jnp.cumsum(n_i)
    n_work = n_tiles + n_e * n_chunks
    w = jnp.arange(n_work, dtype=i32)
    wt = jnp.minimum(jnp.searchsorted(end, w, side="right"), n_tiles - 1).astype(i32)
    wc = jnp.clip(jnp.take(lo, wt) + w - jnp.take(end - n_i, wt), 0, n_chunks - 1).astype(i32)
    wa = ((w < end[-1]) & jnp.take(jnp.any(rt >= 0, axis=1), wt)).astype(i32)
    order = jnp.argsort(jnp.where(wa > 0, wc * n_tiles + wt, n_chunks * n_tiles + w))
    vc = jnp.where(wa > 0, wc, n_chunks - 1)[order]
    return dict(row_tok=row_tok, row_w=row_w, tile_e=tile_e, n_used=tile_end[n_e - 1:].astype(i32),
                n_tiles=n_tiles, n_work=n_work,
                gather=(wt, wc, wa), combine=(vc, wt[order], wa[order]))


def _moe_gather_kernel(wt_ref, wc_ref, wa_ref, tok_ref, u_ref, o_ref):
    w = pl.program_id(0)

    @pl.when((w == 0) | (wt_ref[jnp.maximum(w - 1, 0)] != wt_ref[w]))
    def _():
        o_ref[...] = jnp.zeros_like(o_ref)

    @pl.when(wa_ref[w] > 0)
    def _():
        ct = u_ref.shape[0]
        cols = wc_ref[w] * ct + lax.broadcasted_iota(jnp.int32, (1, ct), 1)
        onehot = jnp.where(tok_ref[...] == cols, 1.0, 0.0).astype(BF16)
        o_ref[...] += _dot(onehot, u_ref[...]).astype(o_ref.dtype)


def _moe_gather(u, plan):
    s_len, d = u.shape
    tm, ct = MOE_ROW_TILE, MOE_TOK_CHUNK
    n_rows = plan["row_tok"].shape[0]
    return pl.pallas_call(
        _moe_gather_kernel,
        grid_spec=pltpu.PrefetchScalarGridSpec(
            num_scalar_prefetch=3, grid=(plan["n_work"],),
            in_specs=[pl.BlockSpec((tm, 1), lambda w, wt, wc, wa: (wt[w], 0)),
                      pl.BlockSpec((ct, d), lambda w, wt, wc, wa: (wc[w], 0))],
            out_specs=pl.BlockSpec((tm, d), lambda w, wt, wc, wa: (wt[w], 0))),
        out_shape=jax.ShapeDtypeStruct((n_rows, d), BF16),
        compiler_params=_cparams(("arbitrary",)),
        name="moe_gather",
    )(*plan["gather"], plan["row_tok"].reshape(n_rows, 1), u)


def _moe_combine_kernel(vc_ref, vt_ref, va_ref, tok_ref, y0_ref, y1_ref, y2_ref, o_ref):
    w = pl.program_id(0)
    chunk = vc_ref[w]

    @pl.when((w == 0) | (vc_ref[jnp.maximum(w - 1, 0)] != chunk))
    def _():
        o_ref[...] = jnp.zeros_like(o_ref)

    @pl.when(va_ref[w] > 0)
    def _():
        ct = o_ref.shape[0]
        rows = chunk * ct + lax.broadcasted_iota(jnp.int32, (ct, 1), 0)
        onehot_t = jnp.where(rows == tok_ref[0], 1.0, 0.0).astype(BF16)
        o_ref[...] += (_dot(onehot_t, y0_ref[...]) + _dot(onehot_t, y1_ref[...])
                       + _dot(onehot_t, y2_ref[...]))


def _moe_combine(ys3, plan, s_len):
    n_rows, d = ys3[0].shape
    tm, ct = MOE_ROW_TILE, MOE_TOK_CHUNK
    yspec = pl.BlockSpec((tm, d), lambda w, vc, vt, va: (vt[w], 0))
    return pl.pallas_call(
        _moe_combine_kernel,
        grid_spec=pltpu.PrefetchScalarGridSpec(
            num_scalar_prefetch=3, grid=(plan["n_work"],),
            in_specs=[pl.BlockSpec((1, 1, tm), lambda w, vc, vt, va: (vt[w], 0, 0)), yspec, yspec, yspec],
            out_specs=pl.BlockSpec((ct, d), lambda w, vc, vt, va: (vc[w], 0))),
        out_shape=jax.ShapeDtypeStruct((s_len, d), F32),
        compiler_params=_cparams(("arbitrary",)),
        name="moe_combine",
    )(*plan["combine"], plan["row_tok"].reshape(plan["n_tiles"], 1, tm), *ys3)


def _gmm_kernel(te_ref, nu_ref, *refs, swiglu):
    it = iter(refs)
    a_ref = next(it)
    w_refs = [next(it), next(it)] if swiglu else [next(it)]
    rw_ref = next(it) if swiglu else None
    o_refs = [next(it)] if swiglu else [next(it), next(it), next(it)]
    wb_refs = [next(it) for _ in w_refs]
    i = pl.program_id(1)
    used = i < nu_ref[0]

    @pl.when(used & ((i == 0) | (te_ref[i] != te_ref[jnp.maximum(i - 1, 0)])))
    def _():
        for w_ref, wb_ref in zip(w_refs, wb_refs):
            wb_ref[...] = w_ref[0, 0].astype(BF16)

    @pl.when(used)
    def _():
        a = a_ref[...]
        if swiglu:
            h = _silu(_dot(a, wb_refs[0][...])) * _dot(a, wb_refs[1][...]) * rw_ref[...]
            o_refs[0][...] = h.astype(BF16)
        else:
            y = _dot(a, wb_refs[0][...])
            hi = y.astype(BF16)
            r1 = y - hi.astype(F32)
            mid = r1.astype(BF16)
            o_refs[0][...] = hi
            o_refs[1][...] = mid
            o_refs[2][...] = (r1 - mid.astype(F32)).astype(BF16)

    @pl.when(jnp.logical_not(used))
    def _():
        for o_ref in o_refs:
            o_ref[...] = jnp.zeros_like(o_ref)


def _gmm(a, ws, layer, plan, *, tn, row_w=None, name="gmm"):
    n_rows, k = a.shape
    n = ws[0].shape[3]
    tm = MOE_GROUP_TILE
    swiglu = len(ws) == 2
    wspec = pl.BlockSpec((1, 1, k, tn), lambda j, i, te, nu: (layer, te[i], 0, j))
    in_specs = [pl.BlockSpec((tm, k), lambda j, i, te, nu: (i, 0))] + [wspec] * len(ws)
    args = [a, *ws]
    if swiglu:
        in_specs.append(pl.BlockSpec((tm, 1), lambda j, i, te, nu: (i, 0)))
        args.append(row_w.reshape(n_rows, 1))
    ospec = pl.BlockSpec((tm, tn), lambda j, i, te, nu: (i, j))
    oshape = jax.ShapeDtypeStruct((n_rows, n), BF16)
    return pl.pallas_call(
        functools.partial(_gmm_kernel, swiglu=swiglu),
        grid_spec=pltpu.PrefetchScalarGridSpec(
            num_scalar_prefetch=2, grid=(n // tn, n_rows // tm),
            in_specs=in_specs,
            out_specs=ospec if swiglu else [ospec] * 3,
            scratch_shapes=[pltpu.VMEM((k, tn), BF16) for _ in ws]),
        out_shape=oshape if swiglu else [oshape] * 3,
        compiler_params=_cparams(("arbitrary", "arbitrary")),
        name=name,
    )(plan["tile_e"], plan["n_used"], *args)


def _permute_w_in(w):
    cols = []
    for nm in _NEW_ORDER:
        if nm.startswith("pad"):
            cols.append(jnp.zeros(w.shape[:2] + (int(nm[3:]),), w.dtype))
        else:
            o, n = _ORIG[nm]
            cols.append(w[:, :, o:o + n])
    out = jnp.concatenate(cols, axis=2)
    assert out.shape[2] == PROJ_W
    return out


def _nsa_cmp_inputs(proj):
    s_len = proj.shape[0]
    n_cmp = (s_len - NSA_CMP_LEN) // NSA_CMP_STRIDE + 1
    ncp = s_len // NSA_CMP_STRIDE
    xs = []
    for jj in range(2):
        for g in range(NSA_GROUPS):
            c0 = COL_DKV + jj * 128 + g * HEAD_DIM
            r = proj[:, c0:c0 + HEAD_DIM].reshape(ncp, NSA_CMP_STRIDE * HEAD_DIM)
            x = jnp.concatenate([r[:-1], r[1:]], axis=1)
            xs.append(jnp.pad(x, ((0, ncp - n_cmp), (0, 0))))
    return jnp.stack(xs)


def _token_mixers(u, layer, p, cfg):
    s_len = u.shape[0]
    tm = cfg["tm"]
    proj = _mm(u, p["w_in"], layer, tm=tm, tn=512, name="in_proj")
    kv_b = _mm(proj, p["dsa_w_ukv"], layer, tm=tm, tn=512, a_blk=BLK128["b_kv"], k=DSA_KV_RANK,
               prologue="rms", gain=p["dsa_kv_norm_g"][layer], name="dsa_kv")
    kv = _kv_pack(proj, kv_b, tm=cfg["tm_ln"])

    lambda_init = 0.8 - 0.6 * math.exp(-0.3 * layer)
    sl_a = _alibi(DA_HEADS)
    units_a = [(2 * h + mp, h * 256, 256, 0,
                ((h * 128 + mp * 64, sl_a[h], ((2 * h + mp) * 128, 128), 0),))
               for h in range(DA_HEADS) for mp in range(2)]
    o_a2 = _flash(proj, kv["a_k"], kv["a_v"], units=units_a,
                  q_spec=(512, BLK512["a_q"]), out_w=1024, tq=cfg["tq"], tk=cfg["tk"], name="diff_attn")
    o_a = _diff_final(o_a2, p["diff_lambda"], layer, p["diff_subln_g"][layer], lambda_init, tm=tm)

    topk = min(DSA_TOPK_MAX, s_len // 4)
    mask_b = _dsa_select(proj, kv["b_ik"], topk=topk)
    sl8 = _alibi(8)
    units_b = [(h, h * 128, 128, 0, ((h * 64, sl8[h], (h * 64, 64), 0),)) for h in range(DSA_HEADS)]
    o_b = _flash(proj, kv["b_k"], kv["b_v"],
                 units=units_b, q_spec=(512, BLK512["b_q"]), out_w=512, tq=cfg["tq"], tk=cfg["tk"],
                 mask=mask_b.reshape(1, s_len, s_len), name="dsa_attn")

    def gqa_units(masked):
        return [(g, g * 128, 128, g if masked else 0,
                 tuple(((g * 4 + r) * 64, sl8[g * 4 + r], ((g * 4 + r) * 64, 64), g * 4 + r)
                       for r in range(4)))
                for g in range(2)]

    def gqa_kv(k_name, v_name):
        return kv[k_name], kv[v_name]

    sinks = jnp.pad(p["swa_sinks"][layer].reshape(1, SWA_HEADS), ((0, 0), (0, LANES - SWA_HEADS)))
    o_c = _flash(proj, *gqa_kv("c_k", "c_v"), units=gqa_units(False), q_spec=(512, BLK512["c_q"]),
                 out_w=512, tq=cfg["tb"], tk=cfg["tb"], window=SWA_WINDOW, sinks=sinks, name="swa_attn")

    kv_cmp = _nsa_compress(_nsa_cmp_inputs(proj), p["nsa_cmp_pos"], p["nsa_cmp_w1"], p["nsa_cmp_w2"],
                           layer)
    o_cmp, mask_d = _nsa_cmp(proj, kv_cmp)
    o_slc = _flash(proj, *gqa_kv("d_ks", "d_vs"), units=gqa_units(True), q_spec=(512, BLK512["d_q"]),
                   out_w=512, tq=cfg["tq"], tk=cfg["tk"], mask=mask_d, name="nsa_slc_attn")
    o_win = _flash(proj, *gqa_kv("d_kw", "d_vw"), units=gqa_units(False), q_spec=(512, BLK512["d_q"]),
                   out_w=512, tq=cfg["tw"], tk=cfg["tw"], window=NSA_WINDOW, name="nsa_win_attn")
    o_d = _nsa_combine(proj, o_cmp, o_slc, o_win, tm=tm)

    merged = _merge(u, (o_a, o_b, o_c, o_d), p["w_gate"], p["w_branch"], layer,
                    tm=cfg["tm_merge"], tn=256)
    return _mm(merged, p["w_o"], layer, tm=tm, tn=512, name="out_proj")


def _config(s_len):
    return dict(tm=min(1024, s_len), tm_merge=min(512, s_len), tm_ln=min(512, s_len),
                tq=min(256, s_len), tk=min(1024, s_len), tb=min(128, s_len), tw=min(512, s_len))


def kernel(x, c, cond_w, cond_b, w_in, diff_lambda, diff_subln_g, dsa_kv_norm_g, dsa_w_uk, dsa_w_uv,
           swa_sinks, nsa_cmp_pos, nsa_cmp_w1, nsa_cmp_w2, w_branch, w_gate, w_o,
           ln1_g, ln1_b, ln2_g, ln2_b, ffn_w_gate, ffn_w_up, ffn_w_down,
           moe_router, moe_w_gate, moe_w_up, moe_w_down):
    bsz, s_len, d = x.shape
    assert bsz == 1 and d == D_MODEL
    depth = cond_w.shape[0]
    cfg = _config(s_len)
    xs = x.reshape(s_len, d)
    c8 = jnp.broadcast_to(c.reshape(1, d), (8, d))
    p = dict(w_in=_permute_w_in(w_in), diff_lambda=diff_lambda, diff_subln_g=diff_subln_g,
             dsa_kv_norm_g=dsa_kv_norm_g, dsa_w_ukv=jnp.concatenate([dsa_w_uk, dsa_w_uv], axis=2),
             swa_sinks=swa_sinks,
             nsa_cmp_pos=nsa_cmp_pos.reshape(depth, 2, 1, NSA_CMP_LEN * HEAD_DIM),
             nsa_cmp_w1=nsa_cmp_w1, nsa_cmp_w2=nsa_cmp_w2, w_branch=w_branch, w_gate=w_gate, w_o=w_o)
    router_p = jnp.pad(moe_router, ((0, 0), (0, 0), (0, LANES - N_EXPERTS)))
    mods = [_mm(c8, cond_w, l, tm=8, tn=512, prologue="silu", bias=cond_b[l], name="cond")[0:1]
            for l in range(depth)]
    u = _modulate(xs, mods[0], 1, 0, tm=cfg["tm_ln"])
    for l in range(depth):
        y = _token_mixers(u, l, p, cfg)
        xs, u = _resid_ln(xs, y, mods[l], 2, ln1_g[l], ln1_b[l], mods[l], 4, 3, tm=cfg["tm_ln"])
        jx = l // 2
        if l % 2 == 0:
            hdn = _swiglu_up(u, ffn_w_gate, ffn_w_up, jx, tm=cfg["tm"], tn=512, name="ffn_up")
            y = _mmk(hdn, ffn_w_down, jx, tm=cfg["tm"], tn=d, tk=512, name="ffn_down")
        else:
            rt = _router(u, router_p, jx, tm=cfg["tm"])
            plan = _moe_plan(rt[:, 0:2], rt[:, 2:4], s_len)
            hdn = _gmm(_moe_gather(u, plan), (moe_w_gate, moe_w_up), jx, plan, tn=512,
                       row_w=plan["row_w"], name="moe_up")
            y = _moe_combine(_gmm(hdn, (moe_w_down,), jx, plan, tn=512, name="moe_down"), plan, s_len)
        nxt = min(l + 1, depth - 1)
        xs, u = _resid_ln(xs, y, mods[l], 5, ln2_g[l], ln2_b[l], mods[nxt], 1, 0, tm=cfg["tm_ln"])
    return xs.reshape(bsz, s_len, d)
```
